```python
import math
import jax, jax.numpy as jnp
from jax import lax
import numpy as np

D_MODEL = 1024
BATCH = 16
SEQ = 2048
DEPTH = 1

RW_HEADS = 8
RW_HEAD_DIM = 64
RW_DIM = RW_HEADS * RW_HEAD_DIM
W_LORA = 64
A_LORA = 64
G_LORA = 128
GN_EPS = 64e-5
MLA_HEADS = 8
QK_NOPE = 64
QK_ROPE = 32
V_HEAD = 64
Q_LORA = 384
KV_LORA = 256
ROPE_THETA = 10000.0
Q_BLOCK = 128
N_GROUPS = 4
EXPERTS_PER_GROUP = 8
N_EXPERTS = N_GROUPS * EXPERTS_PER_GROUP
TOP_K = 2
D_EXPERT = 256
EXPERT_BLOCK = 256
NORM_EPS = 1e-6

RW_COLS = 3 * RW_DIM + W_LORA + A_LORA + G_LORA
MLA_COLS = Q_LORA + KV_LORA + QK_ROPE
GATE_COLS = 2 * D_MODEL
IN_COLS = RW_COLS + MLA_COLS + GATE_COLS

kernel_name = "hybrid_rwkv7_mla_hiermoe"


def rmsnorm(x, g):
    xf = x.astype(jnp.float32)
    y = xf * lax.rsqrt(jnp.mean(xf * xf, axis=-1, keepdims=True) + NORM_EPS)
    return (y * g.astype(jnp.float32)).astype(x.dtype)


def rope(t, cos, sin):
    t1, t2 = jnp.split(t, 2, axis=-1)
    return jnp.concatenate([t1 * cos - t2 * sin, t2 * cos + t1 * sin], axis=-1)


def rwkv7_mix(z_in, mu, w0, w_up, a0, a_up, g_up, k_k, k_a, r_k, gn_w, gn_b):
    B, S, _ = z_in.shape
    dt = z_in.dtype
    prev = jnp.pad(z_in, ((0, 0), (1, 0), (0, 0)))[:, :-1]
    z = z_in + (prev - z_in) * mu
    r, k, v, wd, ad, gd = jnp.split(
        z, [RW_DIM, 2 * RW_DIM, 3 * RW_DIM, 3 * RW_DIM + W_LORA, 3 * RW_DIM + W_LORA + A_LORA], axis=-1)
    w_log = -jax.nn.softplus(-(w0 + jnp.tanh(wd) @ w_up).astype(jnp.float32)) - 0.5
    decay = jnp.exp(-jnp.exp(w_log))
    a = jax.nn.sigmoid(a0 + ad @ a_up)
    g = jax.nn.sigmoid(gd) @ g_up

    def heads(t):
        return t.reshape(B, S, RW_HEADS, RW_HEAD_DIM).astype(jnp.float32)

    kk = heads(k * k_k)
    kk = kk / jnp.maximum(jnp.sqrt(jnp.sum(kk * kk, axis=-1, keepdims=True)), 1e-12)
    k = k * (1.0 + (a - 1.0) * k_a)
    r_h, k_h, v_h, a_h, w_h = heads(r), heads(k), heads(v), heads(a), heads(decay)

    def step(state, inp):
        r_t, w_t, k_t, v_t, kk_t, a_t = inp
        sa = jnp.einsum('bhvk,bhk->bhv', state, -kk_t)
        state = (state * w_t[:, :, None, :]
                 + sa[..., None] * (kk_t * a_t)[:, :, None, :]
                 + v_t[..., None] * k_t[:, :, None, :])
        return state, jnp.einsum('bhvk,bhk->bhv', state, r_t)

    xs = tuple(jnp.moveaxis(t, 1, 0) for t in (r_h, w_h, k_h, v_h, kk, a_h))
    s0 = jnp.zeros((B, RW_HEADS, RW_HEAD_DIM, RW_HEAD_DIM), jnp.float32)
    _, ys = lax.scan(step, s0, xs)
    y = jnp.moveaxis(ys, 0, 1)
    mean = jnp.mean(y, axis=-1, keepdims=True)
    var = jnp.mean(jnp.square(y - mean), axis=-1, keepdims=True)
    y = ((y - mean) * lax.rsqrt(var + GN_EPS)).reshape(B, S, RW_DIM)
    y = y * gn_w.astype(jnp.float32) + gn_b.astype(jnp.float32)
    bonus = jnp.sum(r_h * k_h * r_k.astype(jnp.float32), axis=-1, keepdims=True) * v_h
    y = y + bonus.reshape(B, S, RW_DIM)
    return (y * g.astype(jnp.float32)).astype(dt)


def mla_mix(q_d, kv_d, kr_raw, cos, sin, g_qa, w_q_up, g_kva, w_kv_up):
    B, S, _ = q_d.shape
    q = (rmsnorm(q_d, g_qa) @ w_q_up).reshape(B, S, MLA_HEADS, QK_NOPE + QK_ROPE)
    q_nope, q_rope = q[..., :QK_NOPE], q[..., QK_NOPE:]
    kv = (rmsnorm(kv_d, g_kva) @ w_kv_up).reshape(B, S, MLA_HEADS, QK_NOPE + V_HEAD)
    k_nope, v = kv[..., :QK_NOPE], kv[..., QK_NOPE:]
    q_rope = rope(q_rope, cos[:, :, None, :], sin[:, :, None, :])
    k_rope = rope(kr_raw, cos, sin)
    scale = 1.0 / math.sqrt(QK_NOPE + QK_ROPE)
    outs = []
    for start in range(0, S, Q_BLOCK):
        end = min(start + Q_BLOCK, S)
        s = (jnp.einsum('bqhd,bkhd->bhqk', q_nope[:, start:end], k_nope[:, :end])
             + jnp.einsum('bqhr,bkr->bhqk', q_rope[:, start:end], k_rope[:, :end]))
        s = s.astype(jnp.float32) * scale
        causal = jnp.arange(end)[None, :] <= jnp.arange(start, end)[:, None]
        p = jax.nn.softmax(jnp.where(causal, s, -jnp.inf), axis=-1).astype(v.dtype)
        outs.append(jnp.einsum('bhqk,bkhd->bqhd', p, v[:, :end]))
    return jnp.concatenate(outs, axis=1).reshape(B, S, MLA_HEADS * V_HEAD)


def hier_moe(h, w_group, b_group, w_router, b_router, w_gu, w_down):
    B, S, D = h.shape
    N = B * S
    hf = h.reshape(N, D)
    p_group = jax.nn.softmax((hf @ w_group).astype(jnp.float32) + b_group.astype(jnp.float32), axis=-1)
    g_sel = jnp.argmax(p_group, axis=-1)
    gate_g = jnp.take_along_axis(p_group, g_sel[:, None], axis=-1)
    fine = ((hf @ w_router).astype(jnp.float32) + b_router.astype(jnp.float32)
            ).reshape(N, N_GROUPS, EXPERTS_PER_GROUP)
    fine_sel = jnp.take_along_axis(fine, g_sel[:, None, None], axis=1)[:, 0]
    top_v, top_i = lax.top_k(fine_sel, TOP_K)
    gate = jax.nn.softmax(top_v, axis=-1) * gate_g
    expert = g_sel[:, None] * EXPERTS_PER_GROUP + top_i

    A = N * TOP_K
    e_flat = expert.reshape(A)
    t_flat = jnp.arange(A) // TOP_K
    w_flat = gate.reshape(A)
    order = jnp.argsort(e_flat)
    e_s, t_s, w_s = e_flat[order], t_flat[order], w_flat[order]
    counts = jnp.bincount(e_flat, length=N_EXPERTS)
    starts = jnp.cumsum(counts) - counts
    padded = (counts + EXPERT_BLOCK - 1) // EXPERT_BLOCK * EXPERT_BLOCK
    pad_end = jnp.cumsum(padded)
    pad_start = pad_end - padded
    dest = pad_start[e_s] + (jnp.arange(A) - starts[e_s])
    n_blocks = -(-A // EXPERT_BLOCK) + N_EXPERTS
    P = n_blocks * EXPERT_BLOCK
    tok_buf = jnp.full((P,), N, jnp.int32).at[dest].set(t_s.astype(jnp.int32))
    w_buf = jnp.zeros((P,), jnp.float32).at[dest].set(w_s)
    blk_expert = jnp.minimum(
        jnp.searchsorted(pad_end, jnp.arange(n_blocks) * EXPERT_BLOCK, side='right'), N_EXPERTS - 1)
    x_pad = jnp.concatenate([hf, jnp.zeros((1, D), hf.dtype)], axis=0)
    xb = x_pad[tok_buf].reshape(n_blocks, EXPERT_BLOCK, D)

    def expert_block(args):
        xblk, e = args
        gt, up = jnp.split(xblk @ w_gu[e], 2, axis=-1)
        return (jax.nn.silu(gt) * up) @ w_down[e]

    yb = lax.map(expert_block, (xb, blk_expert)).reshape(P, D)
    out = jnp.zeros((N + 1, D), h.dtype).at[tok_buf].add(yb * w_buf[:, None].astype(h.dtype))
    return out[:N].reshape(B, S, D)


def setup_inputs(seed: int = 0) -> dict:
    key = jax.random.key(seed)
    ks = jax.random.split(key, 32)
    f32 = jnp.float32

    def nrm(k, shape, scale):
        return jax.random.normal(k, shape, f32) * scale

    L, D = DEPTH, D_MODEL
    x = jax.random.normal(ks[0], (BATCH, SEQ, D), f32)
    offsets = jax.random.randint(ks[1], (BATCH, 1), 0, 1024, dtype=jnp.int32)
    positions = offsets + jnp.arange(SEQ, dtype=jnp.int32)[None, :]
    return {
        "x": x,
        "positions": positions,
        "mix_norm_g": 1.0 + nrm(ks[2], (L, D), 0.02),
        "w_in": nrm(ks[3], (L, D, IN_COLS), D ** -0.5),
        "rw_mu": jax.random.uniform(ks[4], (L, RW_COLS), f32),
        "rw_w0": jax.random.uniform(ks[5], (L, RW_DIM), f32, -6.0, -1.0),
        "rw_w_up": nrm(ks[6], (L, W_LORA, RW_DIM), 0.1 * W_LORA ** -0.5),
        "rw_a0": nrm(ks[7], (L, RW_DIM), 0.1),
        "rw_a_up": nrm(ks[8], (L, A_LORA, RW_DIM), 0.5 * A_LORA ** -0.5),
        "rw_g_up": nrm(ks[9], (L, G_LORA, RW_DIM), G_LORA ** -0.5),
        "rw_k_k": 0.85 + nrm(ks[10], (L, RW_DIM), 0.02),
        "rw_k_a": 1.0 + nrm(ks[11], (L, RW_DIM), 0.02),
        "rw_r_k": nrm(ks[12], (L, RW_HEADS, RW_HEAD_DIM), 0.1),
        "rw_gn_w": 1.0 + nrm(ks[13], (L, RW_DIM), 0.02),
        "rw_gn_b": nrm(ks[14], (L, RW_DIM), 0.02),
        "mla_g_qa": 1.0 + nrm(ks[15], (L, Q_LORA), 0.02),
        "mla_w_q_up": nrm(ks[16], (L, Q_LORA, MLA_HEADS * (QK_NOPE + QK_ROPE)), Q_LORA ** -0.5),
        "mla_g_kva": 1.0 + nrm(ks[17], (L, KV_LORA), 0.02),
        "mla_w_kv_up": nrm(ks[18], (L, KV_LORA, MLA_HEADS * (QK_NOPE + V_HEAD)), KV_LORA ** -0.5),
        "w_branch_rw": nrm(ks[19], (L, RW_DIM, D), RW_DIM ** -0.5),
        "w_branch_mla": nrm(ks[20], (L, MLA_HEADS * V_HEAD, D), (MLA_HEADS * V_HEAD) ** -0.5),
        "w_out": nrm(ks[21], (L, D, D), D ** -0.5),
        "ffn_norm_g": 1.0 + nrm(ks[22], (L, D), 0.02),
        "moe_w_group": nrm(ks[23], (L, D, N_GROUPS), D ** -0.5),
        "moe_b_group": nrm(ks[24], (L, N_GROUPS), 0.01),
        "moe_w_router": nrm(ks[25], (L, D, N_EXPERTS), D ** -0.5),
        "moe_b_router": nrm(ks[26], (L, N_EXPERTS), 0.01),
        "moe_w_gu": nrm(ks[27], (L, N_EXPERTS, D, 2 * D_EXPERT), D ** -0.5),
        "moe_w_down": nrm(ks[28], (L, N_EXPERTS, D_EXPERT, D), D_EXPERT ** -0.5),
        "final_norm_g": 1.0 + nrm(ks[29], (D,), 0.02),
    }


def reference(x, positions, mix_norm_g, w_in, rw_mu, rw_w0, rw_w_up, rw_a0, rw_a_up, rw_g_up,
              rw_k_k, rw_k_a, rw_r_k, rw_gn_w, rw_gn_b, mla_g_qa, mla_w_q_up, mla_g_kva, mla_w_kv_up,
              w_branch_rw, w_branch_mla, w_out, ffn_norm_g, moe_w_group, moe_b_group, moe_w_router,
              moe_b_router, moe_w_gu, moe_w_down, final_norm_g):
    inv_freq = ROPE_THETA ** (-jnp.arange(0, QK_ROPE, 2, dtype=jnp.float32) / QK_ROPE)
    ang = positions.astype(jnp.float32)[..., None] * inv_freq
    cos, sin = jnp.cos(ang).astype(x.dtype), jnp.sin(ang).astype(x.dtype)

    for l in range(DEPTH):
        h = rmsnorm(x, mix_norm_g[l])
        cols = h @ w_in[l]
        c_rw = cols[..., :RW_COLS]
        c_q = cols[..., RW_COLS:RW_COLS + Q_LORA]
        c_kv = cols[..., RW_COLS + Q_LORA:RW_COLS + Q_LORA + KV_LORA]
        c_kr = cols[..., RW_COLS + Q_LORA + KV_LORA:RW_COLS + MLA_COLS]
        gate_rw = jax.nn.sigmoid(cols[..., RW_COLS + MLA_COLS:RW_COLS + MLA_COLS + D_MODEL])
        gate_mla = jax.nn.sigmoid(cols[..., RW_COLS + MLA_COLS + D_MODEL:])

        y_rw = rwkv7_mix(c_rw, rw_mu[l], rw_w0[l], rw_w_up[l], rw_a0[l], rw_a_up[l], rw_g_up[l],
                         rw_k_k[l], rw_k_a[l], rw_r_k[l], rw_gn_w[l], rw_gn_b[l])
        y_mla = mla_mix(c_q, c_kv, c_kr, cos, sin, mla_g_qa[l], mla_w_q_up[l], mla_g_kva[l], mla_w_kv_up[l])
        merged = gate_rw * (y_rw @ w_branch_rw[l]) + gate_mla * (y_mla @ w_branch_mla[l])
        x = x + merged @ w_out[l]

        h2 = rmsnorm(x, ffn_norm_g[l])
        x = x + hier_moe(h2, moe_w_group[l], moe_b_group[l], moe_w_router[l], moe_b_router[l],
                         moe_w_gu[l], moe_w_down[l])

    return rmsnorm(x, final_norm_g)
```

```python
import functools
import math

import jax
import jax.numpy as jnp
from jax import lax
from jax.experimental import pallas as pl
from jax.experimental.pallas import tpu as pltpu

F32 = jnp.float32
BF16 = jnp.bfloat16

RW_HEADS = 8
RW_HEAD_DIM = 64
RW_DIM = RW_HEADS * RW_HEAD_DIM
W_LORA = 64
A_LORA = 64
G_LORA = 128
GN_EPS = 64e-5
MLA_HEADS = 8
QK_NOPE = 64
QK_ROPE = 32
V_HEAD = 64
Q_LORA = 384
KV_LORA = 256
ROPE_THETA = 10000.0
N_GROUPS = 4
EXPERTS_PER_GROUP = 8
N_EXPERTS = N_GROUPS * EXPERTS_PER_GROUP
TOP_K = 2
D_EXPERT = 256
EXPERT_BLOCK = 256
NORM_EPS = 1e-6

LANES = 128
HEAD_PAIR = 2 * RW_HEAD_DIM
VMEM_LIMIT = 48 * 1024 * 1024


def _cparams(*sem):
    return pltpu.CompilerParams(dimension_semantics=sem, vmem_limit_bytes=VMEM_LIMIT)


def _mm(a, b, dims=((1,), (0,)), exact=False):
    dn = (dims, ((), ()))
    if exact:
        return lax.dot_general(a.astype(F32), b.astype(F32), dn,
                               precision=lax.Precision.HIGHEST, preferred_element_type=F32)
    return lax.dot_general(a.astype(BF16), b.astype(BF16), dn, preferred_element_type=F32)


def _mm_sel(sel_bf16, x, dims=((1,), (0,))):
    dn = (dims, ((), ()))
    hi = x.astype(BF16)
    r1 = x - hi.astype(F32)
    mid = r1.astype(BF16)
    lo = (r1 - mid.astype(F32)).astype(BF16)
    out = lax.dot_general(sel_bf16, hi, dn, preferred_element_type=F32)
    out = out + lax.dot_general(sel_bf16, mid, dn, preferred_element_type=F32)
    return out + lax.dot_general(sel_bf16, lo, dn, preferred_element_type=F32)


def _seg_sum(x, seg_bf16):
    hi = x.astype(BF16)
    lo = (x - hi.astype(F32)).astype(BF16)
    return (jnp.dot(hi, seg_bf16, preferred_element_type=F32)
            + jnp.dot(lo, seg_bf16, preferred_element_type=F32))


def _rms(x, g):
    return x * lax.rsqrt(jnp.mean(x * x, axis=-1, keepdims=True) + NORM_EPS) * g


def _in_proj_kernel(x_ref, g_ref, wrw_ref, wq_ref, wkvr_ref, wg_ref,
                    crw_ref, cq_ref, ckvr_ref, gate_ref):
    hb = _rms(x_ref[...], g_ref[...]).astype(BF16)
    crw_ref[...] = jnp.dot(hb, wrw_ref[...], preferred_element_type=F32)
    cq_ref[...] = jnp.dot(hb, wq_ref[...], preferred_element_type=F32)
    ckvr_ref[...] = jnp.dot(hb, wkvr_ref[...], preferred_element_type=F32)
    gate_ref[...] = jax.nn.sigmoid(jnp.dot(hb, wg_ref[...], preferred_element_type=F32)).astype(BF16)


def _in_proj(x2, g, w_rw, w_q, w_kvr, w_gate, tm):
    n, d = x2.shape
    full = lambda w: pl.BlockSpec(w.shape, lambda i: (0, 0))
    row = lambda c: pl.BlockSpec((tm, c), lambda i: (i, 0))
    return pl.pallas_call(
        _in_proj_kernel,
        grid=(n // tm,),
        in_specs=[row(d), full(g), full(w_rw), full(w_q), full(w_kvr), full(w_gate)],
        out_specs=[row(w_rw.shape[1]), row(w_q.shape[1]), row(w_kvr.shape[1]), row(w_gate.shape[1])],
        out_shape=[jax.ShapeDtypeStruct((n, w_rw.shape[1]), F32),
                   jax.ShapeDtypeStruct((n, w_q.shape[1]), F32),
                   jax.ShapeDtypeStruct((n, w_kvr.shape[1]), F32),
                   jax.ShapeDtypeStruct((n, w_gate.shape[1]), BF16)],
        compiler_params=_cparams("parallel"),
        name="in_proj",
    )(x2, g, w_rw, w_q, w_kvr, w_gate)


RW_CHUNK = 64
RW_TILE = 128


def _token_shift(cur, halo_ref, first):
    prev_row = jnp.where(first, 0.0, halo_ref[0, 7:8, :])
    rolled = pltpu.roll(cur, 1, 0)
    row = lax.broadcasted_iota(jnp.int32, cur.shape, 0)
    return jnp.where(row == 0, prev_row, rolled)


def _unit_lower_inverse(neg_strict_lower, eye):
    t = eye + neg_strict_lower
    p = neg_strict_lower
    steps = int(math.log2(RW_CHUNK)) - 1
    for _ in range(steps):
        p = _mm(p, p, exact=True)
        t = t + _mm(t, p, exact=True)
    return t


def _rwkv_kernel(r_ref, k_ref, v_ref, l_ref, hr_ref, hk_ref, hv_ref, hl_ref,
                 mur_ref, muk_ref, muv_ref, mul_ref, w0_ref, a0_ref, kk_ref, ka_ref, rk_ref,
                 gnw_ref, gnb_ref, wup_ref, aup_ref, gup_ref, y_ref, st_ref):
    i = pl.program_id(2)
    first = i == 0

    @pl.when(first)
    def _():
        st_ref[...] = jnp.zeros_like(st_ref)

    def mixed(c_ref, h_ref, mu_ref):
        cur = c_ref[0]
        return cur + (_token_shift(cur, h_ref, first) - cur) * mu_ref[...]

    zr = mixed(r_ref, hr_ref, mur_ref)
    zk = mixed(k_ref, hk_ref, muk_ref)
    zv = mixed(v_ref, hv_ref, muv_ref)
    zl = mixed(l_ref, hl_ref, mul_ref)
    z_wa = zl[:, :LANES]
    z_g = zl[:, LANES:]

    lane = lax.broadcasted_iota(jnp.int32, (LANES, LANES), 1)
    sub = lax.broadcasted_iota(jnp.int32, (LANES, LANES), 0)
    same_head = (lane // RW_HEAD_DIM) == (sub // RW_HEAD_DIM)
    seg = jnp.where(same_head, 1.0, 0.0).astype(BF16)

    w = w0_ref[...] + _mm(jnp.tanh(z_wa), wup_ref[...], exact=True)
    u = -w
    softplus = jnp.maximum(u, 0.0) + jnp.log(1.0 + jnp.exp(-jnp.abs(u)))
    log_decay = -jnp.exp(-softplus - 0.5)
    a = jax.nn.sigmoid(a0_ref[...] + _mm(z_wa, aup_ref[...]))
    g = _mm(jax.nn.sigmoid(z_g), gup_ref[...])

    kk = zk * kk_ref[...]
    kk = kk / jnp.maximum(jnp.sqrt(_seg_sum(kk * kk, seg)), 1e-12)
    k2 = zk * (1.0 + (a - 1.0) * ka_ref[...])
    bonus = _seg_sum(zr * k2 * rk_ref[...], seg) * zv
    kka = kk * a

    c = RW_CHUNK
    crow = lax.broadcasted_iota(jnp.int32, (c, c), 0)
    ccol = lax.broadcasted_iota(jnp.int32, (c, c), 1)
    tril_incl = crow >= ccol
    tril_strict = crow > ccol
    cum_sel = jnp.where(tril_incl, 1.0, 0.0).astype(BF16)
    eye_c = jnp.where(crow == ccol, 1.0, 0.0).astype(F32)
    eye_l = jnp.where(lane == sub, 1.0, 0.0).astype(F32)
    lo_half = lax.broadcasted_iota(jnp.int32, (c, LANES), 1) < RW_HEAD_DIM
    nt = ((1,), (1,))
    tn = ((0,), (0,))

    state = st_ref[...]
    ys = []
    for ci in range(RW_TILE // c):
        sl = slice(ci * c, (ci + 1) * c)
        ld = log_decay[sl]
        cum = _mm_sel(cum_sel, ld)
        tot = cum[c - 1:c, :]
        a_t = -kk[sl] * jnp.exp(cum - ld)
        e_neg = jnp.exp(-cum)
        b_t = kka[sl] * e_neg
        k_t = k2[sl] * e_neg
        r_t = zr[sl] * jnp.exp(cum)
        e_rest = jnp.exp(tot - cum)
        b_h = kka[sl] * e_rest
        k_h = k2[sl] * e_rest
        vv = zv[sl]

        a_hat = jnp.zeros((c, LANES), F32)
        u_loc = jnp.zeros((c, LANES), F32)
        q_hat = r_t
        y_loc = jnp.zeros((c, LANES), F32)
        for m in (lo_half, jnp.logical_not(lo_half)):
            a_m = jnp.where(m, a_t, 0.0)
            r_m = jnp.where(m, r_t, 0.0)
            v_m = jnp.where(m, vv, 0.0)
            a_ab = jnp.where(tril_strict, _mm(a_m, b_t, nt), 0.0)
            a_ak = jnp.where(tril_strict, _mm(a_m, k_t, nt), 0.0)
            a_rb = jnp.where(tril_incl, _mm(r_m, b_t, nt), 0.0)
            a_rk = jnp.where(tril_incl, _mm(r_m, k_t, nt), 0.0)
            t_inv = _unit_lower_inverse(a_ab, eye_c)
            a_hat_m = _mm(t_inv, a_m, exact=True)
            u_m = _mm(t_inv, _mm(a_ak, v_m), exact=True)
            a_hat = a_hat + a_hat_m
            u_loc = u_loc + u_m
            q_hat = q_hat + _mm(a_rb, a_hat_m)
            y_loc = y_loc + _mm(a_rb, u_m) + _mm(a_rk, v_m)

        trans = eye_l * jnp.exp(tot) + jnp.where(same_head, _mm(b_h, a_hat, tn), 0.0)
        inject = jnp.where(same_head, _mm(b_h, u_loc, tn) + _mm(k_h, vv, tn), 0.0)
        ys.append(_mm(q_hat, state) + y_loc)
        state = _mm(trans, state, exact=True) + inject
    st_ref[...] = state

    y = jnp.concatenate(ys, axis=0)
    inv_n = 1.0 / RW_HEAD_DIM
    mean = _seg_sum(y, seg) * inv_n
    d = y - mean
    var = _seg_sum(d * d, seg) * inv_n
    yn = d * lax.rsqrt(var + GN_EPS) * gnw_ref[...] + gnb_ref[...]
    y_ref[0] = ((yn + bonus) * g).astype(y_ref.dtype)


def _rwkv(c_rw, p, batch, seq):
    ts = RW_TILE
    n_pairs = RW_DIM // HEAD_PAIR
    lora_blk = (3 * RW_DIM) // (2 * LANES)
    halo = ts // 8

    def col(off):
        return pl.BlockSpec((1, ts, LANES), lambda b, pp, i, off=off: (b, i, off + pp))

    def col_halo(off):
        return pl.BlockSpec((1, 8, LANES),
                            lambda b, pp, i, off=off: (b, jnp.maximum(i * halo - 1, 0), off + pp))

    vec = pl.BlockSpec((1, LANES), lambda b, pp, i: (0, pp))
    lora_w = pl.BlockSpec((LANES, LANES), lambda b, pp, i: (0, pp))
    in_specs = [
        col(0), col(n_pairs), col(2 * n_pairs),
        pl.BlockSpec((1, ts, 2 * LANES), lambda b, pp, i: (b, i, lora_blk)),
        col_halo(0), col_halo(n_pairs), col_halo(2 * n_pairs),
        pl.BlockSpec((1, 8, 2 * LANES), lambda b, pp, i: (b, jnp.maximum(i * halo - 1, 0), lora_blk)),
        vec, vec, vec, pl.BlockSpec((1, 2 * LANES), lambda b, pp, i: (0, 0)),
        vec, vec, vec, vec, vec, vec, vec, lora_w, lora_w, lora_w,
    ]
    return pl.pallas_call(
        _rwkv_kernel,
        grid=(batch, n_pairs, seq // ts),
        in_specs=in_specs,
        out_specs=pl.BlockSpec((1, ts, LANES), lambda b, pp, i: (b, i, pp)),
        out_shape=jax.ShapeDtypeStruct((batch, seq, RW_DIM), BF16),
        scratch_shapes=[pltpu.VMEM((LANES, LANES), F32)],
        compiler_params=_cparams("parallel", "parallel", "arbitrary"),
        name="rwkv",
    )(c_rw, c_rw, c_rw, c_rw, c_rw, c_rw, c_rw, c_rw,
      p["mu_r"], p["mu_k"], p["mu_v"], p["mu_l"], p["w0"], p["a0"], p["k_k"], p["k_a"], p["r_k"],
      p["gn_w"], p["gn_b"], p["w_up"], p["a_up"], p["g_up"])


MLA_SLOT = 128


def _mla_prep_kernel(cq_ref, ckvr_ref, cos_ref, sin_ref, gq_ref, gkv_ref,
                     wqa_ref, wqb_ref, wk_ref, wv_ref, pa_ref, pb_ref,
                     q_ref, k_ref, v_ref):
    cos = cos_ref[...]
    sin = sin_ref[...]
    zq = _rms(cq_ref[...], gq_ref[...]).astype(BF16)
    qa = jnp.dot(zq, wqa_ref[...], preferred_element_type=F32)
    qb = jnp.dot(zq, wqb_ref[...], preferred_element_type=F32)
    ckvr = ckvr_ref[...]
    zkv = _rms(ckvr[:, :KV_LORA], gkv_ref[...]).astype(BF16)
    kn = jnp.dot(zkv, wk_ref[...], preferred_element_type=F32)
    v_ref[...] = jnp.dot(zkv, wv_ref[...], preferred_element_type=F32).astype(BF16)
    kr = ckvr[:, KV_LORA:].astype(BF16)
    k_rope = (jnp.dot(kr, pa_ref[...], preferred_element_type=F32) * cos
              + jnp.dot(kr, pb_ref[...], preferred_element_type=F32) * sin)
    scale = 1.0 / math.sqrt(QK_NOPE + QK_ROPE)
    for h in range(MLA_HEADS):
        sl = slice(h * MLA_SLOT, (h + 1) * MLA_SLOT)
        q_ref[:, sl] = ((qa[:, sl] * cos + qb[:, sl] * sin) * scale).astype(BF16)
        k_ref[:, sl] = (kn[:, sl] + k_rope).astype(BF16)


def _mla_prep(c_q, c_kvr, cos_t, sin_t, p, tm):
    n = c_q.shape[0]
    full = lambda w: pl.BlockSpec(w.shape, lambda i: (0, 0))
    row = lambda c: pl.BlockSpec((tm, c), lambda i: (i, 0))
    ws = [p["g_qa"], p["g_kva"], p["w_qa"], p["w_qb"], p["w_k"], p["w_v"], p["p_a"], p["p_b"]]
    hq = MLA_HEADS * MLA_SLOT
    return pl.pallas_call(
        _mla_prep_kernel,
        grid=(n // tm,),
        in_specs=[row(c_q.shape[1]), row(c_kvr.shape[1]), row(MLA_SLOT), row(MLA_SLOT)]
                 + [full(w) for w in ws],
        out_specs=[row(hq), row(hq), row(MLA_HEADS * V_HEAD)],
        out_shape=[jax.ShapeDtypeStruct((n, hq), BF16), jax.ShapeDtypeStruct((n, hq), BF16),
                   jax.ShapeDtypeStruct((n, MLA_HEADS * V_HEAD), BF16)],
        compiler_params=_cparams("parallel"),
        name="mla_prep",
    )(c_q, c_kvr, cos_t, sin_t, *ws)


ATT_TILE = 256


def _attn_kernel(q_ref, k_ref, v_ref, o_ref):
    qi = pl.program_id(2)
    t = ATT_TILE
    lo_half = lax.broadcasted_iota(jnp.int32, (t, 2 * V_HEAD), 1) < V_HEAD
    row = lax.broadcasted_iota(jnp.int32, (t, t), 0)
    col = lax.broadcasted_iota(jnp.int32, (t, t), 1)
    causal = col <= row
    nt = (((1,), (1,)), ((), ()))
    out = jnp.zeros((t, 2 * V_HEAD), F32)
    for h in range(2):
        q = q_ref[0, :, h * MLA_SLOT:(h + 1) * MLA_SLOT]

        def scores(j):
            kb = k_ref[0, pl.ds(pl.multiple_of(j * t, t), t), h * MLA_SLOT:(h + 1) * MLA_SLOT]
            return lax.dot_general(q, kb, nt, preferred_element_type=F32)

        def update(carry, s, j):
            m, l, acc = carry
            m_new = jnp.maximum(m, jnp.max(s, axis=-1, keepdims=True))
            alpha = jnp.exp(m - m_new)
            pr = jnp.exp(s - m_new)
            vb = v_ref[0, pl.ds(pl.multiple_of(j * t, t), t), :]
            acc = acc * alpha + jnp.dot(pr.astype(BF16), vb, preferred_element_type=F32)
            return m_new, l * alpha + jnp.sum(pr, axis=-1, keepdims=True), acc

        def body(j, carry):
            return update(carry, scores(j), j)

        init = (jnp.full((t, 1), -jnp.inf, F32), jnp.zeros((t, 1), F32),
                jnp.zeros((t, 2 * V_HEAD), F32))
        carry = lax.fori_loop(0, qi, body, init)
        s_diag = jnp.where(causal, scores(qi), -jnp.inf)
        m, l, acc = update(carry, s_diag, qi)
        out = jnp.where(lo_half if h == 0 else jnp.logical_not(lo_half), acc / l, out)
    o_ref[0] = out.astype(o_ref.dtype)


def _mla_attn(q, k, v, batch, seq):
    t = ATT_TILE
    return pl.pallas_call(
        _attn_kernel,
        grid=(batch, MLA_HEADS // 2, seq // t),
        in_specs=[pl.BlockSpec((1, t, 2 * MLA_SLOT), lambda b, hp, i: (b, i, hp)),
                  pl.BlockSpec((1, seq, 2 * MLA_SLOT), lambda b, hp, i: (b, 0, hp)),
                  pl.BlockSpec((1, seq, 2 * V_HEAD), lambda b, hp, i: (b, 0, hp))],
        out_specs=pl.BlockSpec((1, t, 2 * V_HEAD), lambda b, hp, i: (b, i, hp)),
        out_shape=jax.ShapeDtypeStruct((batch, seq, MLA_HEADS * V_HEAD), BF16),
        compiler_params=_cparams("parallel", "parallel", "arbitrary"),
        name="mla_attn",
    )(q, k, v)


ROUTE_W = 128


def _merge_kernel(x_ref, yrw_ref, ymla_ref, gate_ref, wbr_ref, wbm_ref, wo_ref, fg_ref,
                  wr_hi_ref, wr_lo_ref, br_ref, x1_ref, h2_ref, route_ref):
    d = x_ref.shape[1]
    a = jnp.dot(yrw_ref[...], wbr_ref[...], preferred_element_type=F32)
    b = jnp.dot(ymla_ref[...], wbm_ref[...], preferred_element_type=F32)
    merged = gate_ref[:, :d].astype(F32) * a + gate_ref[:, d:].astype(F32) * b
    x1 = x_ref[...] + jnp.dot(merged.astype(BF16), wo_ref[...], preferred_element_type=F32)
    x1_ref[...] = x1
    h2 = _rms(x1, fg_ref[...])
    h2_ref[...] = h2

    h_hi = h2.astype(BF16)
    h_lo = (h2 - h_hi.astype(F32)).astype(BF16)
    logits = (jnp.dot(h_hi, wr_hi_ref[...], preferred_element_type=F32)
              + jnp.dot(h_lo, wr_hi_ref[...], preferred_element_type=F32)
              + jnp.dot(h_hi, wr_lo_ref[...], preferred_element_type=F32)) + br_ref[...]

    lane = lax.broadcasted_iota(jnp.int32, logits.shape, 1)
    big = jnp.int32(ROUTE_W)
    neg = -jnp.inf

    def first_argmax(vals, vmax):
        return jnp.min(jnp.where(vals == vmax, lane, big), axis=-1, keepdims=True)

    grp = jnp.where(lane < N_GROUPS, logits, neg)
    g_max = jnp.max(grp, axis=-1, keepdims=True)
    g_den = jnp.sum(jnp.exp(grp - g_max), axis=-1, keepdims=True)
    g_sel = first_argmax(grp, g_max)
    gate_g = 1.0 / g_den
    lo = N_GROUPS + g_sel * EXPERTS_PER_GROUP
    fine = jnp.where((lane >= lo) & (lane < lo + EXPERTS_PER_GROUP), logits, neg)
    v1 = jnp.max(fine, axis=-1, keepdims=True)
    i1 = first_argmax(fine, v1)
    fine2 = jnp.where(lane == i1, neg, fine)
    v2 = jnp.max(fine2, axis=-1, keepdims=True)
    i2 = first_argmax(fine2, v2)
    e2 = jnp.exp(v2 - v1)
    den = 1.0 + e2
    w1 = gate_g / den
    w2 = gate_g * e2 / den
    route = jnp.where(lane == 0, (i1 - N_GROUPS).astype(F32),
                      jnp.where(lane == 1, (i2 - N_GROUPS).astype(F32),
                                jnp.where(lane == 2, w1, jnp.where(lane == 3, w2, 0.0))))
    route_ref[...] = route


def _merge(x2, y_rw, y_mla, gates, p, tm):
    n, d = x2.shape
    full = lambda w: pl.BlockSpec(w.shape, lambda i: (0, 0))
    row = lambda c: pl.BlockSpec((tm, c), lambda i: (i, 0))
    ws = [p["w_br"], p["w_bm"], p["w_out"], p["ffn_g"], p["wr_hi"], p["wr_lo"], p["b_route"]]
    return pl.pallas_call(
        _merge_kernel,
        grid=(n // tm,),
        in_specs=[row(d), row(y_rw.shape[1]), row(y_mla.shape[1]), row(2 * d)] + [full(w) for w in ws],
        out_specs=[row(d), row(d), row(ROUTE_W)],
        out_shape=[jax.ShapeDtypeStruct((n, d), F32), jax.ShapeDtypeStruct((n, d), F32),
                   jax.ShapeDtypeStruct((n, ROUTE_W), F32)],
        compiler_params=_cparams("parallel"),
        name="merge_route",
    )(x2, y_rw, y_mla, gates, *ws)


DISPATCH_TILE = 256


def _dispatch_kernel(dest_ref, h_ref, zero_ref, xs_ref, sem):
    del zero_ref
    tm = h_ref.shape[0]

    def copy(t, k):
        return pltpu.make_async_copy(h_ref.at[pl.ds(t, 1), :],
                                     xs_ref.at[pl.ds(dest_ref[0, 0, TOP_K * t + k], 1), :], sem)

    def start(t, _):
        for k in range(TOP_K):
            copy(t, k).start()
        return 0

    def wait(t, _):
        for k in range(TOP_K):
            copy(t, k).wait()
        return 0

    lax.fori_loop(0, tm, start, 0)
    lax.fori_loop(0, tm, wait, 0)


def _dispatch(h2, dest, xs_init):
    n, d = h2.shape
    tm = DISPATCH_TILE
    dest3 = dest.reshape(n // tm, 1, TOP_K * tm)
    return pl.pallas_call(
        _dispatch_kernel,
        grid=(n // tm,),
        in_specs=[pl.BlockSpec((1, 1, TOP_K * tm), lambda i: (i, 0, 0), memory_space=pltpu.SMEM),
                  pl.BlockSpec((tm, d), lambda i: (i, 0)),
                  pl.BlockSpec(memory_space=pl.ANY)],
        out_specs=pl.BlockSpec(memory_space=pl.ANY),
        out_shape=jax.ShapeDtypeStruct(xs_init.shape, xs_init.dtype),
        scratch_shapes=[pltpu.SemaphoreType.DMA(())],
        input_output_aliases={2: 0},
        compiler_params=_cparams("arbitrary"),
        name="dispatch",
    )(dest3, h2, xs_init)


def _expert_kernel(blk_e_ref, n_used_ref, x_ref, wgu_ref, wd_ref, y_ref):
    del blk_e_ref

    @pl.when(pl.program_id(0) < n_used_ref[0])
    def _():
        h = jnp.dot(x_ref[...].astype(BF16), wgu_ref[0], preferred_element_type=F32)
        gt = h[:, :D_EXPERT]
        up = h[:, D_EXPERT:]
        act = (gt * jax.nn.sigmoid(gt) * up).astype(BF16)
        y_ref[...] = jnp.dot(act, wd_ref[0], preferred_element_type=F32)

    @pl.when(pl.program_id(0) >= n_used_ref[0])
    def _():
        y_ref[...] = jnp.zeros_like(y_ref)


def _experts(xs, blk_expert, n_used, w_gu, w_down):
    p_rows, d = xs.shape
    n_blocks = p_rows // EXPERT_BLOCK
    grid_spec = pltpu.PrefetchScalarGridSpec(
        num_scalar_prefetch=2,
        grid=(n_blocks,),
        in_specs=[pl.BlockSpec((EXPERT_BLOCK, d), lambda i, be, nu: (i, 0)),
                  pl.BlockSpec((1, d, 2 * D_EXPERT), lambda i, be, nu: (be[i], 0, 0)),
                  pl.BlockSpec((1, D_EXPERT, d), lambda i, be, nu: (be[i], 0, 0))],
        out_specs=pl.BlockSpec((EXPERT_BLOCK, d), lambda i, be, nu: (i, 0)),
    )
    return pl.pallas_call(
        _expert_kernel,
        grid_spec=grid_spec,
        out_shape=jax.ShapeDtypeStruct((p_rows, d), F32),
        compiler_params=_cparams("arbitrary"),
        name="experts",
    )(blk_expert, n_used, xs, w_gu, w_down)


COMBINE_TILE = 256


def _combine_kernel(dest_ref, x1_ref, route_ref, g_ref, yb_ref, o_ref, buf, sem, *, final_norm):
    tm = x1_ref.shape[0]

    def copy(t, k):
        return pltpu.make_async_copy(yb_ref.at[pl.ds(dest_ref[0, 0, TOP_K * t + k], 1), :],
                                     buf.at[k, pl.ds(t, 1), :], sem)

    def start(t, _):
        for k in range(TOP_K):
            copy(t, k).start()
        return 0

    def wait(t, _):
        for k in range(TOP_K):
            copy(t, k).wait()
        return 0

    lax.fori_loop(0, tm, start, 0)
    lax.fori_loop(0, tm, wait, 0)
    route = route_ref[...]
    x2 = x1_ref[...] + route[:, 2:3] * buf[0] + route[:, 3:4] * buf[1]
    o_ref[...] = _rms(x2, g_ref[...]) if final_norm else x2


def _combine(x1, route, dest, yb, final_g, final_norm):
    n, d = x1.shape
    tm = COMBINE_TILE
    dest3 = dest.reshape(n // tm, 1, TOP_K * tm)
    return pl.pallas_call(
        functools.partial(_combine_kernel, final_norm=final_norm),
        grid=(n // tm,),
        in_specs=[pl.BlockSpec((1, 1, TOP_K * tm), lambda i: (i, 0, 0), memory_space=pltpu.SMEM),
                  pl.BlockSpec((tm, d), lambda i: (i, 0)),
                  pl.BlockSpec((tm, ROUTE_W), lambda i: (i, 0)),
                  pl.BlockSpec((1, d), lambda i: (0, 0)),
                  pl.BlockSpec(memory_space=pl.ANY)],
        out_specs=pl.BlockSpec((tm, d), lambda i: (i, 0)),
        out_shape=jax.ShapeDtypeStruct((n, d), F32),
        scratch_shapes=[pltpu.VMEM((TOP_K, tm, d), F32), pltpu.SemaphoreType.DMA(())],
        compiler_params=_cparams("arbitrary"),
        name="combine",
    )(dest3, x1, route, final_g, yb)


def _rwkv_params(rw_mu, rw_w0, rw_w_up, rw_a0, rw_a_up, rw_g_up, rw_k_k, rw_k_a, rw_r_k, rw_gn_w, rw_gn_b):
    row = lambda v: v.reshape(1, -1).astype(F32)
    zeros = jnp.zeros((A_LORA, RW_DIM), F32)
    return {
        "mu_r": row(rw_mu[:RW_DIM]), "mu_k": row(rw_mu[RW_DIM:2 * RW_DIM]),
        "mu_v": row(rw_mu[2 * RW_DIM:3 * RW_DIM]), "mu_l": row(rw_mu[3 * RW_DIM:]),
        "w0": row(rw_w0), "a0": row(rw_a0), "k_k": row(rw_k_k), "k_a": row(rw_k_a),
        "r_k": row(rw_r_k), "gn_w": row(rw_gn_w), "gn_b": row(rw_gn_b),
        "w_up": jnp.concatenate([rw_w_up, zeros], axis=0).astype(F32),
        "a_up": jnp.concatenate([zeros, rw_a_up], axis=0).astype(BF16),
        "g_up": rw_g_up.astype(BF16),
    }


def _mla_params(g_qa, w_q_up, g_kva, w_kv_up):
    half = QK_ROPE // 2
    pad = MLA_SLOT - QK_NOPE - QK_ROPE
    wq = w_q_up.reshape(Q_LORA, MLA_HEADS, QK_NOPE + QK_ROPE)
    q_nope, q_r1, q_r2 = wq[..., :QK_NOPE], wq[..., QK_NOPE:QK_NOPE + half], wq[..., QK_NOPE + half:]
    zq = lambda w: jnp.zeros((Q_LORA, MLA_HEADS, w), F32)
    w_qa = jnp.concatenate([q_nope, q_r1, q_r2, zq(pad)], axis=-1).reshape(Q_LORA, -1)
    w_qb = jnp.concatenate([zq(QK_NOPE), -q_r2, q_r1, zq(pad)], axis=-1).reshape(Q_LORA, -1)
    wkv = w_kv_up.reshape(KV_LORA, MLA_HEADS, QK_NOPE + V_HEAD)
    w_k = jnp.concatenate([wkv[..., :QK_NOPE], jnp.zeros((KV_LORA, MLA_HEADS, MLA_SLOT - QK_NOPE), F32)],
                          axis=-1).reshape(KV_LORA, -1)
    w_v = wkv[..., QK_NOPE:].reshape(KV_LORA, -1)
    eye = jnp.eye(half, dtype=F32)
    z = jnp.zeros((half, half), F32)
    zl = jnp.zeros((QK_ROPE, QK_NOPE), F32)
    zr = jnp.zeros((QK_ROPE, pad), F32)
    p_a = jnp.concatenate([zl, jnp.concatenate([eye, z], 0), jnp.concatenate([z, eye], 0), zr], axis=1)
    p_b = jnp.concatenate([zl, jnp.concatenate([z, -eye], 0), jnp.concatenate([eye, z], 0), zr], axis=1)
    return {"g_qa": g_qa.reshape(1, -1), "g_kva": g_kva.reshape(1, -1),
            "w_qa": w_qa.astype(BF16), "w_qb": w_qb.astype(BF16), "w_k": w_k.astype(BF16),
            "w_v": w_v.astype(BF16), "p_a": p_a.astype(BF16), "p_b": p_b.astype(BF16)}


def _rope_tables(positions):
    half = QK_ROPE // 2
    inv_freq = ROPE_THETA ** (-jnp.arange(0, QK_ROPE, 2, dtype=F32) / QK_ROPE)
    ang = positions.astype(F32).reshape(-1, 1) * inv_freq
    cos, sin = jnp.cos(ang), jnp.sin(ang)
    n = ang.shape[0]
    pad = MLA_SLOT - QK_NOPE - QK_ROPE
    cos_t = jnp.concatenate([jnp.ones((n, QK_NOPE), F32), cos, cos, jnp.zeros((n, pad), F32)], axis=1)
    sin_t = jnp.concatenate([jnp.zeros((n, QK_NOPE), F32), sin, sin, jnp.zeros((n, pad), F32)], axis=1)
    del half
    return cos_t, sin_t


def _route_plan(route, n):
    e_flat = route[:, :TOP_K].astype(jnp.int32).reshape(-1)
    a = e_flat.shape[0]
    onehot = (e_flat[:, None] == jnp.arange(N_EXPERTS, dtype=jnp.int32)[None, :]).astype(jnp.int32)
    before = jnp.cumsum(onehot, axis=0) - onehot
    rank = jnp.sum(before * onehot, axis=1)
    counts = jnp.sum(onehot, axis=0)
    padded = (counts + EXPERT_BLOCK - 1) // EXPERT_BLOCK * EXPERT_BLOCK
    pad_end = jnp.cumsum(padded)
    pad_start = pad_end - padded
    dest = (pad_start[e_flat] + rank).astype(jnp.int32)
    n_blocks = -(-a // EXPERT_BLOCK) + N_EXPERTS
    blk_expert = jnp.minimum(
        jnp.searchsorted(pad_end, jnp.arange(n_blocks, dtype=jnp.int32) * EXPERT_BLOCK, side="right"),
        N_EXPERTS - 1).astype(jnp.int32)
    n_used = (pad_end[-1] // EXPERT_BLOCK).astype(jnp.int32).reshape(1)
    return dest, blk_expert, n_used, n_blocks


def kernel(x, positions, mix_norm_g, w_in, rw_mu, rw_w0, rw_w_up, rw_a0, rw_a_up, rw_g_up, rw_k_k, rw_k_a, rw_r_k, rw_gn_w, rw_gn_b, mla_g_qa, mla_w_q_up, mla_g_kva, mla_w_kv_up, w_branch_rw, w_branch_mla, w_out, ffn_norm_g, moe_w_group, moe_b_group, moe_w_router, moe_b_router, moe_w_gu, moe_w_down, final_norm_g):
    batch, seq, d = x.shape
    n = batch * seq
    depth = w_in.shape[0]
    rw_cols = 3 * RW_DIM + W_LORA + A_LORA + G_LORA
    mla_cols = Q_LORA + KV_LORA + QK_ROPE
    cos_t, sin_t = _rope_tables(positions)
    x2 = x.reshape(n, d)

    for l in range(depth):
        wl = w_in[l].astype(BF16)
        c_rw, c_q, c_kvr, gates = _in_proj(
            x2, mix_norm_g[l].reshape(1, d), wl[:, :rw_cols], wl[:, rw_cols:rw_cols + Q_LORA],
            wl[:, rw_cols + Q_LORA:rw_cols + mla_cols], wl[:, rw_cols + mla_cols:], tm=256)

        rp = _rwkv_params(rw_mu[l], rw_w0[l], rw_w_up[l], rw_a0[l], rw_a_up[l], rw_g_up[l], rw_k_k[l],
                          rw_k_a[l], rw_r_k[l], rw_gn_w[l], rw_gn_b[l])
        y_rw = _rwkv(c_rw.reshape(batch, seq, rw_cols), rp, batch, seq).reshape(n, RW_DIM)

        mp = _mla_params(mla_g_qa[l], mla_w_q_up[l], mla_g_kva[l], mla_w_kv_up[l])
        q, k, v = _mla_prep(c_q, c_kvr, cos_t, sin_t, mp, tm=256)
        y_mla = _mla_attn(q.reshape(batch, seq, -1), k.reshape(batch, seq, -1),
                          v.reshape(batch, seq, -1), batch, seq).reshape(n, MLA_HEADS * V_HEAD)

        w_route = jnp.concatenate(
            [moe_w_group[l], moe_w_router[l], jnp.zeros((d, ROUTE_W - N_GROUPS - N_EXPERTS), F32)], axis=1)
        b_route = jnp.concatenate(
            [moe_b_group[l], moe_b_router[l], jnp.zeros((ROUTE_W - N_GROUPS - N_EXPERTS,), F32)]).reshape(1, -1)
        wr_hi = w_route.astype(BF16)
        wr_lo = (w_route - wr_hi.astype(F32)).astype(BF16)
        mparams = {"w_br": w_branch_rw[l].astype(BF16), "w_bm": w_branch_mla[l].astype(BF16),
                   "w_out": w_out[l].astype(BF16), "ffn_g": ffn_norm_g[l].reshape(1, d),
                   "wr_hi": wr_hi, "wr_lo": wr_lo, "b_route": b_route}
        x1, h2, route = _merge(x2, y_rw, y_mla, gates, mparams, tm=256)

        dest, blk_expert, n_used, n_blocks = _route_plan(route, n)
        xs = _dispatch(h2, dest, jnp.zeros((n_blocks * EXPERT_BLOCK, d), F32))
        yb = _experts(xs, blk_expert, n_used, moe_w_gu[l].astype(BF16), moe_w_down[l].astype(BF16))
        x2 = _combine(x1, route, dest, yb, final_norm_g.reshape(1, d), final_norm=(l == depth - 1))

    return x2.reshape(batch, seq, d)
```

```python
import functools
import math

import jax
import jax.numpy as jnp
from jax import lax
from jax.experimental import pallas as pl
from jax.experimental.pallas import tpu as pltpu

F32 = jnp.float32
BF16 = jnp.bfloat16

RW_HEADS = 8
RW_HEAD_DIM = 64
RW_DIM = RW_HEADS * RW_HEAD_DIM
W_LORA = 64
A_LORA = 64
G_LORA = 128
GN_EPS = 64e-5
MLA_HEADS = 8
QK_NOPE = 64
QK_ROPE = 32
V_HEAD = 64
Q_LORA = 384
KV_LORA = 256
ROPE_THETA = 10000.0
N_GROUPS = 4
EXPERTS_PER_GROUP = 8
N_EXPERTS = N_GROUPS * EXPERTS_PER_GROUP
TOP_K = 2
D_EXPERT = 256
EXPERT_BLOCK = 256
NORM_EPS = 1e-6

LANES = 128
HEAD_PAIR = 2 * RW_HEAD_DIM
VMEM_LIMIT = 48 * 1024 * 1024


def _cparams(*sem):
    return pltpu.CompilerParams(dimension_semantics=sem, vmem_limit_bytes=VMEM_LIMIT)


def _mm(a, b, dims=((1,), (0,)), split=False):
    dn = (dims, ((), ()))
    dot = lambda x, y: lax.dot_general(x, y, dn, preferred_element_type=F32)
    a_hi = a.astype(BF16)
    b_hi = b.astype(BF16)
    if not split:
        return dot(a_hi, b_hi)
    a_lo = (a - a_hi.astype(F32)).astype(BF16)
    b_lo = (b - b_hi.astype(F32)).astype(BF16)
    return dot(a_hi, b_hi) + dot(a_lo, b_hi) + dot(a_hi, b_lo)


def _mm_sel(sel_bf16, x, dims=((1,), (0,))):
    dn = (dims, ((), ()))
    hi = x.astype(BF16)
    r1 = x - hi.astype(F32)
    mid = r1.astype(BF16)
    lo = (r1 - mid.astype(F32)).astype(BF16)
    out = lax.dot_general(sel_bf16, hi, dn, preferred_element_type=F32)
    out = out + lax.dot_general(sel_bf16, mid, dn, preferred_element_type=F32)
    return out + lax.dot_general(sel_bf16, lo, dn, preferred_element_type=F32)


def _seg_sum(x, seg_bf16):
    hi = x.astype(BF16)
    lo = (x - hi.astype(F32)).astype(BF16)
    return (jnp.dot(hi, seg_bf16, preferred_element_type=F32)
            + jnp.dot(lo, seg_bf16, preferred_element_type=F32))


def _rms(x, g):
    return x * lax.rsqrt(jnp.mean(x * x, axis=-1, keepdims=True) + NORM_EPS) * g


def _in_proj_kernel(x_ref, g_ref, wrw_ref, wq_ref, wkvr_ref, wg_ref,
                    crw_ref, cq_ref, ckvr_ref, gate_ref):
    hb = _rms(x_ref[...], g_ref[...]).astype(BF16)
    crw_ref[...] = jnp.dot(hb, wrw_ref[...], preferred_element_type=F32)
    cq_ref[...] = jnp.dot(hb, wq_ref[...], preferred_element_type=F32)
    ckvr_ref[...] = jnp.dot(hb, wkvr_ref[...], preferred_element_type=F32)
    gate_ref[...] = jax.nn.sigmoid(jnp.dot(hb, wg_ref[...], preferred_element_type=F32)).astype(BF16)


def _in_proj(x2, g, w_rw, w_q, w_kvr, w_gate, tm):
    n, d = x2.shape
    full = lambda w: pl.BlockSpec(w.shape, lambda i: (0, 0))
    row = lambda c: pl.BlockSpec((tm, c), lambda i: (i, 0))
    return pl.pallas_call(
        _in_proj_kernel,
        grid=(n // tm,),
        in_specs=[row(d), full(g), full(w_rw), full(w_q), full(w_kvr), full(w_gate)],
        out_specs=[row(w_rw.shape[1]), row(w_q.shape[1]), row(w_kvr.shape[1]), row(w_gate.shape[1])],
        out_shape=[jax.ShapeDtypeStruct((n, w_rw.shape[1]), F32),
                   jax.ShapeDtypeStruct((n, w_q.shape[1]), F32),
                   jax.ShapeDtypeStruct((n, w_kvr.shape[1]), F32),
                   jax.ShapeDtypeStruct((n, w_gate.shape[1]), BF16)],
        compiler_params=_cparams("parallel"),
        name="in_proj",
    )(x2, g, w_rw, w_q, w_kvr, w_gate)


RW_CHUNK = 64
RW_TILE = 256


def _token_shift(cur, halo_ref, first):
    prev_row = jnp.where(first, 0.0, halo_ref[0, 7:8, :])
    rolled = pltpu.roll(cur, 1, 0)
    row = lax.broadcasted_iota(jnp.int32, cur.shape, 0)
    return jnp.where(row == 0, prev_row, rolled)


def _rwkv_kernel(r_ref, k_ref, v_ref, l_ref, hr_ref, hk_ref, hv_ref, hl_ref,
                 mur_ref, muk_ref, muv_ref, mul_ref, w0_ref, a0_ref, kk_ref, ka_ref, rk_ref,
                 gnw_ref, gnb_ref, wup_ref, aup_ref, gup_ref, y_ref, st_ref):
    i = pl.program_id(2)
    first = i == 0

    @pl.when(first)
    def _():
        st_ref[...] = jnp.zeros_like(st_ref)

    def mixed(c_ref, h_ref, mu_ref):
        cur = c_ref[0]
        return cur + (_token_shift(cur, h_ref, first) - cur) * mu_ref[...]

    zr = mixed(r_ref, hr_ref, mur_ref)
    zk = mixed(k_ref, hk_ref, muk_ref)
    zv = mixed(v_ref, hv_ref, muv_ref)
    zl = mixed(l_ref, hl_ref, mul_ref)
    z_wa = zl[:, :LANES]
    z_g = zl[:, LANES:]

    lane = lax.broadcasted_iota(jnp.int32, (LANES, LANES), 1)
    sub = lax.broadcasted_iota(jnp.int32, (LANES, LANES), 0)
    same_head = (lane // RW_HEAD_DIM) == (sub // RW_HEAD_DIM)
    seg = jnp.where(same_head, 1.0, 0.0).astype(BF16)

    w = w0_ref[...] + _mm(jnp.tanh(z_wa), wup_ref[...], split=True)
    u = -w
    softplus = jnp.maximum(u, 0.0) + jnp.log(1.0 + jnp.exp(-jnp.abs(u)))
    log_decay = -jnp.exp(-softplus - 0.5)
    a = jax.nn.sigmoid(a0_ref[...] + _mm(z_wa, aup_ref[...]))
    g = _mm(jax.nn.sigmoid(z_g), gup_ref[...])

    kk = zk * kk_ref[...]
    kk = kk / jnp.maximum(jnp.sqrt(_seg_sum(kk * kk, seg)), 1e-12)
    k2 = zk * (1.0 + (a - 1.0) * ka_ref[...])
    bonus = _seg_sum(zr * k2 * rk_ref[...], seg) * zv
    kka = kk * a

    c = RW_CHUNK
    crow = lax.broadcasted_iota(jnp.int32, (c, c), 0)
    ccol = lax.broadcasted_iota(jnp.int32, (c, c), 1)
    cum_sel = jnp.where(crow >= ccol, 1.0, 0.0).astype(BF16)
    tril_incl = sub >= lane
    tril_strict = sub > lane
    eye_l = jnp.where(lane == sub, 1.0, 0.0).astype(F32)
    lo_half = lax.broadcasted_iota(jnp.int32, (c, LANES), 1) < RW_HEAD_DIM
    nt = ((1,), (1,))
    tn = ((0,), (0,))
    zeros_blk = jnp.zeros((2 * c, LANES), F32)

    def stack(t):
        return jnp.concatenate([jnp.where(lo_half, t, 0.0), jnp.where(lo_half, 0.0, t)], axis=0)

    chunks = range(RW_TILE // c)
    x_a, x_b, x_k, x_r, x_v, x_bh, x_kh, w_tot = [], [], [], [], [], [], [], []
    for ci in chunks:
        sl = slice(ci * c, (ci + 1) * c)
        ld = log_decay[sl]
        cum = _mm_sel(cum_sel, ld)
        tot = cum[c - 1:c, :]
        e_neg = jnp.exp(-cum)
        e_rest = jnp.exp(tot - cum)
        x_a.append(stack(-kk[sl] * jnp.exp(cum - ld)))
        x_b.append(stack(kka[sl] * e_neg))
        x_k.append(stack(k2[sl] * e_neg))
        x_r.append(stack(zr[sl] * jnp.exp(cum)))
        x_v.append(stack(zv[sl]))
        x_bh.append(stack(kka[sl] * e_rest))
        x_kh.append(stack(k2[sl] * e_rest))
        w_tot.append(jnp.exp(tot))

    inter = [_mm(jnp.concatenate([x_a[i], x_r[i]], axis=0), jnp.concatenate([x_b[i], x_k[i]], axis=0), nt)
             for i in chunks]
    tril_incl2 = jnp.concatenate([tril_incl, tril_incl], axis=1)
    a_ab = [jnp.where(tril_strict, m[:2 * c, :2 * c], 0.0) for m in inter]
    a_ak = [jnp.where(tril_strict, m[:2 * c, 2 * c:], 0.0) for m in inter]
    a_r = [jnp.where(tril_incl2, m[2 * c:], 0.0) for m in inter]
    w_ak = [_mm(a_ak[i], x_v[i]) for i in chunks]

    t_inv = [eye_l + m for m in a_ab]
    pw = a_ab
    for _ in range(int(math.log2(c)) - 1):
        pw = [_mm(m, m) for m in pw]
        t_inv = [t_inv[i] + _mm(t_inv[i], pw[i]) for i in chunks]

    solved = [_mm(t_inv[i], jnp.concatenate([x_a[i], w_ak[i]], axis=1)) for i in chunks]
    rhs = [jnp.concatenate([solved[i], jnp.concatenate([zeros_blk, x_v[i]], axis=1)], axis=0)
           for i in chunks]
    out = [_mm(a_r[i], rhs[i]) for i in chunks]
    carry = [_mm(jnp.concatenate([x_bh[i], x_kh[i]], axis=0), rhs[i], tn) for i in chunks]
    q_hat, y_loc = [], []
    for i in chunks:
        q_st = x_r[i] + out[i][:, :LANES]
        q_hat.append(q_st[:c] + q_st[c:])
        y_loc.append(out[i][:c, LANES:] + out[i][c:, LANES:])
    trans = [jnp.concatenate([eye_l * w_tot[i] + carry[i][:, :LANES], carry[i][:, LANES:]], axis=1)
             for i in chunks]

    zeros_sq = jnp.zeros((LANES, LANES), F32)
    prefix = [None, trans[0]]
    for i in chunks[1:]:
        nxt = _mm(trans[i][:, :LANES], prefix[i], split=True)
        prefix.append(nxt + jnp.concatenate([zeros_sq, trans[i][:, LANES:]], axis=1))
    q_pre = [None] + [_mm(q_hat[i], prefix[i]) for i in chunks[1:]]
    q_m = jnp.concatenate([q_hat[0]] + [q_pre[i][:, :LANES] for i in chunks[1:]], axis=0)
    y_off = jnp.concatenate([y_loc[0]] + [y_loc[i] + q_pre[i][:, LANES:] for i in chunks[1:]], axis=0)
    state = st_ref[...]
    y = _mm(q_m, state) + y_off
    st_ref[...] = _mm(prefix[-1][:, :LANES], state, split=True) + prefix[-1][:, LANES:]

    inv_n = 1.0 / RW_HEAD_DIM
    mean = _seg_sum(y, seg) * inv_n
    d = y - mean
    var = _seg_sum(d * d, seg) * inv_n
    yn = d * lax.rsqrt(var + GN_EPS) * gnw_ref[...] + gnb_ref[...]
    y_ref[0] = ((yn + bonus) * g).astype(y_ref.dtype)


def _rwkv(c_rw, p, batch, seq):
    ts = RW_TILE
    n_pairs = RW_DIM // HEAD_PAIR
    lora_blk = (3 * RW_DIM) // (2 * LANES)
    halo = ts // 8

    def col(off):
        return pl.BlockSpec((1, ts, LANES), lambda b, pp, i, off=off: (b, i, off + pp))

    def col_halo(off):
        return pl.BlockSpec((1, 8, LANES),
                            lambda b, pp, i, off=off: (b, jnp.maximum(i * halo - 1, 0), off + pp))

    vec = pl.BlockSpec((1, LANES), lambda b, pp, i: (0, pp))
    lora_w = pl.BlockSpec((LANES, LANES), lambda b, pp, i: (0, pp))
    in_specs = [
        col(0), col(n_pairs), col(2 * n_pairs),
        pl.BlockSpec((1, ts, 2 * LANES), lambda b, pp, i: (b, i, lora_blk)),
        col_halo(0), col_halo(n_pairs), col_halo(2 * n_pairs),
        pl.BlockSpec((1, 8, 2 * LANES), lambda b, pp, i: (b, jnp.maximum(i * halo - 1, 0), lora_blk)),
        vec, vec, vec, pl.BlockSpec((1, 2 * LANES), lambda b, pp, i: (0, 0)),
        vec, vec, vec, vec, vec, vec, vec, lora_w, lora_w, lora_w,
    ]
    return pl.pallas_call(
        _rwkv_kernel,
        grid=(batch, n_pairs, seq // ts),
        in_specs=in_specs,
        out_specs=pl.BlockSpec((1, ts, LANES), lambda b, pp, i: (b, i, pp)),
        out_shape=jax.ShapeDtypeStruct((batch, seq, RW_DIM), BF16),
        scratch_shapes=[pltpu.VMEM((LANES, LANES), F32)],
        compiler_params=_cparams("parallel", "parallel", "arbitrary"),
        name="rwkv",
    )(c_rw, c_rw, c_rw, c_rw, c_rw, c_rw, c_rw, c_rw,
      p["mu_r"], p["mu_k"], p["mu_v"], p["mu_l"], p["w0"], p["a0"], p["k_k"], p["k_a"], p["r_k"],
      p["gn_w"], p["gn_b"], p["w_up"], p["a_up"], p["g_up"])


MLA_SLOT = 128


def _mla_prep_kernel(cq_ref, ckvr_ref, cos_ref, sin_ref, gq_ref, gkv_ref,
                     wqa_ref, wqb_ref, wk_ref, wv_ref, pa_ref, pb_ref,
                     q_ref, k_ref, v_ref):
    cos = cos_ref[...]
    sin = sin_ref[...]
    zq = _rms(cq_ref[...], gq_ref[...]).astype(BF16)
    qa = jnp.dot(zq, wqa_ref[...], preferred_element_type=F32)
    qb = jnp.dot(zq, wqb_ref[...], preferred_element_type=F32)
    ckvr = ckvr_ref[...]
    zkv = _rms(ckvr[:, :KV_LORA], gkv_ref[...]).astype(BF16)
    kn = jnp.dot(zkv, wk_ref[...], preferred_element_type=F32)
    v_ref[...] = jnp.dot(zkv, wv_ref[...], preferred_element_type=F32).astype(BF16)
    kr = ckvr[:, KV_LORA:].astype(BF16)
    k_rope = (jnp.dot(kr, pa_ref[...], preferred_element_type=F32) * cos
              + jnp.dot(kr, pb_ref[...], preferred_element_type=F32) * sin)
    scale = math.log2(math.e) / math.sqrt(QK_NOPE + QK_ROPE)
    for h in range(MLA_HEADS):
        sl = slice(h * MLA_SLOT, (h + 1) * MLA_SLOT)
        q_ref[:, sl] = ((qa[:, sl] * cos + qb[:, sl] * sin) * scale).astype(BF16)
        k_ref[:, sl] = (kn[:, sl] + k_rope).astype(BF16)


def _mla_prep(c_q, c_kvr, cos_t, sin_t, p, tm):
    n = c_q.shape[0]
    full = lambda w: pl.BlockSpec(w.shape, lambda i: (0, 0))
    row = lambda c: pl.BlockSpec((tm, c), lambda i: (i, 0))
    ws = [p["g_qa"], p["g_kva"], p["w_qa"], p["w_qb"], p["w_k"], p["w_v"], p["p_a"], p["p_b"]]
    hq = MLA_HEADS * MLA_SLOT
    return pl.pallas_call(
        _mla_prep_kernel,
        grid=(n // tm,),
        in_specs=[row(c_q.shape[1]), row(c_kvr.shape[1]), row(MLA_SLOT), row(MLA_SLOT)]
                 + [full(w) for w in ws],
        out_specs=[row(hq), row(hq), row(MLA_HEADS * V_HEAD)],
        out_shape=[jax.ShapeDtypeStruct((n, hq), BF16), jax.ShapeDtypeStruct((n, hq), BF16),
                   jax.ShapeDtypeStruct((n, MLA_HEADS * V_HEAD), BF16)],
        compiler_params=_cparams("parallel"),
        name="mla_prep",
    )(c_q, c_kvr, cos_t, sin_t, *ws)


ATT_TILE = 512


def _attn_kernel(q_ref, k_ref, v_ref, o_ref):
    qi = pl.program_id(2)
    t = ATT_TILE
    lo_v = lax.broadcasted_iota(jnp.int32, (t, 2 * V_HEAD), 1) < V_HEAD
    row = lax.broadcasted_iota(jnp.int32, (t, t), 0)
    col = lax.broadcasted_iota(jnp.int32, (t, t), 1)
    causal = col <= row
    nt = (((1,), (1,)), ((), ()))
    qs = [q_ref[0, :, h * MLA_SLOT:(h + 1) * MLA_SLOT] for h in range(2)]
    zero_v = jnp.zeros((t, 2 * V_HEAD), BF16)

    def block(j, carry, masked):
        stats, acc = carry
        rows = pl.ds(pl.multiple_of(j * t, t), t)
        vb = v_ref[0, rows, :]
        v_heads = (jnp.where(lo_v, vb, zero_v), jnp.where(lo_v, zero_v, vb))
        new_stats, alphas = [], []
        pv = None
        for h in range(2):
            m, l = stats[h]
            kb = k_ref[0, rows, h * MLA_SLOT:(h + 1) * MLA_SLOT]
            s = lax.dot_general(qs[h], kb, nt, preferred_element_type=F32)
            if masked:
                s = jnp.where(causal, s, -jnp.inf)
            m_new = jnp.maximum(m, jnp.max(s, axis=-1, keepdims=True))
            alpha = jnp.exp2(m - m_new)
            pr = jnp.exp2(s - m_new)
            new_stats.append((m_new, l * alpha + jnp.sum(pr, axis=-1, keepdims=True)))
            alphas.append(alpha)
            d = jnp.dot(pr.astype(BF16), v_heads[h], preferred_element_type=F32)
            pv = d if pv is None else pv + d
        acc = acc * jnp.where(lo_v, alphas[0], alphas[1]) + pv
        return tuple(new_stats), acc

    stat0 = (jnp.full((t, 1), -jnp.inf, F32), jnp.zeros((t, 1), F32))
    init = ((stat0, stat0), jnp.zeros((t, 2 * V_HEAD), F32))
    carry = lax.fori_loop(0, qi, lambda j, cr: block(j, cr, False), init)
    stats, acc = block(qi, carry, True)
    o_ref[0] = (acc / jnp.where(lo_v, stats[0][1], stats[1][1])).astype(o_ref.dtype)


def _mla_attn(q, k, v, batch, seq):
    t = ATT_TILE
    return pl.pallas_call(
        _attn_kernel,
        grid=(batch, MLA_HEADS // 2, seq // t),
        in_specs=[pl.BlockSpec((1, t, 2 * MLA_SLOT), lambda b, hp, i: (b, i, hp)),
                  pl.BlockSpec((1, seq, 2 * MLA_SLOT), lambda b, hp, i: (b, 0, hp)),
                  pl.BlockSpec((1, seq, 2 * V_HEAD), lambda b, hp, i: (b, 0, hp))],
        out_specs=pl.BlockSpec((1, t, 2 * V_HEAD), lambda b, hp, i: (b, i, hp)),
        out_shape=jax.ShapeDtypeStruct((batch, seq, MLA_HEADS * V_HEAD), BF16),
        compiler_params=_cparams("parallel", "parallel", "arbitrary"),
        name="mla_attn",
    )(q, k, v)


ROUTE_W = 128


def _merge_kernel(x_ref, yrw_ref, ymla_ref, gate_ref, wbr_ref, wbm_ref, wo_ref, fg_ref,
                  wr_hi_ref, wr_lo_ref, br_ref, x1_ref, h2_ref, route_ref):
    d = x_ref.shape[1]
    a = jnp.dot(yrw_ref[...], wbr_ref[...], preferred_element_type=F32)
    b = jnp.dot(ymla_ref[...], wbm_ref[...], preferred_element_type=F32)
    merged = gate_ref[:, :d].astype(F32) * a + gate_ref[:, d:].astype(F32) * b
    x1 = x_ref[...] + jnp.dot(merged.astype(BF16), wo_ref[...], preferred_element_type=F32)
    x1_ref[...] = x1
    h2 = _rms(x1, fg_ref[...])
    h2_ref[...] = h2

    h_hi = h2.astype(BF16)
    h_lo = (h2 - h_hi.astype(F32)).astype(BF16)
    logits = (jnp.dot(h_hi, wr_hi_ref[...], preferred_element_type=F32)
              + jnp.dot(h_lo, wr_hi_ref[...], preferred_element_type=F32)
              + jnp.dot(h_hi, wr_lo_ref[...], preferred_element_type=F32)) + br_ref[...]

    lane = lax.broadcasted_iota(jnp.int32, logits.shape, 1)
    big = jnp.int32(ROUTE_W)
    neg = -jnp.inf

    def first_argmax(vals, vmax):
        return jnp.min(jnp.where(vals == vmax, lane, big), axis=-1, keepdims=True)

    grp = jnp.where(lane < N_GROUPS, logits, neg)
    g_max = jnp.max(grp, axis=-1, keepdims=True)
    g_den = jnp.sum(jnp.exp(grp - g_max), axis=-1, keepdims=True)
    g_sel = first_argmax(grp, g_max)
    gate_g = 1.0 / g_den
    lo = N_GROUPS + g_sel * EXPERTS_PER_GROUP
    fine = jnp.where((lane >= lo) & (lane < lo + EXPERTS_PER_GROUP), logits, neg)
    v1 = jnp.max(fine, axis=-1, keepdims=True)
    i1 = first_argmax(fine, v1)
    fine2 = jnp.where(lane == i1, neg, fine)
    v2 = jnp.max(fine2, axis=-1, keepdims=True)
    i2 = first_argmax(fine2, v2)
    e2 = jnp.exp(v2 - v1)
    den = 1.0 + e2
    w1 = gate_g / den
    w2 = gate_g * e2 / den
    route = jnp.where(lane == 0, (i1 - N_GROUPS).astype(F32),
                      jnp.where(lane == 1, (i2 - N_GROUPS).astype(F32),
                                jnp.where(lane == 2, w1, jnp.where(lane == 3, w2, 0.0))))
    route_ref[...] = route


def _merge(x2, y_rw, y_mla, gates, p, tm):
    n, d = x2.shape
    full = lambda w: pl.BlockSpec(w.shape, lambda i: (0, 0))
    row = lambda c: pl.BlockSpec((tm, c), lambda i: (i, 0))
    ws = [p["w_br"], p["w_bm"], p["w_out"], p["ffn_g"], p["wr_hi"], p["wr_lo"], p["b_route"]]
    return pl.pallas_call(
        _merge_kernel,
        grid=(n // tm,),
        in_specs=[row(d), row(y_rw.shape[1]), row(y_mla.shape[1]), row(2 * d)] + [full(w) for w in ws],
        out_specs=[row(d), row(d), row(ROUTE_W)],
        out_shape=[jax.ShapeDtypeStruct((n, d), F32), jax.ShapeDtypeStruct((n, d), F32),
                   jax.ShapeDtypeStruct((n, ROUTE_W), F32)],
        compiler_params=_cparams("parallel"),
        name="merge_route",
    )(x2, y_rw, y_mla, gates, *ws)


DISPATCH_TILE = 256


def _dispatch_kernel(dest_ref, h_ref, zero_ref, xs_ref, sem):
    del zero_ref
    tm = h_ref.shape[0]

    def copy(t, k):
        return pltpu.make_async_copy(h_ref.at[pl.ds(t, 1), :],
                                     xs_ref.at[pl.ds(dest_ref[0, 0, TOP_K * t + k], 1), :], sem)

    def start(t, _):
        for k in range(TOP_K):
            copy(t, k).start()
        return 0

    def wait(t, _):
        for k in range(TOP_K):
            copy(t, k).wait()
        return 0

    lax.fori_loop(0, tm, start, 0)
    lax.fori_loop(0, tm, wait, 0)


def _dispatch(h2, dest, xs_init):
    n, d = h2.shape
    tm = DISPATCH_TILE
    dest3 = dest.reshape(n // tm, 1, TOP_K * tm)
    return pl.pallas_call(
        _dispatch_kernel,
        grid=(n // tm,),
        in_specs=[pl.BlockSpec((1, 1, TOP_K * tm), lambda i: (i, 0, 0), memory_space=pltpu.SMEM),
                  pl.BlockSpec((tm, d), lambda i: (i, 0)),
                  pl.BlockSpec(memory_space=pl.ANY)],
        out_specs=pl.BlockSpec(memory_space=pl.ANY),
        out_shape=jax.ShapeDtypeStruct(xs_init.shape, xs_init.dtype),
        scratch_shapes=[pltpu.SemaphoreType.DMA(())],
        input_output_aliases={2: 0},
        compiler_params=_cparams("arbitrary"),
        name="dispatch",
    )(dest3, h2, xs_init)


def _expert_kernel(blk_e_ref, n_used_ref, x_ref, wgu_ref, wd_ref, y_ref):
    del blk_e_ref

    @pl.when(pl.program_id(0) < n_used_ref[0])
    def _():
        h = jnp.dot(x_ref[...].astype(BF16), wgu_ref[0], preferred_element_type=F32)
        gt = h[:, :D_EXPERT]
        up = h[:, D_EXPERT:]
        act = (gt * jax.nn.sigmoid(gt) * up).astype(BF16)
        y_ref[...] = jnp.dot(act, wd_ref[0], preferred_element_type=F32)

    @pl.when(pl.program_id(0) >= n_used_ref[0])
    def _():
        y_ref[...] = jnp.zeros_like(y_ref)


def _experts(xs, blk_expert, n_used, w_gu, w_down):
    p_rows, d = xs.shape
    n_blocks = p_rows // EXPERT_BLOCK
    grid_spec = pltpu.PrefetchScalarGridSpec(
        num_scalar_prefetch=2,
        grid=(n_blocks,),
        in_specs=[pl.BlockSpec((EXPERT_BLOCK, d), lambda i, be, nu: (i, 0)),
                  pl.BlockSpec((1, d, 2 * D_EXPERT), lambda i, be, nu: (be[i], 0, 0)),
                  pl.BlockSpec((1, D_EXPERT, d), lambda i, be, nu: (be[i], 0, 0))],
        out_specs=pl.BlockSpec((EXPERT_BLOCK, d), lambda i, be, nu: (i, 0)),
    )
    return pl.pallas_call(
        _expert_kernel,
        grid_spec=grid_spec,
        out_shape=jax.ShapeDtypeStruct((p_rows, d), F32),
        compiler_params=_cparams("arbitrary"),
        name="experts",
    )(blk_expert, n_used, xs, w_gu, w_down)


COMBINE_TILE = 256


def _combine_kernel(dest_ref, x1_ref, route_ref, g_ref, yb_ref, o_ref, buf, sem, *, final_norm):
    tm = x1_ref.shape[0]

    def copy(t, k):
        return pltpu.make_async_copy(yb_ref.at[pl.ds(dest_ref[0, 0, TOP_K * t + k], 1), :],
                                     buf.at[k, pl.ds(t, 1), :], sem)

    def start(t, _):
        for k in range(TOP_K):
            copy(t, k).start()
        return 0

    def wait(t, _):
        for k in range(TOP_K):
            copy(t, k).wait()
        return 0

    lax.fori_loop(0, tm, start, 0)
    lax.fori_loop(0, tm, wait, 0)
    route = route_ref[...]
    x2 = x1_ref[...] + route[:, 2:3] * buf[0] + route[:, 3:4] * buf[1]
    o_ref[...] = _rms(x2, g_ref[...]) if final_norm else x2


def _combine(x1, route, dest, yb, final_g, final_norm):
    n, d = x1.shape
    tm = COMBINE_TILE
    dest3 = dest.reshape(n // tm, 1, TOP_K * tm)
    return pl.pallas_call(
        functools.partial(_combine_kernel, final_norm=final_norm),
        grid=(n // tm,),
        in_specs=[pl.BlockSpec((1, 1, TOP_K * tm), lambda i: (i, 0, 0), memory_space=pltpu.SMEM),
                  pl.BlockSpec((tm, d), lambda i: (i, 0)),
                  pl.BlockSpec((tm, ROUTE_W), lambda i: (i, 0)),
                  pl.BlockSpec((1, d), lambda i: (0, 0)),
                  pl.BlockSpec(memory_space=pl.ANY)],
        out_specs=pl.BlockSpec((tm, d), lambda i: (i, 0)),
        out_shape=jax.ShapeDtypeStruct((n, d), F32),
        scratch_shapes=[pltpu.VMEM((TOP_K, tm, d), F32), pltpu.SemaphoreType.DMA(())],
        compiler_params=_cparams("arbitrary"),
        name="combine",
    )(dest3, x1, route, final_g, yb)


def _rwkv_params(rw_mu, rw_w0, rw_w_up, rw_a0, rw_a_up, rw_g_up, rw_k_k, rw_k_a, rw_r_k, rw_gn_w, rw_gn_b):
    row = lambda v: v.reshape(1, -1).astype(F32)
    zeros = jnp.zeros((A_LORA, RW_DIM), F32)
    return {
        "mu_r": row(rw_mu[:RW_DIM]), "mu_k": row(rw_mu[RW_DIM:2 * RW_DIM]),
        "mu_v": row(rw_mu[2 * RW_DIM:3 * RW_DIM]), "mu_l": row(rw_mu[3 * RW_DIM:]),
        "w0": row(rw_w0), "a0": row(rw_a0), "k_k": row(rw_k_k), "k_a": row(rw_k_a),
        "r_k": row(rw_r_k), "gn_w": row(rw_gn_w), "gn_b": row(rw_gn_b),
        "w_up": jnp.concatenate([rw_w_up, zeros], axis=0).astype(F32),
        "a_up": jnp.concatenate([zeros, rw_a_up], axis=0).astype(BF16),
        "g_up": rw_g_up.astype(BF16),
    }


def _mla_params(g_qa, w_q_up, g_kva, w_kv_up):
    half = QK_ROPE // 2
    pad = MLA_SLOT - QK_NOPE - QK_ROPE
    wq = w_q_up.reshape(Q_LORA, MLA_HEADS, QK_NOPE + QK_ROPE)
    q_nope, q_r1, q_r2 = wq[..., :QK_NOPE], wq[..., QK_NOPE:QK_NOPE + half], wq[..., QK_NOPE + half:]
    zq = lambda w: jnp.zeros((Q_LORA, MLA_HEADS, w), F32)
    w_qa = jnp.concatenate([q_nope, q_r1, q_r2, zq(pad)], axis=-1).reshape(Q_LORA, -1)
    w_qb = jnp.concatenate([zq(QK_NOPE), -q_r2, q_r1, zq(pad)], axis=-1).reshape(Q_LORA, -1)
    wkv = w_kv_up.reshape(KV_LORA, MLA_HEADS, QK_NOPE + V_HEAD)
    w_k = jnp.concatenate([wkv[..., :QK_NOPE], jnp.zeros((KV_LORA, MLA_HEADS, MLA_SLOT - QK_NOPE), F32)],
                          axis=-1).reshape(KV_LORA, -1)
    w_v = wkv[..., QK_NOPE:].reshape(KV_LORA, -1)
    eye = jnp.eye(half, dtype=F32)
    z = jnp.zeros((half, half), F32)
    zl = jnp.zeros((QK_ROPE, QK_NOPE), F32)
    zr = jnp.zeros((QK_ROPE, pad), F32)
    p_a = jnp.concatenate([zl, jnp.concatenate([eye, z], 0), jnp.concatenate([z, eye], 0), zr], axis=1)
    p_b = jnp.concatenate([zl, jnp.concatenate([z, -eye], 0), jnp.concatenate([eye, z], 0), zr], axis=1)
    return {"g_qa": g_qa.reshape(1, -1), "g_kva": g_kva.reshape(1, -1),
            "w_qa": w_qa.astype(BF16), "w_qb": w_qb.astype(BF16), "w_k": w_k.astype(BF16),
            "w_v": w_v.astype(BF16), "p_a": p_a.astype(BF16), "p_b": p_b.astype(BF16)}


def _rope_tables(positions):
    half = QK_ROPE // 2
    inv_freq = ROPE_THETA ** (-jnp.arange(0, QK_ROPE, 2, dtype=F32) / QK_ROPE)
    ang = positions.astype(F32).reshape(-1, 1) * inv_freq
    cos, sin = jnp.cos(ang), jnp.sin(ang)
    n = ang.shape[0]
    pad = MLA_SLOT - QK_NOPE - QK_ROPE
    cos_t = jnp.concatenate([jnp.ones((n, QK_NOPE), F32), cos, cos, jnp.zeros((n, pad), F32)], axis=1)
    sin_t = jnp.concatenate([jnp.zeros((n, QK_NOPE), F32), sin, sin, jnp.zeros((n, pad), F32)], axis=1)
    del half
    return cos_t, sin_t


def _route_plan(route, n):
    e_flat = route[:, :TOP_K].astype(jnp.int32).reshape(-1)
    a = e_flat.shape[0]
    onehot = (e_flat[:, None] == jnp.arange(N_EXPERTS, dtype=jnp.int32)[None, :]).astype(jnp.int32)
    before = jnp.cumsum(onehot, axis=0) - onehot
    rank = jnp.sum(before * onehot, axis=1)
    counts = jnp.sum(onehot, axis=0)
    padded = (counts + EXPERT_BLOCK - 1) // EXPERT_BLOCK * EXPERT_BLOCK
    pad_end = jnp.cumsum(padded)
    pad_start = pad_end - padded
    dest = (pad_start[e_flat] + rank).astype(jnp.int32)
    n_blocks = -(-a // EXPERT_BLOCK) + N_EXPERTS
    blk_expert = jnp.minimum(
        jnp.searchsorted(pad_end, jnp.arange(n_blocks, dtype=jnp.int32) * EXPERT_BLOCK, side="right"),
        N_EXPERTS - 1).astype(jnp.int32)
    n_used = (pad_end[-1] // EXPERT_BLOCK).astype(jnp.int32).reshape(1)
    return dest, blk_expert, n_used, n_blocks


def kernel(x, positions, mix_norm_g, w_in, rw_mu, rw_w0, rw_w_up, rw_a0, rw_a_up, rw_g_up, rw_k_k, rw_k_a, rw_r_k, rw_gn_w, rw_gn_b, mla_g_qa, mla_w_q_up, mla_g_kva, mla_w_kv_up, w_branch_rw, w_branch_mla, w_out, ffn_norm_g, moe_w_group, moe_b_group, moe_w_router, moe_b_router, moe_w_gu, moe_w_down, final_norm_g):
    batch, seq, d = x.shape
    n = batch * seq
    depth = w_in.shape[0]
    rw_cols = 3 * RW_DIM + W_LORA + A_LORA + G_LORA
    mla_cols = Q_LORA + KV_LORA + QK_ROPE
    cos_t, sin_t = _rope_tables(positions)
    x2 = x.reshape(n, d)

    for l in range(depth):
        wl = w_in[l].astype(BF16)
        c_rw, c_q, c_kvr, gates = _in_proj(
            x2, mix_norm_g[l].reshape(1, d), wl[:, :rw_cols], wl[:, rw_cols:rw_cols + Q_LORA],
            wl[:, rw_cols + Q_LORA:rw_cols + mla_cols], wl[:, rw_cols + mla_cols:], tm=256)

        rp = _rwkv_params(rw_mu[l], rw_w0[l], rw_w_up[l], rw_a0[l], rw_a_up[l], rw_g_up[l], rw_k_k[l],
                          rw_k_a[l], rw_r_k[l], rw_gn_w[l], rw_gn_b[l])
        y_rw = _rwkv(c_rw.reshape(batch, seq, rw_cols), rp, batch, seq).reshape(n, RW_DIM)

        mp = _mla_params(mla_g_qa[l], mla_w_q_up[l], mla_g_kva[l], mla_w_kv_up[l])
        q, k, v = _mla_prep(c_q, c_kvr, cos_t, sin_t, mp, tm=256)
        y_mla = _mla_attn(q.reshape(batch, seq, -1), k.reshape(batch, seq, -1),
                          v.reshape(batch, seq, -1), batch, seq).reshape(n, MLA_HEADS * V_HEAD)

        w_route = jnp.concatenate(
            [moe_w_group[l], moe_w_router[l], jnp.zeros((d, ROUTE_W - N_GROUPS - N_EXPERTS), F32)], axis=1)
        b_route = jnp.concatenate(
            [moe_b_group[l], moe_b_router[l], jnp.zeros((ROUTE_W - N_GROUPS - N_EXPERTS,), F32)]).reshape(1, -1)
        wr_hi = w_route.astype(BF16)
        wr_lo = (w_route - wr_hi.astype(F32)).astype(BF16)
        mparams = {"w_br": w_branch_rw[l].astype(BF16), "w_bm": w_branch_mla[l].astype(BF16),
                   "w_out": w_out[l].astype(BF16), "ffn_g": ffn_norm_g[l].reshape(1, d),
                   "wr_hi": wr_hi, "wr_lo": wr_lo, "b_route": b_route}
        x1, h2, route = _merge(x2, y_rw, y_mla, gates, mparams, tm=256)

        dest, blk_expert, n_used, n_blocks = _route_plan(route, n)
        xs = _dispatch(h2, dest, jnp.zeros((n_blocks * EXPERT_BLOCK, d), F32))
        yb = _experts(xs, blk_expert, n_used, moe_w_gu[l].astype(BF16), moe_w_down[l].astype(BF16))
        x2 = _combine(x1, route, dest, yb, final_norm_g.reshape(1, d), final_norm=(l == depth - 1))

    return x2.reshape(batch, seq, d)
```

```python
import functools
import math

import jax
import jax.numpy as jnp
from jax import lax
from jax.experimental import pallas as pl
from jax.experimental.pallas import tpu as pltpu

F32 = jnp.float32
BF16 = jnp.bfloat16

RW_HEADS = 8
RW_HEAD_DIM = 64
RW_DIM = RW_HEADS * RW_HEAD_DIM
W_LORA = 64
A_LORA = 64
G_LORA = 128
GN_EPS = 64e-5
MLA_HEADS = 8
QK_NOPE = 64
QK_ROPE = 32
V_HEAD = 64
Q_LORA = 384
KV_LORA = 256
ROPE_THETA = 10000.0
N_GROUPS = 4
EXPERTS_PER_GROUP = 8
N_EXPERTS = N_GROUPS * EXPERTS_PER_GROUP
TOP_K = 2
D_EXPERT = 256
EXPERT_BLOCK = 256
NORM_EPS = 1e-6

LANES = 128
HEAD_PAIR = 2 * RW_HEAD_DIM
VMEM_LIMIT = 48 * 1024 * 1024


def _cparams(*sem):
    return pltpu.CompilerParams(dimension_semantics=sem, vmem_limit_bytes=VMEM_LIMIT)


def _mm(a, b, dims=((1,), (0,)), split=False):
    dn = (dims, ((), ()))
    dot = lambda x, y: lax.dot_general(x, y, dn, preferred_element_type=F32)
    a_hi = a.astype(BF16)
    b_hi = b.astype(BF16)
    if not split:
        return dot(a_hi, b_hi)
    a_lo = (a - a_hi.astype(F32)).astype(BF16)
    b_lo = (b - b_hi.astype(F32)).astype(BF16)
    return dot(a_hi, b_hi) + dot(a_lo, b_hi) + dot(a_hi, b_lo)


def _mm_sel(sel_bf16, x, dims=((1,), (0,))):
    dn = (dims, ((), ()))
    hi = x.astype(BF16)
    r1 = x - hi.astype(F32)
    mid = r1.astype(BF16)
    lo = (r1 - mid.astype(F32)).astype(BF16)
    out = lax.dot_general(sel_bf16, hi, dn, preferred_element_type=F32)
    out = out + lax.dot_general(sel_bf16, mid, dn, preferred_element_type=F32)
    return out + lax.dot_general(sel_bf16, lo, dn, preferred_element_type=F32)


def _seg_sum(x, seg_bf16):
    hi = x.astype(BF16)
    lo = (x - hi.astype(F32)).astype(BF16)
    return (jnp.dot(hi, seg_bf16, preferred_element_type=F32)
            + jnp.dot(lo, seg_bf16, preferred_element_type=F32))


def _rms(x, g):
    return x * lax.rsqrt(jnp.mean(x * x, axis=-1, keepdims=True) + NORM_EPS) * g


def _pack_rows(x):
    half = x.shape[1] // 2
    bits = lambda v: lax.bitcast_convert_type(v.astype(BF16).astype(F32), jnp.uint32)
    return bits(x[:, :half]) | (bits(x[:, half:]) >> 16)


def _unpack_rows(p):
    hi = lax.bitcast_convert_type(p & jnp.uint32(0xFFFF0000), F32)
    lo = lax.bitcast_convert_type(p << 16, F32)
    return jnp.concatenate([hi, lo], axis=1)


def _in_proj_kernel(x_ref, g_ref, wrw_ref, wq_ref, wkvr_ref, wg_ref,
                    crw_ref, cq_ref, ckvr_ref, gate_ref):
    hb = _rms(x_ref[...], g_ref[...]).astype(BF16)
    crw_ref[...] = jnp.dot(hb, wrw_ref[...], preferred_element_type=F32)
    cq_ref[...] = jnp.dot(hb, wq_ref[...], preferred_element_type=F32)
    ckvr_ref[...] = jnp.dot(hb, wkvr_ref[...], preferred_element_type=F32)
    gate_ref[...] = jax.nn.sigmoid(jnp.dot(hb, wg_ref[...], preferred_element_type=F32)).astype(BF16)


def _in_proj(x2, g, w_rw, w_q, w_kvr, w_gate, tm):
    n, d = x2.shape
    full = lambda w: pl.BlockSpec(w.shape, lambda i: (0, 0))
    row = lambda c: pl.BlockSpec((tm, c), lambda i: (i, 0))
    return pl.pallas_call(
        _in_proj_kernel,
        grid=(n // tm,),
        in_specs=[row(d), full(g), full(w_rw), full(w_q), full(w_kvr), full(w_gate)],
        out_specs=[row(w_rw.shape[1]), row(w_q.shape[1]), row(w_kvr.shape[1]), row(w_gate.shape[1])],
        out_shape=[jax.ShapeDtypeStruct((n, w_rw.shape[1]), F32),
                   jax.ShapeDtypeStruct((n, w_q.shape[1]), F32),
                   jax.ShapeDtypeStruct((n, w_kvr.shape[1]), F32),
                   jax.ShapeDtypeStruct((n, w_gate.shape[1]), BF16)],
        compiler_params=_cparams("parallel"),
        name="in_proj",
    )(x2, g, w_rw, w_q, w_kvr, w_gate)


RW_CHUNK = 64
RW_TILE = 256


def _token_shift(cur, halo_ref, first):
    prev_row = jnp.where(first, 0.0, halo_ref[0, 7:8, :])
    rolled = pltpu.roll(cur, 1, 0)
    row = lax.broadcasted_iota(jnp.int32, cur.shape, 0)
    return jnp.where(row == 0, prev_row, rolled)


def _rwkv_kernel(r_ref, k_ref, v_ref, l_ref, hr_ref, hk_ref, hv_ref, hl_ref,
                 mur_ref, muk_ref, muv_ref, mul_ref, w0_ref, a0_ref, kk_ref, ka_ref, rk_ref,
                 gnw_ref, gnb_ref, wup_ref, aup_ref, gup_ref, y_ref, st_ref):
    i = pl.program_id(2)
    first = i == 0

    @pl.when(first)
    def _():
        st_ref[...] = jnp.zeros_like(st_ref)

    def mixed(c_ref, h_ref, mu_ref):
        cur = c_ref[0]
        return cur + (_token_shift(cur, h_ref, first) - cur) * mu_ref[...]

    zr = mixed(r_ref, hr_ref, mur_ref)
    zk = mixed(k_ref, hk_ref, muk_ref)
    zv = mixed(v_ref, hv_ref, muv_ref)
    zl = mixed(l_ref, hl_ref, mul_ref)
    z_wa = zl[:, :LANES]
    z_g = zl[:, LANES:]

    lane = lax.broadcasted_iota(jnp.int32, (LANES, LANES), 1)
    sub = lax.broadcasted_iota(jnp.int32, (LANES, LANES), 0)
    same_head = (lane // RW_HEAD_DIM) == (sub // RW_HEAD_DIM)
    seg = jnp.where(same_head, 1.0, 0.0).astype(BF16)

    w = w0_ref[...] + _mm(jnp.tanh(z_wa), wup_ref[...], split=True)
    u = -w
    softplus = jnp.maximum(u, 0.0) + jnp.log(1.0 + jnp.exp(-jnp.abs(u)))
    log_decay = -jnp.exp(-softplus - 0.5)
    a = jax.nn.sigmoid(a0_ref[...] + _mm(z_wa, aup_ref[...]))
    g = _mm(jax.nn.sigmoid(z_g), gup_ref[...])

    kk = zk * kk_ref[...]
    kk = kk / jnp.maximum(jnp.sqrt(_seg_sum(kk * kk, seg)), 1e-12)
    k2 = zk * (1.0 + (a - 1.0) * ka_ref[...])
    bonus = _seg_sum(zr * k2 * rk_ref[...], seg) * zv
    kka = kk * a

    c = RW_CHUNK
    crow = lax.broadcasted_iota(jnp.int32, (c, c), 0)
    ccol = lax.broadcasted_iota(jnp.int32, (c, c), 1)
    cum_sel = jnp.where(crow >= ccol, 1.0, 0.0).astype(BF16)
    tril_incl = sub >= lane
    tril_strict = sub > lane
    eye_l = jnp.where(lane == sub, 1.0, 0.0).astype(F32)
    lo_half = lax.broadcasted_iota(jnp.int32, (c, LANES), 1) < RW_HEAD_DIM
    nt = ((1,), (1,))
    tn = ((0,), (0,))
    zeros_blk = jnp.zeros((2 * c, LANES), F32)

    def stack(t):
        return jnp.concatenate([jnp.where(lo_half, t, 0.0), jnp.where(lo_half, 0.0, t)], axis=0)

    chunks = range(RW_TILE // c)
    x_a, x_b, x_k, x_r, x_v, x_bh, x_kh, w_tot = [], [], [], [], [], [], [], []
    for ci in chunks:
        sl = slice(ci * c, (ci + 1) * c)
        ld = log_decay[sl]
        cum = _mm_sel(cum_sel, ld)
        tot = cum[c - 1:c, :]
        e_neg = jnp.exp(-cum)
        e_rest = jnp.exp(tot - cum)
        x_a.append(stack(-kk[sl] * jnp.exp(cum - ld)))
        x_b.append(stack(kka[sl] * e_neg))
        x_k.append(stack(k2[sl] * e_neg))
        x_r.append(stack(zr[sl] * jnp.exp(cum)))
        x_v.append(stack(zv[sl]))
        x_bh.append(stack(kka[sl] * e_rest))
        x_kh.append(stack(k2[sl] * e_rest))
        w_tot.append(jnp.exp(tot))

    inter = [_mm(jnp.concatenate([x_a[i], x_r[i]], axis=0), jnp.concatenate([x_b[i], x_k[i]], axis=0), nt)
             for i in chunks]
    tril_incl2 = jnp.concatenate([tril_incl, tril_incl], axis=1)
    a_ab = [jnp.where(tril_strict, m[:2 * c, :2 * c], 0.0) for m in inter]
    a_ak = [jnp.where(tril_strict, m[:2 * c, 2 * c:], 0.0) for m in inter]
    a_r = [jnp.where(tril_incl2, m[2 * c:], 0.0) for m in inter]
    w_ak = [_mm(a_ak[i], x_v[i]) for i in chunks]

    t_inv = [eye_l + m for m in a_ab]
    pw = a_ab
    for _ in range(int(math.log2(c)) - 1):
        pw = [_mm(m, m) for m in pw]
        t_inv = [t_inv[i] + _mm(t_inv[i], pw[i]) for i in chunks]

    solved = [_mm(t_inv[i], jnp.concatenate([x_a[i], w_ak[i]], axis=1)) for i in chunks]
    rhs = [jnp.concatenate([solved[i], jnp.concatenate([zeros_blk, x_v[i]], axis=1)], axis=0)
           for i in chunks]
    out = [_mm(a_r[i], rhs[i]) for i in chunks]
    carry = [_mm(jnp.concatenate([x_bh[i], x_kh[i]], axis=0), rhs[i], tn) for i in chunks]
    q_hat, y_loc = [], []
    for i in chunks:
        q_st = x_r[i] + out[i][:, :LANES]
        q_hat.append(q_st[:c] + q_st[c:])
        y_loc.append(out[i][:c, LANES:] + out[i][c:, LANES:])
    trans = [jnp.concatenate([eye_l * w_tot[i] + carry[i][:, :LANES], carry[i][:, LANES:]], axis=1)
             for i in chunks]

    zeros_sq = jnp.zeros((LANES, LANES), F32)
    prefix = [None, trans[0]]
    for i in chunks[1:]:
        nxt = _mm(trans[i][:, :LANES], prefix[i], split=True)
        prefix.append(nxt + jnp.concatenate([zeros_sq, trans[i][:, LANES:]], axis=1))
    q_pre = [None] + [_mm(q_hat[i], prefix[i]) for i in chunks[1:]]
    q_m = jnp.concatenate([q_hat[0]] + [q_pre[i][:, :LANES] for i in chunks[1:]], axis=0)
    y_off = jnp.concatenate([y_loc[0]] + [y_loc[i] + q_pre[i][:, LANES:] for i in chunks[1:]], axis=0)
    state = st_ref[...]
    y = _mm(q_m, state) + y_off
    st_ref[...] = _mm(prefix[-1][:, :LANES], state, split=True) + prefix[-1][:, LANES:]

    inv_n = 1.0 / RW_HEAD_DIM
    mean = _seg_sum(y, seg) * inv_n
    d = y - mean
    var = _seg_sum(d * d, seg) * inv_n
    yn = d * lax.rsqrt(var + GN_EPS) * gnw_ref[...] + gnb_ref[...]
    y_ref[0] = ((yn + bonus) * g).astype(y_ref.dtype)


def _rwkv(c_rw, p, batch, seq):
    ts = RW_TILE
    n_pairs = RW_DIM // HEAD_PAIR
    lora_blk = (3 * RW_DIM) // (2 * LANES)
    halo = ts // 8

    def col(off):
        return pl.BlockSpec((1, ts, LANES), lambda b, pp, i, off=off: (b, i, off + pp))

    def col_halo(off):
        return pl.BlockSpec((1, 8, LANES),
                            lambda b, pp, i, off=off: (b, jnp.maximum(i * halo - 1, 0), off + pp))

    vec = pl.BlockSpec((1, LANES), lambda b, pp, i: (0, pp))
    lora_w = pl.BlockSpec((LANES, LANES), lambda b, pp, i: (0, pp))
    in_specs = [
        col(0), col(n_pairs), col(2 * n_pairs),
        pl.BlockSpec((1, ts, 2 * LANES), lambda b, pp, i: (b, i, lora_blk)),
        col_halo(0), col_halo(n_pairs), col_halo(2 * n_pairs),
        pl.BlockSpec((1, 8, 2 * LANES), lambda b, pp, i: (b, jnp.maximum(i * halo - 1, 0), lora_blk)),
        vec, vec, vec, pl.BlockSpec((1, 2 * LANES), lambda b, pp, i: (0, 0)),
        vec, vec, vec, vec, vec, vec, vec, lora_w, lora_w, lora_w,
    ]
    return pl.pallas_call(
        _rwkv_kernel,
        grid=(batch, n_pairs, seq // ts),
        in_specs=in_specs,
        out_specs=pl.BlockSpec((1, ts, LANES), lambda b, pp, i: (b, i, pp)),
        out_shape=jax.ShapeDtypeStruct((batch, seq, RW_DIM), BF16),
        scratch_shapes=[pltpu.VMEM((LANES, LANES), F32)],
        compiler_params=_cparams("parallel", "parallel", "arbitrary"),
        name="rwkv",
    )(c_rw, c_rw, c_rw, c_rw, c_rw, c_rw, c_rw, c_rw,
      p["mu_r"], p["mu_k"], p["mu_v"], p["mu_l"], p["w0"], p["a0"], p["k_k"], p["k_a"], p["r_k"],
      p["gn_w"], p["gn_b"], p["w_up"], p["a_up"], p["g_up"])


MLA_SLOT = 128


def _mla_prep_kernel(cq_ref, ckvr_ref, cos_ref, sin_ref, gq_ref, gkv_ref,
                     wqa_ref, wqb_ref, wk_ref, wv_ref, pa_ref, pb_ref,
                     q_ref, k_ref, v_ref):
    cos = cos_ref[...]
    sin = sin_ref[...]
    zq = _rms(cq_ref[...], gq_ref[...]).astype(BF16)
    qa = jnp.dot(zq, wqa_ref[...], preferred_element_type=F32)
    qb = jnp.dot(zq, wqb_ref[...], preferred_element_type=F32)
    ckvr = ckvr_ref[...]
    zkv = _rms(ckvr[:, :KV_LORA], gkv_ref[...]).astype(BF16)
    kn = jnp.dot(zkv, wk_ref[...], preferred_element_type=F32)
    v_ref[...] = jnp.dot(zkv, wv_ref[...], preferred_element_type=F32).astype(BF16)
    kr = ckvr[:, KV_LORA:].astype(BF16)
    k_rope = (jnp.dot(kr, pa_ref[...], preferred_element_type=F32) * cos
              + jnp.dot(kr, pb_ref[...], preferred_element_type=F32) * sin)
    scale = math.log2(math.e) / math.sqrt(QK_NOPE + QK_ROPE)
    for h in range(MLA_HEADS):
        sl = slice(h * MLA_SLOT, (h + 1) * MLA_SLOT)
        q_ref[:, sl] = ((qa[:, sl] * cos + qb[:, sl] * sin) * scale).astype(BF16)
        k_ref[:, sl] = (kn[:, sl] + k_rope).astype(BF16)


def _mla_prep(c_q, c_kvr, cos_t, sin_t, p, tm):
    n = c_q.shape[0]
    full = lambda w: pl.BlockSpec(w.shape, lambda i: (0, 0))
    row = lambda c: pl.BlockSpec((tm, c), lambda i: (i, 0))
    ws = [p["g_qa"], p["g_kva"], p["w_qa"], p["w_qb"], p["w_k"], p["w_v"], p["p_a"], p["p_b"]]
    hq = MLA_HEADS * MLA_SLOT
    return pl.pallas_call(
        _mla_prep_kernel,
        grid=(n // tm,),
        in_specs=[row(c_q.shape[1]), row(c_kvr.shape[1]), row(MLA_SLOT), row(MLA_SLOT)]
                 + [full(w) for w in ws],
        out_specs=[row(hq), row(hq), row(MLA_HEADS * V_HEAD)],
        out_shape=[jax.ShapeDtypeStruct((n, hq), BF16), jax.ShapeDtypeStruct((n, hq), BF16),
                   jax.ShapeDtypeStruct((n, MLA_HEADS * V_HEAD), BF16)],
        compiler_params=_cparams("parallel"),
        name="mla_prep",
    )(c_q, c_kvr, cos_t, sin_t, *ws)


ATT_TILE = 512


def _attn_kernel(q_ref, k_ref, v_ref, o_ref):
    qi = pl.program_id(2)
    t = ATT_TILE
    lo_v = lax.broadcasted_iota(jnp.int32, (t, 2 * V_HEAD), 1) < V_HEAD
    row = lax.broadcasted_iota(jnp.int32, (t, t), 0)
    col = lax.broadcasted_iota(jnp.int32, (t, t), 1)
    causal = col <= row
    nt = (((1,), (1,)), ((), ()))
    qs = [q_ref[0, :, h * MLA_SLOT:(h + 1) * MLA_SLOT] for h in range(2)]
    zero_v = jnp.zeros((t, 2 * V_HEAD), BF16)

    def block(j, carry, masked):
        stats, acc = carry
        rows = pl.ds(pl.multiple_of(j * t, t), t)
        vb = v_ref[0, rows, :]
        v_heads = (jnp.where(lo_v, vb, zero_v), jnp.where(lo_v, zero_v, vb))
        new_stats, alphas = [], []
        pv = None
        for h in range(2):
            m, l = stats[h]
            kb = k_ref[0, rows, h * MLA_SLOT:(h + 1) * MLA_SLOT]
            s = lax.dot_general(qs[h], kb, nt, preferred_element_type=F32)
            if masked:
                s = jnp.where(causal, s, -jnp.inf)
            m_new = jnp.maximum(m, jnp.max(s, axis=-1, keepdims=True))
            alpha = jnp.exp2(m - m_new)
            pr = jnp.exp2(s - m_new)
            new_stats.append((m_new, l * alpha + jnp.sum(pr, axis=-1, keepdims=True)))
            alphas.append(alpha)
            d = jnp.dot(pr.astype(BF16), v_heads[h], preferred_element_type=F32)
            pv = d if pv is None else pv + d
        acc = acc * jnp.where(lo_v, alphas[0], alphas[1]) + pv
        return tuple(new_stats), acc

    stat0 = (jnp.full((t, 1), -jnp.inf, F32), jnp.zeros((t, 1), F32))
    init = ((stat0, stat0), jnp.zeros((t, 2 * V_HEAD), F32))
    carry = lax.fori_loop(0, qi, lambda j, cr: block(j, cr, False), init)
    stats, acc = block(qi, carry, True)
    o_ref[0] = (acc / jnp.where(lo_v, stats[0][1], stats[1][1])).astype(o_ref.dtype)


def _mla_attn(q, k, v, batch, seq):
    t = ATT_TILE
    return pl.pallas_call(
        _attn_kernel,
        grid=(batch, MLA_HEADS // 2, seq // t),
        in_specs=[pl.BlockSpec((1, t, 2 * MLA_SLOT), lambda b, hp, i: (b, i, hp)),
                  pl.BlockSpec((1, seq, 2 * MLA_SLOT), lambda b, hp, i: (b, 0, hp)),
                  pl.BlockSpec((1, seq, 2 * V_HEAD), lambda b, hp, i: (b, 0, hp))],
        out_specs=pl.BlockSpec((1, t, 2 * V_HEAD), lambda b, hp, i: (b, i, hp)),
        out_shape=jax.ShapeDtypeStruct((batch, seq, MLA_HEADS * V_HEAD), BF16),
        compiler_params=_cparams("parallel", "parallel", "arbitrary"),
        name="mla_attn",
    )(q, k, v)


ROUTE_W = 128


def _merge_kernel(x_ref, yrw_ref, ymla_ref, gate_ref, wbr_ref, wbm_ref, wo_ref, fg_ref,
                  wr_hi_ref, wr_lo_ref, br_ref, x1_ref, h2p_ref, route_ref, hist_ref):
    d = x_ref.shape[1]
    a = jnp.dot(yrw_ref[...], wbr_ref[...], preferred_element_type=F32)
    b = jnp.dot(ymla_ref[...], wbm_ref[...], preferred_element_type=F32)
    merged = gate_ref[:, :d].astype(F32) * a + gate_ref[:, d:].astype(F32) * b
    x1 = x_ref[...] + jnp.dot(merged.astype(BF16), wo_ref[...], preferred_element_type=F32)
    x1_ref[...] = x1
    h2 = _rms(x1, fg_ref[...])
    h2p_ref[...] = _pack_rows(h2)

    h_hi = h2.astype(BF16)
    h_lo = (h2 - h_hi.astype(F32)).astype(BF16)
    logits = (jnp.dot(h_hi, wr_hi_ref[...], preferred_element_type=F32)
              + jnp.dot(h_lo, wr_hi_ref[...], preferred_element_type=F32)
              + jnp.dot(h_hi, wr_lo_ref[...], preferred_element_type=F32)) + br_ref[...]

    lane = lax.broadcasted_iota(jnp.int32, logits.shape, 1)
    big = jnp.int32(ROUTE_W)
    neg = -jnp.inf

    def first_argmax(vals, vmax):
        return jnp.min(jnp.where(vals == vmax, lane, big), axis=-1, keepdims=True)

    grp = jnp.where(lane < N_GROUPS, logits, neg)
    g_max = jnp.max(grp, axis=-1, keepdims=True)
    g_den = jnp.sum(jnp.exp(grp - g_max), axis=-1, keepdims=True)
    g_sel = first_argmax(grp, g_max)
    gate_g = 1.0 / g_den
    lo = N_GROUPS + g_sel * EXPERTS_PER_GROUP
    fine = jnp.where((lane >= lo) & (lane < lo + EXPERTS_PER_GROUP), logits, neg)
    v1 = jnp.max(fine, axis=-1, keepdims=True)
    i1 = first_argmax(fine, v1)
    fine2 = jnp.where(lane == i1, neg, fine)
    v2 = jnp.max(fine2, axis=-1, keepdims=True)
    i2 = first_argmax(fine2, v2)
    e2 = jnp.exp(v2 - v1)
    den = 1.0 + e2
    w1 = gate_g / den
    w2 = gate_g * e2 / den
    route = jnp.where(lane == 0, (i1 - N_GROUPS).astype(F32),
                      jnp.where(lane == 1, (i2 - N_GROUPS).astype(F32),
                                jnp.where(lane == 2, w1, jnp.where(lane == 3, w2, 0.0))))
    route_ref[...] = route
    chosen = jnp.where((lane == i1) | (lane == i2), 1.0, 0.0)
    hist_ref[0] = jnp.broadcast_to(jnp.sum(chosen, axis=0, keepdims=True), hist_ref.shape[1:])


def _merge(x2, y_rw, y_mla, gates, p, tm):
    n, d = x2.shape
    full = lambda w: pl.BlockSpec(w.shape, lambda i: (0, 0))
    row = lambda c: pl.BlockSpec((tm, c), lambda i: (i, 0))
    ws = [p["w_br"], p["w_bm"], p["w_out"], p["ffn_g"], p["wr_hi"], p["wr_lo"], p["b_route"]]
    return pl.pallas_call(
        _merge_kernel,
        grid=(n // tm,),
        in_specs=[row(d), row(y_rw.shape[1]), row(y_mla.shape[1]), row(2 * d)] + [full(w) for w in ws],
        out_specs=[row(d), row(d // 2), row(ROUTE_W), pl.BlockSpec((1, 8, ROUTE_W), lambda i: (i, 0, 0))],
        out_shape=[jax.ShapeDtypeStruct((n, d), F32), jax.ShapeDtypeStruct((n, d // 2), jnp.uint32),
                   jax.ShapeDtypeStruct((n, ROUTE_W), F32),
                   jax.ShapeDtypeStruct((n // tm, 8, ROUTE_W), F32)],
        compiler_params=_cparams("parallel"),
        name="merge_route",
    )(x2, y_rw, y_mla, gates, *ws)


def _plan_kernel(route_ref, base_ref, dest_ref):
    tm = route_ref.shape[0]
    route = route_ref[...]
    lane = lax.broadcasted_iota(jnp.int32, route.shape, 1).astype(F32)
    pick = [lane == route[:, k:k + 1] for k in range(TOP_K)]
    both = jnp.where(pick[0] | pick[1], 1.0, 0.0).astype(BF16)
    r = lax.broadcasted_iota(jnp.int32, (tm, tm), 0)
    c = lax.broadcasted_iota(jnp.int32, (tm, tm), 1)
    earlier = jnp.where(r > c, 1.0, 0.0).astype(BF16)
    offs = jnp.dot(earlier, both, preferred_element_type=F32) + base_ref[0]
    rows = [jnp.sum(jnp.where(pk, offs, 0.0), axis=-1, keepdims=True) for pk in pick]
    dest_ref[...] = jnp.where(lane == 0.0, rows[0], jnp.where(lane == 1.0, rows[1], 0.0)).astype(jnp.int32)


def _plan(route, base, tm):
    n = route.shape[0]
    return pl.pallas_call(
        _plan_kernel,
        grid=(n // tm,),
        in_specs=[pl.BlockSpec((tm, ROUTE_W), lambda i: (i, 0)),
                  pl.BlockSpec((1, 1, ROUTE_W), lambda i: (i, 0, 0))],
        out_specs=pl.BlockSpec((tm, ROUTE_W), lambda i: (i, 0)),
        out_shape=jax.ShapeDtypeStruct((n, ROUTE_W), jnp.int32),
        compiler_params=_cparams("parallel"),
        name="route_plan",
    )(route, base)


DISPATCH_TILE = 256
ROW_DMA_UNROLL = 8


def _dispatch_kernel(dest_ref, h_ref, zero_ref, xs_ref, sem):
    del zero_ref
    tm = h_ref.shape[0]

    def start(t, _):
        for k in range(TOP_K):
            pltpu.make_async_copy(h_ref.at[pl.ds(t, 1), :],
                                  xs_ref.at[pl.ds(dest_ref[0, 0, TOP_K * t + k], 1), :], sem).start()
        return 0

    lax.fori_loop(0, tm, start, 0, unroll=ROW_DMA_UNROLL)
    all_rows = xs_ref.at[pl.ds(0, TOP_K * tm), :]
    pltpu.make_async_copy(all_rows, all_rows, sem).wait()


def _dispatch(h2, dest, xs_init):
    n, d = h2.shape
    tm = DISPATCH_TILE
    dest3 = dest.reshape(n // tm, 1, TOP_K * tm)
    return pl.pallas_call(
        _dispatch_kernel,
        grid=(n // tm,),
        in_specs=[pl.BlockSpec((1, 1, TOP_K * tm), lambda i: (i, 0, 0), memory_space=pltpu.SMEM),
                  pl.BlockSpec((tm, d), lambda i: (i, 0)),
                  pl.BlockSpec(memory_space=pl.ANY)],
        out_specs=pl.BlockSpec(memory_space=pl.ANY),
        out_shape=jax.ShapeDtypeStruct(xs_init.shape, xs_init.dtype),
        scratch_shapes=[pltpu.SemaphoreType.DMA(())],
        input_output_aliases={2: 0},
        compiler_params=_cparams("arbitrary"),
        name="dispatch",
    )(dest3, h2, xs_init)


def _expert_kernel(blk_e_ref, n_used_ref, x_ref, wgu_ref, wd_ref, y_ref):
    del blk_e_ref

    @pl.when(pl.program_id(0) < n_used_ref[0])
    def _():
        h = jnp.dot(_unpack_rows(x_ref[...]).astype(BF16), wgu_ref[0], preferred_element_type=F32)
        gt = h[:, :D_EXPERT]
        up = h[:, D_EXPERT:]
        act = (gt * jax.nn.sigmoid(gt) * up).astype(BF16)
        y_ref[...] = _pack_rows(jnp.dot(act, wd_ref[0], preferred_element_type=F32))

    @pl.when(pl.program_id(0) >= n_used_ref[0])
    def _():
        y_ref[...] = jnp.zeros_like(y_ref)


def _experts(xs, blk_expert, n_used, w_gu, w_down):
    p_rows, dp = xs.shape
    d = 2 * dp
    n_blocks = p_rows // EXPERT_BLOCK
    grid_spec = pltpu.PrefetchScalarGridSpec(
        num_scalar_prefetch=2,
        grid=(n_blocks,),
        in_specs=[pl.BlockSpec((EXPERT_BLOCK, dp), lambda i, be, nu: (i, 0)),
                  pl.BlockSpec((1, d, 2 * D_EXPERT), lambda i, be, nu: (be[i], 0, 0)),
                  pl.BlockSpec((1, D_EXPERT, d), lambda i, be, nu: (be[i], 0, 0))],
        out_specs=pl.BlockSpec((EXPERT_BLOCK, dp), lambda i, be, nu: (i, 0)),
    )
    return pl.pallas_call(
        _expert_kernel,
        grid_spec=grid_spec,
        out_shape=jax.ShapeDtypeStruct((p_rows, dp), jnp.uint32),
        compiler_params=_cparams("arbitrary"),
        name="experts",
    )(blk_expert, n_used, xs, w_gu, w_down)


COMBINE_TILE = 256


def _combine_kernel(dest_ref, x1_ref, route_ref, g_ref, yb_ref, o_ref, buf, sem, *, final_norm):
    tm = x1_ref.shape[0]

    def start(t, _):
        for k in range(TOP_K):
            pltpu.make_async_copy(yb_ref.at[pl.ds(dest_ref[0, 0, TOP_K * t + k], 1), :],
                                  buf.at[k, pl.ds(t, 1), :], sem).start()
        return 0

    lax.fori_loop(0, tm, start, 0, unroll=ROW_DMA_UNROLL)
    pltpu.make_async_copy(buf, buf, sem).wait()
    route = route_ref[...]
    x2 = (x1_ref[...] + route[:, 2:3] * _unpack_rows(buf[0]) + route[:, 3:4] * _unpack_rows(buf[1]))
    o_ref[...] = _rms(x2, g_ref[...]) if final_norm else x2


def _combine(x1, route, dest, yb, final_g, final_norm):
    n, d = x1.shape
    tm = COMBINE_TILE
    dest3 = dest.reshape(n // tm, 1, TOP_K * tm)
    return pl.pallas_call(
        functools.partial(_combine_kernel, final_norm=final_norm),
        grid=(n // tm,),
        in_specs=[pl.BlockSpec((1, 1, TOP_K * tm), lambda i: (i, 0, 0), memory_space=pltpu.SMEM),
                  pl.BlockSpec((tm, d), lambda i: (i, 0)),
                  pl.BlockSpec((tm, ROUTE_W), lambda i: (i, 0)),
                  pl.BlockSpec((1, d), lambda i: (0, 0)),
                  pl.BlockSpec(memory_space=pl.ANY)],
        out_specs=pl.BlockSpec((tm, d), lambda i: (i, 0)),
        out_shape=jax.ShapeDtypeStruct((n, d), F32),
        scratch_shapes=[pltpu.VMEM((TOP_K, tm, d // 2), jnp.uint32), pltpu.SemaphoreType.DMA(())],
        compiler_params=_cparams("arbitrary"),
        name="combine",
    )(dest3, x1, route, final_g, yb)


def _rwkv_params(rw_mu, rw_w0, rw_w_up, rw_a0, rw_a_up, rw_g_up, rw_k_k, rw_k_a, rw_r_k, rw_gn_w, rw_gn_b):
    row = lambda v: v.reshape(1, -1).astype(F32)
    zeros = jnp.zeros((A_LORA, RW_DIM), F32)
    return {
        "mu_r": row(rw_mu[:RW_DIM]), "mu_k": row(rw_mu[RW_DIM:2 * RW_DIM]),
        "mu_v": row(rw_mu[2 * RW_DIM:3 * RW_DIM]), "mu_l": row(rw_mu[3 * RW_DIM:]),
        "w0": row(rw_w0), "a0": row(rw_a0), "k_k": row(rw_k_k), "k_a": row(rw_k_a),
        "r_k": row(rw_r_k), "gn_w": row(rw_gn_w), "gn_b": row(rw_gn_b),
        "w_up": jnp.concatenate([rw_w_up, zeros], axis=0).astype(F32),
        "a_up": jnp.concatenate([zeros, rw_a_up], axis=0).astype(BF16),
        "g_up": rw_g_up.astype(BF16),
    }


def _mla_params(g_qa, w_q_up, g_kva, w_kv_up):
    half = QK_ROPE // 2
    pad = MLA_SLOT - QK_NOPE - QK_ROPE
    wq = w_q_up.reshape(Q_LORA, MLA_HEADS, QK_NOPE + QK_ROPE)
    q_nope, q_r1, q_r2 = wq[..., :QK_NOPE], wq[..., QK_NOPE:QK_NOPE + half], wq[..., QK_NOPE + half:]
    zq = lambda w: jnp.zeros((Q_LORA, MLA_HEADS, w), F32)
    w_qa = jnp.concatenate([q_nope, q_r1, q_r2, zq(pad)], axis=-1).reshape(Q_LORA, -1)
    w_qb = jnp.concatenate([zq(QK_NOPE), -q_r2, q_r1, zq(pad)], axis=-1).reshape(Q_LORA, -1)
    wkv = w_kv_up.reshape(KV_LORA, MLA_HEADS, QK_NOPE + V_HEAD)
    w_k = jnp.concatenate([wkv[..., :QK_NOPE], jnp.zeros((KV_LORA, MLA_HEADS, MLA_SLOT - QK_NOPE), F32)],
                          axis=-1).reshape(KV_LORA, -1)
    w_v = wkv[..., QK_NOPE:].reshape(KV_LORA, -1)
    eye = jnp.eye(half, dtype=F32)
    z = jnp.zeros((half, half), F32)
    zl = jnp.zeros((QK_ROPE, QK_NOPE), F32)
    zr = jnp.zeros((QK_ROPE, pad), F32)
    p_a = jnp.concatenate([zl, jnp.concatenate([eye, z], 0), jnp.concatenate([z, eye], 0), zr], axis=1)
    p_b = jnp.concatenate([zl, jnp.concatenate([z, -eye], 0), jnp.concatenate([eye, z], 0), zr], axis=1)
    return {"g_qa": g_qa.reshape(1, -1), "g_kva": g_kva.reshape(1, -1),
            "w_qa": w_qa.astype(BF16), "w_qb": w_qb.astype(BF16), "w_k": w_k.astype(BF16),
            "w_v": w_v.astype(BF16), "p_a": p_a.astype(BF16), "p_b": p_b.astype(BF16)}


def _rope_tables(positions):
    half = QK_ROPE // 2
    inv_freq = ROPE_THETA ** (-jnp.arange(0, QK_ROPE, 2, dtype=F32) / QK_ROPE)
    ang = positions.astype(F32).reshape(-1, 1) * inv_freq
    cos, sin = jnp.cos(ang), jnp.sin(ang)
    n = ang.shape[0]
    pad = MLA_SLOT - QK_NOPE - QK_ROPE
    cos_t = jnp.concatenate([jnp.ones((n, QK_NOPE), F32), cos, cos, jnp.zeros((n, pad), F32)], axis=1)
    sin_t = jnp.concatenate([jnp.zeros((n, QK_NOPE), F32), sin, sin, jnp.zeros((n, pad), F32)], axis=1)
    del half
    return cos_t, sin_t


def _block_layout(hist, n_assign):
    tile_counts = hist[:, 0, N_GROUPS:N_GROUPS + N_EXPERTS].astype(jnp.int32)
    counts = jnp.sum(tile_counts, axis=0)
    padded = (counts + EXPERT_BLOCK - 1) // EXPERT_BLOCK * EXPERT_BLOCK
    pad_end = jnp.cumsum(padded)
    pad_start = pad_end - padded
    tile_base = jnp.cumsum(tile_counts, axis=0) - tile_counts + pad_start[None, :]
    base = jnp.pad(tile_base.astype(F32), ((0, 0), (0, ROUTE_W - N_EXPERTS)))[:, None, :]
    n_blocks = -(-n_assign // EXPERT_BLOCK) + N_EXPERTS
    blk_expert = jnp.minimum(
        jnp.searchsorted(pad_end, jnp.arange(n_blocks, dtype=jnp.int32) * EXPERT_BLOCK, side="right"),
        N_EXPERTS - 1).astype(jnp.int32)
    n_used = (pad_end[-1] // EXPERT_BLOCK).astype(jnp.int32).reshape(1)
    return base, blk_expert, n_used, n_blocks


def kernel(x, positions, mix_norm_g, w_in, rw_mu, rw_w0, rw_w_up, rw_a0, rw_a_up, rw_g_up, rw_k_k, rw_k_a, rw_r_k, rw_gn_w, rw_gn_b, mla_g_qa, mla_w_q_up, mla_g_kva, mla_w_kv_up, w_branch_rw, w_branch_mla, w_out, ffn_norm_g, moe_w_group, moe_b_group, moe_w_router, moe_b_router, moe_w_gu, moe_w_down, final_norm_g):
    batch, seq, d = x.shape
    n = batch * seq
    depth = w_in.shape[0]
    rw_cols = 3 * RW_DIM + W_LORA + A_LORA + G_LORA
    mla_cols = Q_LORA + KV_LORA + QK_ROPE
    cos_t, sin_t = _rope_tables(positions)
    x2 = x.reshape(n, d)

    for l in range(depth):
        wl = w_in[l].astype(BF16)
        c_rw, c_q, c_kvr, gates = _in_proj(
            x2, mix_norm_g[l].reshape(1, d), wl[:, :rw_cols], wl[:, rw_cols:rw_cols + Q_LORA],
            wl[:, rw_cols + Q_LORA:rw_cols + mla_cols], wl[:, rw_cols + mla_cols:], tm=256)

        rp = _rwkv_params(rw_mu[l], rw_w0[l], rw_w_up[l], rw_a0[l], rw_a_up[l], rw_g_up[l], rw_k_k[l],
                          rw_k_a[l], rw_r_k[l], rw_gn_w[l], rw_gn_b[l])
        y_rw = _rwkv(c_rw.reshape(batch, seq, rw_cols), rp, batch, seq).reshape(n, RW_DIM)

        mp = _mla_params(mla_g_qa[l], mla_w_q_up[l], mla_g_kva[l], mla_w_kv_up[l])
        q, k, v = _mla_prep(c_q, c_kvr, cos_t, sin_t, mp, tm=256)
        y_mla = _mla_attn(q.reshape(batch, seq, -1), k.reshape(batch, seq, -1),
                          v.reshape(batch, seq, -1), batch, seq).reshape(n, MLA_HEADS * V_HEAD)

        w_route = jnp.concatenate(
            [moe_w_group[l], moe_w_router[l], jnp.zeros((d, ROUTE_W - N_GROUPS - N_EXPERTS), F32)], axis=1)
        b_route = jnp.concatenate(
            [moe_b_group[l], moe_b_router[l], jnp.zeros((ROUTE_W - N_GROUPS - N_EXPERTS,), F32)]).reshape(1, -1)
        wr_hi = w_route.astype(BF16)
        wr_lo = (w_route - wr_hi.astype(F32)).astype(BF16)
        mparams = {"w_br": w_branch_rw[l].astype(BF16), "w_bm": w_branch_mla[l].astype(BF16),
                   "w_out": w_out[l].astype(BF16), "ffn_g": ffn_norm_g[l].reshape(1, d),
                   "wr_hi": wr_hi, "wr_lo": wr_lo, "b_route": b_route}
        x1, h2p, route, hist = _merge(x2, y_rw, y_mla, gates, mparams, tm=256)

        base, blk_expert, n_used, n_blocks = _block_layout(hist, n * TOP_K)
        dest = _plan(route, base, tm=256)[:, :TOP_K]
        xs = _dispatch(h2p, dest, jnp.zeros((n_blocks * EXPERT_BLOCK, d // 2), jnp.uint32))
        yb = _experts(xs, blk_expert, n_used, moe_w_gu[l].astype(BF16), moe_w_down[l].astype(BF16))
        x2 = _combine(x1, route, dest, yb, final_norm_g.reshape(1, d), final_norm=(l == depth - 1))

    return x2.reshape(batch, seq, d)
```

```python
import functools
import math

import jax
import jax.numpy as jnp
from jax import lax
from jax.experimental import pallas as pl
from jax.experimental.pallas import tpu as pltpu

F32 = jnp.float32
BF16 = jnp.bfloat16

RW_HEADS = 8
RW_HEAD_DIM = 64
RW_DIM = RW_HEADS * RW_HEAD_DIM
W_LORA = 64
A_LORA = 64
G_LORA = 128
GN_EPS = 64e-5
MLA_HEADS = 8
QK_NOPE = 64
QK_ROPE = 32
V_HEAD = 64
Q_LORA = 384
KV_LORA = 256
ROPE_THETA = 10000.0
N_GROUPS = 4
EXPERTS_PER_GROUP = 8
N_EXPERTS = N_GROUPS * EXPERTS_PER_GROUP
TOP_K = 2
D_EXPERT = 256
EXPERT_BLOCK = 256
NORM_EPS = 1e-6

LANES = 128
HEAD_PAIR = 2 * RW_HEAD_DIM
VMEM_LIMIT = 48 * 1024 * 1024


def _cparams(*sem):
    return pltpu.CompilerParams(dimension_semantics=sem, vmem_limit_bytes=VMEM_LIMIT)


def _mm(a, b, dims=((1,), (0,)), split=False):
    dn = (dims, ((), ()))
    dot = lambda x, y: lax.dot_general(x, y, dn, preferred_element_type=F32)
    a_hi = a.astype(BF16)
    b_hi = b.astype(BF16)
    if not split:
        return dot(a_hi, b_hi)
    a_lo = (a - a_hi.astype(F32)).astype(BF16)
    b_lo = (b - b_hi.astype(F32)).astype(BF16)
    return dot(a_hi, b_hi) + dot(a_lo, b_hi) + dot(a_hi, b_lo)


def _mm_sel(sel_bf16, x, dims=((1,), (0,))):
    dn = (dims, ((), ()))
    hi = x.astype(BF16)
    r1 = x - hi.astype(F32)
    mid = r1.astype(BF16)
    lo = (r1 - mid.astype(F32)).astype(BF16)
    out = lax.dot_general(sel_bf16, hi, dn, preferred_element_type=F32)
    out = out + lax.dot_general(sel_bf16, mid, dn, preferred_element_type=F32)
    return out + lax.dot_general(sel_bf16, lo, dn, preferred_element_type=F32)


def _seg_sum(x, seg_bf16):
    hi = x.astype(BF16)
    lo = (x - hi.astype(F32)).astype(BF16)
    return (jnp.dot(hi, seg_bf16, preferred_element_type=F32)
            + jnp.dot(lo, seg_bf16, preferred_element_type=F32))


def _rms(x, g):
    return x * lax.rsqrt(jnp.mean(x * x, axis=-1, keepdims=True) + NORM_EPS) * g


def _pack_rows(x):
    half = x.shape[1] // 2
    bits = lambda v: lax.bitcast_convert_type(v.astype(BF16).astype(F32), jnp.uint32)
    return bits(x[:, :half]) | (bits(x[:, half:]) >> 16)


def _unpack_rows(p):
    hi = lax.bitcast_convert_type(p & jnp.uint32(0xFFFF0000), F32)
    lo = lax.bitcast_convert_type(p << 16, F32)
    return jnp.concatenate([hi, lo], axis=1)


def _in_proj_kernel(x_ref, g_ref, wrw_ref, wq_ref, wkvr_ref, wg_ref,
                    crw_ref, cq_ref, ckvr_ref, gate_ref):
    hb = _rms(x_ref[...], g_ref[...]).astype(BF16)
    crw_ref[...] = jnp.dot(hb, wrw_ref[...], preferred_element_type=F32)
    cq_ref[...] = jnp.dot(hb, wq_ref[...], preferred_element_type=F32)
    ckvr_ref[...] = jnp.dot(hb, wkvr_ref[...], preferred_element_type=F32)
    gate_ref[...] = jax.nn.sigmoid(jnp.dot(hb, wg_ref[...], preferred_element_type=F32)).astype(BF16)


def _in_proj(x2, g, w_rw, w_q, w_kvr, w_gate, tm):
    n, d = x2.shape
    full = lambda w: pl.BlockSpec(w.shape, lambda i: (0, 0))
    row = lambda c: pl.BlockSpec((tm, c), lambda i: (i, 0))
    return pl.pallas_call(
        _in_proj_kernel,
        grid=(n // tm,),
        in_specs=[row(d), full(g), full(w_rw), full(w_q), full(w_kvr), full(w_gate)],
        out_specs=[row(w_rw.shape[1]), row(w_q.shape[1]), row(w_kvr.shape[1]), row(w_gate.shape[1])],
        out_shape=[jax.ShapeDtypeStruct((n, w_rw.shape[1]), F32),
                   jax.ShapeDtypeStruct((n, w_q.shape[1]), F32),
                   jax.ShapeDtypeStruct((n, w_kvr.shape[1]), F32),
                   jax.ShapeDtypeStruct((n, w_gate.shape[1]), BF16)],
        compiler_params=_cparams("parallel"),
        name="in_proj",
    )(x2, g, w_rw, w_q, w_kvr, w_gate)


RW_CHUNK = 64
RW_TILE = 512


def _token_shift(cur, halo_ref, first):
    prev_row = jnp.where(first, 0.0, halo_ref[0, 7:8, :])
    rolled = pltpu.roll(cur, 1, 0)
    row = lax.broadcasted_iota(jnp.int32, cur.shape, 0)
    return jnp.where(row == 0, prev_row, rolled)


def _rwkv_kernel(r_ref, k_ref, v_ref, l_ref, hr_ref, hk_ref, hv_ref, hl_ref,
                 mur_ref, muk_ref, muv_ref, mul_ref, w0_ref, a0_ref, kk_ref, ka_ref, rk_ref,
                 gnw_ref, gnb_ref, wup_ref, aup_ref, gup_ref, y_ref, st_ref):
    i = pl.program_id(2)
    first = i == 0

    @pl.when(first)
    def _():
        st_ref[...] = jnp.zeros_like(st_ref)

    def mixed(c_ref, h_ref, mu_ref):
        cur = c_ref[0]
        return cur + (_token_shift(cur, h_ref, first) - cur) * mu_ref[...]

    zr = mixed(r_ref, hr_ref, mur_ref)
    zk = mixed(k_ref, hk_ref, muk_ref)
    zv = mixed(v_ref, hv_ref, muv_ref)
    zl = mixed(l_ref, hl_ref, mul_ref)
    z_wa = zl[:, :LANES]
    z_g = zl[:, LANES:]

    lane = lax.broadcasted_iota(jnp.int32, (LANES, LANES), 1)
    sub = lax.broadcasted_iota(jnp.int32, (LANES, LANES), 0)
    same_head = (lane // RW_HEAD_DIM) == (sub // RW_HEAD_DIM)
    seg = jnp.where(same_head, 1.0, 0.0).astype(BF16)

    w = w0_ref[...] + _mm(jnp.tanh(z_wa), wup_ref[...], split=True)
    u = -w
    softplus = jnp.maximum(u, 0.0) + jnp.log(1.0 + jnp.exp(-jnp.abs(u)))
    log_decay = -jnp.exp(-softplus - 0.5)
    a = jax.nn.sigmoid(a0_ref[...] + _mm(z_wa, aup_ref[...]))
    g = _mm(jax.nn.sigmoid(z_g), gup_ref[...])

    kk = zk * kk_ref[...]
    kk = kk / jnp.maximum(jnp.sqrt(_seg_sum(kk * kk, seg)), 1e-12)
    k2 = zk * (1.0 + (a - 1.0) * ka_ref[...])
    bonus = _seg_sum(zr * k2 * rk_ref[...], seg) * zv
    kka = kk * a

    c = RW_CHUNK
    crow = lax.broadcasted_iota(jnp.int32, (c, c), 0)
    ccol = lax.broadcasted_iota(jnp.int32, (c, c), 1)
    cum_sel = jnp.where(crow >= ccol, 1.0, 0.0).astype(BF16)
    tril_incl = sub >= lane
    tril_strict = sub > lane
    eye_l = jnp.where(lane == sub, 1.0, 0.0).astype(F32)
    lo_half = lax.broadcasted_iota(jnp.int32, (c, LANES), 1) < RW_HEAD_DIM
    nt = ((1,), (1,))
    tn = ((0,), (0,))
    zeros_blk = jnp.zeros((2 * c, LANES), F32)

    def stack(t):
        return jnp.concatenate([jnp.where(lo_half, t, 0.0), jnp.where(lo_half, 0.0, t)], axis=0)

    chunks = range(RW_TILE // c)
    x_a, x_b, x_k, x_r, x_v, x_bh, x_kh, w_tot = [], [], [], [], [], [], [], []
    for ci in chunks:
        sl = slice(ci * c, (ci + 1) * c)
        ld = log_decay[sl]
        cum = _mm_sel(cum_sel, ld)
        tot = cum[c - 1:c, :]
        e_neg = jnp.exp(-cum)
        e_rest = jnp.exp(tot - cum)
        x_a.append(stack(-kk[sl] * jnp.exp(cum - ld)))
        x_b.append(stack(kka[sl] * e_neg))
        x_k.append(stack(k2[sl] * e_neg))
        x_r.append(stack(zr[sl] * jnp.exp(cum)))
        x_v.append(stack(zv[sl]))
        x_bh.append(stack(kka[sl] * e_rest))
        x_kh.append(stack(k2[sl] * e_rest))
        w_tot.append(jnp.exp(tot))

    inter = [_mm(jnp.concatenate([x_a[i], x_r[i]], axis=0), jnp.concatenate([x_b[i], x_k[i]], axis=0), nt)
             for i in chunks]
    tril_incl2 = jnp.concatenate([tril_incl, tril_incl], axis=1)
    a_ab = [jnp.where(tril_strict, m[:2 * c, :2 * c], 0.0) for m in inter]
    a_ak = [jnp.where(tril_strict, m[:2 * c, 2 * c:], 0.0) for m in inter]
    a_r = [jnp.where(tril_incl2, m[2 * c:], 0.0) for m in inter]
    w_ak = [_mm(a_ak[i], x_v[i]) for i in chunks]

    t_inv = [eye_l + m for m in a_ab]
    pw = a_ab
    for _ in range(int(math.log2(c)) - 1):
        pw = [_mm(m, m) for m in pw]
        t_inv = [t_inv[i] + _mm(t_inv[i], pw[i]) for i in chunks]

    solved = [_mm(t_inv[i], jnp.concatenate([x_a[i], w_ak[i]], axis=1)) for i in chunks]
    rhs = [jnp.concatenate([solved[i], jnp.concatenate([zeros_blk, x_v[i]], axis=1)], axis=0)
           for i in chunks]
    out = [_mm(a_r[i], rhs[i]) for i in chunks]
    carry = [_mm(jnp.concatenate([x_bh[i], x_kh[i]], axis=0), rhs[i], tn) for i in chunks]
    q_hat, y_loc = [], []
    for i in chunks:
        q_st = x_r[i] + out[i][:, :LANES]
        q_hat.append(q_st[:c] + q_st[c:])
        y_loc.append(out[i][:c, LANES:] + out[i][c:, LANES:])
    trans = [jnp.concatenate([eye_l * w_tot[i] + carry[i][:, :LANES], carry[i][:, LANES:]], axis=1)
             for i in chunks]

    zeros_sq = jnp.zeros((LANES, LANES), F32)

    def compose(later, earlier):
        return _mm(later[:, :LANES], earlier) + jnp.concatenate([zeros_sq, later[:, LANES:]], axis=1)

    scan = list(trans)
    dist = 1
    while dist < len(scan):
        scan = [scan[i] if i < dist else compose(scan[i], scan[i - dist]) for i in range(len(scan))]
        dist *= 2
    prefix = [None] + scan
    q_pre = [None] + [_mm(q_hat[i], prefix[i]) for i in chunks[1:]]
    q_m = jnp.concatenate([q_hat[0]] + [q_pre[i][:, :LANES] for i in chunks[1:]], axis=0)
    y_off = jnp.concatenate([y_loc[0]] + [y_loc[i] + q_pre[i][:, LANES:] for i in chunks[1:]], axis=0)
    state = st_ref[...]
    y = _mm(q_m, state) + y_off
    st_ref[...] = _mm(prefix[-1][:, :LANES], state) + prefix[-1][:, LANES:]

    inv_n = 1.0 / RW_HEAD_DIM
    mean = _seg_sum(y, seg) * inv_n
    d = y - mean
    var = _seg_sum(d * d, seg) * inv_n
    yn = d * lax.rsqrt(var + GN_EPS) * gnw_ref[...] + gnb_ref[...]
    y_ref[0] = ((yn + bonus) * g).astype(y_ref.dtype)


def _rwkv(c_rw, p, batch, seq):
    ts = RW_TILE
    n_pairs = RW_DIM // HEAD_PAIR
    lora_blk = (3 * RW_DIM) // (2 * LANES)
    halo = ts // 8

    def col(off):
        return pl.BlockSpec((1, ts, LANES), lambda b, pp, i, off=off: (b, i, off + pp))

    def col_halo(off):
        return pl.BlockSpec((1, 8, LANES),
                            lambda b, pp, i, off=off: (b, jnp.maximum(i * halo - 1, 0), off + pp))

    vec = pl.BlockSpec((1, LANES), lambda b, pp, i: (0, pp))
    lora_w = pl.BlockSpec((LANES, LANES), lambda b, pp, i: (0, pp))
    in_specs = [
        col(0), col(n_pairs), col(2 * n_pairs),
        pl.BlockSpec((1, ts, 2 * LANES), lambda b, pp, i: (b, i, lora_blk)),
        col_halo(0), col_halo(n_pairs), col_halo(2 * n_pairs),
        pl.BlockSpec((1, 8, 2 * LANES), lambda b, pp, i: (b, jnp.maximum(i * halo - 1, 0), lora_blk)),
        vec, vec, vec, pl.BlockSpec((1, 2 * LANES), lambda b, pp, i: (0, 0)),
        vec, vec, vec, vec, vec, vec, vec, lora_w, lora_w, lora_w,
    ]
    return pl.pallas_call(
        _rwkv_kernel,
        grid=(batch, n_pairs, seq // ts),
        in_specs=in_specs,
        out_specs=pl.BlockSpec((1, ts, LANES), lambda b, pp, i: (b, i, pp)),
        out_shape=jax.ShapeDtypeStruct((batch, seq, RW_DIM), BF16),
        scratch_shapes=[pltpu.VMEM((LANES, LANES), F32)],
        compiler_params=_cparams("parallel", "parallel", "arbitrary"),
        name="rwkv",
    )(c_rw, c_rw, c_rw, c_rw, c_rw, c_rw, c_rw, c_rw,
      p["mu_r"], p["mu_k"], p["mu_v"], p["mu_l"], p["w0"], p["a0"], p["k_k"], p["k_a"], p["r_k"],
      p["gn_w"], p["gn_b"], p["w_up"], p["a_up"], p["g_up"])


MLA_SLOT = 128


def _mla_prep_kernel(cq_ref, ckvr_ref, cos_ref, sin_ref, gq_ref, gkv_ref,
                     wqa_ref, wqb_ref, wk_ref, wv_ref, pa_ref, pb_ref,
                     q_ref, k_ref, v_ref):
    cos = cos_ref[...]
    sin = sin_ref[...]
    zq = _rms(cq_ref[...], gq_ref[...]).astype(BF16)
    qa = jnp.dot(zq, wqa_ref[...], preferred_element_type=F32)
    qb = jnp.dot(zq, wqb_ref[...], preferred_element_type=F32)
    ckvr = ckvr_ref[...]
    zkv = _rms(ckvr[:, :KV_LORA], gkv_ref[...]).astype(BF16)
    kn = jnp.dot(zkv, wk_ref[...], preferred_element_type=F32)
    v_ref[...] = jnp.dot(zkv, wv_ref[...], preferred_element_type=F32).astype(BF16)
    kr = ckvr[:, KV_LORA:].astype(BF16)
    k_rope = (jnp.dot(kr, pa_ref[...], preferred_element_type=F32) * cos
              + jnp.dot(kr, pb_ref[...], preferred_element_type=F32) * sin)
    scale = math.log2(math.e) / math.sqrt(QK_NOPE + QK_ROPE)
    for h in range(MLA_HEADS):
        sl = slice(h * MLA_SLOT, (h + 1) * MLA_SLOT)
        q_ref[:, sl] = ((qa[:, sl] * cos + qb[:, sl] * sin) * scale).astype(BF16)
        k_ref[:, sl] = (kn[:, sl] + k_rope).astype(BF16)


def _mla_prep(c_q, c_kvr, cos_t, sin_t, p, tm):
    n = c_q.shape[0]
    full = lambda w: pl.BlockSpec(w.shape, lambda i: (0, 0))
    row = lambda c: pl.BlockSpec((tm, c), lambda i: (i, 0))
    ws = [p["g_qa"], p["g_kva"], p["w_qa"], p["w_qb"], p["w_k"], p["w_v"], p["p_a"], p["p_b"]]
    hq = MLA_HEADS * MLA_SLOT
    return pl.pallas_call(
        _mla_prep_kernel,
        grid=(n // tm,),
        in_specs=[row(c_q.shape[1]), row(c_kvr.shape[1]), row(MLA_SLOT), row(MLA_SLOT)]
                 + [full(w) for w in ws],
        out_specs=[row(hq), row(hq), row(MLA_HEADS * V_HEAD)],
        out_shape=[jax.ShapeDtypeStruct((n, hq), BF16), jax.ShapeDtypeStruct((n, hq), BF16),
                   jax.ShapeDtypeStruct((n, MLA_HEADS * V_HEAD), BF16)],
        compiler_params=_cparams("parallel"),
        name="mla_prep",
    )(c_q, c_kvr, cos_t, sin_t, *ws)


ATT_TILE = 512


def _attn_kernel(q_ref, k_ref, v_ref, o_ref):
    qi = pl.program_id(2)
    t = ATT_TILE
    lo_v = lax.broadcasted_iota(jnp.int32, (t, 2 * V_HEAD), 1) < V_HEAD
    row = lax.broadcasted_iota(jnp.int32, (t, t), 0)
    col = lax.broadcasted_iota(jnp.int32, (t, t), 1)
    causal = col <= row
    nt = (((1,), (1,)), ((), ()))
    qs = [q_ref[0, :, h * MLA_SLOT:(h + 1) * MLA_SLOT] for h in range(2)]
    zero_v = jnp.zeros((t, 2 * V_HEAD), BF16)

    def block(j, carry, masked):
        stats, acc = carry
        rows = pl.ds(pl.multiple_of(j * t, t), t)
        vb = v_ref[0, rows, :]
        v_heads = (jnp.where(lo_v, vb, zero_v), jnp.where(lo_v, zero_v, vb))
        new_stats, alphas = [], []
        pv = None
        for h in range(2):
            m, l = stats[h]
            kb = k_ref[0, rows, h * MLA_SLOT:(h + 1) * MLA_SLOT]
            s = lax.dot_general(qs[h], kb, nt, preferred_element_type=F32)
            if masked:
                s = jnp.where(causal, s, -jnp.inf)
            m_new = jnp.maximum(m, jnp.max(s, axis=-1, keepdims=True))
            alpha = jnp.exp2(m - m_new)
            pr = jnp.exp2(s - m_new)
            new_stats.append((m_new, l * alpha + jnp.sum(pr, axis=-1, keepdims=True)))
            alphas.append(alpha)
            d = jnp.dot(pr.astype(BF16), v_heads[h], preferred_element_type=F32)
            pv = d if pv is None else pv + d
        acc = acc * jnp.where(lo_v, alphas[0], alphas[1]) + pv
        return tuple(new_stats), acc

    stat0 = (jnp.full((t, 1), -jnp.inf, F32), jnp.zeros((t, 1), F32))
    init = ((stat0, stat0), jnp.zeros((t, 2 * V_HEAD), F32))
    carry = lax.fori_loop(0, qi, lambda j, cr: block(j, cr, False), init)
    stats, acc = block(qi, carry, True)
    o_ref[0] = (acc / jnp.where(lo_v, stats[0][1], stats[1][1])).astype(o_ref.dtype)


def _mla_attn(q, k, v, batch, seq):
    t = ATT_TILE
    return pl.pallas_call(
        _attn_kernel,
        grid=(batch, MLA_HEADS // 2, seq // t),
        in_specs=[pl.BlockSpec((1, t, 2 * MLA_SLOT), lambda b, hp, i: (b, i, hp)),
                  pl.BlockSpec((1, seq, 2 * MLA_SLOT), lambda b, hp, i: (b, 0, hp)),
                  pl.BlockSpec((1, seq, 2 * V_HEAD), lambda b, hp, i: (b, 0, hp))],
        out_specs=pl.BlockSpec((1, t, 2 * V_HEAD), lambda b, hp, i: (b, i, hp)),
        out_shape=jax.ShapeDtypeStruct((batch, seq, MLA_HEADS * V_HEAD), BF16),
        compiler_params=_cparams("parallel", "parallel", "arbitrary"),
        name="mla_attn",
    )(q, k, v)


ROUTE_W = 128


def _merge_kernel(x_ref, yrw_ref, ymla_ref, gate_ref, wbr_ref, wbm_ref, wo_ref, fg_ref,
                  wr_hi_ref, wr_lo_ref, br_ref, x1_ref, h2p_ref, route_ref, hist_ref):
    d = x_ref.shape[1]
    a = jnp.dot(yrw_ref[...], wbr_ref[...], preferred_element_type=F32)
    b = jnp.dot(ymla_ref[...], wbm_ref[...], preferred_element_type=F32)
    merged = gate_ref[:, :d].astype(F32) * a + gate_ref[:, d:].astype(F32) * b
    x1 = x_ref[...] + jnp.dot(merged.astype(BF16), wo_ref[...], preferred_element_type=F32)
    x1_ref[...] = x1
    h2 = _rms(x1, fg_ref[...])
    h2p_ref[...] = _pack_rows(h2)

    h_hi = h2.astype(BF16)
    h_lo = (h2 - h_hi.astype(F32)).astype(BF16)
    logits = (jnp.dot(h_hi, wr_hi_ref[...], preferred_element_type=F32)
              + jnp.dot(h_lo, wr_hi_ref[...], preferred_element_type=F32)
              + jnp.dot(h_hi, wr_lo_ref[...], preferred_element_type=F32)) + br_ref[...]

    lane = lax.broadcasted_iota(jnp.int32, logits.shape, 1)
    big = jnp.int32(ROUTE_W)
    neg = -jnp.inf

    def first_argmax(vals, vmax):
        return jnp.min(jnp.where(vals == vmax, lane, big), axis=-1, keepdims=True)

    grp = jnp.where(lane < N_GROUPS, logits, neg)
    g_max = jnp.max(grp, axis=-1, keepdims=True)
    g_den = jnp.sum(jnp.exp(grp - g_max), axis=-1, keepdims=True)
    g_sel = first_argmax(grp, g_max)
    gate_g = 1.0 / g_den
    lo = N_GROUPS + g_sel * EXPERTS_PER_GROUP
    fine = jnp.where((lane >= lo) & (lane < lo + EXPERTS_PER_GROUP), logits, neg)
    v1 = jnp.max(fine, axis=-1, keepdims=True)
    i1 = first_argmax(fine, v1)
    fine2 = jnp.where(lane == i1, neg, fine)
    v2 = jnp.max(fine2, axis=-1, keepdims=True)
    i2 = first_argmax(fine2, v2)
    e2 = jnp.exp(v2 - v1)
    den = 1.0 + e2
    w1 = gate_g / den
    w2 = gate_g * e2 / den
    route = jnp.where(lane == 0, (i1 - N_GROUPS).astype(F32),
                      jnp.where(lane == 1, (i2 - N_GROUPS).astype(F32),
                                jnp.where(lane == 2, w1, jnp.where(lane == 3, w2, 0.0))))
    route_ref[...] = route
    chosen = jnp.where((lane == i1) | (lane == i2), 1.0, 0.0)
    hist_ref[0] = jnp.broadcast_to(jnp.sum(chosen, axis=0, keepdims=True), hist_ref.shape[1:])


def _merge(x2, y_rw, y_mla, gates, p, tm):
    n, d = x2.shape
    full = lambda w: pl.BlockSpec(w.shape, lambda i: (0, 0))
    row = lambda c: pl.BlockSpec((tm, c), lambda i: (i, 0))
    ws = [p["w_br"], p["w_bm"], p["w_out"], p["ffn_g"], p["wr_hi"], p["wr_lo"], p["b_route"]]
    return pl.pallas_call(
        _merge_kernel,
        grid=(n // tm,),
        in_specs=[row(d), row(y_rw.shape[1]), row(y_mla.shape[1]), row(2 * d)] + [full(w) for w in ws],
        out_specs=[row(d), row(d // 2), row(ROUTE_W), pl.BlockSpec((1, 8, ROUTE_W), lambda i: (i, 0, 0))],
        out_shape=[jax.ShapeDtypeStruct((n, d), F32), jax.ShapeDtypeStruct((n, d // 2), jnp.uint32),
                   jax.ShapeDtypeStruct((n, ROUTE_W), F32),
                   jax.ShapeDtypeStruct((n // tm, 8, ROUTE_W), F32)],
        compiler_params=_cparams("parallel"),
        name="merge_route",
    )(x2, y_rw, y_mla, gates, *ws)


def _plan_kernel(route_ref, base_ref, dest_ref):
    tm = route_ref.shape[0]
    route = route_ref[...]
    lane = lax.broadcasted_iota(jnp.int32, route.shape, 1).astype(F32)
    pick = [lane == route[:, k:k + 1] for k in range(TOP_K)]
    both = jnp.where(pick[0] | pick[1], 1.0, 0.0).astype(BF16)
    r = lax.broadcasted_iota(jnp.int32, (tm, tm), 0)
    c = lax.broadcasted_iota(jnp.int32, (tm, tm), 1)
    earlier = jnp.where(r > c, 1.0, 0.0).astype(BF16)
    offs = jnp.dot(earlier, both, preferred_element_type=F32) + base_ref[0]
    rows = [jnp.sum(jnp.where(pk, offs, 0.0), axis=-1, keepdims=True) for pk in pick]
    dest_ref[...] = jnp.where(lane == 0.0, rows[0], jnp.where(lane == 1.0, rows[1], 0.0)).astype(jnp.int32)


def _plan(route, base, tm):
    n = route.shape[0]
    return pl.pallas_call(
        _plan_kernel,
        grid=(n // tm,),
        in_specs=[pl.BlockSpec((tm, ROUTE_W), lambda i: (i, 0)),
                  pl.BlockSpec((1, 1, ROUTE_W), lambda i: (i, 0, 0))],
        out_specs=pl.BlockSpec((tm, ROUTE_W), lambda i: (i, 0)),
        out_shape=jax.ShapeDtypeStruct((n, ROUTE_W), jnp.int32),
        compiler_params=_cparams("parallel"),
        name="route_plan",
    )(route, base)


DISPATCH_TILE = 256
ROW_DMA_UNROLL = 8


def _dispatch_kernel(dest_ref, h_ref, zero_ref, xs_ref, sem):
    del zero_ref
    tm = h_ref.shape[0]

    def start(t, _):
        for k in range(TOP_K):
            pltpu.make_async_copy(h_ref.at[pl.ds(t, 1), :],
                                  xs_ref.at[pl.ds(dest_ref[0, 0, TOP_K * t + k], 1), :], sem).start()
        return 0

    lax.fori_loop(0, tm, start, 0, unroll=ROW_DMA_UNROLL)
    all_rows = xs_ref.at[pl.ds(0, TOP_K * tm), :]
    pltpu.make_async_copy(all_rows, all_rows, sem).wait()


def _dispatch(h2, dest, xs_init):
    n, d = h2.shape
    tm = DISPATCH_TILE
    dest3 = dest.reshape(n // tm, 1, TOP_K * tm)
    return pl.pallas_call(
        _dispatch_kernel,
        grid=(n // tm,),
        in_specs=[pl.BlockSpec((1, 1, TOP_K * tm), lambda i: (i, 0, 0), memory_space=pltpu.SMEM),
                  pl.BlockSpec((tm, d), lambda i: (i, 0)),
                  pl.BlockSpec(memory_space=pl.ANY)],
        out_specs=pl.BlockSpec(memory_space=pl.ANY),
        out_shape=jax.ShapeDtypeStruct(xs_init.shape, xs_init.dtype),
        scratch_shapes=[pltpu.SemaphoreType.DMA(())],
        input_output_aliases={2: 0},
        compiler_params=_cparams("arbitrary"),
        name="dispatch",
    )(dest3, h2, xs_init)


def _expert_kernel(blk_e_ref, n_used_ref, x_ref, wgu_ref, wd_ref, y_ref):
    del blk_e_ref

    @pl.when(pl.program_id(0) < n_used_ref[0])
    def _():
        h = jnp.dot(_unpack_rows(x_ref[...]).astype(BF16), wgu_ref[0], preferred_element_type=F32)
        gt = h[:, :D_EXPERT]
        up = h[:, D_EXPERT:]
        act = (gt * jax.nn.sigmoid(gt) * up).astype(BF16)
        y_ref[...] = _pack_rows(jnp.dot(act, wd_ref[0], preferred_element_type=F32))

    @pl.when(pl.program_id(0) >= n_used_ref[0])
    def _():
        y_ref[...] = jnp.zeros_like(y_ref)


def _experts(xs, blk_expert, n_used, w_gu, w_down):
    p_rows, dp = xs.shape
    d = 2 * dp
    n_blocks = p_rows // EXPERT_BLOCK
    grid_spec = pltpu.PrefetchScalarGridSpec(
        num_scalar_prefetch=2,
        grid=(n_blocks,),
        in_specs=[pl.BlockSpec((EXPERT_BLOCK, dp), lambda i, be, nu: (i, 0)),
                  pl.BlockSpec((1, d, 2 * D_EXPERT), lambda i, be, nu: (be[i], 0, 0)),
                  pl.BlockSpec((1, D_EXPERT, d), lambda i, be, nu: (be[i], 0, 0))],
        out_specs=pl.BlockSpec((EXPERT_BLOCK, dp), lambda i, be, nu: (i, 0)),
    )
    return pl.pallas_call(
        _expert_kernel,
        grid_spec=grid_spec,
        out_shape=jax.ShapeDtypeStruct((p_rows, dp), jnp.uint32),
        compiler_params=_cparams("arbitrary"),
        name="experts",
    )(blk_expert, n_used, xs, w_gu, w_down)


COMBINE_TILE = 256


def _combine_kernel(dest_ref, x1_ref, route_ref, g_ref, yb_ref, o_ref, buf, sem, *, final_norm):
    tm = x1_ref.shape[0]

    def start(t, _):
        for k in range(TOP_K):
            pltpu.make_async_copy(yb_ref.at[pl.ds(dest_ref[0, 0, TOP_K * t + k], 1), :],
                                  buf.at[k, pl.ds(t, 1), :], sem).start()
        return 0

    lax.fori_loop(0, tm, start, 0, unroll=ROW_DMA_UNROLL)
    pltpu.make_async_copy(buf, buf, sem).wait()
    route = route_ref[...]
    x2 = (x1_ref[...] + route[:, 2:3] * _unpack_rows(buf[0]) + route[:, 3:4] * _unpack_rows(buf[1]))
    o_ref[...] = _rms(x2, g_ref[...]) if final_norm else x2


def _combine(x1, route, dest, yb, final_g, final_norm):
    n, d = x1.shape
    tm = COMBINE_TILE
    dest3 = dest.reshape(n // tm, 1, TOP_K * tm)
    return pl.pallas_call(
        functools.partial(_combine_kernel, final_norm=final_norm),
        grid=(n // tm,),
        in_specs=[pl.BlockSpec((1, 1, TOP_K * tm), lambda i: (i, 0, 0), memory_space=pltpu.SMEM),
                  pl.BlockSpec((tm, d), lambda i: (i, 0)),
                  pl.BlockSpec((tm, ROUTE_W), lambda i: (i, 0)),
                  pl.BlockSpec((1, d), lambda i: (0, 0)),
                  pl.BlockSpec(memory_space=pl.ANY)],
        out_specs=pl.BlockSpec((tm, d), lambda i: (i, 0)),
        out_shape=jax.ShapeDtypeStruct((n, d), F32),
        scratch_shapes=[pltpu.VMEM((TOP_K, tm, d // 2), jnp.uint32), pltpu.SemaphoreType.DMA(())],
        compiler_params=_cparams("arbitrary"),
        name="combine",
    )(dest3, x1, route, final_g, yb)


def _rwkv_params(rw_mu, rw_w0, rw_w_up, rw_a0, rw_a_up, rw_g_up, rw_k_k, rw_k_a, rw_r_k, rw_gn_w, rw_gn_b):
    row = lambda v: v.reshape(1, -1).astype(F32)
    zeros = jnp.zeros((A_LORA, RW_DIM), F32)
    return {
        "mu_r": row(rw_mu[:RW_DIM]), "mu_k": row(rw_mu[RW_DIM:2 * RW_DIM]),
        "mu_v": row(rw_mu[2 * RW_DIM:3 * RW_DIM]), "mu_l": row(rw_mu[3 * RW_DIM:]),
        "w0": row(rw_w0), "a0": row(rw_a0), "k_k": row(rw_k_k), "k_a": row(rw_k_a),
        "r_k": row(rw_r_k), "gn_w": row(rw_gn_w), "gn_b": row(rw_gn_b),
        "w_up": jnp.concatenate([rw_w_up, zeros], axis=0).astype(F32),
        "a_up": jnp.concatenate([zeros, rw_a_up], axis=0).astype(BF16),
        "g_up": rw_g_up.astype(BF16),
    }


def _mla_params(g_qa, w_q_up, g_kva, w_kv_up):
    half = QK_ROPE // 2
    pad = MLA_SLOT - QK_NOPE - QK_ROPE
    wq = w_q_up.reshape(Q_LORA, MLA_HEADS, QK_NOPE + QK_ROPE)
    q_nope, q_r1, q_r2 = wq[..., :QK_NOPE], wq[..., QK_NOPE:QK_NOPE + half], wq[..., QK_NOPE + half:]
    zq = lambda w: jnp.zeros((Q_LORA, MLA_HEADS, w), F32)
    w_qa = jnp.concatenate([q_nope, q_r1, q_r2, zq(pad)], axis=-1).reshape(Q_LORA, -1)
    w_qb = jnp.concatenate([zq(QK_NOPE), -q_r2, q_r1, zq(pad)], axis=-1).reshape(Q_LORA, -1)
    wkv = w_kv_up.reshape(KV_LORA, MLA_HEADS, QK_NOPE + V_HEAD)
    w_k = jnp.concatenate([wkv[..., :QK_NOPE], jnp.zeros((KV_LORA, MLA_HEADS, MLA_SLOT - QK_NOPE), F32)],
                          axis=-1).reshape(KV_LORA, -1)
    w_v = wkv[..., QK_NOPE:].reshape(KV_LORA, -1)
    eye = jnp.eye(half, dtype=F32)
    z = jnp.zeros((half, half), F32)
    zl = jnp.zeros((QK_ROPE, QK_NOPE), F32)
    zr = jnp.zeros((QK_ROPE, pad), F32)
    p_a = jnp.concatenate([zl, jnp.concatenate([eye, z], 0), jnp.concatenate([z, eye], 0), zr], axis=1)
    p_b = jnp.concatenate([zl, jnp.concatenate([z, -eye], 0), jnp.concatenate([eye, z], 0), zr], axis=1)
    return {"g_qa": g_qa.reshape(1, -1), "g_kva": g_kva.reshape(1, -1),
            "w_qa": w_qa.astype(BF16), "w_qb": w_qb.astype(BF16), "w_k": w_k.astype(BF16),
            "w_v": w_v.astype(BF16), "p_a": p_a.astype(BF16), "p_b": p_b.astype(BF16)}


def _rope_tables(positions):
    half = QK_ROPE // 2
    inv_freq = ROPE_THETA ** (-jnp.arange(0, QK_ROPE, 2, dtype=F32) / QK_ROPE)
    ang = positions.astype(F32).reshape(-1, 1) * inv_freq
    cos, sin = jnp.cos(ang), jnp.sin(ang)
    n = ang.shape[0]
    pad = MLA_SLOT - QK_NOPE - QK_ROPE
    cos_t = jnp.concatenate([jnp.ones((n, QK_NOPE), F32), cos, cos, jnp.zeros((n, pad), F32)], axis=1)
    sin_t = jnp.concatenate([jnp.zeros((n, QK_NOPE), F32), sin, sin, jnp.zeros((n, pad), F32)], axis=1)
    del half
    return cos_t, sin_t


def _block_layout(hist, n_assign):
    tile_counts = hist[:, 0, N_GROUPS:N_GROUPS + N_EXPERTS].astype(jnp.int32)
    counts = jnp.sum(tile_counts, axis=0)
    padded = (counts + EXPERT_BLOCK - 1) // EXPERT_BLOCK * EXPERT_BLOCK
    pad_end = jnp.cumsum(padded)
    pad_start = pad_end - padded
    tile_base = jnp.cumsum(tile_counts, axis=0) - tile_counts + pad_start[None, :]
    base = jnp.pad(tile_base.astype(F32), ((0, 0), (0, ROUTE_W - N_EXPERTS)))[:, None, :]
    n_blocks = -(-n_assign // EXPERT_BLOCK) + N_EXPERTS
    blk_row = jnp.arange(n_blocks, dtype=jnp.int32) * EXPERT_BLOCK
    blk_expert = jnp.minimum(jnp.sum((pad_end[None, :] <= blk_row[:, None]).astype(jnp.int32), axis=1),
                             N_EXPERTS - 1)
    n_used = (pad_end[-1] // EXPERT_BLOCK).astype(jnp.int32).reshape(1)
    return base, blk_expert, n_used, n_blocks


def kernel(x, positions, mix_norm_g, w_in, rw_mu, rw_w0, rw_w_up, rw_a0, rw_a_up, rw_g_up, rw_k_k, rw_k_a, rw_r_k, rw_gn_w, rw_gn_b, mla_g_qa, mla_w_q_up, mla_g_kva, mla_w_kv_up, w_branch_rw, w_branch_mla, w_out, ffn_norm_g, moe_w_group, moe_b_group, moe_w_router, moe_b_router, moe_w_gu, moe_w_down, final_norm_g):
    batch, seq, d = x.shape
    n = batch * seq
    depth = w_in.shape[0]
    rw_cols = 3 * RW_DIM + W_LORA + A_LORA + G_LORA
    mla_cols = Q_LORA + KV_LORA + QK_ROPE
    cos_t, sin_t = _rope_tables(positions)
    x2 = x.reshape(n, d)

    for l in range(depth):
        wl = w_in[l].astype(BF16)
        c_rw, c_q, c_kvr, gates = _in_proj(
            x2, mix_norm_g[l].reshape(1, d), wl[:, :rw_cols], wl[:, rw_cols:rw_cols + Q_LORA],
            wl[:, rw_cols + Q_LORA:rw_cols + mla_cols], wl[:, rw_cols + mla_cols:], tm=256)

        rp = _rwkv_params(rw_mu[l], rw_w0[l], rw_w_up[l], rw_a0[l], rw_a_up[l], rw_g_up[l], rw_k_k[l],
                          rw_k_a[l], rw_r_k[l], rw_gn_w[l], rw_gn_b[l])
        y_rw = _rwkv(c_rw.reshape(batch, seq, rw_cols), rp, batch, seq).reshape(n, RW_DIM)

        mp = _mla_params(mla_g_qa[l], mla_w_q_up[l], mla_g_kva[l], mla_w_kv_up[l])
        q, k, v = _mla_prep(c_q, c_kvr, cos_t, sin_t, mp, tm=256)
        y_mla = _mla_attn(q.reshape(batch, seq, -1), k.reshape(batch, seq, -1),
                          v.reshape(batch, seq, -1), batch, seq).reshape(n, MLA_HEADS * V_HEAD)

        w_route = jnp.concatenate(
            [moe_w_group[l], moe_w_router[l], jnp.zeros((d, ROUTE_W - N_GROUPS - N_EXPERTS), F32)], axis=1)
        b_route = jnp.concatenate(
            [moe_b_group[l], moe_b_router[l], jnp.zeros((ROUTE_W - N_GROUPS - N_EXPERTS,), F32)]).reshape(1, -1)
        wr_hi = w_route.astype(BF16)
        wr_lo = (w_route - wr_hi.astype(F32)).astype(BF16)
        mparams = {"w_br": w_branch_rw[l].astype(BF16), "w_bm": w_branch_mla[l].astype(BF16),
                   "w_out": w_out[l].astype(BF16), "ffn_g": ffn_norm_g[l].reshape(1, d),
                   "wr_hi": wr_hi, "wr_lo": wr_lo, "b_route": b_route}
        x1, h2p, route, hist = _merge(x2, y_rw, y_mla, gates, mparams, tm=256)

        base, blk_expert, n_used, n_blocks = _block_layout(hist, n * TOP_K)
        dest = _plan(route, base, tm=256)[:, :TOP_K]
        xs = _dispatch(h2p, dest, jnp.zeros((n_blocks * EXPERT_BLOCK, d // 2), jnp.uint32))
        yb = _experts(xs, blk_expert, n_used, moe_w_gu[l].astype(BF16), moe_w_down[l].astype(BF16))
        x2 = _combine(x1, route, dest, yb, final_norm_g.reshape(1, d), final_norm=(l == depth - 1))

    return x2.reshape(batch, seq, d)
```

```python
import functools
import math

import jax
import jax.numpy as jnp
from jax import lax
from jax.experimental import pallas as pl
from jax.experimental.pallas import tpu as pltpu

F32 = jnp.float32
BF16 = jnp.bfloat16

RW_HEADS = 8
RW_HEAD_DIM = 64
RW_DIM = RW_HEADS * RW_HEAD_DIM
W_LORA = 64
A_LORA = 64
G_LORA = 128
GN_EPS = 64e-5
MLA_HEADS = 8
QK_NOPE = 64
QK_ROPE = 32
V_HEAD = 64
Q_LORA = 384
KV_LORA = 256
ROPE_THETA = 10000.0
N_GROUPS = 4
EXPERTS_PER_GROUP = 8
N_EXPERTS = N_GROUPS * EXPERTS_PER_GROUP
TOP_K = 2
D_EXPERT = 256
EXPERT_BLOCK = 256
NORM_EPS = 1e-6

LANES = 128
HEAD_PAIR = 2 * RW_HEAD_DIM
VMEM_LIMIT = 48 * 1024 * 1024
ROW_TILE = 512


def _cparams(*sem):
    return pltpu.CompilerParams(dimension_semantics=sem, vmem_limit_bytes=VMEM_LIMIT)


def _mm(a, b, dims=((1,), (0,)), split=False):
    dn = (dims, ((), ()))
    dot = lambda x, y: lax.dot_general(x, y, dn, preferred_element_type=F32)
    a_hi = a.astype(BF16)
    b_hi = b.astype(BF16)
    if not split:
        return dot(a_hi, b_hi)
    a_lo = (a - a_hi.astype(F32)).astype(BF16)
    b_lo = (b - b_hi.astype(F32)).astype(BF16)
    return dot(a_hi, b_hi) + dot(a_lo, b_hi) + dot(a_hi, b_lo)


def _mm_sel(sel_bf16, x, dims=((1,), (0,))):
    dn = (dims, ((), ()))
    hi = x.astype(BF16)
    r1 = x - hi.astype(F32)
    mid = r1.astype(BF16)
    lo = (r1 - mid.astype(F32)).astype(BF16)
    out = lax.dot_general(sel_bf16, hi, dn, preferred_element_type=F32)
    out = out + lax.dot_general(sel_bf16, mid, dn, preferred_element_type=F32)
    return out + lax.dot_general(sel_bf16, lo, dn, preferred_element_type=F32)


def _seg_sum(x, seg_bf16):
    hi = x.astype(BF16)
    lo = (x - hi.astype(F32)).astype(BF16)
    return (jnp.dot(hi, seg_bf16, preferred_element_type=F32)
            + jnp.dot(lo, seg_bf16, preferred_element_type=F32))


def _rms(x, g):
    return x * lax.rsqrt(jnp.mean(x * x, axis=-1, keepdims=True) + NORM_EPS) * g


def _pack_rows(x):
    half = x.shape[1] // 2
    bits = lambda v: lax.bitcast_convert_type(v.astype(BF16).astype(F32), jnp.uint32)
    return bits(x[:, :half]) | (bits(x[:, half:]) >> 16)


def _unpack_rows(p):
    hi = lax.bitcast_convert_type(p & jnp.uint32(0xFFFF0000), F32)
    lo = lax.bitcast_convert_type(p << 16, F32)
    return jnp.concatenate([hi, lo], axis=1)


SLABS = 4


def _slab_rows(ref, r):
    return ref.at[pl.ds(pl.multiple_of(r * SLABS, SLABS), SLABS), :]


def _slab_load(ref):
    rows = ref.shape[0] // SLABS
    return jnp.concatenate([ref[pl.ds(j, rows, stride=SLABS), :] for j in range(SLABS)], axis=1)


def _slab_store(ref, x):
    rows = ref.shape[0] // SLABS
    for j in range(SLABS):
        ref[pl.ds(j, rows, stride=SLABS), :] = x[:, j * LANES:(j + 1) * LANES]


def _in_proj_kernel(x_ref, g_ref, wrw_ref, wq_ref, wkvr_ref, wg_ref,
                    crw_ref, cq_ref, ckvr_ref, gate_ref):
    hb = _rms(x_ref[...], g_ref[...]).astype(BF16)
    crw_ref[...] = jnp.dot(hb, wrw_ref[...], preferred_element_type=F32)
    cq_ref[...] = jnp.dot(hb, wq_ref[...], preferred_element_type=F32)
    ckvr_ref[...] = jnp.dot(hb, wkvr_ref[...], preferred_element_type=F32)
    gate_ref[...] = jax.nn.sigmoid(jnp.dot(hb, wg_ref[...], preferred_element_type=F32)).astype(BF16)


def _in_proj(x2, g, w_rw, w_q, w_kvr, w_gate, tm):
    n, d = x2.shape
    full = lambda w: pl.BlockSpec(w.shape, lambda i: (0, 0))
    row = lambda c: pl.BlockSpec((tm, c), lambda i: (i, 0))
    return pl.pallas_call(
        _in_proj_kernel,
        grid=(n // tm,),
        in_specs=[row(d), full(g), full(w_rw), full(w_q), full(w_kvr), full(w_gate)],
        out_specs=[row(w_rw.shape[1]), row(w_q.shape[1]), row(w_kvr.shape[1]), row(w_gate.shape[1])],
        out_shape=[jax.ShapeDtypeStruct((n, w_rw.shape[1]), F32),
                   jax.ShapeDtypeStruct((n, w_q.shape[1]), F32),
                   jax.ShapeDtypeStruct((n, w_kvr.shape[1]), F32),
                   jax.ShapeDtypeStruct((n, w_gate.shape[1]), BF16)],
        compiler_params=_cparams("parallel"),
        name="in_proj",
    )(x2, g, w_rw, w_q, w_kvr, w_gate)


RW_CHUNK = 64
RW_TILE = 512


def _token_shift(cur, halo_ref, first):
    prev_row = jnp.where(first, 0.0, halo_ref[0, 7:8, :])
    rolled = pltpu.roll(cur, 1, 0)
    row = lax.broadcasted_iota(jnp.int32, cur.shape, 0)
    return jnp.where(row == 0, prev_row, rolled)


def _rwkv_kernel(r_ref, k_ref, v_ref, l_ref, hr_ref, hk_ref, hv_ref, hl_ref,
                 mur_ref, muk_ref, muv_ref, mul_ref, w0_ref, a0_ref, kk_ref, ka_ref, rk_ref,
                 gnw_ref, gnb_ref, wup_ref, aup_ref, gup_ref, y_ref, st_ref):
    i = pl.program_id(2)
    first = i == 0

    @pl.when(first)
    def _():
        st_ref[...] = jnp.zeros_like(st_ref)

    def mixed(c_ref, h_ref, mu_ref):
        cur = c_ref[0]
        return cur + (_token_shift(cur, h_ref, first) - cur) * mu_ref[...]

    zr = mixed(r_ref, hr_ref, mur_ref)
    zk = mixed(k_ref, hk_ref, muk_ref)
    zv = mixed(v_ref, hv_ref, muv_ref)
    zl = mixed(l_ref, hl_ref, mul_ref)
    z_wa = zl[:, :LANES]
    z_g = zl[:, LANES:]

    lane = lax.broadcasted_iota(jnp.int32, (LANES, LANES), 1)
    sub = lax.broadcasted_iota(jnp.int32, (LANES, LANES), 0)
    same_head = (lane // RW_HEAD_DIM) == (sub // RW_HEAD_DIM)
    seg = jnp.where(same_head, 1.0, 0.0).astype(BF16)

    w = w0_ref[...] + _mm(jnp.tanh(z_wa), wup_ref[...], split=True)
    u = -w
    softplus = jnp.maximum(u, 0.0) + jnp.log(1.0 + jnp.exp(-jnp.abs(u)))
    log_decay = -jnp.exp(-softplus - 0.5)
    a = jax.nn.sigmoid(a0_ref[...] + _mm(z_wa, aup_ref[...]))
    g = _mm(jax.nn.sigmoid(z_g), gup_ref[...])

    kk = zk * kk_ref[...]
    kk = kk / jnp.maximum(jnp.sqrt(_seg_sum(kk * kk, seg)), 1e-12)
    k2 = zk * (1.0 + (a - 1.0) * ka_ref[...])
    bonus = _seg_sum(zr * k2 * rk_ref[...], seg) * zv
    kka = kk * a

    c = RW_CHUNK
    crow = lax.broadcasted_iota(jnp.int32, (c, c), 0)
    ccol = lax.broadcasted_iota(jnp.int32, (c, c), 1)
    cum_sel = jnp.where(crow >= ccol, 1.0, 0.0).astype(BF16)
    tril_incl = sub >= lane
    tril_strict = sub > lane
    eye_l = jnp.where(lane == sub, 1.0, 0.0).astype(F32)
    lo_half = lax.broadcasted_iota(jnp.int32, (c, LANES), 1) < RW_HEAD_DIM
    nt = ((1,), (1,))
    tn = ((0,), (0,))
    zeros_blk = jnp.zeros((2 * c, LANES), F32)

    def stack(t):
        return jnp.concatenate([jnp.where(lo_half, t, 0.0), jnp.where(lo_half, 0.0, t)], axis=0)

    chunks = range(RW_TILE // c)
    x_a, x_b, x_k, x_r, x_v, x_bh, x_kh, w_tot = [], [], [], [], [], [], [], []
    for ci in chunks:
        sl = slice(ci * c, (ci + 1) * c)
        ld = log_decay[sl]
        cum = _mm_sel(cum_sel, ld)
        tot = cum[c - 1:c, :]
        e_neg = jnp.exp(-cum)
        e_rest = jnp.exp(tot - cum)
        x_a.append(stack(-kk[sl] * jnp.exp(cum - ld)))
        x_b.append(stack(kka[sl] * e_neg))
        x_k.append(stack(k2[sl] * e_neg))
        x_r.append(stack(zr[sl] * jnp.exp(cum)))
        x_v.append(stack(zv[sl]))
        x_bh.append(stack(kka[sl] * e_rest))
        x_kh.append(stack(k2[sl] * e_rest))
        w_tot.append(jnp.exp(tot))

    inter = [_mm(jnp.concatenate([x_a[i], x_r[i]], axis=0), jnp.concatenate([x_b[i], x_k[i]], axis=0), nt)
             for i in chunks]
    tril_incl2 = jnp.concatenate([tril_incl, tril_incl], axis=1)
    a_ab = [jnp.where(tril_strict, m[:2 * c, :2 * c], 0.0) for m in inter]
    a_ak = [jnp.where(tril_strict, m[:2 * c, 2 * c:], 0.0) for m in inter]
    a_r = [jnp.where(tril_incl2, m[2 * c:], 0.0) for m in inter]
    w_ak = [_mm(a_ak[i], x_v[i]) for i in chunks]

    t_inv = [eye_l + m for m in a_ab]
    pw = a_ab
    for _ in range(int(math.log2(c)) - 1):
        pw = [_mm(m, m) for m in pw]
        t_inv = [t_inv[i] + _mm(t_inv[i], pw[i]) for i in chunks]

    solved = [_mm(t_inv[i], jnp.concatenate([x_a[i], w_ak[i]], axis=1)) for i in chunks]
    rhs = [jnp.concatenate([solved[i], jnp.concatenate([zeros_blk, x_v[i]], axis=1)], axis=0)
           for i in chunks]
    out = [_mm(a_r[i], rhs[i]) for i in chunks]
    carry = [_mm(jnp.concatenate([x_bh[i], x_kh[i]], axis=0), rhs[i], tn) for i in chunks]
    q_hat, y_loc = [], []
    for i in chunks:
        q_st = x_r[i] + out[i][:, :LANES]
        q_hat.append(q_st[:c] + q_st[c:])
        y_loc.append(out[i][:c, LANES:] + out[i][c:, LANES:])
    trans = [jnp.concatenate([eye_l * w_tot[i] + carry[i][:, :LANES], carry[i][:, LANES:]], axis=1)
             for i in chunks]

    zeros_sq = jnp.zeros((LANES, LANES), F32)

    def compose(later, earlier):
        return _mm(later[:, :LANES], earlier) + jnp.concatenate([zeros_sq, later[:, LANES:]], axis=1)

    scan = list(trans)
    dist = 1
    while dist < len(scan):
        scan = [scan[i] if i < dist else compose(scan[i], scan[i - dist]) for i in range(len(scan))]
        dist *= 2
    prefix = [None] + scan
    q_pre = [None] + [_mm(q_hat[i], prefix[i]) for i in chunks[1:]]
    q_m = jnp.concatenate([q_hat[0]] + [q_pre[i][:, :LANES] for i in chunks[1:]], axis=0)
    y_off = jnp.concatenate([y_loc[0]] + [y_loc[i] + q_pre[i][:, LANES:] for i in chunks[1:]], axis=0)
    state = st_ref[...]
    y = _mm(q_m, state) + y_off
    st_ref[...] = _mm(prefix[-1][:, :LANES], state) + prefix[-1][:, LANES:]

    inv_n = 1.0 / RW_HEAD_DIM
    mean = _seg_sum(y, seg) * inv_n
    d = y - mean
    var = _seg_sum(d * d, seg) * inv_n
    yn = d * lax.rsqrt(var + GN_EPS) * gnw_ref[...] + gnb_ref[...]
    y_ref[0] = ((yn + bonus) * g).astype(y_ref.dtype)


def _rwkv(c_rw, p, batch, seq):
    ts = RW_TILE
    n_pairs = RW_DIM // HEAD_PAIR
    lora_blk = (3 * RW_DIM) // (2 * LANES)
    halo = ts // 8

    def col(off):
        return pl.BlockSpec((1, ts, LANES), lambda b, pp, i, off=off: (b, i, off + pp))

    def col_halo(off):
        return pl.BlockSpec((1, 8, LANES),
                            lambda b, pp, i, off=off: (b, jnp.maximum(i * halo - 1, 0), off + pp))

    vec = pl.BlockSpec((1, LANES), lambda b, pp, i: (0, pp))
    lora_w = pl.BlockSpec((LANES, LANES), lambda b, pp, i: (0, pp))
    in_specs = [
        col(0), col(n_pairs), col(2 * n_pairs),
        pl.BlockSpec((1, ts, 2 * LANES), lambda b, pp, i: (b, i, lora_blk)),
        col_halo(0), col_halo(n_pairs), col_halo(2 * n_pairs),
        pl.BlockSpec((1, 8, 2 * LANES), lambda b, pp, i: (b, jnp.maximum(i * halo - 1, 0), lora_blk)),
        vec, vec, vec, pl.BlockSpec((1, 2 * LANES), lambda b, pp, i: (0, 0)),
        vec, vec, vec, vec, vec, vec, vec, lora_w, lora_w, lora_w,
    ]
    return pl.pallas_call(
        _rwkv_kernel,
        grid=(batch, n_pairs, seq // ts),
        in_specs=in_specs,
        out_specs=pl.BlockSpec((1, ts, LANES), lambda b, pp, i: (b, i, pp)),
        out_shape=jax.ShapeDtypeStruct((batch, seq, RW_DIM), BF16),
        scratch_shapes=[pltpu.VMEM((LANES, LANES), F32)],
        compiler_params=_cparams("parallel", "parallel", "arbitrary"),
        name="rwkv",
    )(c_rw, c_rw, c_rw, c_rw, c_rw, c_rw, c_rw, c_rw,
      p["mu_r"], p["mu_k"], p["mu_v"], p["mu_l"], p["w0"], p["a0"], p["k_k"], p["k_a"], p["r_k"],
      p["gn_w"], p["gn_b"], p["w_up"], p["a_up"], p["g_up"])


MLA_SLOT = 128


def _mla_prep_kernel(cq_ref, ckvr_ref, cos_ref, sin_ref, gq_ref, gkv_ref,
                     wqa_ref, wqb_ref, wk_ref, wv_ref, pa_ref, pb_ref,
                     q_ref, k_ref, v_ref):
    cos = cos_ref[...]
    sin = sin_ref[...]
    zq = _rms(cq_ref[...], gq_ref[...]).astype(BF16)
    qa = jnp.dot(zq, wqa_ref[...], preferred_element_type=F32)
    qb = jnp.dot(zq, wqb_ref[...], preferred_element_type=F32)
    ckvr = ckvr_ref[...]
    zkv = _rms(ckvr[:, :KV_LORA], gkv_ref[...]).astype(BF16)
    kn = jnp.dot(zkv, wk_ref[...], preferred_element_type=F32)
    v_ref[...] = jnp.dot(zkv, wv_ref[...], preferred_element_type=F32).astype(BF16)
    kr = ckvr[:, KV_LORA:].astype(BF16)
    k_rope = (jnp.dot(kr, pa_ref[...], preferred_element_type=F32) * cos
              + jnp.dot(kr, pb_ref[...], preferred_element_type=F32) * sin)
    scale = math.log2(math.e) / math.sqrt(QK_NOPE + QK_ROPE)
    for h in range(MLA_HEADS):
        sl = slice(h * MLA_SLOT, (h + 1) * MLA_SLOT)
        q_ref[:, sl] = ((qa[:, sl] * cos + qb[:, sl] * sin) * scale).astype(BF16)
        k_ref[:, sl] = (kn[:, sl] + k_rope).astype(BF16)


def _mla_prep(c_q, c_kvr, cos_t, sin_t, p, tm):
    n = c_q.shape[0]
    full = lambda w: pl.BlockSpec(w.shape, lambda i: (0, 0))
    row = lambda c: pl.BlockSpec((tm, c), lambda i: (i, 0))
    ws = [p["g_qa"], p["g_kva"], p["w_qa"], p["w_qb"], p["w_k"], p["w_v"], p["p_a"], p["p_b"]]
    hq = MLA_HEADS * MLA_SLOT
    return pl.pallas_call(
        _mla_prep_kernel,
        grid=(n // tm,),
        in_specs=[row(c_q.shape[1]), row(c_kvr.shape[1]), row(MLA_SLOT), row(MLA_SLOT)]
                 + [full(w) for w in ws],
        out_specs=[row(hq), row(hq), row(MLA_HEADS * V_HEAD)],
        out_shape=[jax.ShapeDtypeStruct((n, hq), BF16), jax.ShapeDtypeStruct((n, hq), BF16),
                   jax.ShapeDtypeStruct((n, MLA_HEADS * V_HEAD), BF16)],
        compiler_params=_cparams("parallel"),
        name="mla_prep",
    )(c_q, c_kvr, cos_t, sin_t, *ws)


ATT_TILE = 512


def _attn_kernel(q_ref, k_ref, v_ref, o_ref):
    qi = pl.program_id(2)
    t = ATT_TILE
    lo_v = lax.broadcasted_iota(jnp.int32, (t, 2 * V_HEAD), 1) < V_HEAD
    row = lax.broadcasted_iota(jnp.int32, (t, t), 0)
    col = lax.broadcasted_iota(jnp.int32, (t, t), 1)
    causal = col <= row
    nt = (((1,), (1,)), ((), ()))
    qs = [q_ref[0, :, h * MLA_SLOT:(h + 1) * MLA_SLOT] for h in range(2)]
    lane_v = lax.broadcasted_iota(jnp.int32, (t, 2 * V_HEAD), 1)
    den_lane = (V_HEAD, 0)
    ones_col = [jnp.where(lane_v == den_lane[h], 1.0, 0.0).astype(BF16) for h in range(2)]
    own = (lo_v, jnp.logical_not(lo_v))

    def block(j, carry, masked):
        rows = pl.ds(pl.multiple_of(j * t, t), t)
        vb = v_ref[0, rows, :]
        out = []
        for h in range(2):
            m, acc = carry[h]
            kb = k_ref[0, rows, h * MLA_SLOT:(h + 1) * MLA_SLOT]
            s = lax.dot_general(qs[h], kb, nt, preferred_element_type=F32)
            if masked:
                s = jnp.where(causal, s, -jnp.inf)
            m_new = jnp.maximum(m, jnp.max(s, axis=-1, keepdims=True))
            pr = jnp.exp2((s - m_new).astype(BF16))
            v_aug = jnp.where(own[h], vb, ones_col[h])
            acc = acc * jnp.exp2(m - m_new) + jnp.dot(pr, v_aug, preferred_element_type=F32)
            out.append((m_new, acc))
        return tuple(out)

    init1 = (jnp.full((t, 1), -jnp.inf, F32), jnp.zeros((t, 2 * V_HEAD), F32))
    carry = lax.fori_loop(0, qi, lambda j, cr: block(j, cr, False), (init1, init1))
    (_, acc0), (_, acc1) = block(qi, carry, True)
    den0 = acc0[:, den_lane[0]:den_lane[0] + 1]
    den1 = acc1[:, den_lane[1]:den_lane[1] + 1]
    o_ref[0] = jnp.where(lo_v, acc0 / den0, acc1 / den1).astype(o_ref.dtype)


def _mla_attn(q, k, v, batch, seq):
    t = ATT_TILE
    return pl.pallas_call(
        _attn_kernel,
        grid=(batch, MLA_HEADS // 2, seq // t),
        in_specs=[pl.BlockSpec((1, t, 2 * MLA_SLOT), lambda b, hp, i: (b, i, hp)),
                  pl.BlockSpec((1, seq, 2 * MLA_SLOT), lambda b, hp, i: (b, 0, hp)),
                  pl.BlockSpec((1, seq, 2 * V_HEAD), lambda b, hp, i: (b, 0, hp))],
        out_specs=pl.BlockSpec((1, t, 2 * V_HEAD), lambda b, hp, i: (b, i, hp)),
        out_shape=jax.ShapeDtypeStruct((batch, seq, MLA_HEADS * V_HEAD), BF16),
        compiler_params=_cparams("parallel", "parallel", "arbitrary"),
        name="mla_attn",
    )(q, k, v)


ROUTE_W = 128


def _merge_kernel(x_ref, yrw_ref, ymla_ref, gate_ref, wbr_ref, wbm_ref, wo_ref, fg_ref,
                  wr_hi_ref, wr_lo_ref, br_ref, x1_ref, h2p_ref, route_ref, hist_ref):
    d = x_ref.shape[1]
    a = jnp.dot(yrw_ref[...], wbr_ref[...], preferred_element_type=F32)
    b = jnp.dot(ymla_ref[...], wbm_ref[...], preferred_element_type=F32)
    merged = gate_ref[:, :d].astype(F32) * a + gate_ref[:, d:].astype(F32) * b
    x1 = x_ref[...] + jnp.dot(merged.astype(BF16), wo_ref[...], preferred_element_type=F32)
    x1_ref[...] = x1
    h2 = _rms(x1, fg_ref[...])
    _slab_store(h2p_ref, _pack_rows(h2))

    h_hi = h2.astype(BF16)
    h_lo = (h2 - h_hi.astype(F32)).astype(BF16)
    logits = (jnp.dot(h_hi, wr_hi_ref[...], preferred_element_type=F32)
              + jnp.dot(h_lo, wr_hi_ref[...], preferred_element_type=F32)
              + jnp.dot(h_hi, wr_lo_ref[...], preferred_element_type=F32)) + br_ref[...]

    lane = lax.broadcasted_iota(jnp.int32, logits.shape, 1)
    big = jnp.int32(ROUTE_W)
    neg = -jnp.inf

    def first_argmax(vals, vmax):
        return jnp.min(jnp.where(vals == vmax, lane, big), axis=-1, keepdims=True)

    grp = jnp.where(lane < N_GROUPS, logits, neg)
    g_max = jnp.max(grp, axis=-1, keepdims=True)
    g_den = jnp.sum(jnp.exp(grp - g_max), axis=-1, keepdims=True)
    g_sel = first_argmax(grp, g_max)
    gate_g = 1.0 / g_den
    lo = N_GROUPS + g_sel * EXPERTS_PER_GROUP
    fine = jnp.where((lane >= lo) & (lane < lo + EXPERTS_PER_GROUP), logits, neg)
    v1 = jnp.max(fine, axis=-1, keepdims=True)
    i1 = first_argmax(fine, v1)
    fine2 = jnp.where(lane == i1, neg, fine)
    v2 = jnp.max(fine2, axis=-1, keepdims=True)
    i2 = first_argmax(fine2, v2)
    e2 = jnp.exp(v2 - v1)
    den = 1.0 + e2
    w1 = gate_g / den
    w2 = gate_g * e2 / den
    route = jnp.where(lane == 0, (i1 - N_GROUPS).astype(F32),
                      jnp.where(lane == 1, (i2 - N_GROUPS).astype(F32),
                                jnp.where(lane == 2, w1, jnp.where(lane == 3, w2, 0.0))))
    route_ref[...] = route
    chosen = jnp.where((lane == i1) | (lane == i2), 1.0, 0.0)
    hist_ref[0] = jnp.broadcast_to(jnp.sum(chosen, axis=0, keepdims=True), hist_ref.shape[1:])


def _merge(x2, y_rw, y_mla, gates, p, tm):
    n, d = x2.shape
    full = lambda w: pl.BlockSpec(w.shape, lambda i: (0, 0))
    row = lambda c: pl.BlockSpec((tm, c), lambda i: (i, 0))
    ws = [p["w_br"], p["w_bm"], p["w_out"], p["ffn_g"], p["wr_hi"], p["wr_lo"], p["b_route"]]
    return pl.pallas_call(
        _merge_kernel,
        grid=(n // tm,),
        in_specs=[row(d), row(y_rw.shape[1]), row(y_mla.shape[1]), row(2 * d)] + [full(w) for w in ws],
        out_specs=[row(d), pl.BlockSpec((tm * SLABS, LANES), lambda i: (i, 0)), row(ROUTE_W),
                   pl.BlockSpec((1, 8, ROUTE_W), lambda i: (i, 0, 0))],
        out_shape=[jax.ShapeDtypeStruct((n, d), F32), jax.ShapeDtypeStruct((n * SLABS, LANES), jnp.uint32),
                   jax.ShapeDtypeStruct((n, ROUTE_W), F32),
                   jax.ShapeDtypeStruct((n // tm, 8, ROUTE_W), F32)],
        compiler_params=_cparams("parallel"),
        name="merge_route",
    )(x2, y_rw, y_mla, gates, *ws)


def _plan_kernel(route_ref, base_ref, dest_ref):
    tm = route_ref.shape[0]
    route = route_ref[...]
    lane = lax.broadcasted_iota(jnp.int32, route.shape, 1).astype(F32)
    pick = [lane == route[:, k:k + 1] for k in range(TOP_K)]
    both = jnp.where(pick[0] | pick[1], 1.0, 0.0).astype(BF16)
    r = lax.broadcasted_iota(jnp.int32, (tm, tm), 0)
    c = lax.broadcasted_iota(jnp.int32, (tm, tm), 1)
    earlier = jnp.where(r > c, 1.0, 0.0).astype(BF16)
    offs = jnp.dot(earlier, both, preferred_element_type=F32) + base_ref[0]
    rows = [jnp.sum(jnp.where(pk, offs, 0.0), axis=-1, keepdims=True) for pk in pick]
    dest_ref[...] = jnp.where(lane == 0.0, rows[0], jnp.where(lane == 1.0, rows[1], 0.0)).astype(jnp.int32)


def _plan(route, base, tm):
    n = route.shape[0]
    return pl.pallas_call(
        _plan_kernel,
        grid=(n // tm,),
        in_specs=[pl.BlockSpec((tm, ROUTE_W), lambda i: (i, 0)),
                  pl.BlockSpec((1, 1, ROUTE_W), lambda i: (i, 0, 0))],
        out_specs=pl.BlockSpec((tm, ROUTE_W), lambda i: (i, 0)),
        out_shape=jax.ShapeDtypeStruct((n, ROUTE_W), jnp.int32),
        compiler_params=_cparams("parallel"),
        name="route_plan",
    )(route, base)


DISPATCH_TILE = 256
ROW_DMA_UNROLL = 8


def _dispatch_kernel(dest_ref, h_ref, zero_ref, xs_ref, sem):
    del zero_ref
    tm = h_ref.shape[0] // SLABS

    def start(t, _):
        for k in range(TOP_K):
            pltpu.make_async_copy(_slab_rows(h_ref, t),
                                  _slab_rows(xs_ref, dest_ref[0, 0, TOP_K * t + k]), sem).start()
        return 0

    lax.fori_loop(0, tm, start, 0, unroll=ROW_DMA_UNROLL)
    all_rows = xs_ref.at[pl.ds(0, TOP_K * tm * SLABS), :]
    pltpu.make_async_copy(all_rows, all_rows, sem).wait()


def _dispatch(h2, dest, xs_init):
    n = h2.shape[0] // SLABS
    tm = DISPATCH_TILE
    dest3 = dest.reshape(n // tm, 1, TOP_K * tm)
    return pl.pallas_call(
        _dispatch_kernel,
        grid=(n // tm,),
        in_specs=[pl.BlockSpec((1, 1, TOP_K * tm), lambda i: (i, 0, 0), memory_space=pltpu.SMEM),
                  pl.BlockSpec((tm * SLABS, LANES), lambda i: (i, 0)),
                  pl.BlockSpec(memory_space=pl.ANY)],
        out_specs=pl.BlockSpec(memory_space=pl.ANY),
        out_shape=jax.ShapeDtypeStruct(xs_init.shape, xs_init.dtype),
        scratch_shapes=[pltpu.SemaphoreType.DMA(())],
        input_output_aliases={2: 0},
        compiler_params=_cparams("arbitrary"),
        name="dispatch",
    )(dest3, h2, xs_init)


def _expert_kernel(blk_e_ref, n_used_ref, x_ref, wgu_ref, wd_ref, y_ref):
    del blk_e_ref

    @pl.when(pl.program_id(0) < n_used_ref[0])
    def _():
        x = _unpack_rows(_slab_load(x_ref)).astype(BF16)
        h = jnp.dot(x, wgu_ref[0], preferred_element_type=F32)
        gt = h[:, :D_EXPERT]
        up = h[:, D_EXPERT:]
        act = (gt * jax.nn.sigmoid(gt) * up).astype(BF16)
        _slab_store(y_ref, _pack_rows(jnp.dot(act, wd_ref[0], preferred_element_type=F32)))

    @pl.when(pl.program_id(0) >= n_used_ref[0])
    def _():
        y_ref[...] = jnp.zeros_like(y_ref)


def _experts(xs, blk_expert, n_used, w_gu, w_down):
    p_rows = xs.shape[0] // SLABS
    d = 2 * SLABS * LANES
    n_blocks = p_rows // EXPERT_BLOCK
    rows = pl.BlockSpec((EXPERT_BLOCK * SLABS, LANES), lambda i, be, nu: (i, 0))
    grid_spec = pltpu.PrefetchScalarGridSpec(
        num_scalar_prefetch=2,
        grid=(n_blocks,),
        in_specs=[rows,
                  pl.BlockSpec((1, d, 2 * D_EXPERT), lambda i, be, nu: (be[i], 0, 0)),
                  pl.BlockSpec((1, D_EXPERT, d), lambda i, be, nu: (be[i], 0, 0))],
        out_specs=rows,
    )
    return pl.pallas_call(
        _expert_kernel,
        grid_spec=grid_spec,
        out_shape=jax.ShapeDtypeStruct(xs.shape, jnp.uint32),
        compiler_params=_cparams("arbitrary"),
        name="experts",
    )(blk_expert, n_used, xs, w_gu, w_down)


COMBINE_TILE = 256


def _combine_kernel(dest_ref, x1_ref, route_ref, g_ref, yb_ref, o_ref, buf0, buf1, sem, *, final_norm):
    tm = x1_ref.shape[0]
    bufs = (buf0, buf1)

    def start(t, _):
        for k in range(TOP_K):
            pltpu.make_async_copy(_slab_rows(yb_ref, dest_ref[0, 0, TOP_K * t + k]),
                                  _slab_rows(bufs[k], t), sem).start()
        return 0

    lax.fori_loop(0, tm, start, 0, unroll=ROW_DMA_UNROLL)
    for b in bufs:
        pltpu.make_async_copy(b, b, sem).wait()
    route = route_ref[...]
    x2 = (x1_ref[...] + route[:, 2:3] * _unpack_rows(_slab_load(buf0))
          + route[:, 3:4] * _unpack_rows(_slab_load(buf1)))
    o_ref[...] = _rms(x2, g_ref[...]) if final_norm else x2


def _combine(x1, route, dest, yb, final_g, final_norm):
    n, d = x1.shape
    tm = COMBINE_TILE
    dest3 = dest.reshape(n // tm, 1, TOP_K * tm)
    return pl.pallas_call(
        functools.partial(_combine_kernel, final_norm=final_norm),
        grid=(n // tm,),
        in_specs=[pl.BlockSpec((1, 1, TOP_K * tm), lambda i: (i, 0, 0), memory_space=pltpu.SMEM),
                  pl.BlockSpec((tm, d), lambda i: (i, 0)),
                  pl.BlockSpec((tm, ROUTE_W), lambda i: (i, 0)),
                  pl.BlockSpec((1, d), lambda i: (0, 0)),
                  pl.BlockSpec(memory_space=pl.ANY)],
        out_specs=pl.BlockSpec((tm, d), lambda i: (i, 0)),
        out_shape=jax.ShapeDtypeStruct((n, d), F32),
        scratch_shapes=[pltpu.VMEM((tm * SLABS, LANES), jnp.uint32) for _ in range(TOP_K)]
                       + [pltpu.SemaphoreType.DMA(())],
        compiler_params=_cparams("arbitrary"),
        name="combine",
    )(dest3, x1, route, final_g, yb)


def _rwkv_params(rw_mu, rw_w0, rw_w_up, rw_a0, rw_a_up, rw_g_up, rw_k_k, rw_k_a, rw_r_k, rw_gn_w, rw_gn_b):
    row = lambda v: v.reshape(1, -1).astype(F32)
    zeros = jnp.zeros((A_LORA, RW_DIM), F32)
    return {
        "mu_r": row(rw_mu[:RW_DIM]), "mu_k": row(rw_mu[RW_DIM:2 * RW_DIM]),
        "mu_v": row(rw_mu[2 * RW_DIM:3 * RW_DIM]), "mu_l": row(rw_mu[3 * RW_DIM:]),
        "w0": row(rw_w0), "a0": row(rw_a0), "k_k": row(rw_k_k), "k_a": row(rw_k_a),
        "r_k": row(rw_r_k), "gn_w": row(rw_gn_w), "gn_b": row(rw_gn_b),
        "w_up": jnp.concatenate([rw_w_up, zeros], axis=0).astype(F32),
        "a_up": jnp.concatenate([zeros, rw_a_up], axis=0).astype(BF16),
        "g_up": rw_g_up.astype(BF16),
    }


def _mla_params(g_qa, w_q_up, g_kva, w_kv_up):
    half = QK_ROPE // 2
    pad = MLA_SLOT - QK_NOPE - QK_ROPE
    wq = w_q_up.reshape(Q_LORA, MLA_HEADS, QK_NOPE + QK_ROPE)
    q_nope, q_r1, q_r2 = wq[..., :QK_NOPE], wq[..., QK_NOPE:QK_NOPE + half], wq[..., QK_NOPE + half:]
    zq = lambda w: jnp.zeros((Q_LORA, MLA_HEADS, w), F32)
    w_qa = jnp.concatenate([q_nope, q_r1, q_r2, zq(pad)], axis=-1).reshape(Q_LORA, -1)
    w_qb = jnp.concatenate([zq(QK_NOPE), -q_r2, q_r1, zq(pad)], axis=-1).reshape(Q_LORA, -1)
    wkv = w_kv_up.reshape(KV_LORA, MLA_HEADS, QK_NOPE + V_HEAD)
    w_k = jnp.concatenate([wkv[..., :QK_NOPE], jnp.zeros((KV_LORA, MLA_HEADS, MLA_SLOT - QK_NOPE), F32)],
                          axis=-1).reshape(KV_LORA, -1)
    w_v = wkv[..., QK_NOPE:].reshape(KV_LORA, -1)
    eye = jnp.eye(half, dtype=F32)
    z = jnp.zeros((half, half), F32)
    zl = jnp.zeros((QK_ROPE, QK_NOPE), F32)
    zr = jnp.zeros((QK_ROPE, pad), F32)
    p_a = jnp.concatenate([zl, jnp.concatenate([eye, z], 0), jnp.concatenate([z, eye], 0), zr], axis=1)
    p_b = jnp.concatenate([zl, jnp.concatenate([z, -eye], 0), jnp.concatenate([eye, z], 0), zr], axis=1)
    return {"g_qa": g_qa.reshape(1, -1), "g_kva": g_kva.reshape(1, -1),
            "w_qa": w_qa.astype(BF16), "w_qb": w_qb.astype(BF16), "w_k": w_k.astype(BF16),
            "w_v": w_v.astype(BF16), "p_a": p_a.astype(BF16), "p_b": p_b.astype(BF16)}


def _rope_tables(positions):
    half = QK_ROPE // 2
    inv_freq = ROPE_THETA ** (-jnp.arange(0, QK_ROPE, 2, dtype=F32) / QK_ROPE)
    ang = positions.astype(F32).reshape(-1, 1) * inv_freq
    cos, sin = jnp.cos(ang), jnp.sin(ang)
    n = ang.shape[0]
    pad = MLA_SLOT - QK_NOPE - QK_ROPE
    cos_t = jnp.concatenate([jnp.ones((n, QK_NOPE), F32), cos, cos, jnp.zeros((n, pad), F32)], axis=1)
    sin_t = jnp.concatenate([jnp.zeros((n, QK_NOPE), F32), sin, sin, jnp.zeros((n, pad), F32)], axis=1)
    del half
    return cos_t, sin_t


def _block_layout(hist, n_assign):
    tile_counts = hist[:, 0, N_GROUPS:N_GROUPS + N_EXPERTS].astype(jnp.int32)
    counts = jnp.sum(tile_counts, axis=0)
    padded = (counts + EXPERT_BLOCK - 1) // EXPERT_BLOCK * EXPERT_BLOCK
    pad_end = jnp.cumsum(padded)
    pad_start = pad_end - padded
    tile_base = jnp.cumsum(tile_counts, axis=0) - tile_counts + pad_start[None, :]
    base = jnp.pad(tile_base.astype(F32), ((0, 0), (0, ROUTE_W - N_EXPERTS)))[:, None, :]
    n_blocks = -(-n_assign // EXPERT_BLOCK) + N_EXPERTS
    blk_row = jnp.arange(n_blocks, dtype=jnp.int32) * EXPERT_BLOCK
    blk_expert = jnp.minimum(jnp.sum((pad_end[None, :] <= blk_row[:, None]).astype(jnp.int32), axis=1),
                             N_EXPERTS - 1)
    n_used = (pad_end[-1] // EXPERT_BLOCK).astype(jnp.int32).reshape(1)
    return base, blk_expert, n_used, n_blocks


def kernel(x, positions, mix_norm_g, w_in, rw_mu, rw_w0, rw_w_up, rw_a0, rw_a_up, rw_g_up, rw_k_k, rw_k_a, rw_r_k, rw_gn_w, rw_gn_b, mla_g_qa, mla_w_q_up, mla_g_kva, mla_w_kv_up, w_branch_rw, w_branch_mla, w_out, ffn_norm_g, moe_w_group, moe_b_group, moe_w_router, moe_b_router, moe_w_gu, moe_w_down, final_norm_g):
    batch, seq, d = x.shape
    assert d == 2 * SLABS * LANES
    n = batch * seq
    depth = w_in.shape[0]
    rw_cols = 3 * RW_DIM + W_LORA + A_LORA + G_LORA
    mla_cols = Q_LORA + KV_LORA + QK_ROPE
    cos_t, sin_t = _rope_tables(positions)
    x2 = x.reshape(n, d)

    for l in range(depth):
        wl = w_in[l].astype(BF16)
        c_rw, c_q, c_kvr, gates = _in_proj(
            x2, mix_norm_g[l].reshape(1, d), wl[:, :rw_cols], wl[:, rw_cols:rw_cols + Q_LORA],
            wl[:, rw_cols + Q_LORA:rw_cols + mla_cols], wl[:, rw_cols + mla_cols:], tm=ROW_TILE)

        rp = _rwkv_params(rw_mu[l], rw_w0[l], rw_w_up[l], rw_a0[l], rw_a_up[l], rw_g_up[l], rw_k_k[l],
                          rw_k_a[l], rw_r_k[l], rw_gn_w[l], rw_gn_b[l])
        y_rw = _rwkv(c_rw.reshape(batch, seq, rw_cols), rp, batch, seq).reshape(n, RW_DIM)

        mp = _mla_params(mla_g_qa[l], mla_w_q_up[l], mla_g_kva[l], mla_w_kv_up[l])
        q, k, v = _mla_prep(c_q, c_kvr, cos_t, sin_t, mp, tm=ROW_TILE)
        y_mla = _mla_attn(q.reshape(batch, seq, -1), k.reshape(batch, seq, -1),
                          v.reshape(batch, seq, -1), batch, seq).reshape(n, MLA_HEADS * V_HEAD)

        w_route = jnp.concatenate(
            [moe_w_group[l], moe_w_router[l], jnp.zeros((d, ROUTE_W - N_GROUPS - N_EXPERTS), F32)], axis=1)
        b_route = jnp.concatenate(
            [moe_b_group[l], moe_b_router[l], jnp.zeros((ROUTE_W - N_GROUPS - N_EXPERTS,), F32)]).reshape(1, -1)
        wr_hi = w_route.astype(BF16)
        wr_lo = (w_route - wr_hi.astype(F32)).astype(BF16)
        mparams = {"w_br": w_branch_rw[l].astype(BF16), "w_bm": w_branch_mla[l].astype(BF16),
                   "w_out": w_out[l].astype(BF16), "ffn_g": ffn_norm_g[l].reshape(1, d),
                   "wr_hi": wr_hi, "wr_lo": wr_lo, "b_route": b_route}
        x1, h2p, route, hist = _merge(x2, y_rw, y_mla, gates, mparams, tm=ROW_TILE)

        base, blk_expert, n_used, n_blocks = _block_layout(hist, n * TOP_K)
        dest = _plan(route, base, tm=ROW_TILE)[:, :TOP_K]
        xs = _dispatch(h2p, dest,
                       jnp.zeros((n_blocks * EXPERT_BLOCK * SLABS, LANES), jnp.uint32))
        yb = _experts(xs, blk_expert, n_used, moe_w_gu[l].astype(BF16), moe_w_down[l].astype(BF16))
        x2 = _combine(x1, route, dest, yb, final_norm_g.reshape(1, d), final_norm=(l == depth - 1))

    return x2.reshape(batch, seq, d)
```

```python
import functools
import math

import jax
import jax.numpy as jnp
from jax import lax
from jax.experimental import pallas as pl
from jax.experimental.pallas import tpu as pltpu

F32 = jnp.float32
BF16 = jnp.bfloat16

RW_HEADS = 8
RW_HEAD_DIM = 64
RW_DIM = RW_HEADS * RW_HEAD_DIM
W_LORA = 64
A_LORA = 64
G_LORA = 128
GN_EPS = 64e-5
MLA_HEADS = 8
QK_NOPE = 64
QK_ROPE = 32
V_HEAD = 64
Q_LORA = 384
KV_LORA = 256
ROPE_THETA = 10000.0
N_GROUPS = 4
EXPERTS_PER_GROUP = 8
N_EXPERTS = N_GROUPS * EXPERTS_PER_GROUP
TOP_K = 2
D_EXPERT = 256
EXPERT_BLOCK = 256
NORM_EPS = 1e-6

LANES = 128
HEAD_PAIR = 2 * RW_HEAD_DIM
VMEM_LIMIT = 48 * 1024 * 1024
ROW_TILE = 512


def _cparams(*sem):
    return pltpu.CompilerParams(dimension_semantics=sem, vmem_limit_bytes=VMEM_LIMIT)


def _mm(a, b, dims=((1,), (0,)), split=False):
    dn = (dims, ((), ()))
    dot = lambda x, y: lax.dot_general(x, y, dn, preferred_element_type=F32)
    a_hi = a.astype(BF16)
    b_hi = b.astype(BF16)
    if not split:
        return dot(a_hi, b_hi)
    a_lo = (a - a_hi.astype(F32)).astype(BF16)
    b_lo = (b - b_hi.astype(F32)).astype(BF16)
    return dot(a_hi, b_hi) + dot(a_lo, b_hi) + dot(a_hi, b_lo)


def _mm_sel(sel_bf16, x, dims=((1,), (0,))):
    dn = (dims, ((), ()))
    hi = x.astype(BF16)
    r1 = x - hi.astype(F32)
    mid = r1.astype(BF16)
    lo = (r1 - mid.astype(F32)).astype(BF16)
    out = lax.dot_general(sel_bf16, hi, dn, preferred_element_type=F32)
    out = out + lax.dot_general(sel_bf16, mid, dn, preferred_element_type=F32)
    return out + lax.dot_general(sel_bf16, lo, dn, preferred_element_type=F32)


def _seg_sum(x, seg_bf16):
    hi = x.astype(BF16)
    lo = (x - hi.astype(F32)).astype(BF16)
    return (jnp.dot(hi, seg_bf16, preferred_element_type=F32)
            + jnp.dot(lo, seg_bf16, preferred_element_type=F32))


def _rms(x, g):
    return x * lax.rsqrt(jnp.mean(x * x, axis=-1, keepdims=True) + NORM_EPS) * g


def _pack_rows(x):
    half = x.shape[1] // 2
    bits = lambda v: lax.bitcast_convert_type(v.astype(BF16).astype(F32), jnp.uint32)
    return bits(x[:, :half]) | (bits(x[:, half:]) >> 16)


def _unpack_rows(p):
    hi = lax.bitcast_convert_type(p & jnp.uint32(0xFFFF0000), F32)
    lo = lax.bitcast_convert_type(p << 16, F32)
    return jnp.concatenate([hi, lo], axis=1)


SLABS = 4


def _slab_rows(ref, r):
    return ref.at[pl.ds(pl.multiple_of(r * SLABS, SLABS), SLABS), :]


def _slab_load(ref):
    rows = ref.shape[0] // SLABS
    return jnp.concatenate([ref[pl.ds(j, rows, stride=SLABS), :] for j in range(SLABS)], axis=1)


def _slab_store(ref, x):
    rows = ref.shape[0] // SLABS
    for j in range(SLABS):
        ref[pl.ds(j, rows, stride=SLABS), :] = x[:, j * LANES:(j + 1) * LANES]


def _in_proj_kernel(x_ref, g_ref, wrw_ref, wq_ref, wkvr_ref, wg_ref,
                    crw_ref, cq_ref, ckvr_ref, gate_ref):
    hb = _rms(x_ref[...], g_ref[...]).astype(BF16)
    crw_ref[...] = jnp.dot(hb, wrw_ref[...], preferred_element_type=F32)
    cq_ref[...] = jnp.dot(hb, wq_ref[...], preferred_element_type=F32)
    ckvr_ref[...] = jnp.dot(hb, wkvr_ref[...], preferred_element_type=F32)
    gate_ref[...] = jax.nn.sigmoid(jnp.dot(hb, wg_ref[...], preferred_element_type=F32)).astype(BF16)


def _in_proj(x2, g, w_rw, w_q, w_kvr, w_gate, tm):
    n, d = x2.shape
    full = lambda w: pl.BlockSpec(w.shape, lambda i: (0, 0))
    row = lambda c: pl.BlockSpec((tm, c), lambda i: (i, 0))
    return pl.pallas_call(
        _in_proj_kernel,
        grid=(n // tm,),
        in_specs=[row(d), full(g), full(w_rw), full(w_q), full(w_kvr), full(w_gate)],
        out_specs=[row(w_rw.shape[1]), row(w_q.shape[1]), row(w_kvr.shape[1]), row(w_gate.shape[1])],
        out_shape=[jax.ShapeDtypeStruct((n, w_rw.shape[1]), F32),
                   jax.ShapeDtypeStruct((n, w_q.shape[1]), F32),
                   jax.ShapeDtypeStruct((n, w_kvr.shape[1]), F32),
                   jax.ShapeDtypeStruct((n, w_gate.shape[1]), BF16)],
        compiler_params=_cparams("parallel"),
        name="in_proj",
    )(x2, g, w_rw, w_q, w_kvr, w_gate)


RW_CHUNK = 64
RW_TILE = 1024
RW_GROUP = 16


def _token_shift(cur, halo_ref, first):
    prev_row = jnp.where(first, 0.0, halo_ref[0, 7:8, :])
    rolled = pltpu.roll(cur, 1, 0)
    row = lax.broadcasted_iota(jnp.int32, cur.shape, 0)
    return jnp.where(row == 0, prev_row, rolled)


def _rwkv_kernel(r_ref, k_ref, v_ref, l_ref, hr_ref, hk_ref, hv_ref, hl_ref,
                 mur_ref, muk_ref, muv_ref, mul_ref, w0_ref, a0_ref, kk_ref, ka_ref, rk_ref,
                 gnw_ref, gnb_ref, wup_ref, aup_ref, gup_ref, y_ref, st_ref):
    i = pl.program_id(2)
    first = i == 0

    @pl.when(first)
    def _():
        st_ref[...] = jnp.zeros_like(st_ref)

    def mixed(c_ref, h_ref, mu_ref):
        cur = c_ref[0]
        return cur + (_token_shift(cur, h_ref, first) - cur) * mu_ref[...]

    zr = mixed(r_ref, hr_ref, mur_ref)
    zk = mixed(k_ref, hk_ref, muk_ref)
    zv = mixed(v_ref, hv_ref, muv_ref)
    zl = mixed(l_ref, hl_ref, mul_ref)
    z_wa = zl[:, :LANES]
    z_g = zl[:, LANES:]

    lane = lax.broadcasted_iota(jnp.int32, (LANES, LANES), 1)
    sub = lax.broadcasted_iota(jnp.int32, (LANES, LANES), 0)
    same_head = (lane // RW_HEAD_DIM) == (sub // RW_HEAD_DIM)
    seg = jnp.where(same_head, 1.0, 0.0).astype(BF16)

    w = w0_ref[...] + _mm(jnp.tanh(z_wa), wup_ref[...], split=True)
    u = -w
    softplus = jnp.maximum(u, 0.0) + jnp.log(1.0 + jnp.exp(-jnp.abs(u)))
    log_decay = -jnp.exp(-softplus - 0.5)
    a = jax.nn.sigmoid(a0_ref[...] + _mm(z_wa, aup_ref[...]))
    g = _mm(jax.nn.sigmoid(z_g), gup_ref[...])

    kk = zk * kk_ref[...]
    kk = kk / jnp.maximum(jnp.sqrt(_seg_sum(kk * kk, seg)), 1e-12)
    k2 = zk * (1.0 + (a - 1.0) * ka_ref[...])
    bonus = _seg_sum(zr * k2 * rk_ref[...], seg) * zv
    kka = kk * a

    c = RW_CHUNK
    crow = lax.broadcasted_iota(jnp.int32, (c, c), 0)
    ccol = lax.broadcasted_iota(jnp.int32, (c, c), 1)
    cum_sel = jnp.where(crow >= ccol, 1.0, 0.0).astype(BF16)
    tril_incl = sub >= lane
    tril_strict = sub > lane
    eye_l = jnp.where(lane == sub, 1.0, 0.0).astype(F32)
    lo_half = lax.broadcasted_iota(jnp.int32, (c, LANES), 1) < RW_HEAD_DIM
    nt = ((1,), (1,))
    tn = ((0,), (0,))
    zeros_blk = jnp.zeros((2 * c, LANES), F32)

    def stack(t):
        return jnp.concatenate([jnp.where(lo_half, t, 0.0), jnp.where(lo_half, 0.0, t)], axis=0)

    tril_incl2 = jnp.concatenate([tril_incl, tril_incl], axis=1)

    def chunk_group(ids):
        chunks = range(len(ids))
        x_a, x_b, x_k, x_r, x_v, x_bh, x_kh, w_tot = [], [], [], [], [], [], [], []
        for ci in ids:
            sl = slice(ci * c, (ci + 1) * c)
            ld = log_decay[sl]
            cum = _mm_sel(cum_sel, ld)
            tot = cum[c - 1:c, :]
            e_neg = jnp.exp(-cum)
            e_rest = jnp.exp(tot - cum)
            x_a.append(stack(-kk[sl] * jnp.exp(cum - ld)))
            x_b.append(stack(kka[sl] * e_neg))
            x_k.append(stack(k2[sl] * e_neg))
            x_r.append(stack(zr[sl] * jnp.exp(cum)))
            x_v.append(stack(zv[sl]))
            x_bh.append(stack(kka[sl] * e_rest))
            x_kh.append(stack(k2[sl] * e_rest))
            w_tot.append(jnp.exp(tot))

        inter = [_mm(jnp.concatenate([x_a[i], x_r[i]], axis=0),
                     jnp.concatenate([x_b[i], x_k[i]], axis=0), nt) for i in chunks]
        a_ab = [jnp.where(tril_strict, m[:2 * c, :2 * c], 0.0) for m in inter]
        a_ak = [jnp.where(tril_strict, m[:2 * c, 2 * c:], 0.0) for m in inter]
        a_r = [jnp.where(tril_incl2, m[2 * c:], 0.0) for m in inter]
        w_ak = [_mm(a_ak[i], x_v[i]) for i in chunks]

        t_inv = [eye_l + m for m in a_ab]
        pw = a_ab
        for _ in range(int(math.log2(c)) - 1):
            pw = [_mm(m, m) for m in pw]
            t_inv = [t_inv[i] + _mm(t_inv[i], pw[i]) for i in chunks]

        solved = [_mm(t_inv[i], jnp.concatenate([x_a[i], w_ak[i]], axis=1)) for i in chunks]
        rhs = [jnp.concatenate([solved[i], jnp.concatenate([zeros_blk, x_v[i]], axis=1)], axis=0)
               for i in chunks]
        out = [_mm(a_r[i], rhs[i]) for i in chunks]
        carry = [_mm(jnp.concatenate([x_bh[i], x_kh[i]], axis=0), rhs[i], tn) for i in chunks]
        q_hat, y_loc = [], []
        for i in chunks:
            q_st = x_r[i] + out[i][:, :LANES]
            q_hat.append(q_st[:c] + q_st[c:])
            y_loc.append(out[i][:c, LANES:] + out[i][c:, LANES:])
        trans = [jnp.concatenate([eye_l * w_tot[i] + carry[i][:, :LANES], carry[i][:, LANES:]], axis=1)
                 for i in chunks]
        return q_hat, y_loc, trans

    n_chunks = RW_TILE // c
    groups = [chunk_group(range(g, g + RW_GROUP)) for g in range(0, n_chunks, RW_GROUP)]
    q_hat, y_loc, trans = (sum((g[j] for g in groups), []) for j in range(3))
    chunks = range(n_chunks)

    zeros_sq = jnp.zeros((LANES, LANES), F32)

    def compose(later, earlier):
        return _mm(later[:, :LANES], earlier) + jnp.concatenate([zeros_sq, later[:, LANES:]], axis=1)

    scan = list(trans)
    dist = 1
    while dist < len(scan):
        scan = [scan[i] if i < dist else compose(scan[i], scan[i - dist]) for i in range(len(scan))]
        dist *= 2
    prefix = [None] + scan
    q_pre = [None] + [_mm(q_hat[i], prefix[i]) for i in chunks[1:]]
    q_m = jnp.concatenate([q_hat[0]] + [q_pre[i][:, :LANES] for i in chunks[1:]], axis=0)
    y_off = jnp.concatenate([y_loc[0]] + [y_loc[i] + q_pre[i][:, LANES:] for i in chunks[1:]], axis=0)
    state = st_ref[...]
    y = _mm(q_m, state) + y_off
    st_ref[...] = _mm(prefix[-1][:, :LANES], state) + prefix[-1][:, LANES:]

    inv_n = 1.0 / RW_HEAD_DIM
    mean = _seg_sum(y, seg) * inv_n
    d = y - mean
    var = _seg_sum(d * d, seg) * inv_n
    yn = d * lax.rsqrt(var + GN_EPS) * gnw_ref[...] + gnb_ref[...]
    y_ref[0] = ((yn + bonus) * g).astype(y_ref.dtype)


def _rwkv(c_rw, p, batch, seq):
    ts = RW_TILE
    n_pairs = RW_DIM // HEAD_PAIR
    lora_blk = (3 * RW_DIM) // (2 * LANES)
    halo = ts // 8

    def col(off):
        return pl.BlockSpec((1, ts, LANES), lambda b, pp, i, off=off: (b, i, off + pp))

    def col_halo(off):
        return pl.BlockSpec((1, 8, LANES),
                            lambda b, pp, i, off=off: (b, jnp.maximum(i * halo - 1, 0), off + pp))

    vec = pl.BlockSpec((1, LANES), lambda b, pp, i: (0, pp))
    lora_w = pl.BlockSpec((LANES, LANES), lambda b, pp, i: (0, pp))
    in_specs = [
        col(0), col(n_pairs), col(2 * n_pairs),
        pl.BlockSpec((1, ts, 2 * LANES), lambda b, pp, i: (b, i, lora_blk)),
        col_halo(0), col_halo(n_pairs), col_halo(2 * n_pairs),
        pl.BlockSpec((1, 8, 2 * LANES), lambda b, pp, i: (b, jnp.maximum(i * halo - 1, 0), lora_blk)),
        vec, vec, vec, pl.BlockSpec((1, 2 * LANES), lambda b, pp, i: (0, 0)),
        vec, vec, vec, vec, vec, vec, vec, lora_w, lora_w, lora_w,
    ]
    return pl.pallas_call(
        _rwkv_kernel,
        grid=(batch, n_pairs, seq // ts),
        in_specs=in_specs,
        out_specs=pl.BlockSpec((1, ts, LANES), lambda b, pp, i: (b, i, pp)),
        out_shape=jax.ShapeDtypeStruct((batch, seq, RW_DIM), BF16),
        scratch_shapes=[pltpu.VMEM((LANES, LANES), F32)],
        compiler_params=_cparams("parallel", "parallel", "arbitrary"),
        name="rwkv",
    )(c_rw, c_rw, c_rw, c_rw, c_rw, c_rw, c_rw, c_rw,
      p["mu_r"], p["mu_k"], p["mu_v"], p["mu_l"], p["w0"], p["a0"], p["k_k"], p["k_a"], p["r_k"],
      p["gn_w"], p["gn_b"], p["w_up"], p["a_up"], p["g_up"])


MLA_SLOT = 128


def _mla_prep_kernel(cq_ref, ckvr_ref, cos_ref, sin_ref, gq_ref, gkv_ref,
                     wqa_ref, wqb_ref, wk_ref, wv_ref, pa_ref, pb_ref,
                     q_ref, k_ref, v_ref):
    cos = cos_ref[...]
    sin = sin_ref[...]
    zq = _rms(cq_ref[...], gq_ref[...]).astype(BF16)
    qa = jnp.dot(zq, wqa_ref[...], preferred_element_type=F32)
    qb = jnp.dot(zq, wqb_ref[...], preferred_element_type=F32)
    ckvr = ckvr_ref[...]
    zkv = _rms(ckvr[:, :KV_LORA], gkv_ref[...]).astype(BF16)
    kn = jnp.dot(zkv, wk_ref[...], preferred_element_type=F32)
    v_ref[...] = jnp.dot(zkv, wv_ref[...], preferred_element_type=F32).astype(BF16)
    kr = ckvr[:, KV_LORA:].astype(BF16)
    k_rope = (jnp.dot(kr, pa_ref[...], preferred_element_type=F32) * cos
              + jnp.dot(kr, pb_ref[...], preferred_element_type=F32) * sin)
    scale = math.log2(math.e) / math.sqrt(QK_NOPE + QK_ROPE)
    for h in range(MLA_HEADS):
        sl = slice(h * MLA_SLOT, (h + 1) * MLA_SLOT)
        q_ref[:, sl] = ((qa[:, sl] * cos + qb[:, sl] * sin) * scale).astype(BF16)
        k_ref[:, sl] = (kn[:, sl] + k_rope).astype(BF16)


def _mla_prep(c_q, c_kvr, cos_t, sin_t, p, tm):
    n = c_q.shape[0]
    full = lambda w: pl.BlockSpec(w.shape, lambda i: (0, 0))
    row = lambda c: pl.BlockSpec((tm, c), lambda i: (i, 0))
    ws = [p["g_qa"], p["g_kva"], p["w_qa"], p["w_qb"], p["w_k"], p["w_v"], p["p_a"], p["p_b"]]
    hq = MLA_HEADS * MLA_SLOT
    return pl.pallas_call(
        _mla_prep_kernel,
        grid=(n // tm,),
        in_specs=[row(c_q.shape[1]), row(c_kvr.shape[1]), row(MLA_SLOT), row(MLA_SLOT)]
                 + [full(w) for w in ws],
        out_specs=[row(hq), row(hq), row(MLA_HEADS * V_HEAD)],
        out_shape=[jax.ShapeDtypeStruct((n, hq), BF16), jax.ShapeDtypeStruct((n, hq), BF16),
                   jax.ShapeDtypeStruct((n, MLA_HEADS * V_HEAD), BF16)],
        compiler_params=_cparams("parallel"),
        name="mla_prep",
    )(c_q, c_kvr, cos_t, sin_t, *ws)


ATT_TILE = 512


def _attn_kernel(q_ref, k_ref, v_ref, o_ref):
    qi = pl.program_id(2)
    t = ATT_TILE
    lo_v = lax.broadcasted_iota(jnp.int32, (t, 2 * V_HEAD), 1) < V_HEAD
    row = lax.broadcasted_iota(jnp.int32, (t, t), 0)
    col = lax.broadcasted_iota(jnp.int32, (t, t), 1)
    causal = col <= row
    nt = (((1,), (1,)), ((), ()))
    qs = [q_ref[0, :, h * MLA_SLOT:(h + 1) * MLA_SLOT] for h in range(2)]
    lane_v = lax.broadcasted_iota(jnp.int32, (t, 2 * V_HEAD), 1)
    den_lane = (V_HEAD, 0)
    ones_col = [jnp.where(lane_v == den_lane[h], 1.0, 0.0).astype(BF16) for h in range(2)]
    own = (lo_v, jnp.logical_not(lo_v))

    def block(j, carry, masked):
        rows = pl.ds(pl.multiple_of(j * t, t), t)
        vb = v_ref[0, rows, :]
        out = []
        for h in range(2):
            m, acc = carry[h]
            kb = k_ref[0, rows, h * MLA_SLOT:(h + 1) * MLA_SLOT]
            s = lax.dot_general(qs[h], kb, nt, preferred_element_type=F32)
            if masked:
                s = jnp.where(causal, s, -jnp.inf)
            m_new = jnp.maximum(m, jnp.max(s, axis=-1, keepdims=True))
            pr = jnp.exp2((s - m_new).astype(BF16))
            v_aug = jnp.where(own[h], vb, ones_col[h])
            acc = acc * jnp.exp2(m - m_new) + jnp.dot(pr, v_aug, preferred_element_type=F32)
            out.append((m_new, acc))
        return tuple(out)

    init1 = (jnp.full((t, 1), -jnp.inf, F32), jnp.zeros((t, 2 * V_HEAD), F32))
    carry = lax.fori_loop(0, qi, lambda j, cr: block(j, cr, False), (init1, init1))
    (_, acc0), (_, acc1) = block(qi, carry, True)
    den0 = acc0[:, den_lane[0]:den_lane[0] + 1]
    den1 = acc1[:, den_lane[1]:den_lane[1] + 1]
    o_ref[0] = jnp.where(lo_v, acc0 / den0, acc1 / den1).astype(o_ref.dtype)


def _mla_attn(q, k, v, batch, seq):
    t = ATT_TILE
    return pl.pallas_call(
        _attn_kernel,
        grid=(batch, MLA_HEADS // 2, seq // t),
        in_specs=[pl.BlockSpec((1, t, 2 * MLA_SLOT), lambda b, hp, i: (b, i, hp)),
                  pl.BlockSpec((1, seq, 2 * MLA_SLOT), lambda b, hp, i: (b, 0, hp)),
                  pl.BlockSpec((1, seq, 2 * V_HEAD), lambda b, hp, i: (b, 0, hp))],
        out_specs=pl.BlockSpec((1, t, 2 * V_HEAD), lambda b, hp, i: (b, i, hp)),
        out_shape=jax.ShapeDtypeStruct((batch, seq, MLA_HEADS * V_HEAD), BF16),
        compiler_params=_cparams("parallel", "parallel", "arbitrary"),
        name="mla_attn",
    )(q, k, v)


ROUTE_W = 128


def _merge_kernel(x_ref, yrw_ref, ymla_ref, gate_ref, wbr_ref, wbm_ref, wo_ref, fg_ref,
                  wr_hi_ref, wr_lo_ref, br_ref, x1_ref, h2p_ref, route_ref, hist_ref):
    d = x_ref.shape[1]
    a = jnp.dot(yrw_ref[...], wbr_ref[...], preferred_element_type=F32)
    b = jnp.dot(ymla_ref[...], wbm_ref[...], preferred_element_type=F32)
    merged = gate_ref[:, :d].astype(F32) * a + gate_ref[:, d:].astype(F32) * b
    x1 = x_ref[...] + jnp.dot(merged.astype(BF16), wo_ref[...], preferred_element_type=F32)
    x1_ref[...] = x1
    h2 = _rms(x1, fg_ref[...])
    _slab_store(h2p_ref, _pack_rows(h2))

    h_hi = h2.astype(BF16)
    h_lo = (h2 - h_hi.astype(F32)).astype(BF16)
    logits = (jnp.dot(h_hi, wr_hi_ref[...], preferred_element_type=F32)
              + jnp.dot(h_lo, wr_hi_ref[...], preferred_element_type=F32)
              + jnp.dot(h_hi, wr_lo_ref[...], preferred_element_type=F32)) + br_ref[...]

    lane = lax.broadcasted_iota(jnp.int32, logits.shape, 1)
    big = jnp.int32(ROUTE_W)
    neg = -jnp.inf

    def first_argmax(vals, vmax):
        return jnp.min(jnp.where(vals == vmax, lane, big), axis=-1, keepdims=True)

    grp = jnp.where(lane < N_GROUPS, logits, neg)
    g_max = jnp.max(grp, axis=-1, keepdims=True)
    g_den = jnp.sum(jnp.exp(grp - g_max), axis=-1, keepdims=True)
    g_sel = first_argmax(grp, g_max)
    gate_g = 1.0 / g_den
    lo = N_GROUPS + g_sel * EXPERTS_PER_GROUP
    fine = jnp.where((lane >= lo) & (lane < lo + EXPERTS_PER_GROUP), logits, neg)
    v1 = jnp.max(fine, axis=-1, keepdims=True)
    i1 = first_argmax(fine, v1)
    fine2 = jnp.where(lane == i1, neg, fine)
    v2 = jnp.max(fine2, axis=-1, keepdims=True)
    i2 = first_argmax(fine2, v2)
    e2 = jnp.exp(v2 - v1)
    den = 1.0 + e2
    w1 = gate_g / den
    w2 = gate_g * e2 / den
    route = jnp.where(lane == 0, (i1 - N_GROUPS).astype(F32),
                      jnp.where(lane == 1, (i2 - N_GROUPS).astype(F32),
                                jnp.where(lane == 2, w1, jnp.where(lane == 3, w2, 0.0))))
    route_ref[...] = route
    chosen = jnp.where((lane == i1) | (lane == i2), 1.0, 0.0)
    hist_ref[0] = jnp.broadcast_to(jnp.sum(chosen, axis=0, keepdims=True), hist_ref.shape[1:])


def _merge(x2, y_rw, y_mla, gates, p, tm):
    n, d = x2.shape
    full = lambda w: pl.BlockSpec(w.shape, lambda i: (0, 0))
    row = lambda c: pl.BlockSpec((tm, c), lambda i: (i, 0))
    ws = [p["w_br"], p["w_bm"], p["w_out"], p["ffn_g"], p["wr_hi"], p["wr_lo"], p["b_route"]]
    return pl.pallas_call(
        _merge_kernel,
        grid=(n // tm,),
        in_specs=[row(d), row(y_rw.shape[1]), row(y_mla.shape[1]), row(2 * d)] + [full(w) for w in ws],
        out_specs=[row(d), pl.BlockSpec((tm * SLABS, LANES), lambda i: (i, 0)), row(ROUTE_W),
                   pl.BlockSpec((1, 8, ROUTE_W), lambda i: (i, 0, 0))],
        out_shape=[jax.ShapeDtypeStruct((n, d), F32), jax.ShapeDtypeStruct((n * SLABS, LANES), jnp.uint32),
                   jax.ShapeDtypeStruct((n, ROUTE_W), F32),
                   jax.ShapeDtypeStruct((n // tm, 8, ROUTE_W), F32)],
        compiler_params=_cparams("parallel"),
        name="merge_route",
    )(x2, y_rw, y_mla, gates, *ws)


def _plan_kernel(route_ref, base_ref, dest_ref):
    tm = route_ref.shape[0]
    route = route_ref[...]
    lane = lax.broadcasted_iota(jnp.int32, route.shape, 1).astype(F32)
    pick = [lane == route[:, k:k + 1] for k in range(TOP_K)]
    both = jnp.where(pick[0] | pick[1], 1.0, 0.0).astype(BF16)
    r = lax.broadcasted_iota(jnp.int32, (tm, tm), 0)
    c = lax.broadcasted_iota(jnp.int32, (tm, tm), 1)
    earlier = jnp.where(r > c, 1.0, 0.0).astype(BF16)
    offs = jnp.dot(earlier, both, preferred_element_type=F32) + base_ref[0]
    rows = [jnp.sum(jnp.where(pk, offs, 0.0), axis=-1, keepdims=True) for pk in pick]
    dest_ref[...] = jnp.where(lane == 0.0, rows[0], jnp.where(lane == 1.0, rows[1], 0.0)).astype(jnp.int32)


def _plan(route, base, tm):
    n = route.shape[0]
    return pl.pallas_call(
        _plan_kernel,
        grid=(n // tm,),
        in_specs=[pl.BlockSpec((tm, ROUTE_W), lambda i: (i, 0)),
                  pl.BlockSpec((1, 1, ROUTE_W), lambda i: (i, 0, 0))],
        out_specs=pl.BlockSpec((tm, ROUTE_W), lambda i: (i, 0)),
        out_shape=jax.ShapeDtypeStruct((n, ROUTE_W), jnp.int32),
        compiler_params=_cparams("parallel"),
        name="route_plan",
    )(route, base)


DISPATCH_TILE = 1024
ROW_DMA_UNROLL = 8


def _dispatch_kernel(dest_ref, h_ref, zero_ref, xs_ref, sem):
    del zero_ref
    tm = h_ref.shape[0] // SLABS

    def start(t, _):
        for k in range(TOP_K):
            pltpu.make_async_copy(_slab_rows(h_ref, t), _slab_rows(xs_ref, dest_ref[0, 0, TOP_K * t + k]),
                                  sem).start(priority=k % 2)
        return 0

    lax.fori_loop(0, tm, start, 0, unroll=ROW_DMA_UNROLL)
    all_rows = xs_ref.at[pl.ds(0, TOP_K * tm * SLABS), :]
    pltpu.make_async_copy(all_rows, all_rows, sem).wait()


def _dispatch(h2, dest, xs_init):
    n = h2.shape[0] // SLABS
    tm = DISPATCH_TILE
    dest3 = dest.reshape(n // tm, 1, TOP_K * tm)
    return pl.pallas_call(
        _dispatch_kernel,
        grid=(n // tm,),
        in_specs=[pl.BlockSpec((1, 1, TOP_K * tm), lambda i: (i, 0, 0), memory_space=pltpu.SMEM),
                  pl.BlockSpec((tm * SLABS, LANES), lambda i: (i, 0)),
                  pl.BlockSpec(memory_space=pl.ANY)],
        out_specs=pl.BlockSpec(memory_space=pl.ANY),
        out_shape=jax.ShapeDtypeStruct(xs_init.shape, xs_init.dtype),
        scratch_shapes=[pltpu.SemaphoreType.DMA(())],
        input_output_aliases={2: 0},
        compiler_params=_cparams("arbitrary"),
        name="dispatch",
    )(dest3, h2, xs_init)


def _expert_kernel(blk_e_ref, n_used_ref, x_ref, wgu_ref, wd_ref, y_ref):
    del blk_e_ref

    @pl.when(pl.program_id(0) < n_used_ref[0])
    def _():
        x = _unpack_rows(_slab_load(x_ref)).astype(BF16)
        h = jnp.dot(x, wgu_ref[0], preferred_element_type=F32)
        gt = h[:, :D_EXPERT]
        up = h[:, D_EXPERT:]
        act = (gt * jax.nn.sigmoid(gt) * up).astype(BF16)
        _slab_store(y_ref, _pack_rows(jnp.dot(act, wd_ref[0], preferred_element_type=F32)))

    @pl.when(pl.program_id(0) >= n_used_ref[0])
    def _():
        y_ref[...] = jnp.zeros_like(y_ref)


def _experts(xs, blk_expert, n_used, w_gu, w_down):
    p_rows = xs.shape[0] // SLABS
    d = 2 * SLABS * LANES
    n_blocks = p_rows // EXPERT_BLOCK
    rows = pl.BlockSpec((EXPERT_BLOCK * SLABS, LANES), lambda i, be, nu: (i, 0))
    grid_spec = pltpu.PrefetchScalarGridSpec(
        num_scalar_prefetch=2,
        grid=(n_blocks,),
        in_specs=[rows,
                  pl.BlockSpec((1, d, 2 * D_EXPERT), lambda i, be, nu: (be[i], 0, 0)),
                  pl.BlockSpec((1, D_EXPERT, d), lambda i, be, nu: (be[i], 0, 0))],
        out_specs=rows,
    )
    return pl.pallas_call(
        _expert_kernel,
        grid_spec=grid_spec,
        out_shape=jax.ShapeDtypeStruct(xs.shape, jnp.uint32),
        compiler_params=_cparams("arbitrary"),
        name="experts",
    )(blk_expert, n_used, xs, w_gu, w_down)


COMBINE_TILE = 256


def _combine_kernel(dest_ref, dest_next_ref, x1_ref, route_ref, g_ref, yb_ref, o_ref,
                    buf00, buf01, buf10, buf11, sems, *, final_norm):
    tm = x1_ref.shape[0]
    i = pl.program_id(0)
    bufs = ((buf00, buf01), (buf10, buf11))

    def issue(d_ref, slot):
        def start(t, _):
            for k in range(TOP_K):
                pltpu.make_async_copy(_slab_rows(yb_ref, d_ref[0, 0, TOP_K * t + k]),
                                      _slab_rows(bufs[slot][k], t), sems.at[slot]).start(priority=k % 2)
            return 0

        lax.fori_loop(0, tm, start, 0, unroll=ROW_DMA_UNROLL)

    @pl.when(i == 0)
    def _():
        issue(dest_ref, 0)

    for slot in range(2):
        @pl.when((i % 2 == slot) & (i + 1 < pl.num_programs(0)))
        def _(slot=slot):
            issue(dest_next_ref, 1 - slot)

    for slot in range(2):
        @pl.when(i % 2 == slot)
        def _(slot=slot):
            for b in bufs[slot]:
                pltpu.make_async_copy(b, b, sems.at[slot]).wait()
            route = route_ref[...]
            x2 = (x1_ref[...] + route[:, 2:3] * _unpack_rows(_slab_load(bufs[slot][0]))
                  + route[:, 3:4] * _unpack_rows(_slab_load(bufs[slot][1])))
            o_ref[...] = _rms(x2, g_ref[...]) if final_norm else x2


def _combine(x1, route, dest, yb, final_g, final_norm):
    n, d = x1.shape
    tm = COMBINE_TILE
    n_tiles = n // tm
    dest3 = dest.reshape(n_tiles, 1, TOP_K * tm)
    return pl.pallas_call(
        functools.partial(_combine_kernel, final_norm=final_norm),
        grid=(n_tiles,),
        in_specs=[pl.BlockSpec((1, 1, TOP_K * tm), lambda i: (i, 0, 0), memory_space=pltpu.SMEM),
                  pl.BlockSpec((1, 1, TOP_K * tm), lambda i: (jnp.minimum(i + 1, n_tiles - 1), 0, 0),
                               memory_space=pltpu.SMEM),
                  pl.BlockSpec((tm, d), lambda i: (i, 0)),
                  pl.BlockSpec((tm, ROUTE_W), lambda i: (i, 0)),
                  pl.BlockSpec((1, d), lambda i: (0, 0)),
                  pl.BlockSpec(memory_space=pl.ANY)],
        out_specs=pl.BlockSpec((tm, d), lambda i: (i, 0)),
        out_shape=jax.ShapeDtypeStruct((n, d), F32),
        scratch_shapes=[pltpu.VMEM((tm * SLABS, LANES), jnp.uint32) for _ in range(2 * TOP_K)]
                       + [pltpu.SemaphoreType.DMA((2,))],
        compiler_params=_cparams("arbitrary"),
        name="combine",
    )(dest3, dest3, x1, route, final_g, yb)


def _rwkv_params(rw_mu, rw_w0, rw_w_up, rw_a0, rw_a_up, rw_g_up, rw_k_k, rw_k_a, rw_r_k, rw_gn_w, rw_gn_b):
    row = lambda v: v.reshape(1, -1).astype(F32)
    zeros = jnp.zeros((A_LORA, RW_DIM), F32)
    return {
        "mu_r": row(rw_mu[:RW_DIM]), "mu_k": row(rw_mu[RW_DIM:2 * RW_DIM]),
        "mu_v": row(rw_mu[2 * RW_DIM:3 * RW_DIM]), "mu_l": row(rw_mu[3 * RW_DIM:]),
        "w0": row(rw_w0), "a0": row(rw_a0), "k_k": row(rw_k_k), "k_a": row(rw_k_a),
        "r_k": row(rw_r_k), "gn_w": row(rw_gn_w), "gn_b": row(rw_gn_b),
        "w_up": jnp.concatenate([rw_w_up, zeros], axis=0).astype(F32),
        "a_up": jnp.concatenate([zeros, rw_a_up], axis=0).astype(BF16),
        "g_up": rw_g_up.astype(BF16),
    }


def _mla_params(g_qa, w_q_up, g_kva, w_kv_up):
    half = QK_ROPE // 2
    pad = MLA_SLOT - QK_NOPE - QK_ROPE
    wq = w_q_up.reshape(Q_LORA, MLA_HEADS, QK_NOPE + QK_ROPE)
    q_nope, q_r1, q_r2 = wq[..., :QK_NOPE], wq[..., QK_NOPE:QK_NOPE + half], wq[..., QK_NOPE + half:]
    zq = lambda w: jnp.zeros((Q_LORA, MLA_HEADS, w), F32)
    w_qa = jnp.concatenate([q_nope, q_r1, q_r2, zq(pad)], axis=-1).reshape(Q_LORA, -1)
    w_qb = jnp.concatenate([zq(QK_NOPE), -q_r2, q_r1, zq(pad)], axis=-1).reshape(Q_LORA, -1)
    wkv = w_kv_up.reshape(KV_LORA, MLA_HEADS, QK_NOPE + V_HEAD)
    w_k = jnp.concatenate([wkv[..., :QK_NOPE], jnp.zeros((KV_LORA, MLA_HEADS, MLA_SLOT - QK_NOPE), F32)],
                          axis=-1).reshape(KV_LORA, -1)
    w_v = wkv[..., QK_NOPE:].reshape(KV_LORA, -1)
    eye = jnp.eye(half, dtype=F32)
    z = jnp.zeros((half, half), F32)
    zl = jnp.zeros((QK_ROPE, QK_NOPE), F32)
    zr = jnp.zeros((QK_ROPE, pad), F32)
    p_a = jnp.concatenate([zl, jnp.concatenate([eye, z], 0), jnp.concatenate([z, eye], 0), zr], axis=1)
    p_b = jnp.concatenate([zl, jnp.concatenate([z, -eye], 0), jnp.concatenate([eye, z], 0), zr], axis=1)
    return {"g_qa": g_qa.reshape(1, -1), "g_kva": g_kva.reshape(1, -1),
            "w_qa": w_qa.astype(BF16), "w_qb": w_qb.astype(BF16), "w_k": w_k.astype(BF16),
            "w_v": w_v.astype(BF16), "p_a": p_a.astype(BF16), "p_b": p_b.astype(BF16)}


def _rope_tables(positions):
    half = QK_ROPE // 2
    inv_freq = ROPE_THETA ** (-jnp.arange(0, QK_ROPE, 2, dtype=F32) / QK_ROPE)
    ang = positions.astype(F32).reshape(-1, 1) * inv_freq
    cos, sin = jnp.cos(ang), jnp.sin(ang)
    n = ang.shape[0]
    pad = MLA_SLOT - QK_NOPE - QK_ROPE
    cos_t = jnp.concatenate([jnp.ones((n, QK_NOPE), F32), cos, cos, jnp.zeros((n, pad), F32)], axis=1)
    sin_t = jnp.concatenate([jnp.zeros((n, QK_NOPE), F32), sin, sin, jnp.zeros((n, pad), F32)], axis=1)
    del half
    return cos_t, sin_t


def _block_layout(hist, n_assign):
    tile_counts = hist[:, 0, N_GROUPS:N_GROUPS + N_EXPERTS].astype(jnp.int32)
    counts = jnp.sum(tile_counts, axis=0)
    padded = (counts + EXPERT_BLOCK - 1) // EXPERT_BLOCK * EXPERT_BLOCK
    pad_end = jnp.cumsum(padded)
    pad_start = pad_end - padded
    tile_base = jnp.cumsum(tile_counts, axis=0) - tile_counts + pad_start[None, :]
    base = jnp.pad(tile_base.astype(F32), ((0, 0), (0, ROUTE_W - N_EXPERTS)))[:, None, :]
    n_blocks = -(-n_assign // EXPERT_BLOCK) + N_EXPERTS
    blk_row = jnp.arange(n_blocks, dtype=jnp.int32) * EXPERT_BLOCK
    blk_expert = jnp.minimum(jnp.sum((pad_end[None, :] <= blk_row[:, None]).astype(jnp.int32), axis=1),
                             N_EXPERTS - 1)
    n_used = (pad_end[-1] // EXPERT_BLOCK).astype(jnp.int32).reshape(1)
    return base, blk_expert, n_used, n_blocks


def kernel(x, positions, mix_norm_g, w_in, rw_mu, rw_w0, rw_w_up, rw_a0, rw_a_up, rw_g_up, rw_k_k, rw_k_a, rw_r_k, rw_gn_w, rw_gn_b, mla_g_qa, mla_w_q_up, mla_g_kva, mla_w_kv_up, w_branch_rw, w_branch_mla, w_out, ffn_norm_g, moe_w_group, moe_b_group, moe_w_router, moe_b_router, moe_w_gu, moe_w_down, final_norm_g):
    batch, seq, d = x.shape
    assert d == 2 * SLABS * LANES
    n = batch * seq
    depth = w_in.shape[0]
    rw_cols = 3 * RW_DIM + W_LORA + A_LORA + G_LORA
    mla_cols = Q_LORA + KV_LORA + QK_ROPE
    cos_t, sin_t = _rope_tables(positions)
    x2 = x.reshape(n, d)

    for l in range(depth):
        wl = w_in[l].astype(BF16)
        c_rw, c_q, c_kvr, gates = _in_proj(
            x2, mix_norm_g[l].reshape(1, d), wl[:, :rw_cols], wl[:, rw_cols:rw_cols + Q_LORA],
            wl[:, rw_cols + Q_LORA:rw_cols + mla_cols], wl[:, rw_cols + mla_cols:], tm=ROW_TILE)

        rp = _rwkv_params(rw_mu[l], rw_w0[l], rw_w_up[l], rw_a0[l], rw_a_up[l], rw_g_up[l], rw_k_k[l],
                          rw_k_a[l], rw_r_k[l], rw_gn_w[l], rw_gn_b[l])
        y_rw = _rwkv(c_rw.reshape(batch, seq, rw_cols), rp, batch, seq).reshape(n, RW_DIM)

        mp = _mla_params(mla_g_qa[l], mla_w_q_up[l], mla_g_kva[l], mla_w_kv_up[l])
        q, k, v = _mla_prep(c_q, c_kvr, cos_t, sin_t, mp, tm=ROW_TILE)
        y_mla = _mla_attn(q.reshape(batch, seq, -1), k.reshape(batch, seq, -1),
                          v.reshape(batch, seq, -1), batch, seq).reshape(n, MLA_HEADS * V_HEAD)

        w_route = jnp.concatenate(
            [moe_w_group[l], moe_w_router[l], jnp.zeros((d, ROUTE_W - N_GROUPS - N_EXPERTS), F32)], axis=1)
        b_route = jnp.concatenate(
            [moe_b_group[l], moe_b_router[l], jnp.zeros((ROUTE_W - N_GROUPS - N_EXPERTS,), F32)]).reshape(1, -1)
        wr_hi = w_route.astype(BF16)
        wr_lo = (w_route - wr_hi.astype(F32)).astype(BF16)
        mparams = {"w_br": w_branch_rw[l].astype(BF16), "w_bm": w_branch_mla[l].astype(BF16),
                   "w_out": w_out[l].astype(BF16), "ffn_g": ffn_norm_g[l].reshape(1, d),
                   "wr_hi": wr_hi, "wr_lo": wr_lo, "b_route": b_route}
        x1, h2p, route, hist = _merge(x2, y_rw, y_mla, gates, mparams, tm=ROW_TILE)

        base, blk_expert, n_used, n_blocks = _block_layout(hist, n * TOP_K)
        dest = _plan(route, base, tm=ROW_TILE)[:, :TOP_K]
        xs = _dispatch(h2p, dest,
                       jnp.zeros((n_blocks * EXPERT_BLOCK * SLABS, LANES), jnp.uint32))
        yb = _experts(xs, blk_expert, n_used, moe_w_gu[l].astype(BF16), moe_w_down[l].astype(BF16))
        x2 = _combine(x1, route, dest, yb, final_norm_g.reshape(1, d), final_norm=(l == depth - 1))

    return x2.reshape(batch, seq, d)
```

```python
import functools
import math

import jax
import jax.numpy as jnp
from jax import lax
from jax.experimental import pallas as pl
from jax.experimental.pallas import tpu as pltpu

F32 = jnp.float32
BF16 = jnp.bfloat16

RW_HEADS = 8
RW_HEAD_DIM = 64
RW_DIM = RW_HEADS * RW_HEAD_DIM
W_LORA = 64
A_LORA = 64
G_LORA = 128
GN_EPS = 64e-5
MLA_HEADS = 8
QK_NOPE = 64
QK_ROPE = 32
V_HEAD = 64
Q_LORA = 384
KV_LORA = 256
ROPE_THETA = 10000.0
N_GROUPS = 4
EXPERTS_PER_GROUP = 8
N_EXPERTS = N_GROUPS * EXPERTS_PER_GROUP
TOP_K = 2
D_EXPERT = 256
EXPERT_BLOCK = 512
NORM_EPS = 1e-6

LANES = 128
HEAD_PAIR = 2 * RW_HEAD_DIM
VMEM_LIMIT = 48 * 1024 * 1024
ROW_TILE = 512


def _cparams(*sem):
    return pltpu.CompilerParams(dimension_semantics=sem, vmem_limit_bytes=VMEM_LIMIT)


def _mm(a, b, dims=((1,), (0,)), split=False):
    dn = (dims, ((), ()))
    dot = lambda x, y: lax.dot_general(x, y, dn, preferred_element_type=F32)
    a_hi = a.astype(BF16)
    b_hi = b.astype(BF16)
    if not split:
        return dot(a_hi, b_hi)
    a_lo = (a - a_hi.astype(F32)).astype(BF16)
    b_lo = (b - b_hi.astype(F32)).astype(BF16)
    return dot(a_hi, b_hi) + dot(a_lo, b_hi) + dot(a_hi, b_lo)


def _mm_sel(sel_bf16, x, dims=((1,), (0,))):
    dn = (dims, ((), ()))
    hi = x.astype(BF16)
    r1 = x - hi.astype(F32)
    mid = r1.astype(BF16)
    lo = (r1 - mid.astype(F32)).astype(BF16)
    out = lax.dot_general(sel_bf16, hi, dn, preferred_element_type=F32)
    out = out + lax.dot_general(sel_bf16, mid, dn, preferred_element_type=F32)
    return out + lax.dot_general(sel_bf16, lo, dn, preferred_element_type=F32)


def _seg_sum(x, seg_bf16):
    hi = x.astype(BF16)
    lo = (x - hi.astype(F32)).astype(BF16)
    return (jnp.dot(hi, seg_bf16, preferred_element_type=F32)
            + jnp.dot(lo, seg_bf16, preferred_element_type=F32))


def _rms(x, g):
    return x * lax.rsqrt(jnp.mean(x * x, axis=-1, keepdims=True) + NORM_EPS) * g


def _pack_rows(x):
    half = x.shape[1] // 2
    bits = lambda v: lax.bitcast_convert_type(v.astype(BF16).astype(F32), jnp.uint32)
    return bits(x[:, :half]) | (bits(x[:, half:]) >> 16)


def _unpack_rows(p):
    hi = lax.bitcast_convert_type(p & jnp.uint32(0xFFFF0000), F32)
    lo = lax.bitcast_convert_type(p << 16, F32)
    return jnp.concatenate([hi, lo], axis=1)


SLABS = 4


def _slab_rows(ref, r):
    return ref.at[pl.ds(pl.multiple_of(r * SLABS, SLABS), SLABS), :]


def _slab_load(ref):
    rows = ref.shape[0] // SLABS
    return jnp.concatenate([ref[pl.ds(j, rows, stride=SLABS), :] for j in range(SLABS)], axis=1)


def _slab_store(ref, x):
    rows = ref.shape[0] // SLABS
    for j in range(SLABS):
        ref[pl.ds(j, rows, stride=SLABS), :] = x[:, j * LANES:(j + 1) * LANES]


def _in_proj_kernel(x_ref, g_ref, wrw_ref, wq_ref, wkvr_ref, wg_ref,
                    crw_ref, cq_ref, ckvr_ref, gate_ref):
    hb = _rms(x_ref[...], g_ref[...]).astype(BF16)
    crw_ref[...] = jnp.dot(hb, wrw_ref[...], preferred_element_type=F32)
    cq_ref[...] = jnp.dot(hb, wq_ref[...], preferred_element_type=F32)
    ckvr_ref[...] = jnp.dot(hb, wkvr_ref[...], preferred_element_type=F32)
    gate_ref[...] = jax.nn.sigmoid(jnp.dot(hb, wg_ref[...], preferred_element_type=F32)).astype(BF16)


def _in_proj(x2, g, w_rw, w_q, w_kvr, w_gate, tm):
    n, d = x2.shape
    full = lambda w: pl.BlockSpec(w.shape, lambda i: (0, 0))
    row = lambda c: pl.BlockSpec((tm, c), lambda i: (i, 0))
    return pl.pallas_call(
        _in_proj_kernel,
        grid=(n // tm,),
        in_specs=[row(d), full(g), full(w_rw), full(w_q), full(w_kvr), full(w_gate)],
        out_specs=[row(w_rw.shape[1]), row(w_q.shape[1]), row(w_kvr.shape[1]), row(w_gate.shape[1])],
        out_shape=[jax.ShapeDtypeStruct((n, w_rw.shape[1]), F32),
                   jax.ShapeDtypeStruct((n, w_q.shape[1]), F32),
                   jax.ShapeDtypeStruct((n, w_kvr.shape[1]), F32),
                   jax.ShapeDtypeStruct((n, w_gate.shape[1]), BF16)],
        compiler_params=_cparams("parallel"),
        name="in_proj",
    )(x2, g, w_rw, w_q, w_kvr, w_gate)


RW_CHUNK = 64
RW_TILE = 1024
RW_GROUP = 16


def _token_shift(cur, halo_ref, first):
    prev_row = jnp.where(first, 0.0, halo_ref[0, 7:8, :])
    rolled = pltpu.roll(cur, 1, 0)
    row = lax.broadcasted_iota(jnp.int32, cur.shape, 0)
    return jnp.where(row == 0, prev_row, rolled)


def _rwkv_kernel(r_ref, k_ref, v_ref, l_ref, hr_ref, hk_ref, hv_ref, hl_ref,
                 mur_ref, muk_ref, muv_ref, mul_ref, w0_ref, a0_ref, kk_ref, ka_ref, rk_ref,
                 gnw_ref, gnb_ref, wup_ref, aup_ref, gup_ref, y_ref, st_ref):
    i = pl.program_id(2)
    first = i == 0

    @pl.when(first)
    def _():
        st_ref[...] = jnp.zeros_like(st_ref)

    def mixed(c_ref, h_ref, mu_ref):
        cur = c_ref[0]
        return cur + (_token_shift(cur, h_ref, first) - cur) * mu_ref[...]

    zr = mixed(r_ref, hr_ref, mur_ref)
    zk = mixed(k_ref, hk_ref, muk_ref)
    zv = mixed(v_ref, hv_ref, muv_ref)
    zl = mixed(l_ref, hl_ref, mul_ref)
    z_wa = zl[:, :LANES]
    z_g = zl[:, LANES:]

    lane = lax.broadcasted_iota(jnp.int32, (LANES, LANES), 1)
    sub = lax.broadcasted_iota(jnp.int32, (LANES, LANES), 0)
    same_head = (lane // RW_HEAD_DIM) == (sub // RW_HEAD_DIM)
    seg = jnp.where(same_head, 1.0, 0.0).astype(BF16)

    w = w0_ref[...] + _mm(jnp.tanh(z_wa), wup_ref[...], split=True)
    u = -w
    softplus = jnp.maximum(u, 0.0) + jnp.log(1.0 + jnp.exp(-jnp.abs(u)))
    log_decay = -jnp.exp(-softplus - 0.5)
    a = jax.nn.sigmoid(a0_ref[...] + _mm(z_wa, aup_ref[...]))
    g = _mm(jax.nn.sigmoid(z_g), gup_ref[...])

    kk = zk * kk_ref[...]
    kk = kk / jnp.maximum(jnp.sqrt(_seg_sum(kk * kk, seg)), 1e-12)
    k2 = zk * (1.0 + (a - 1.0) * ka_ref[...])
    bonus = _seg_sum(zr * k2 * rk_ref[...], seg) * zv
    kka = kk * a

    c = RW_CHUNK
    crow = lax.broadcasted_iota(jnp.int32, (c, c), 0)
    ccol = lax.broadcasted_iota(jnp.int32, (c, c), 1)
    cum_sel = jnp.where(crow >= ccol, 1.0, 0.0).astype(BF16)
    tril_incl = sub >= lane
    tril_strict = sub > lane
    eye_l = jnp.where(lane == sub, 1.0, 0.0).astype(F32)
    lo_half = lax.broadcasted_iota(jnp.int32, (c, LANES), 1) < RW_HEAD_DIM
    nt = ((1,), (1,))
    tn = ((0,), (0,))
    zeros_blk = jnp.zeros((2 * c, LANES), F32)

    def stack(t):
        return jnp.concatenate([jnp.where(lo_half, t, 0.0), jnp.where(lo_half, 0.0, t)], axis=0)

    tril_incl2 = jnp.concatenate([tril_incl, tril_incl], axis=1)

    def chunk_group(ids):
        chunks = range(len(ids))
        x_a, x_b, x_k, x_r, x_v, x_bh, x_kh, w_tot = [], [], [], [], [], [], [], []
        for ci in ids:
            sl = slice(ci * c, (ci + 1) * c)
            ld = log_decay[sl]
            cum = _mm_sel(cum_sel, ld)
            tot = cum[c - 1:c, :]
            e_neg = jnp.exp(-cum)
            e_rest = jnp.exp(tot - cum)
            x_a.append(stack(-kk[sl] * jnp.exp(cum - ld)))
            x_b.append(stack(kka[sl] * e_neg))
            x_k.append(stack(k2[sl] * e_neg))
            x_r.append(stack(zr[sl] * jnp.exp(cum)))
            x_v.append(stack(zv[sl]))
            x_bh.append(stack(kka[sl] * e_rest))
            x_kh.append(stack(k2[sl] * e_rest))
            w_tot.append(jnp.exp(tot))

        inter = [_mm(jnp.concatenate([x_a[i], x_r[i]], axis=0),
                     jnp.concatenate([x_b[i], x_k[i]], axis=0), nt) for i in chunks]
        a_ab = [jnp.where(tril_strict, m[:2 * c, :2 * c], 0.0) for m in inter]
        a_ak = [jnp.where(tril_strict, m[:2 * c, 2 * c:], 0.0) for m in inter]
        a_r = [jnp.where(tril_incl2, m[2 * c:], 0.0) for m in inter]
        w_ak = [_mm(a_ak[i], x_v[i]) for i in chunks]

        t_inv = [eye_l + m for m in a_ab]
        pw = a_ab
        for _ in range(int(math.log2(c)) - 1):
            pw = [_mm(m, m) for m in pw]
            t_inv = [t_inv[i] + _mm(t_inv[i], pw[i]) for i in chunks]

        solved = [_mm(t_inv[i], jnp.concatenate([x_a[i], w_ak[i]], axis=1)) for i in chunks]
        rhs = [jnp.concatenate([solved[i], jnp.concatenate([zeros_blk, x_v[i]], axis=1)], axis=0)
               for i in chunks]
        out = [_mm(a_r[i], rhs[i]) for i in chunks]
        carry = [_mm(jnp.concatenate([x_bh[i], x_kh[i]], axis=0), rhs[i], tn) for i in chunks]
        q_hat, y_loc = [], []
        for i in chunks:
            q_st = x_r[i] + out[i][:, :LANES]
            q_hat.append(q_st[:c] + q_st[c:])
            y_loc.append(out[i][:c, LANES:] + out[i][c:, LANES:])
        trans = [jnp.concatenate([eye_l * w_tot[i] + carry[i][:, :LANES], carry[i][:, LANES:]], axis=1)
                 for i in chunks]
        return q_hat, y_loc, trans

    n_chunks = RW_TILE // c
    groups = [chunk_group(range(g, g + RW_GROUP)) for g in range(0, n_chunks, RW_GROUP)]
    q_hat, y_loc, trans = (sum((g[j] for g in groups), []) for j in range(3))
    chunks = range(n_chunks)

    zeros_sq = jnp.zeros((LANES, LANES), F32)

    def compose(later, earlier):
        return _mm(later[:, :LANES], earlier) + jnp.concatenate([zeros_sq, later[:, LANES:]], axis=1)

    scan = list(trans)
    dist = 1
    while dist < len(scan):
        scan = [scan[i] if i < dist else compose(scan[i], scan[i - dist]) for i in range(len(scan))]
        dist *= 2
    prefix = [None] + scan
    q_pre = [None] + [_mm(q_hat[i], prefix[i]) for i in chunks[1:]]
    q_m = jnp.concatenate([q_hat[0]] + [q_pre[i][:, :LANES] for i in chunks[1:]], axis=0)
    y_off = jnp.concatenate([y_loc[0]] + [y_loc[i] + q_pre[i][:, LANES:] for i in chunks[1:]], axis=0)
    state = st_ref[...]
    y = _mm(q_m, state) + y_off
    st_ref[...] = _mm(prefix[-1][:, :LANES], state) + prefix[-1][:, LANES:]

    inv_n = 1.0 / RW_HEAD_DIM
    mean = _seg_sum(y, seg) * inv_n
    d = y - mean
    var = _seg_sum(d * d, seg) * inv_n
    yn = d * lax.rsqrt(var + GN_EPS) * gnw_ref[...] + gnb_ref[...]
    y_ref[0] = ((yn + bonus) * g).astype(y_ref.dtype)


def _rwkv(c_rw, p, batch, seq):
    ts = RW_TILE
    n_pairs = RW_DIM // HEAD_PAIR
    lora_blk = (3 * RW_DIM) // (2 * LANES)
    halo = ts // 8

    def col(off):
        return pl.BlockSpec((1, ts, LANES), lambda b, pp, i, off=off: (b, i, off + pp))

    def col_halo(off):
        return pl.BlockSpec((1, 8, LANES),
                            lambda b, pp, i, off=off: (b, jnp.maximum(i * halo - 1, 0), off + pp))

    vec = pl.BlockSpec((1, LANES), lambda b, pp, i: (0, pp))
    lora_w = pl.BlockSpec((LANES, LANES), lambda b, pp, i: (0, pp))
    in_specs = [
        col(0), col(n_pairs), col(2 * n_pairs),
        pl.BlockSpec((1, ts, 2 * LANES), lambda b, pp, i: (b, i, lora_blk)),
        col_halo(0), col_halo(n_pairs), col_halo(2 * n_pairs),
        pl.BlockSpec((1, 8, 2 * LANES), lambda b, pp, i: (b, jnp.maximum(i * halo - 1, 0), lora_blk)),
        vec, vec, vec, pl.BlockSpec((1, 2 * LANES), lambda b, pp, i: (0, 0)),
        vec, vec, vec, vec, vec, vec, vec, lora_w, lora_w, lora_w,
    ]
    return pl.pallas_call(
        _rwkv_kernel,
        grid=(batch, n_pairs, seq // ts),
        in_specs=in_specs,
        out_specs=pl.BlockSpec((1, ts, LANES), lambda b, pp, i: (b, i, pp)),
        out_shape=jax.ShapeDtypeStruct((batch, seq, RW_DIM), BF16),
        scratch_shapes=[pltpu.VMEM((LANES, LANES), F32)],
        compiler_params=_cparams("parallel", "parallel", "arbitrary"),
        name="rwkv",
    )(c_rw, c_rw, c_rw, c_rw, c_rw, c_rw, c_rw, c_rw,
      p["mu_r"], p["mu_k"], p["mu_v"], p["mu_l"], p["w0"], p["a0"], p["k_k"], p["k_a"], p["r_k"],
      p["gn_w"], p["gn_b"], p["w_up"], p["a_up"], p["g_up"])


MLA_SLOT = 128


def _mla_prep_kernel(cq_ref, ckvr_ref, cs_ref, gq_ref, gkv_ref,
                     wqa_ref, wqb_ref, wk_ref, wv_ref, pa_ref, pb_ref, place_ref, one_ref,
                     q_ref, k_ref, v_ref):
    cs = cs_ref[...]
    cs_hi = cs.astype(BF16)
    cs_lo = (cs - cs_hi.astype(F32)).astype(BF16)
    tables = (jnp.dot(cs_hi, place_ref[...], preferred_element_type=F32)
              + jnp.dot(cs_lo, place_ref[...], preferred_element_type=F32))
    cos = tables[:, :MLA_SLOT] + one_ref[...]
    sin = tables[:, MLA_SLOT:]
    zq = _rms(cq_ref[...], gq_ref[...]).astype(BF16)
    qa = jnp.dot(zq, wqa_ref[...], preferred_element_type=F32)
    qb = jnp.dot(zq, wqb_ref[...], preferred_element_type=F32)
    ckvr = ckvr_ref[...]
    zkv = _rms(ckvr[:, :KV_LORA], gkv_ref[...]).astype(BF16)
    kn = jnp.dot(zkv, wk_ref[...], preferred_element_type=F32)
    v_ref[...] = jnp.dot(zkv, wv_ref[...], preferred_element_type=F32).astype(BF16)
    kr = ckvr[:, KV_LORA:].astype(BF16)
    k_rope = (jnp.dot(kr, pa_ref[...], preferred_element_type=F32) * cos
              + jnp.dot(kr, pb_ref[...], preferred_element_type=F32) * sin)
    scale = math.log2(math.e) / math.sqrt(QK_NOPE + QK_ROPE)
    for h in range(MLA_HEADS):
        sl = slice(h * MLA_SLOT, (h + 1) * MLA_SLOT)
        q_ref[:, sl] = ((qa[:, sl] * cos + qb[:, sl] * sin) * scale).astype(BF16)
        k_ref[:, sl] = (kn[:, sl] + k_rope).astype(BF16)


def _mla_prep(c_q, c_kvr, cos_sin, p, tm):
    n = c_q.shape[0]
    full = lambda w: pl.BlockSpec(w.shape, lambda i: (0, 0))
    row = lambda c: pl.BlockSpec((tm, c), lambda i: (i, 0))
    ws = [p["g_qa"], p["g_kva"], p["w_qa"], p["w_qb"], p["w_k"], p["w_v"], p["p_a"], p["p_b"],
          p["place"], p["one"]]
    hq = MLA_HEADS * MLA_SLOT
    return pl.pallas_call(
        _mla_prep_kernel,
        grid=(n // tm,),
        in_specs=[row(c_q.shape[1]), row(c_kvr.shape[1]), row(cos_sin.shape[1])] + [full(w) for w in ws],
        out_specs=[row(hq), row(hq), row(MLA_HEADS * V_HEAD)],
        out_shape=[jax.ShapeDtypeStruct((n, hq), BF16), jax.ShapeDtypeStruct((n, hq), BF16),
                   jax.ShapeDtypeStruct((n, MLA_HEADS * V_HEAD), BF16)],
        compiler_params=_cparams("parallel"),
        name="mla_prep",
    )(c_q, c_kvr, cos_sin, *ws)


ATT_TILE = 512


def _attn_kernel(q_ref, k_ref, v_ref, o_ref):
    qi = pl.program_id(2)
    t = ATT_TILE
    lo_v = lax.broadcasted_iota(jnp.int32, (t, 2 * V_HEAD), 1) < V_HEAD
    nt = (((1,), (1,)), ((), ()))
    def q_rows(h, lo, hi):
        return q_ref[0, lo:hi, h * MLA_SLOT:(h + 1) * MLA_SLOT]
    den_lane = (V_HEAD, 0)

    def augment(h, vb):
        lane_v = lax.broadcasted_iota(jnp.int32, vb.shape, 1)
        own = (lane_v < V_HEAD) if h == 0 else (lane_v >= V_HEAD)
        return jnp.where(own, vb, jnp.where(lane_v == den_lane[h], 1.0, 0.0).astype(BF16))

    def update(h, q, m, acc, k_rows, mask):
        kb = k_ref[0, k_rows, h * MLA_SLOT:(h + 1) * MLA_SLOT]
        vb = v_ref[0, k_rows, :]
        s = lax.dot_general(q, kb, nt, preferred_element_type=F32)
        if mask is not None:
            s = jnp.where(mask, s, -jnp.inf)
        m_new = jnp.maximum(m, jnp.max(s, axis=-1, keepdims=True))
        pr = jnp.exp2((s - m_new[:, :1]).astype(BF16))
        return m_new, acc * jnp.exp2(m - m_new) + jnp.dot(pr, augment(h, vb), preferred_element_type=F32)

    def full_block(j, carry):
        rows = pl.ds(pl.multiple_of(j * t, t), t)
        return tuple(update(h, q_rows(h, 0, t), *carry[h], rows, None) for h in range(2))

    init1 = (jnp.full((t, 2 * V_HEAD), -jnp.inf, F32), jnp.zeros((t, 2 * V_HEAD), F32))
    carry = lax.fori_loop(0, qi, full_block, (init1, init1))

    causal = lax.broadcasted_iota(jnp.int32, (t, t), 1) <= lax.broadcasted_iota(jnp.int32, (t, t), 0)
    diag_rows = pl.ds(pl.multiple_of(qi * t, t), t)
    (_, acc0), (_, acc1) = (update(h, q_rows(h, 0, t), *carry[h], diag_rows, causal) for h in range(2))
    den0 = acc0[:, den_lane[0]:den_lane[0] + 1]
    den1 = acc1[:, den_lane[1]:den_lane[1] + 1]
    o_ref[0] = jnp.where(lo_v, acc0 / den0, acc1 / den1).astype(o_ref.dtype)


def _mla_attn(q, k, v, batch, seq):
    t = ATT_TILE
    return pl.pallas_call(
        _attn_kernel,
        grid=(batch, MLA_HEADS // 2, seq // t),
        in_specs=[pl.BlockSpec((1, t, 2 * MLA_SLOT), lambda b, hp, i: (b, i, hp)),
                  pl.BlockSpec((1, seq, 2 * MLA_SLOT), lambda b, hp, i: (b, 0, hp)),
                  pl.BlockSpec((1, seq, 2 * V_HEAD), lambda b, hp, i: (b, 0, hp))],
        out_specs=pl.BlockSpec((1, t, 2 * V_HEAD), lambda b, hp, i: (b, i, hp)),
        out_shape=jax.ShapeDtypeStruct((batch, seq, MLA_HEADS * V_HEAD), BF16),
        compiler_params=_cparams("parallel", "parallel", "arbitrary"),
        name="mla_attn",
    )(q, k, v)


ROUTE_W = 128


def _merge_kernel(x_ref, yrw_ref, ymla_ref, gate_ref, wbr_ref, wbm_ref, wo_ref, fg_ref,
                  wr_hi_ref, wr_lo_ref, br_ref, x1_ref, h2p_ref, route_ref, hist_ref):
    d = x_ref.shape[1]
    a = jnp.dot(yrw_ref[...], wbr_ref[...], preferred_element_type=F32)
    b = jnp.dot(ymla_ref[...], wbm_ref[...], preferred_element_type=F32)
    merged = gate_ref[:, :d].astype(F32) * a + gate_ref[:, d:].astype(F32) * b
    x1 = x_ref[...] + jnp.dot(merged.astype(BF16), wo_ref[...], preferred_element_type=F32)
    x1_ref[...] = x1
    h2 = _rms(x1, fg_ref[...])
    _slab_store(h2p_ref, _pack_rows(h2))

    h_hi = h2.astype(BF16)
    h_lo = (h2 - h_hi.astype(F32)).astype(BF16)
    logits = (jnp.dot(h_hi, wr_hi_ref[...], preferred_element_type=F32)
              + jnp.dot(h_lo, wr_hi_ref[...], preferred_element_type=F32)
              + jnp.dot(h_hi, wr_lo_ref[...], preferred_element_type=F32)) + br_ref[...]

    lane = lax.broadcasted_iota(jnp.int32, logits.shape, 1)
    big = jnp.int32(ROUTE_W)
    neg = -jnp.inf

    def first_argmax(vals, vmax):
        return jnp.min(jnp.where(vals == vmax, lane, big), axis=-1, keepdims=True)

    grp = jnp.where(lane < N_GROUPS, logits, neg)
    g_max = jnp.max(grp, axis=-1, keepdims=True)
    g_den = jnp.sum(jnp.exp(grp - g_max), axis=-1, keepdims=True)
    g_sel = first_argmax(grp, g_max)
    gate_g = 1.0 / g_den
    lo = N_GROUPS + g_sel * EXPERTS_PER_GROUP
    fine = jnp.where((lane >= lo) & (lane < lo + EXPERTS_PER_GROUP), logits, neg)
    v1 = jnp.max(fine, axis=-1, keepdims=True)
    i1 = first_argmax(fine, v1)
    fine2 = jnp.where(lane == i1, neg, fine)
    v2 = jnp.max(fine2, axis=-1, keepdims=True)
    i2 = first_argmax(fine2, v2)
    e2 = jnp.exp(v2 - v1)
    den = 1.0 + e2
    w1 = gate_g / den
    w2 = gate_g * e2 / den
    route = jnp.where(lane == 0, (i1 - N_GROUPS).astype(F32),
                      jnp.where(lane == 1, (i2 - N_GROUPS).astype(F32),
                                jnp.where(lane == 2, w1, jnp.where(lane == 3, w2, 0.0))))
    route_ref[...] = route
    chosen = jnp.where((lane == i1) | (lane == i2), 1.0, 0.0)
    hist_ref[0] = jnp.broadcast_to(jnp.sum(chosen, axis=0, keepdims=True), hist_ref.shape[1:])


def _merge(x2, y_rw, y_mla, gates, p, tm):
    n, d = x2.shape
    full = lambda w: pl.BlockSpec(w.shape, lambda i: (0, 0))
    row = lambda c: pl.BlockSpec((tm, c), lambda i: (i, 0))
    ws = [p["w_br"], p["w_bm"], p["w_out"], p["ffn_g"], p["wr_hi"], p["wr_lo"], p["b_route"]]
    return pl.pallas_call(
        _merge_kernel,
        grid=(n // tm,),
        in_specs=[row(d), row(y_rw.shape[1]), row(y_mla.shape[1]), row(2 * d)] + [full(w) for w in ws],
        out_specs=[row(d), pl.BlockSpec((tm * SLABS, LANES), lambda i: (i, 0)), row(ROUTE_W),
                   pl.BlockSpec((1, 8, ROUTE_W), lambda i: (i, 0, 0))],
        out_shape=[jax.ShapeDtypeStruct((n, d), F32), jax.ShapeDtypeStruct((n * SLABS, LANES), jnp.uint32),
                   jax.ShapeDtypeStruct((n, ROUTE_W), F32),
                   jax.ShapeDtypeStruct((n // tm, 8, ROUTE_W), F32)],
        compiler_params=_cparams("parallel"),
        name="merge_route",
    )(x2, y_rw, y_mla, gates, *ws)


def _plan_kernel(route_ref, base_ref, dest_ref):
    tm = route_ref.shape[0]
    route = route_ref[...]
    lane = lax.broadcasted_iota(jnp.int32, route.shape, 1).astype(F32)
    pick = [lane == route[:, k:k + 1] for k in range(TOP_K)]
    both = jnp.where(pick[0] | pick[1], 1.0, 0.0).astype(BF16)
    r = lax.broadcasted_iota(jnp.int32, (tm, tm), 0)
    c = lax.broadcasted_iota(jnp.int32, (tm, tm), 1)
    earlier = jnp.where(r > c, 1.0, 0.0).astype(BF16)
    offs = jnp.dot(earlier, both, preferred_element_type=F32) + base_ref[0]
    rows = [jnp.sum(jnp.where(pk, offs, 0.0), axis=-1, keepdims=True) for pk in pick]
    dest_ref[...] = jnp.where(lane == 0.0, rows[0], jnp.where(lane == 1.0, rows[1], 0.0)).astype(jnp.int32)


def _plan(route, base, tm):
    n = route.shape[0]
    return pl.pallas_call(
        _plan_kernel,
        grid=(n // tm,),
        in_specs=[pl.BlockSpec((tm, ROUTE_W), lambda i: (i, 0)),
                  pl.BlockSpec((1, 1, ROUTE_W), lambda i: (i, 0, 0))],
        out_specs=pl.BlockSpec((tm, ROUTE_W), lambda i: (i, 0)),
        out_shape=jax.ShapeDtypeStruct((n, ROUTE_W), jnp.int32),
        compiler_params=_cparams("parallel"),
        name="route_plan",
    )(route, base)


DISPATCH_TILE = 1024
ROW_DMA_UNROLL = 8


def _dispatch_kernel(dest_ref, h_ref, zero_ref, xs_ref, sem):
    del zero_ref
    tm = h_ref.shape[0] // SLABS

    def start(t, _):
        for k in range(TOP_K):
            pltpu.make_async_copy(_slab_rows(h_ref, t), _slab_rows(xs_ref, dest_ref[0, 0, TOP_K * t + k]),
                                  sem).start(priority=k % 2)
        return 0

    lax.fori_loop(0, tm, start, 0, unroll=ROW_DMA_UNROLL)
    all_rows = xs_ref.at[pl.ds(0, TOP_K * tm * SLABS), :]
    pltpu.make_async_copy(all_rows, all_rows, sem).wait()


def _dispatch(h2, dest, xs_init):
    n = h2.shape[0] // SLABS
    tm = DISPATCH_TILE
    dest3 = dest.reshape(n // tm, 1, TOP_K * tm)
    return pl.pallas_call(
        _dispatch_kernel,
        grid=(n // tm,),
        in_specs=[pl.BlockSpec((1, 1, TOP_K * tm), lambda i: (i, 0, 0), memory_space=pltpu.SMEM),
                  pl.BlockSpec((tm * SLABS, LANES), lambda i: (i, 0)),
                  pl.BlockSpec(memory_space=pl.ANY)],
        out_specs=pl.BlockSpec(memory_space=pl.ANY),
        out_shape=jax.ShapeDtypeStruct(xs_init.shape, xs_init.dtype),
        scratch_shapes=[pltpu.SemaphoreType.DMA(())],
        input_output_aliases={2: 0},
        compiler_params=_cparams("arbitrary"),
        name="dispatch",
    )(dest3, h2, xs_init)


def _expert_kernel(blk_e_ref, n_used_ref, x_ref, wgu_ref, wd_ref, y_ref):
    del blk_e_ref

    @pl.when(pl.program_id(0) < n_used_ref[0])
    def _():
        x = _unpack_rows(_slab_load(x_ref)).astype(BF16)
        h = jnp.dot(x, wgu_ref[0], preferred_element_type=F32)
        gt = h[:, :D_EXPERT]
        up = h[:, D_EXPERT:]
        act = (gt * jax.nn.sigmoid(gt) * up).astype(BF16)
        _slab_store(y_ref, _pack_rows(jnp.dot(act, wd_ref[0], preferred_element_type=F32)))

    @pl.when(pl.program_id(0) >= n_used_ref[0])
    def _():
        y_ref[...] = jnp.zeros_like(y_ref)


def _experts(xs, blk_expert, n_used, w_gu, w_down):
    p_rows = xs.shape[0] // SLABS
    d = 2 * SLABS * LANES
    n_blocks = p_rows // EXPERT_BLOCK
    rows = pl.BlockSpec((EXPERT_BLOCK * SLABS, LANES), lambda i, be, nu: (i, 0))
    grid_spec = pltpu.PrefetchScalarGridSpec(
        num_scalar_prefetch=2,
        grid=(n_blocks,),
        in_specs=[rows,
                  pl.BlockSpec((1, d, 2 * D_EXPERT), lambda i, be, nu: (be[i], 0, 0)),
                  pl.BlockSpec((1, D_EXPERT, d), lambda i, be, nu: (be[i], 0, 0))],
        out_specs=rows,
    )
    return pl.pallas_call(
        _expert_kernel,
        grid_spec=grid_spec,
        out_shape=jax.ShapeDtypeStruct(xs.shape, jnp.uint32),
        compiler_params=_cparams("arbitrary"),
        name="experts",
    )(blk_expert, n_used, xs, w_gu, w_down)


COMBINE_TILE = 256


def _combine_kernel(dest_ref, dest_next_ref, x1_ref, route_ref, g_ref, yb_ref, o_ref,
                    buf00, buf01, buf10, buf11, sems, *, final_norm):
    tm = x1_ref.shape[0]
    i = pl.program_id(0)
    bufs = ((buf00, buf01), (buf10, buf11))

    def issue(d_ref, slot):
        def start(t, _):
            for k in range(TOP_K):
                pltpu.make_async_copy(_slab_rows(yb_ref, d_ref[0, 0, TOP_K * t + k]),
                                      _slab_rows(bufs[slot][k], t), sems.at[slot]).start(priority=k % 2)
            return 0

        lax.fori_loop(0, tm, start, 0, unroll=ROW_DMA_UNROLL)

    @pl.when(i == 0)
    def _():
        issue(dest_ref, 0)

    for slot in range(2):
        @pl.when((i % 2 == slot) & (i + 1 < pl.num_programs(0)))
        def _(slot=slot):
            issue(dest_next_ref, 1 - slot)

    for slot in range(2):
        @pl.when(i % 2 == slot)
        def _(slot=slot):
            for b in bufs[slot]:
                pltpu.make_async_copy(b, b, sems.at[slot]).wait()
            route = route_ref[...]
            x2 = (x1_ref[...] + route[:, 2:3] * _unpack_rows(_slab_load(bufs[slot][0]))
                  + route[:, 3:4] * _unpack_rows(_slab_load(bufs[slot][1])))
            o_ref[...] = _rms(x2, g_ref[...]) if final_norm else x2


def _combine(x1, route, dest, yb, final_g, final_norm):
    n, d = x1.shape
    tm = COMBINE_TILE
    n_tiles = n // tm
    dest3 = dest.reshape(n_tiles, 1, TOP_K * tm)
    return pl.pallas_call(
        functools.partial(_combine_kernel, final_norm=final_norm),
        grid=(n_tiles,),
        in_specs=[pl.BlockSpec((1, 1, TOP_K * tm), lambda i: (i, 0, 0), memory_space=pltpu.SMEM),
                  pl.BlockSpec((1, 1, TOP_K * tm), lambda i: (jnp.minimum(i + 1, n_tiles - 1), 0, 0),
                               memory_space=pltpu.SMEM),
                  pl.BlockSpec((tm, d), lambda i: (i, 0)),
                  pl.BlockSpec((tm, ROUTE_W), lambda i: (i, 0)),
                  pl.BlockSpec((1, d), lambda i: (0, 0)),
                  pl.BlockSpec(memory_space=pl.ANY)],
        out_specs=pl.BlockSpec((tm, d), lambda i: (i, 0)),
        out_shape=jax.ShapeDtypeStruct((n, d), F32),
        scratch_shapes=[pltpu.VMEM((tm * SLABS, LANES), jnp.uint32) for _ in range(2 * TOP_K)]
                       + [pltpu.SemaphoreType.DMA((2,))],
        compiler_params=_cparams("arbitrary"),
        name="combine",
    )(dest3, dest3, x1, route, final_g, yb)


def _rwkv_params(rw_mu, rw_w0, rw_w_up, rw_a0, rw_a_up, rw_g_up, rw_k_k, rw_k_a, rw_r_k, rw_gn_w, rw_gn_b):
    row = lambda v: v.reshape(1, -1).astype(F32)
    zeros = jnp.zeros((A_LORA, RW_DIM), F32)
    return {
        "mu_r": row(rw_mu[:RW_DIM]), "mu_k": row(rw_mu[RW_DIM:2 * RW_DIM]),
        "mu_v": row(rw_mu[2 * RW_DIM:3 * RW_DIM]), "mu_l": row(rw_mu[3 * RW_DIM:]),
        "w0": row(rw_w0), "a0": row(rw_a0), "k_k": row(rw_k_k), "k_a": row(rw_k_a),
        "r_k": row(rw_r_k), "gn_w": row(rw_gn_w), "gn_b": row(rw_gn_b),
        "w_up": jnp.concatenate([rw_w_up, zeros], axis=0).astype(F32),
        "a_up": jnp.concatenate([zeros, rw_a_up], axis=0).astype(BF16),
        "g_up": rw_g_up.astype(BF16),
    }


def _mla_params(g_qa, w_q_up, g_kva, w_kv_up):
    half = QK_ROPE // 2
    pad = MLA_SLOT - QK_NOPE - QK_ROPE
    wq = w_q_up.reshape(Q_LORA, MLA_HEADS, QK_NOPE + QK_ROPE)
    q_nope, q_r1, q_r2 = wq[..., :QK_NOPE], wq[..., QK_NOPE:QK_NOPE + half], wq[..., QK_NOPE + half:]
    zq = lambda w: jnp.zeros((Q_LORA, MLA_HEADS, w), F32)
    w_qa = jnp.concatenate([q_nope, q_r1, q_r2, zq(pad)], axis=-1).reshape(Q_LORA, -1)
    w_qb = jnp.concatenate([zq(QK_NOPE), -q_r2, q_r1, zq(pad)], axis=-1).reshape(Q_LORA, -1)
    wkv = w_kv_up.reshape(KV_LORA, MLA_HEADS, QK_NOPE + V_HEAD)
    w_k = jnp.concatenate([wkv[..., :QK_NOPE], jnp.zeros((KV_LORA, MLA_HEADS, MLA_SLOT - QK_NOPE), F32)],
                          axis=-1).reshape(KV_LORA, -1)
    w_v = wkv[..., QK_NOPE:].reshape(KV_LORA, -1)
    eye = jnp.eye(half, dtype=F32)
    z = jnp.zeros((half, half), F32)
    zl = jnp.zeros((QK_ROPE, QK_NOPE), F32)
    zr = jnp.zeros((QK_ROPE, pad), F32)
    p_a = jnp.concatenate([zl, jnp.concatenate([eye, z], 0), jnp.concatenate([z, eye], 0), zr], axis=1)
    p_b = jnp.concatenate([zl, jnp.concatenate([z, -eye], 0), jnp.concatenate([eye, z], 0), zr], axis=1)
    place_half = jnp.concatenate([jnp.zeros((half, QK_NOPE), F32), eye, eye, jnp.zeros((half, pad), F32)], axis=1)
    zh = jnp.zeros_like(place_half)
    place = jnp.concatenate([jnp.concatenate([place_half, zh], 1), jnp.concatenate([zh, place_half], 1)], 0)
    one = jnp.concatenate([jnp.ones((1, QK_NOPE), F32), jnp.zeros((1, MLA_SLOT - QK_NOPE), F32)], axis=1)
    return {"g_qa": g_qa.reshape(1, -1), "g_kva": g_kva.reshape(1, -1),
            "w_qa": w_qa.astype(BF16), "w_qb": w_qb.astype(BF16), "w_k": w_k.astype(BF16),
            "w_v": w_v.astype(BF16), "p_a": p_a.astype(BF16), "p_b": p_b.astype(BF16),
            "place": place.astype(BF16), "one": one}


def _rope_cos_sin(positions):
    inv_freq = ROPE_THETA ** (-jnp.arange(0, QK_ROPE, 2, dtype=F32) / QK_ROPE)
    ang = positions.astype(F32).reshape(-1, 1) * inv_freq
    return jnp.concatenate([jnp.cos(ang), jnp.sin(ang)], axis=1)


def _block_layout(hist, n_assign):
    tile_counts = hist[:, 0, N_GROUPS:N_GROUPS + N_EXPERTS].astype(jnp.int32)
    counts = jnp.sum(tile_counts, axis=0)
    padded = (counts + EXPERT_BLOCK - 1) // EXPERT_BLOCK * EXPERT_BLOCK
    pad_end = jnp.cumsum(padded)
    pad_start = pad_end - padded
    tile_base = jnp.cumsum(tile_counts, axis=0) - tile_counts + pad_start[None, :]
    base = jnp.pad(tile_base.astype(F32), ((0, 0), (0, ROUTE_W - N_EXPERTS)))[:, None, :]
    n_blocks = -(-n_assign // EXPERT_BLOCK) + N_EXPERTS
    blk_row = jnp.arange(n_blocks, dtype=jnp.int32) * EXPERT_BLOCK
    blk_expert = jnp.minimum(jnp.sum((pad_end[None, :] <= blk_row[:, None]).astype(jnp.int32), axis=1),
                             N_EXPERTS - 1)
    n_used = (pad_end[-1] // EXPERT_BLOCK).astype(jnp.int32).reshape(1)
    return base, blk_expert, n_used, n_blocks


def kernel(x, positions, mix_norm_g, w_in, rw_mu, rw_w0, rw_w_up, rw_a0, rw_a_up, rw_g_up, rw_k_k, rw_k_a, rw_r_k, rw_gn_w, rw_gn_b, mla_g_qa, mla_w_q_up, mla_g_kva, mla_w_kv_up, w_branch_rw, w_branch_mla, w_out, ffn_norm_g, moe_w_group, moe_b_group, moe_w_router, moe_b_router, moe_w_gu, moe_w_down, final_norm_g):
    batch, seq, d = x.shape
    assert d == 2 * SLABS * LANES
    n = batch * seq
    depth = w_in.shape[0]
    rw_cols = 3 * RW_DIM + W_LORA + A_LORA + G_LORA
    mla_cols = Q_LORA + KV_LORA + QK_ROPE
    cos_sin = _rope_cos_sin(positions)
    x2 = x.reshape(n, d)

    for l in range(depth):
        wl = w_in[l].astype(BF16)
        c_rw, c_q, c_kvr, gates = _in_proj(
            x2, mix_norm_g[l].reshape(1, d), wl[:, :rw_cols], wl[:, rw_cols:rw_cols + Q_LORA],
            wl[:, rw_cols + Q_LORA:rw_cols + mla_cols], wl[:, rw_cols + mla_cols:], tm=ROW_TILE)

        rp = _rwkv_params(rw_mu[l], rw_w0[l], rw_w_up[l], rw_a0[l], rw_a_up[l], rw_g_up[l], rw_k_k[l],
                          rw_k_a[l], rw_r_k[l], rw_gn_w[l], rw_gn_b[l])
        y_rw = _rwkv(c_rw.reshape(batch, seq, rw_cols), rp, batch, seq).reshape(n, RW_DIM)

        mp = _mla_params(mla_g_qa[l], mla_w_q_up[l], mla_g_kva[l], mla_w_kv_up[l])
        q, k, v = _mla_prep(c_q, c_kvr, cos_sin, mp, tm=ROW_TILE)
        y_mla = _mla_attn(q.reshape(batch, seq, -1), k.reshape(batch, seq, -1),
                          v.reshape(batch, seq, -1), batch, seq).reshape(n, MLA_HEADS * V_HEAD)

        w_route = jnp.concatenate(
            [moe_w_group[l], moe_w_router[l], jnp.zeros((d, ROUTE_W - N_GROUPS - N_EXPERTS), F32)], axis=1)
        b_route = jnp.concatenate(
            [moe_b_group[l], moe_b_router[l], jnp.zeros((ROUTE_W - N_GROUPS - N_EXPERTS,), F32)]).reshape(1, -1)
        wr_hi = w_route.astype(BF16)
        wr_lo = (w_route - wr_hi.astype(F32)).astype(BF16)
        mparams = {"w_br": w_branch_rw[l].astype(BF16), "w_bm": w_branch_mla[l].astype(BF16),
                   "w_out": w_out[l].astype(BF16), "ffn_g": ffn_norm_g[l].reshape(1, d),
                   "wr_hi": wr_hi, "wr_lo": wr_lo, "b_route": b_route}
        x1, h2p, route, hist = _merge(x2, y_rw, y_mla, gates, mparams, tm=ROW_TILE)

        base, blk_expert, n_used, n_blocks = _block_layout(hist, n * TOP_K)
        dest = _plan(route, base, tm=ROW_TILE)[:, :TOP_K]
        xs = _dispatch(h2p, dest,
                       jnp.zeros((n_blocks * EXPERT_BLOCK * SLABS, LANES), jnp.uint32))
        yb = _experts(xs, blk_expert, n_used, moe_w_gu[l].astype(BF16), moe_w_down[l].astype(BF16))
        x2 = _combine(x1, route, dest, yb, final_norm_g.reshape(1, d), final_norm=(l == depth - 1))

    return x2.reshape(batch, seq, d)
```

```python
import functools
import math

import jax
import jax.numpy as jnp
from jax import lax
from jax.experimental import pallas as pl
from jax.experimental.pallas import tpu as pltpu

F32 = jnp.float32
BF16 = jnp.bfloat16

RW_HEADS = 8
RW_HEAD_DIM = 64
RW_DIM = RW_HEADS * RW_HEAD_DIM
W_LORA = 64
A_LORA = 64
G_LORA = 128
GN_EPS = 64e-5
MLA_HEADS = 8
QK_NOPE = 64
QK_ROPE = 32
V_HEAD = 64
Q_LORA = 384
KV_LORA = 256
ROPE_THETA = 10000.0
N_GROUPS = 4
EXPERTS_PER_GROUP = 8
N_EXPERTS = N_GROUPS * EXPERTS_PER_GROUP
TOP_K = 2
D_EXPERT = 256
EXPERT_BLOCK = 512
NORM_EPS = 1e-6

LANES = 128
HEAD_PAIR = 2 * RW_HEAD_DIM
VMEM_LIMIT = 48 * 1024 * 1024
ROW_TILE = 512


def _cparams(*sem):
    return pltpu.CompilerParams(dimension_semantics=sem, vmem_limit_bytes=VMEM_LIMIT)


def _mm(a, b, dims=((1,), (0,)), split=False):
    dn = (dims, ((), ()))
    dot = lambda x, y: lax.dot_general(x, y, dn, preferred_element_type=F32)
    a_hi = a.astype(BF16)
    b_hi = b.astype(BF16)
    if not split:
        return dot(a_hi, b_hi)
    a_lo = (a - a_hi.astype(F32)).astype(BF16)
    b_lo = (b - b_hi.astype(F32)).astype(BF16)
    return dot(a_hi, b_hi) + dot(a_lo, b_hi) + dot(a_hi, b_lo)


def _mm_sel(sel_bf16, x, dims=((1,), (0,))):
    dn = (dims, ((), ()))
    hi = x.astype(BF16)
    r1 = x - hi.astype(F32)
    mid = r1.astype(BF16)
    lo = (r1 - mid.astype(F32)).astype(BF16)
    out = lax.dot_general(sel_bf16, hi, dn, preferred_element_type=F32)
    out = out + lax.dot_general(sel_bf16, mid, dn, preferred_element_type=F32)
    return out + lax.dot_general(sel_bf16, lo, dn, preferred_element_type=F32)


def _seg_sum(x, seg_bf16):
    hi = x.astype(BF16)
    lo = (x - hi.astype(F32)).astype(BF16)
    return (jnp.dot(hi, seg_bf16, preferred_element_type=F32)
            + jnp.dot(lo, seg_bf16, preferred_element_type=F32))


def _rms(x, g):
    return x * lax.rsqrt(jnp.mean(x * x, axis=-1, keepdims=True) + NORM_EPS) * g


def _pack_rows(x):
    half = x.shape[1] // 2
    bits = lambda v: lax.bitcast_convert_type(v.astype(BF16).astype(F32), jnp.uint32)
    return bits(x[:, :half]) | (bits(x[:, half:]) >> 16)


def _unpack_rows(p):
    hi = lax.bitcast_convert_type(p & jnp.uint32(0xFFFF0000), F32)
    lo = lax.bitcast_convert_type(p << 16, F32)
    return jnp.concatenate([hi, lo], axis=1)


SLABS = 4


def _slab_rows(ref, r):
    return ref.at[pl.ds(pl.multiple_of(r * SLABS, SLABS), SLABS), :]


def _slab_load(ref):
    rows = ref.shape[0] // SLABS
    return jnp.concatenate([ref[pl.ds(j, rows, stride=SLABS), :] for j in range(SLABS)], axis=1)


def _slab_store(ref, x):
    rows = ref.shape[0] // SLABS
    for j in range(SLABS):
        ref[pl.ds(j, rows, stride=SLABS), :] = x[:, j * LANES:(j + 1) * LANES]


def _in_proj_kernel(x_ref, g_ref, wrw_ref, wq_ref, wkvr_ref, wg_ref,
                    crw_ref, cq_ref, ckvr_ref, gate_ref):
    hb = _rms(x_ref[...], g_ref[...]).astype(BF16)
    crw_ref[...] = jnp.dot(hb, wrw_ref[...], preferred_element_type=F32)
    cq_ref[...] = jnp.dot(hb, wq_ref[...], preferred_element_type=F32)
    ckvr_ref[...] = jnp.dot(hb, wkvr_ref[...], preferred_element_type=F32)
    gate_ref[...] = jax.nn.sigmoid(jnp.dot(hb, wg_ref[...], preferred_element_type=F32)).astype(BF16)


def _in_proj(x2, g, w_rw, w_q, w_kvr, w_gate, tm):
    n, d = x2.shape
    full = lambda w: pl.BlockSpec(w.shape, lambda i: (0, 0))
    row = lambda c: pl.BlockSpec((tm, c), lambda i: (i, 0))
    return pl.pallas_call(
        _in_proj_kernel,
        grid=(n // tm,),
        in_specs=[row(d), full(g), full(w_rw), full(w_q), full(w_kvr), full(w_gate)],
        out_specs=[row(w_rw.shape[1]), row(w_q.shape[1]), row(w_kvr.shape[1]), row(w_gate.shape[1])],
        out_shape=[jax.ShapeDtypeStruct((n, w_rw.shape[1]), F32),
                   jax.ShapeDtypeStruct((n, w_q.shape[1]), F32),
                   jax.ShapeDtypeStruct((n, w_kvr.shape[1]), F32),
                   jax.ShapeDtypeStruct((n, w_gate.shape[1]), BF16)],
        compiler_params=_cparams("parallel"),
        name="in_proj",
    )(x2, g, w_rw, w_q, w_kvr, w_gate)


RW_CHUNK = 64
RW_TILE = 1024
RW_GROUP = 16

def _token_shift(cur, halo_ref, first):
    prev_row = jnp.where(first, 0.0, halo_ref[0, 7:8, :])
    rolled = pltpu.roll(cur, 1, 0)
    row = lax.broadcasted_iota(jnp.int32, cur.shape, 0)
    return jnp.where(row == 0, prev_row, rolled)


def _rwkv_kernel(r_ref, k_ref, v_ref, l_ref, hr_ref, hk_ref, hv_ref, hl_ref,
                 mur_ref, muk_ref, muv_ref, mul_ref, w0_ref, a0_ref, kk_ref, ka_ref, rk_ref,
                 gnw_ref, gnb_ref, wup_ref, aup_ref, gup_ref, y_ref, st_ref):
    i = pl.program_id(2)
    first = i == 0

    @pl.when(first)
    def _():
        st_ref[...] = jnp.zeros_like(st_ref)

    def mixed(c_ref, h_ref, mu_ref):
        cur = c_ref[0]
        return cur + (_token_shift(cur, h_ref, first) - cur) * mu_ref[...]

    zr = mixed(r_ref, hr_ref, mur_ref)
    zk = mixed(k_ref, hk_ref, muk_ref)
    zv = mixed(v_ref, hv_ref, muv_ref)
    zl = mixed(l_ref, hl_ref, mul_ref)
    z_wa = zl[:, :LANES]
    z_g = zl[:, LANES:]

    lane = lax.broadcasted_iota(jnp.int32, (LANES, LANES), 1)
    sub = lax.broadcasted_iota(jnp.int32, (LANES, LANES), 0)
    same_head = (lane // RW_HEAD_DIM) == (sub // RW_HEAD_DIM)
    seg = jnp.where(same_head, 1.0, 0.0).astype(BF16)

    w = w0_ref[...] + _mm(jnp.tanh(z_wa), wup_ref[...], split=True)
    u = -w
    softplus = jnp.maximum(u, 0.0) + jnp.log(1.0 + jnp.exp(-jnp.abs(u)))
    log_decay = -jnp.exp(-softplus - 0.5)
    a = jax.nn.sigmoid(a0_ref[...] + _mm(z_wa, aup_ref[...]))
    g = _mm(jax.nn.sigmoid(z_g), gup_ref[...])

    kk = zk * kk_ref[...]
    kk = kk / jnp.maximum(jnp.sqrt(_seg_sum(kk * kk, seg)), 1e-12)
    k2 = zk * (1.0 + (a - 1.0) * ka_ref[...])
    bonus = _seg_sum(zr * k2 * rk_ref[...], seg) * zv
    kka = kk * a

    c = RW_CHUNK
    crow = lax.broadcasted_iota(jnp.int32, (c, c), 0)
    ccol = lax.broadcasted_iota(jnp.int32, (c, c), 1)
    cum_sel = jnp.where(crow >= ccol, 1.0, 0.0).astype(BF16)
    tril_incl = sub >= lane
    tril_strict = sub > lane
    eye_l = jnp.where(lane == sub, 1.0, 0.0).astype(F32)
    lo_half = lax.broadcasted_iota(jnp.int32, (c, LANES), 1) < RW_HEAD_DIM
    nt = ((1,), (1,))
    tn = ((0,), (0,))
    zeros_blk = jnp.zeros((2 * c, LANES), BF16)
    zeros_half = jnp.zeros((c, LANES), BF16)

    def stack(t):
        tb = t.astype(BF16)
        return jnp.concatenate([jnp.where(lo_half, tb, zeros_half), jnp.where(lo_half, zeros_half, tb)], axis=0)

    tril_incl2 = jnp.concatenate([tril_incl, tril_incl], axis=1)

    def chunk_group(ids):
        chunks = range(len(ids))
        x_a, x_b, x_k, x_r, x_v, x_bh, x_kh, w_tot, r_dec = [], [], [], [], [], [], [], [], []
        for ci in ids:
            sl = slice(ci * c, (ci + 1) * c)
            ld = log_decay[sl]
            cum = _mm_sel(cum_sel, ld)
            tot = cum[c - 1:c, :]
            e_neg = jnp.exp(-cum)
            e_rest = jnp.exp(tot - cum)
            x_a.append(stack(-kk[sl] * jnp.exp(cum - ld)))
            x_b.append(stack(kka[sl] * e_neg))
            x_k.append(stack(k2[sl] * e_neg))
            r_dec.append(zr[sl] * jnp.exp(cum))
            x_r.append(stack(r_dec[-1]))
            x_v.append(stack(zv[sl]))
            x_bh.append(stack(kka[sl] * e_rest))
            x_kh.append(stack(k2[sl] * e_rest))
            w_tot.append(jnp.exp(tot))

        inter = [_mm(jnp.concatenate([x_a[i], x_r[i]], axis=0),
                     jnp.concatenate([x_b[i], x_k[i]], axis=0), nt) for i in chunks]
        inter = [m.astype(BF16) for m in inter]
        zeros_sq_b = jnp.zeros((LANES, LANES), BF16)
        a_ab = [jnp.where(tril_strict, m[:2 * c, :2 * c], zeros_sq_b) for m in inter]
        a_ak = [jnp.where(tril_strict, m[:2 * c, 2 * c:], zeros_sq_b) for m in inter]
        a_r = [jnp.where(tril_incl2, m[2 * c:], jnp.concatenate([zeros_sq_b, zeros_sq_b], axis=1))
               for m in inter]
        w_ak = [_mm(a_ak[i], x_v[i]).astype(BF16) for i in chunks]

        t_inv = [eye_l + m.astype(F32) for m in a_ab]
        pw = a_ab
        for _ in range(int(math.log2(c)) - 1):
            pw = [_mm(m, m).astype(BF16) for m in pw]
            t_inv = [t_inv[i] + _mm(t_inv[i], pw[i]) for i in chunks]

        solved = [_mm(t_inv[i], jnp.concatenate([x_a[i], w_ak[i]], axis=1)).astype(BF16)
                  for i in chunks]
        rhs = [jnp.concatenate([solved[i], jnp.concatenate([zeros_blk, x_v[i]], axis=1)], axis=0)
               for i in chunks]
        out = [_mm(a_r[i], rhs[i]) for i in chunks]
        carry = [_mm(jnp.concatenate([x_bh[i], x_kh[i]], axis=0), rhs[i], tn) for i in chunks]
        q_hat, y_loc = [], []
        for i in chunks:
            q_hat.append(r_dec[i] + out[i][:c, :LANES] + out[i][c:, :LANES])
            y_loc.append(out[i][:c, LANES:] + out[i][c:, LANES:])
        trans = [jnp.concatenate([eye_l * w_tot[i] + carry[i][:, :LANES], carry[i][:, LANES:]], axis=1)
                 for i in chunks]
        return q_hat, y_loc, trans

    n_chunks = RW_TILE // c
    groups = [chunk_group(range(g, g + RW_GROUP)) for g in range(0, n_chunks, RW_GROUP)]
    q_hat, y_loc, trans = (sum((g[j] for g in groups), []) for j in range(3))
    chunks = range(n_chunks)

    zeros_sq = jnp.zeros((LANES, LANES), F32)

    def compose(later, earlier):
        return _mm(later[:, :LANES], earlier) + jnp.concatenate([zeros_sq, later[:, LANES:]], axis=1)

    scan = list(trans)
    dist = 1
    while dist < len(scan):
        scan = [scan[i] if i < dist else compose(scan[i], scan[i - dist]) for i in range(len(scan))]
        dist *= 2
    prefix = [None] + scan
    q_pre = [None] + [_mm(q_hat[i], prefix[i]) for i in chunks[1:]]
    q_m = jnp.concatenate([q_hat[0]] + [q_pre[i][:, :LANES] for i in chunks[1:]], axis=0)
    y_off = jnp.concatenate([y_loc[0]] + [y_loc[i] + q_pre[i][:, LANES:] for i in chunks[1:]], axis=0)
    state = st_ref[...]
    y = _mm(q_m, state) + y_off
    st_ref[...] = _mm(prefix[-1][:, :LANES], state) + prefix[-1][:, LANES:]

    inv_n = 1.0 / RW_HEAD_DIM
    mean = _seg_sum(y, seg) * inv_n
    d = y - mean
    var = _seg_sum(d * d, seg) * inv_n
    yn = d * lax.rsqrt(var + GN_EPS) * gnw_ref[...] + gnb_ref[...]
    y_ref[0] = ((yn + bonus) * g).astype(y_ref.dtype)


def _rwkv(c_rw, p, batch, seq):
    ts = RW_TILE
    n_pairs = RW_DIM // HEAD_PAIR
    lora_blk = (3 * RW_DIM) // (2 * LANES)
    halo = ts // 8

    def col(off):
        return pl.BlockSpec((1, ts, LANES), lambda b, pp, i, off=off: (b, i, off + pp))

    def col_halo(off):
        return pl.BlockSpec((1, 8, LANES),
                            lambda b, pp, i, off=off: (b, jnp.maximum(i * halo - 1, 0), off + pp))

    vec = pl.BlockSpec((1, LANES), lambda b, pp, i: (0, pp))
    lora_w = pl.BlockSpec((LANES, LANES), lambda b, pp, i: (0, pp))
    in_specs = [
        col(0), col(n_pairs), col(2 * n_pairs),
        pl.BlockSpec((1, ts, 2 * LANES), lambda b, pp, i: (b, i, lora_blk)),
        col_halo(0), col_halo(n_pairs), col_halo(2 * n_pairs),
        pl.BlockSpec((1, 8, 2 * LANES), lambda b, pp, i: (b, jnp.maximum(i * halo - 1, 0), lora_blk)),
        vec, vec, vec, pl.BlockSpec((1, 2 * LANES), lambda b, pp, i: (0, 0)),
        vec, vec, vec, vec, vec, vec, vec, lora_w, lora_w, lora_w,
    ]
    return pl.pallas_call(
        _rwkv_kernel,
        grid=(batch, n_pairs, seq // ts),
        in_specs=in_specs,
        out_specs=pl.BlockSpec((1, ts, LANES), lambda b, pp, i: (b, i, pp)),
        out_shape=jax.ShapeDtypeStruct((batch, seq, RW_DIM), BF16),
        scratch_shapes=[pltpu.VMEM((LANES, LANES), F32)],
        compiler_params=_cparams("parallel", "parallel", "arbitrary"),
        name="rwkv",
    )(c_rw, c_rw, c_rw, c_rw, c_rw, c_rw, c_rw, c_rw,
      p["mu_r"], p["mu_k"], p["mu_v"], p["mu_l"], p["w0"], p["a0"], p["k_k"], p["k_a"], p["r_k"],
      p["gn_w"], p["gn_b"], p["w_up"], p["a_up"], p["g_up"])


MLA_SLOT = 128


def _mla_prep_kernel(cq_ref, ckvr_ref, cs_ref, gq_ref, gkv_ref,
                     wqa_ref, wqb_ref, wk_ref, wv_ref, pa_ref, pb_ref, place_ref, one_ref,
                     q_ref, k_ref, v_ref):
    cs = cs_ref[...]
    cs_hi = cs.astype(BF16)
    cs_lo = (cs - cs_hi.astype(F32)).astype(BF16)
    tables = (jnp.dot(cs_hi, place_ref[...], preferred_element_type=F32)
              + jnp.dot(cs_lo, place_ref[...], preferred_element_type=F32))
    cos = tables[:, :MLA_SLOT] + one_ref[...]
    sin = tables[:, MLA_SLOT:]
    zq = _rms(cq_ref[...], gq_ref[...]).astype(BF16)
    qa = jnp.dot(zq, wqa_ref[...], preferred_element_type=F32)
    qb = jnp.dot(zq, wqb_ref[...], preferred_element_type=F32)
    ckvr = ckvr_ref[...]
    zkv = _rms(ckvr[:, :KV_LORA], gkv_ref[...]).astype(BF16)
    kn = jnp.dot(zkv, wk_ref[...], preferred_element_type=F32)
    v_ref[...] = jnp.dot(zkv, wv_ref[...], preferred_element_type=F32).astype(BF16)
    kr = ckvr[:, KV_LORA:].astype(BF16)
    k_rope = (jnp.dot(kr, pa_ref[...], preferred_element_type=F32) * cos
              + jnp.dot(kr, pb_ref[...], preferred_element_type=F32) * sin)
    scale = math.log2(math.e) / math.sqrt(QK_NOPE + QK_ROPE)
    for h in range(MLA_HEADS):
        sl = slice(h * MLA_SLOT, (h + 1) * MLA_SLOT)
        q_ref[:, sl] = ((qa[:, sl] * cos + qb[:, sl] * sin) * scale).astype(BF16)
        k_ref[:, sl] = (kn[:, sl] + k_rope).astype(BF16)


def _mla_prep(c_q, c_kvr, cos_sin, p, tm):
    n = c_q.shape[0]
    full = lambda w: pl.BlockSpec(w.shape, lambda i: (0, 0))
    row = lambda c: pl.BlockSpec((tm, c), lambda i: (i, 0))
    ws = [p["g_qa"], p["g_kva"], p["w_qa"], p["w_qb"], p["w_k"], p["w_v"], p["p_a"], p["p_b"],
          p["place"], p["one"]]
    hq = MLA_HEADS * MLA_SLOT
    return pl.pallas_call(
        _mla_prep_kernel,
        grid=(n // tm,),
        in_specs=[row(c_q.shape[1]), row(c_kvr.shape[1]), row(cos_sin.shape[1])] + [full(w) for w in ws],
        out_specs=[row(hq), row(hq), row(MLA_HEADS * V_HEAD)],
        out_shape=[jax.ShapeDtypeStruct((n, hq), BF16), jax.ShapeDtypeStruct((n, hq), BF16),
                   jax.ShapeDtypeStruct((n, MLA_HEADS * V_HEAD), BF16)],
        compiler_params=_cparams("parallel"),
        name="mla_prep",
    )(c_q, c_kvr, cos_sin, *ws)


ATT_TILE = 512
ATT_HEADS = 4


def _attn_kernel(q_ref, k_ref, v_ref, o_ref):
    qi = pl.program_id(2)
    t = ATT_TILE
    heads = range(ATT_HEADS)
    lo_v = lax.broadcasted_iota(jnp.int32, (t, 2 * V_HEAD), 1) < V_HEAD
    nt = (((1,), (1,)), ((), ()))
    den_lane = (V_HEAD, 0)

    def augment(h, vb):
        lane_v = lax.broadcasted_iota(jnp.int32, vb.shape, 1)
        own = (lane_v < V_HEAD) if h % 2 == 0 else (lane_v >= V_HEAD)
        return jnp.where(own, vb, jnp.where(lane_v == den_lane[h % 2], 1.0, 0.0).astype(BF16))

    def update(h, m, acc, k_rows, mask):
        q = q_ref[0, :, h * MLA_SLOT:(h + 1) * MLA_SLOT]
        kb = k_ref[0, k_rows, h * MLA_SLOT:(h + 1) * MLA_SLOT]
        vb = v_ref[0, k_rows, (h // 2) * 2 * V_HEAD:(h // 2 + 1) * 2 * V_HEAD]
        s = lax.dot_general(q, kb, nt, preferred_element_type=F32)
        if mask is not None:
            s = jnp.where(mask, s, -jnp.inf)
        m_new = jnp.maximum(m, jnp.max(s, axis=-1, keepdims=True))
        pr = jnp.exp2((s - m_new[:, :1]).astype(BF16))
        return m_new, acc * jnp.exp2(m - m_new) + jnp.dot(pr, augment(h, vb), preferred_element_type=F32)

    def full_block(j, carry):
        rows = pl.ds(pl.multiple_of(j * t, t), t)
        return tuple(update(h, *carry[h], rows, None) for h in heads)

    init1 = (jnp.full((t, 2 * V_HEAD), -jnp.inf, F32), jnp.zeros((t, 2 * V_HEAD), F32))
    carry = lax.fori_loop(0, qi, full_block, tuple(init1 for _ in heads))

    causal = lax.broadcasted_iota(jnp.int32, (t, t), 1) <= lax.broadcasted_iota(jnp.int32, (t, t), 0)
    diag_rows = pl.ds(pl.multiple_of(qi * t, t), t)
    accs = [update(h, *carry[h], diag_rows, causal)[1] for h in heads]
    for p in range(ATT_HEADS // 2):
        acc0, acc1 = accs[2 * p], accs[2 * p + 1]
        den0 = acc0[:, den_lane[0]:den_lane[0] + 1]
        den1 = acc1[:, den_lane[1]:den_lane[1] + 1]
        o_ref[0, :, p * 2 * V_HEAD:(p + 1) * 2 * V_HEAD] = (
            jnp.where(lo_v, acc0 / den0, acc1 / den1).astype(o_ref.dtype))


def _mla_attn(q, k, v, batch, seq):
    t = ATT_TILE
    g = ATT_HEADS
    return pl.pallas_call(
        _attn_kernel,
        grid=(batch, MLA_HEADS // g, seq // t),
        in_specs=[pl.BlockSpec((1, t, g * MLA_SLOT), lambda b, hp, i: (b, i, hp)),
                  pl.BlockSpec((1, seq, g * MLA_SLOT), lambda b, hp, i: (b, 0, hp)),
                  pl.BlockSpec((1, seq, g * V_HEAD), lambda b, hp, i: (b, 0, hp))],
        out_specs=pl.BlockSpec((1, t, g * V_HEAD), lambda b, hp, i: (b, i, hp)),
        out_shape=jax.ShapeDtypeStruct((batch, seq, MLA_HEADS * V_HEAD), BF16),
        compiler_params=_cparams("parallel", "parallel", "arbitrary"),
        name="mla_attn",
    )(q, k, v)


ROUTE_W = 128


def _merge_kernel(x_ref, yrw_ref, ymla_ref, gate_ref, wbr_ref, wbm_ref, wo_ref, fg_ref,
                  wr_hi_ref, wr_lo_ref, br_ref, x1_ref, h2p_ref, route_ref, hist_ref):
    d = x_ref.shape[1]
    a = jnp.dot(yrw_ref[...], wbr_ref[...], preferred_element_type=F32)
    b = jnp.dot(ymla_ref[...], wbm_ref[...], preferred_element_type=F32)
    merged = gate_ref[:, :d].astype(F32) * a + gate_ref[:, d:].astype(F32) * b
    x1 = x_ref[...] + jnp.dot(merged.astype(BF16), wo_ref[...], preferred_element_type=F32)
    x1_ref[...] = x1
    h2 = _rms(x1, fg_ref[...])
    _slab_store(h2p_ref, _pack_rows(h2))

    h_hi = h2.astype(BF16)
    h_lo = (h2 - h_hi.astype(F32)).astype(BF16)
    logits = (jnp.dot(h_hi, wr_hi_ref[...], preferred_element_type=F32)
              + jnp.dot(h_lo, wr_hi_ref[...], preferred_element_type=F32)
              + jnp.dot(h_hi, wr_lo_ref[...], preferred_element_type=F32)) + br_ref[...]

    lane = lax.broadcasted_iota(jnp.int32, logits.shape, 1)
    big = jnp.int32(ROUTE_W)
    neg = -jnp.inf

    def first_argmax(vals, vmax):
        return jnp.min(jnp.where(vals == vmax, lane, big), axis=-1, keepdims=True)

    grp = jnp.where(lane < N_GROUPS, logits, neg)
    g_max = jnp.max(grp, axis=-1, keepdims=True)
    g_den = jnp.sum(jnp.exp(grp - g_max), axis=-1, keepdims=True)
    g_sel = first_argmax(grp, g_max)
    gate_g = 1.0 / g_den
    lo = N_GROUPS + g_sel * EXPERTS_PER_GROUP
    fine = jnp.where((lane >= lo) & (lane < lo + EXPERTS_PER_GROUP), logits, neg)
    v1 = jnp.max(fine, axis=-1, keepdims=True)
    i1 = first_argmax(fine, v1)
    fine2 = jnp.where(lane == i1, neg, fine)
    v2 = jnp.max(fine2, axis=-1, keepdims=True)
    i2 = first_argmax(fine2, v2)
    e2 = jnp.exp(v2 - v1)
    den = 1.0 + e2
    w1 = gate_g / den
    w2 = gate_g * e2 / den
    route = jnp.where(lane == 0, (i1 - N_GROUPS).astype(F32),
                      jnp.where(lane == 1, (i2 - N_GROUPS).astype(F32),
                                jnp.where(lane == 2, w1, jnp.where(lane == 3, w2, 0.0))))
    route_ref[...] = route
    chosen = jnp.where((lane == i1) | (lane == i2), 1.0, 0.0)
    hist_ref[0] = jnp.broadcast_to(jnp.sum(chosen, axis=0, keepdims=True), hist_ref.shape[1:])


def _merge(x2, y_rw, y_mla, gates, p, tm):
    n, d = x2.shape
    full = lambda w: pl.BlockSpec(w.shape, lambda i: (0, 0))
    row = lambda c: pl.BlockSpec((tm, c), lambda i: (i, 0))
    ws = [p["w_br"], p["w_bm"], p["w_out"], p["ffn_g"], p["wr_hi"], p["wr_lo"], p["b_route"]]
    return pl.pallas_call(
        _merge_kernel,
        grid=(n // tm,),
        in_specs=[row(d), row(y_rw.shape[1]), row(y_mla.shape[1]), row(2 * d)] + [full(w) for w in ws],
        out_specs=[row(d), pl.BlockSpec((tm * SLABS, LANES), lambda i: (i, 0)), row(ROUTE_W),
                   pl.BlockSpec((1, 8, ROUTE_W), lambda i: (i, 0, 0))],
        out_shape=[jax.ShapeDtypeStruct((n, d), F32), jax.ShapeDtypeStruct((n * SLABS, LANES), jnp.uint32),
                   jax.ShapeDtypeStruct((n, ROUTE_W), F32),
                   jax.ShapeDtypeStruct((n // tm, 8, ROUTE_W), F32)],
        compiler_params=_cparams("parallel"),
        name="merge_route",
    )(x2, y_rw, y_mla, gates, *ws)


def _plan_kernel(route_ref, base_ref, dest_ref):
    tm = route_ref.shape[0]
    route = route_ref[...]
    lane = lax.broadcasted_iota(jnp.int32, route.shape, 1).astype(F32)
    pick = [lane == route[:, k:k + 1] for k in range(TOP_K)]
    both = jnp.where(pick[0] | pick[1], 1.0, 0.0).astype(BF16)
    r = lax.broadcasted_iota(jnp.int32, (tm, tm), 0)
    c = lax.broadcasted_iota(jnp.int32, (tm, tm), 1)
    earlier = jnp.where(r > c, 1.0, 0.0).astype(BF16)
    offs = jnp.dot(earlier, both, preferred_element_type=F32) + base_ref[0]
    rows = [jnp.sum(jnp.where(pk, offs, 0.0), axis=-1, keepdims=True) for pk in pick]
    dest_ref[...] = jnp.where(lane == 0.0, rows[0], jnp.where(lane == 1.0, rows[1], 0.0)).astype(jnp.int32)


def _plan(route, base, tm):
    n = route.shape[0]
    return pl.pallas_call(
        _plan_kernel,
        grid=(n // tm,),
        in_specs=[pl.BlockSpec((tm, ROUTE_W), lambda i: (i, 0)),
                  pl.BlockSpec((1, 1, ROUTE_W), lambda i: (i, 0, 0))],
        out_specs=pl.BlockSpec((tm, ROUTE_W), lambda i: (i, 0)),
        out_shape=jax.ShapeDtypeStruct((n, ROUTE_W), jnp.int32),
        compiler_params=_cparams("parallel"),
        name="route_plan",
    )(route, base)


DISPATCH_TILE = 1024
ROW_DMA_UNROLL = 8


def _dispatch_kernel(dest_ref, h_ref, zero_ref, xs_ref, sem):
    del zero_ref
    tm = h_ref.shape[0] // SLABS

    def start(t, _):
        for k in range(TOP_K):
            pltpu.make_async_copy(_slab_rows(h_ref, t), _slab_rows(xs_ref, dest_ref[0, 0, TOP_K * t + k]),
                                  sem).start(priority=k % 2)
        return 0

    lax.fori_loop(0, tm, start, 0, unroll=ROW_DMA_UNROLL)
    all_rows = xs_ref.at[pl.ds(0, TOP_K * tm * SLABS), :]
    pltpu.make_async_copy(all_rows, all_rows, sem).wait()


def _dispatch(h2, dest, xs_init):
    n = h2.shape[0] // SLABS
    tm = DISPATCH_TILE
    dest3 = dest.reshape(n // tm, 1, TOP_K * tm)
    return pl.pallas_call(
        _dispatch_kernel,
        grid=(n // tm,),
        in_specs=[pl.BlockSpec((1, 1, TOP_K * tm), lambda i: (i, 0, 0), memory_space=pltpu.SMEM),
                  pl.BlockSpec((tm * SLABS, LANES), lambda i: (i, 0)),
                  pl.BlockSpec(memory_space=pl.ANY)],
        out_specs=pl.BlockSpec(memory_space=pl.ANY),
        out_shape=jax.ShapeDtypeStruct(xs_init.shape, xs_init.dtype),
        scratch_shapes=[pltpu.SemaphoreType.DMA(())],
        input_output_aliases={2: 0},
        compiler_params=_cparams("arbitrary"),
        name="dispatch",
    )(dest3, h2, xs_init)


def _expert_kernel(blk_e_ref, n_used_ref, x_ref, wgu_ref, wd_ref, y_ref):
    del blk_e_ref

    @pl.when(pl.program_id(0) < n_used_ref[0])
    def _():
        x = _unpack_rows(_slab_load(x_ref)).astype(BF16)
        h = jnp.dot(x, wgu_ref[0], preferred_element_type=F32)
        gt = h[:, :D_EXPERT]
        up = h[:, D_EXPERT:]
        act = (gt * jax.nn.sigmoid(gt) * up).astype(BF16)
        _slab_store(y_ref, _pack_rows(jnp.dot(act, wd_ref[0], preferred_element_type=F32)))

    @pl.when(pl.program_id(0) >= n_used_ref[0])
    def _():
        y_ref[...] = jnp.zeros_like(y_ref)


def _experts(xs, blk_expert, n_used, w_gu, w_down):
    p_rows = xs.shape[0] // SLABS
    d = 2 * SLABS * LANES
    n_blocks = p_rows // EXPERT_BLOCK
    rows = pl.BlockSpec((EXPERT_BLOCK * SLABS, LANES), lambda i, be, nu: (i, 0))
    grid_spec = pltpu.PrefetchScalarGridSpec(
        num_scalar_prefetch=2,
        grid=(n_blocks,),
        in_specs=[rows,
                  pl.BlockSpec((1, d, 2 * D_EXPERT), lambda i, be, nu: (be[i], 0, 0)),
                  pl.BlockSpec((1, D_EXPERT, d), lambda i, be, nu: (be[i], 0, 0))],
        out_specs=rows,
    )
    return pl.pallas_call(
        _expert_kernel,
        grid_spec=grid_spec,
        out_shape=jax.ShapeDtypeStruct(xs.shape, jnp.uint32),
        compiler_params=_cparams("arbitrary"),
        name="experts",
    )(blk_expert, n_used, xs, w_gu, w_down)


COMBINE_TILE = 256


def _combine_kernel(dest_ref, dest_next_ref, x1_ref, route_ref, g_ref, yb_ref, o_ref,
                    buf00, buf01, buf10, buf11, sems, *, final_norm):
    tm = x1_ref.shape[0]
    i = pl.program_id(0)
    bufs = ((buf00, buf01), (buf10, buf11))

    def issue(d_ref, slot):
        def start(t, _):
            for k in range(TOP_K):
                pltpu.make_async_copy(_slab_rows(yb_ref, d_ref[0, 0, TOP_K * t + k]),
                                      _slab_rows(bufs[slot][k], t), sems.at[slot]).start(priority=k % 2)
            return 0

        lax.fori_loop(0, tm, start, 0, unroll=ROW_DMA_UNROLL)

    @pl.when(i == 0)
    def _():
        issue(dest_ref, 0)

    for slot in range(2):
        @pl.when((i % 2 == slot) & (i + 1 < pl.num_programs(0)))
        def _(slot=slot):
            issue(dest_next_ref, 1 - slot)

    for slot in range(2):
        @pl.when(i % 2 == slot)
        def _(slot=slot):
            for b in bufs[slot]:
                pltpu.make_async_copy(b, b, sems.at[slot]).wait()
            route = route_ref[...]
            x2 = (x1_ref[...] + route[:, 2:3] * _unpack_rows(_slab_load(bufs[slot][0]))
                  + route[:, 3:4] * _unpack_rows(_slab_load(bufs[slot][1])))
            o_ref[...] = _rms(x2, g_ref[...]) if final_norm else x2


def _combine(x1, route, dest, yb, final_g, final_norm):
    n, d = x1.shape
    tm = COMBINE_TILE
    n_tiles = n // tm
    dest3 = dest.reshape(n_tiles, 1, TOP_K * tm)
    return pl.pallas_call(
        functools.partial(_combine_kernel, final_norm=final_norm),
        grid=(n_tiles,),
        in_specs=[pl.BlockSpec((1, 1, TOP_K * tm), lambda i: (i, 0, 0), memory_space=pltpu.SMEM),
                  pl.BlockSpec((1, 1, TOP_K * tm), lambda i: (jnp.minimum(i + 1, n_tiles - 1), 0, 0),
                               memory_space=pltpu.SMEM),
                  pl.BlockSpec((tm, d), lambda i: (i, 0)),
                  pl.BlockSpec((tm, ROUTE_W), lambda i: (i, 0)),
                  pl.BlockSpec((1, d), lambda i: (0, 0)),
                  pl.BlockSpec(memory_space=pl.ANY)],
        out_specs=pl.BlockSpec((tm, d), lambda i: (i, 0)),
        out_shape=jax.ShapeDtypeStruct((n, d), F32),
        scratch_shapes=[pltpu.VMEM((tm * SLABS, LANES), jnp.uint32) for _ in range(2 * TOP_K)]
                       + [pltpu.SemaphoreType.DMA((2,))],
        compiler_params=_cparams("arbitrary"),
        name="combine",
    )(dest3, dest3, x1, route, final_g, yb)


def _rwkv_params(rw_mu, rw_w0, rw_w_up, rw_a0, rw_a_up, rw_g_up, rw_k_k, rw_k_a, rw_r_k, rw_gn_w, rw_gn_b):
    row = lambda v: v.reshape(1, -1).astype(F32)
    zeros = jnp.zeros((A_LORA, RW_DIM), F32)
    return {
        "mu_r": row(rw_mu[:RW_DIM]), "mu_k": row(rw_mu[RW_DIM:2 * RW_DIM]),
        "mu_v": row(rw_mu[2 * RW_DIM:3 * RW_DIM]), "mu_l": row(rw_mu[3 * RW_DIM:]),
        "w0": row(rw_w0), "a0": row(rw_a0), "k_k": row(rw_k_k), "k_a": row(rw_k_a),
        "r_k": row(rw_r_k), "gn_w": row(rw_gn_w), "gn_b": row(rw_gn_b),
        "w_up": jnp.concatenate([rw_w_up, zeros], axis=0).astype(F32),
        "a_up": jnp.concatenate([zeros, rw_a_up], axis=0).astype(BF16),
        "g_up": rw_g_up.astype(BF16),
    }


def _mla_params(g_qa, w_q_up, g_kva, w_kv_up):
    half = QK_ROPE // 2
    pad = MLA_SLOT - QK_NOPE - QK_ROPE
    wq = w_q_up.reshape(Q_LORA, MLA_HEADS, QK_NOPE + QK_ROPE)
    q_nope, q_r1, q_r2 = wq[..., :QK_NOPE], wq[..., QK_NOPE:QK_NOPE + half], wq[..., QK_NOPE + half:]
    zq = lambda w: jnp.zeros((Q_LORA, MLA_HEADS, w), F32)
    w_qa = jnp.concatenate([q_nope, q_r1, q_r2, zq(pad)], axis=-1).reshape(Q_LORA, -1)
    w_qb = jnp.concatenate([zq(QK_NOPE), -q_r2, q_r1, zq(pad)], axis=-1).reshape(Q_LORA, -1)
    wkv = w_kv_up.reshape(KV_LORA, MLA_HEADS, QK_NOPE + V_HEAD)
    w_k = jnp.concatenate([wkv[..., :QK_NOPE], jnp.zeros((KV_LORA, MLA_HEADS, MLA_SLOT - QK_NOPE), F32)],
                          axis=-1).reshape(KV_LORA, -1)
    w_v = wkv[..., QK_NOPE:].reshape(KV_LORA, -1)
    eye = jnp.eye(half, dtype=F32)
    z = jnp.zeros((half, half), F32)
    zl = jnp.zeros((QK_ROPE, QK_NOPE), F32)
    zr = jnp.zeros((QK_ROPE, pad), F32)
    p_a = jnp.concatenate([zl, jnp.concatenate([eye, z], 0), jnp.concatenate([z, eye], 0), zr], axis=1)
    p_b = jnp.concatenate([zl, jnp.concatenate([z, -eye], 0), jnp.concatenate([eye, z], 0), zr], axis=1)
    place_half = jnp.concatenate([jnp.zeros((half, QK_NOPE), F32), eye, eye, jnp.zeros((half, pad), F32)], axis=1)
    zh = jnp.zeros_like(place_half)
    place = jnp.concatenate([jnp.concatenate([place_half, zh], 1), jnp.concatenate([zh, place_half], 1)], 0)
    one = jnp.concatenate([jnp.ones((1, QK_NOPE), F32), jnp.zeros((1, MLA_SLOT - QK_NOPE), F32)], axis=1)
    return {"g_qa": g_qa.reshape(1, -1), "g_kva": g_kva.reshape(1, -1),
            "w_qa": w_qa.astype(BF16), "w_qb": w_qb.astype(BF16), "w_k": w_k.astype(BF16),
            "w_v": w_v.astype(BF16), "p_a": p_a.astype(BF16), "p_b": p_b.astype(BF16),
            "place": place.astype(BF16), "one": one}


def _rope_cos_sin(positions):
    inv_freq = ROPE_THETA ** (-jnp.arange(0, QK_ROPE, 2, dtype=F32) / QK_ROPE)
    ang = positions.astype(F32).reshape(-1, 1) * inv_freq
    return jnp.concatenate([jnp.cos(ang), jnp.sin(ang)], axis=1)


def _block_layout(hist, n_assign):
    tile_counts = hist[:, 0, N_GROUPS:N_GROUPS + N_EXPERTS].astype(jnp.int32)
    counts = jnp.sum(tile_counts, axis=0)
    padded = (counts + EXPERT_BLOCK - 1) // EXPERT_BLOCK * EXPERT_BLOCK
    pad_end = jnp.cumsum(padded)
    pad_start = pad_end - padded
    tile_base = jnp.cumsum(tile_counts, axis=0) - tile_counts + pad_start[None, :]
    base = jnp.pad(tile_base.astype(F32), ((0, 0), (0, ROUTE_W - N_EXPERTS)))[:, None, :]
    n_blocks = -(-n_assign // EXPERT_BLOCK) + N_EXPERTS
    blk_row = jnp.arange(n_blocks, dtype=jnp.int32) * EXPERT_BLOCK
    blk_expert = jnp.minimum(jnp.sum((pad_end[None, :] <= blk_row[:, None]).astype(jnp.int32), axis=1),
                             N_EXPERTS - 1)
    n_used = (pad_end[-1] // EXPERT_BLOCK).astype(jnp.int32).reshape(1)
    return base, blk_expert, n_used, n_blocks


def kernel(x, positions, mix_norm_g, w_in, rw_mu, rw_w0, rw_w_up, rw_a0, rw_a_up, rw_g_up, rw_k_k, rw_k_a, rw_r_k, rw_gn_w, rw_gn_b, mla_g_qa, mla_w_q_up, mla_g_kva, mla_w_kv_up, w_branch_rw, w_branch_mla, w_out, ffn_norm_g, moe_w_group, moe_b_group, moe_w_router, moe_b_router, moe_w_gu, moe_w_down, final_norm_g):
    batch, seq, d = x.shape
    assert d == 2 * SLABS * LANES
    n = batch * seq
    depth = w_in.shape[0]
    rw_cols = 3 * RW_DIM + W_LORA + A_LORA + G_LORA
    mla_cols = Q_LORA + KV_LORA + QK_ROPE
    cos_sin = _rope_cos_sin(positions)
    x2 = x.reshape(n, d)

    for l in range(depth):
        wl = w_in[l].astype(BF16)
        c_rw, c_q, c_kvr, gates = _in_proj(
            x2, mix_norm_g[l].reshape(1, d), wl[:, :rw_cols], wl[:, rw_cols:rw_cols + Q_LORA],
            wl[:, rw_cols + Q_LORA:rw_cols + mla_cols], wl[:, rw_cols + mla_cols:], tm=ROW_TILE)

        rp = _rwkv_params(rw_mu[l], rw_w0[l], rw_w_up[l], rw_a0[l], rw_a_up[l], rw_g_up[l], rw_k_k[l],
                          rw_k_a[l], rw_r_k[l], rw_gn_w[l], rw_gn_b[l])
        y_rw = _rwkv(c_rw.reshape(batch, seq, rw_cols), rp, batch, seq).reshape(n, RW_DIM)

        mp = _mla_params(mla_g_qa[l], mla_w_q_up[l], mla_g_kva[l], mla_w_kv_up[l])
        q, k, v = _mla_prep(c_q, c_kvr, cos_sin, mp, tm=ROW_TILE)
        y_mla = _mla_attn(q.reshape(batch, seq, -1), k.reshape(batch, seq, -1),
                          v.reshape(batch, seq, -1), batch, seq).reshape(n, MLA_HEADS * V_HEAD)

        w_route = jnp.concatenate(
            [moe_w_group[l], moe_w_router[l], jnp.zeros((d, ROUTE_W - N_GROUPS - N_EXPERTS), F32)], axis=1)
        b_route = jnp.concatenate(
            [moe_b_group[l], moe_b_router[l], jnp.zeros((ROUTE_W - N_GROUPS - N_EXPERTS,), F32)]).reshape(1, -1)
        wr_hi = w_route.astype(BF16)
        wr_lo = (w_route - wr_hi.astype(F32)).astype(BF16)
        mparams = {"w_br": w_branch_rw[l].astype(BF16), "w_bm": w_branch_mla[l].astype(BF16),
                   "w_out": w_out[l].astype(BF16), "ffn_g": ffn_norm_g[l].reshape(1, d),
                   "wr_hi": wr_hi, "wr_lo": wr_lo, "b_route": b_route}
        x1, h2p, route, hist = _merge(x2, y_rw, y_mla, gates, mparams, tm=ROW_TILE)

        base, blk_expert, n_used, n_blocks = _block_layout(hist, n * TOP_K)
        dest = _plan(route, base, tm=ROW_TILE)[:, :TOP_K]
        xs = _dispatch(h2p, dest,
                       jnp.zeros((n_blocks * EXPERT_BLOCK * SLABS, LANES), jnp.uint32))
        yb = _experts(xs, blk_expert, n_used, moe_w_gu[l].astype(BF16), moe_w_down[l].astype(BF16))
        x2 = _combine(x1, route, dest, yb, final_norm_g.reshape(1, d), final_norm=(l == depth - 1))

    return x2.reshape(batch, seq, d)
```

```python
import functools
import math

import jax
import jax.numpy as jnp
from jax import lax
from jax.experimental import pallas as pl
from jax.experimental.pallas import tpu as pltpu

F32 = jnp.float32
BF16 = jnp.bfloat16

RW_HEADS = 8
RW_HEAD_DIM = 64
RW_DIM = RW_HEADS * RW_HEAD_DIM
W_LORA = 64
A_LORA = 64
G_LORA = 128
GN_EPS = 64e-5
MLA_HEADS = 8
QK_NOPE = 64
QK_ROPE = 32
V_HEAD = 64
Q_LORA = 384
KV_LORA = 256
ROPE_THETA = 10000.0
N_GROUPS = 4
EXPERTS_PER_GROUP = 8
N_EXPERTS = N_GROUPS * EXPERTS_PER_GROUP
TOP_K = 2
D_EXPERT = 256
EXPERT_BLOCK = 512
NORM_EPS = 1e-6

LANES = 128
HEAD_PAIR = 2 * RW_HEAD_DIM
VMEM_LIMIT = 48 * 1024 * 1024
ROW_TILE = 512


def _cparams(*sem):
    return pltpu.CompilerParams(dimension_semantics=sem, vmem_limit_bytes=VMEM_LIMIT)


def _mm(a, b, dims=((1,), (0,)), split=False):
    dn = (dims, ((), ()))
    dot = lambda x, y: lax.dot_general(x, y, dn, preferred_element_type=F32)
    a_hi = a.astype(BF16)
    b_hi = b.astype(BF16)
    if not split:
        return dot(a_hi, b_hi)
    a_lo = (a - a_hi.astype(F32)).astype(BF16)
    b_lo = (b - b_hi.astype(F32)).astype(BF16)
    return dot(a_hi, b_hi) + dot(a_lo, b_hi) + dot(a_hi, b_lo)


def _mm_sel(sel_bf16, x, dims=((1,), (0,))):
    dn = (dims, ((), ()))
    hi = x.astype(BF16)
    r1 = x - hi.astype(F32)
    mid = r1.astype(BF16)
    lo = (r1 - mid.astype(F32)).astype(BF16)
    out = lax.dot_general(sel_bf16, hi, dn, preferred_element_type=F32)
    out = out + lax.dot_general(sel_bf16, mid, dn, preferred_element_type=F32)
    return out + lax.dot_general(sel_bf16, lo, dn, preferred_element_type=F32)


def _seg_sum(x, seg_bf16):
    hi = x.astype(BF16)
    lo = (x - hi.astype(F32)).astype(BF16)
    return (jnp.dot(hi, seg_bf16, preferred_element_type=F32)
            + jnp.dot(lo, seg_bf16, preferred_element_type=F32))


def _rms(x, g):
    return x * lax.rsqrt(jnp.mean(x * x, axis=-1, keepdims=True) + NORM_EPS) * g


def _pack_rows(x):
    half = x.shape[1] // 2
    bits = lambda v: lax.bitcast_convert_type(v.astype(BF16).astype(F32), jnp.uint32)
    return bits(x[:, :half]) | (bits(x[:, half:]) >> 16)


def _unpack_rows(p):
    hi = lax.bitcast_convert_type(p & jnp.uint32(0xFFFF0000), F32)
    lo = lax.bitcast_convert_type(p << 16, F32)
    return jnp.concatenate([hi, lo], axis=1)


SLABS = 4


def _slab_rows(ref, r):
    return ref.at[pl.ds(pl.multiple_of(r * SLABS, SLABS), SLABS), :]


def _slab_load(ref):
    rows = ref.shape[0] // SLABS
    return jnp.concatenate([ref[pl.ds(j, rows, stride=SLABS), :] for j in range(SLABS)], axis=1)


def _slab_store(ref, x):
    rows = ref.shape[0] // SLABS
    for j in range(SLABS):
        ref[pl.ds(j, rows, stride=SLABS), :] = x[:, j * LANES:(j + 1) * LANES]


def _in_proj_kernel(x_ref, g_ref, wrw_ref, wq_ref, wkvr_ref, wg_ref,
                    crw_ref, cq_ref, ckvr_ref, gate_ref):
    hb = _rms(x_ref[...], g_ref[...]).astype(BF16)
    crw_ref[...] = jnp.dot(hb, wrw_ref[...], preferred_element_type=F32)
    cq_ref[...] = jnp.dot(hb, wq_ref[...], preferred_element_type=F32)
    ckvr_ref[...] = jnp.dot(hb, wkvr_ref[...], preferred_element_type=F32)
    gate_ref[...] = jax.nn.sigmoid(jnp.dot(hb, wg_ref[...], preferred_element_type=F32)).astype(BF16)


def _in_proj(x2, g, w_rw, w_q, w_kvr, w_gate, tm):
    n, d = x2.shape
    full = lambda w: pl.BlockSpec(w.shape, lambda i: (0, 0))
    row = lambda c: pl.BlockSpec((tm, c), lambda i: (i, 0))
    return pl.pallas_call(
        _in_proj_kernel,
        grid=(n // tm,),
        in_specs=[row(d), full(g), full(w_rw), full(w_q), full(w_kvr), full(w_gate)],
        out_specs=[row(w_rw.shape[1]), row(w_q.shape[1]), row(w_kvr.shape[1]), row(w_gate.shape[1])],
        out_shape=[jax.ShapeDtypeStruct((n, w_rw.shape[1]), F32),
                   jax.ShapeDtypeStruct((n, w_q.shape[1]), F32),
                   jax.ShapeDtypeStruct((n, w_kvr.shape[1]), F32),
                   jax.ShapeDtypeStruct((n, w_gate.shape[1]), BF16)],
        compiler_params=_cparams("parallel"),
        name="in_proj",
    )(x2, g, w_rw, w_q, w_kvr, w_gate)


RW_CHUNK = 64
RW_TILE = 1024
RW_GROUP = 16

def _token_shift(cur, halo_ref, first):
    prev_row = jnp.where(first, 0.0, halo_ref[0, 7:8, :])
    rolled = pltpu.roll(cur, 1, 0)
    row = lax.broadcasted_iota(jnp.int32, cur.shape, 0)
    return jnp.where(row == 0, prev_row, rolled)


def _rwkv_kernel(r_ref, k_ref, v_ref, l_ref, hr_ref, hk_ref, hv_ref, hl_ref,
                 mur_ref, muk_ref, muv_ref, mul_ref, w0_ref, a0_ref, kk_ref, ka_ref, rk_ref,
                 gnw_ref, gnb_ref, wup_ref, aup_ref, gup_ref, y_ref, st_ref):
    i = pl.program_id(2)
    first = i == 0

    @pl.when(first)
    def _():
        st_ref[...] = jnp.zeros_like(st_ref)

    def mixed(c_ref, h_ref, mu_ref):
        cur = c_ref[0]
        return cur + (_token_shift(cur, h_ref, first) - cur) * mu_ref[...]

    zr = mixed(r_ref, hr_ref, mur_ref)
    zk = mixed(k_ref, hk_ref, muk_ref)
    zv = mixed(v_ref, hv_ref, muv_ref)
    zl = mixed(l_ref, hl_ref, mul_ref)
    z_wa = zl[:, :LANES]
    z_g = zl[:, LANES:]

    lane = lax.broadcasted_iota(jnp.int32, (LANES, LANES), 1)
    sub = lax.broadcasted_iota(jnp.int32, (LANES, LANES), 0)
    same_head = (lane // RW_HEAD_DIM) == (sub // RW_HEAD_DIM)
    seg = jnp.where(same_head, 1.0, 0.0).astype(BF16)

    w = w0_ref[...] + _mm(jnp.tanh(z_wa), wup_ref[...], split=True)
    u = -w
    softplus = jnp.maximum(u, 0.0) + jnp.log(1.0 + jnp.exp(-jnp.abs(u)))
    log_decay = -jnp.exp(-softplus - 0.5)
    a = jax.nn.sigmoid(a0_ref[...] + _mm(z_wa, aup_ref[...]))
    g = _mm(jax.nn.sigmoid(z_g), gup_ref[...])

    kk = zk * kk_ref[...]
    kk = kk / jnp.maximum(jnp.sqrt(_seg_sum(kk * kk, seg)), 1e-12)
    k2 = zk * (1.0 + (a - 1.0) * ka_ref[...])
    bonus = _seg_sum(zr * k2 * rk_ref[...], seg) * zv
    kka = kk * a

    c = RW_CHUNK
    crow = lax.broadcasted_iota(jnp.int32, (c, c), 0)
    ccol = lax.broadcasted_iota(jnp.int32, (c, c), 1)
    cum_sel = jnp.where(crow >= ccol, 1.0, 0.0).astype(BF16)
    tril_incl = sub >= lane
    tril_strict = sub > lane
    eye_l = jnp.where(lane == sub, 1.0, 0.0).astype(F32)
    lo_half = lax.broadcasted_iota(jnp.int32, (c, LANES), 1) < RW_HEAD_DIM
    nt = ((1,), (1,))
    tn = ((0,), (0,))
    zeros_blk = jnp.zeros((2 * c, LANES), BF16)
    zeros_half = jnp.zeros((c, LANES), BF16)

    def stack(t):
        tb = t.astype(BF16)
        return jnp.concatenate([jnp.where(lo_half, tb, zeros_half), jnp.where(lo_half, zeros_half, tb)], axis=0)

    tril_incl2 = jnp.concatenate([tril_incl, tril_incl], axis=1)

    def chunk_group(ids):
        chunks = range(len(ids))
        x_a, x_b, x_k, x_r, x_v, x_bh, x_kh, w_tot, r_dec = [], [], [], [], [], [], [], [], []
        for ci in ids:
            sl = slice(ci * c, (ci + 1) * c)
            ld = log_decay[sl]
            cum = _mm_sel(cum_sel, ld)
            tot = cum[c - 1:c, :]
            e_neg = jnp.exp(-cum)
            e_rest = jnp.exp(tot - cum)
            x_a.append(stack(-kk[sl] * jnp.exp(cum - ld)))
            x_b.append(stack(kka[sl] * e_neg))
            x_k.append(stack(k2[sl] * e_neg))
            r_dec.append(zr[sl] * jnp.exp(cum))
            x_r.append(stack(r_dec[-1]))
            x_v.append(stack(zv[sl]))
            x_bh.append(stack(kka[sl] * e_rest))
            x_kh.append(stack(k2[sl] * e_rest))
            w_tot.append(jnp.exp(tot))

        inter = [_mm(jnp.concatenate([x_a[i], x_r[i]], axis=0),
                     jnp.concatenate([x_b[i], x_k[i]], axis=0), nt) for i in chunks]
        inter = [m.astype(BF16) for m in inter]
        zeros_sq_b = jnp.zeros((LANES, LANES), BF16)
        a_ab = [jnp.where(tril_strict, m[:2 * c, :2 * c], zeros_sq_b) for m in inter]
        a_ak = [jnp.where(tril_strict, m[:2 * c, 2 * c:], zeros_sq_b) for m in inter]
        a_r = [jnp.where(tril_incl2, m[2 * c:], jnp.concatenate([zeros_sq_b, zeros_sq_b], axis=1))
               for m in inter]
        w_ak = [_mm(a_ak[i], x_v[i]).astype(BF16) for i in chunks]

        t_inv = [eye_l + m.astype(F32) for m in a_ab]
        pw = a_ab
        for _ in range(int(math.log2(c)) - 1):
            pw = [_mm(m, m).astype(BF16) for m in pw]
            t_inv = [t_inv[i] + _mm(t_inv[i], pw[i]) for i in chunks]

        solved = [_mm(t_inv[i], jnp.concatenate([x_a[i], w_ak[i]], axis=1)).astype(BF16)
                  for i in chunks]
        rhs = [jnp.concatenate([solved[i], jnp.concatenate([zeros_blk, x_v[i]], axis=1)], axis=0)
               for i in chunks]
        out = [_mm(a_r[i], rhs[i]) for i in chunks]
        carry = [_mm(jnp.concatenate([x_bh[i], x_kh[i]], axis=0), rhs[i], tn) for i in chunks]
        q_hat, y_loc = [], []
        for i in chunks:
            q_hat.append(r_dec[i] + out[i][:c, :LANES] + out[i][c:, :LANES])
            y_loc.append(out[i][:c, LANES:] + out[i][c:, LANES:])
        trans = [jnp.concatenate([eye_l * w_tot[i] + carry[i][:, :LANES], carry[i][:, LANES:]], axis=1)
                 for i in chunks]
        return q_hat, y_loc, trans

    n_chunks = RW_TILE // c
    groups = [chunk_group(range(g, g + RW_GROUP)) for g in range(0, n_chunks, RW_GROUP)]
    q_hat, y_loc, trans = (sum((g[j] for g in groups), []) for j in range(3))
    chunks = range(n_chunks)

    zeros_sq = jnp.zeros((LANES, LANES), F32)

    def compose(later, earlier):
        return _mm(later[:, :LANES], earlier) + jnp.concatenate([zeros_sq, later[:, LANES:]], axis=1)

    scan = list(trans)
    dist = 1
    while dist < len(scan):
        scan = [scan[i] if i < dist else compose(scan[i], scan[i - dist]) for i in range(len(scan))]
        dist *= 2
    prefix = [None] + scan
    q_pre = [None] + [_mm(q_hat[i], prefix[i]) for i in chunks[1:]]
    q_m = jnp.concatenate([q_hat[0]] + [q_pre[i][:, :LANES] for i in chunks[1:]], axis=0)
    y_off = jnp.concatenate([y_loc[0]] + [y_loc[i] + q_pre[i][:, LANES:] for i in chunks[1:]], axis=0)
    state = st_ref[...]
    y = _mm(q_m, state) + y_off
    st_ref[...] = _mm(prefix[-1][:, :LANES], state) + prefix[-1][:, LANES:]

    inv_n = 1.0 / RW_HEAD_DIM
    mean = _seg_sum(y, seg) * inv_n
    d = y - mean
    var = _seg_sum(d * d, seg) * inv_n
    yn = d * lax.rsqrt(var + GN_EPS) * gnw_ref[...] + gnb_ref[...]
    y_ref[0] = ((yn + bonus) * g).astype(y_ref.dtype)


def _rwkv(c_rw, p, batch, seq):
    ts = RW_TILE
    n_pairs = RW_DIM // HEAD_PAIR
    lora_blk = (3 * RW_DIM) // (2 * LANES)
    halo = ts // 8

    def col(off):
        return pl.BlockSpec((1, ts, LANES), lambda b, pp, i, off=off: (b, i, off + pp))

    def col_halo(off):
        return pl.BlockSpec((1, 8, LANES),
                            lambda b, pp, i, off=off: (b, jnp.maximum(i * halo - 1, 0), off + pp))

    vec = pl.BlockSpec((1, LANES), lambda b, pp, i: (0, pp))
    lora_w = pl.BlockSpec((LANES, LANES), lambda b, pp, i: (0, pp))
    in_specs = [
        col(0), col(n_pairs), col(2 * n_pairs),
        pl.BlockSpec((1, ts, 2 * LANES), lambda b, pp, i: (b, i, lora_blk)),
        col_halo(0), col_halo(n_pairs), col_halo(2 * n_pairs),
        pl.BlockSpec((1, 8, 2 * LANES), lambda b, pp, i: (b, jnp.maximum(i * halo - 1, 0), lora_blk)),
        vec, vec, vec, pl.BlockSpec((1, 2 * LANES), lambda b, pp, i: (0, 0)),
        vec, vec, vec, vec, vec, vec, vec, lora_w, lora_w, lora_w,
    ]
    return pl.pallas_call(
        _rwkv_kernel,
        grid=(batch, n_pairs, seq // ts),
        in_specs=in_specs,
        out_specs=pl.BlockSpec((1, ts, LANES), lambda b, pp, i: (b, i, pp)),
        out_shape=jax.ShapeDtypeStruct((batch, seq, RW_DIM), BF16),
        scratch_shapes=[pltpu.VMEM((LANES, LANES), F32)],
        compiler_params=_cparams("parallel", "parallel", "arbitrary"),
        name="rwkv",
    )(c_rw, c_rw, c_rw, c_rw, c_rw, c_rw, c_rw, c_rw,
      p["mu_r"], p["mu_k"], p["mu_v"], p["mu_l"], p["w0"], p["a0"], p["k_k"], p["k_a"], p["r_k"],
      p["gn_w"], p["gn_b"], p["w_up"], p["a_up"], p["g_up"])


MLA_SLOT = 128


def _mla_prep_kernel(cq_ref, ckvr_ref, cs_ref, gq_ref, gkv_ref,
                     wqa_ref, wqb_ref, wk_ref, wv_ref, pa_ref, pb_ref, place_ref, one_ref,
                     q_ref, k_ref, v_ref):
    cs = cs_ref[...]
    cs_hi = cs.astype(BF16)
    cs_lo = (cs - cs_hi.astype(F32)).astype(BF16)
    tables = (jnp.dot(cs_hi, place_ref[...], preferred_element_type=F32)
              + jnp.dot(cs_lo, place_ref[...], preferred_element_type=F32))
    cos = tables[:, :MLA_SLOT] + one_ref[...]
    sin = tables[:, MLA_SLOT:]
    zq = _rms(cq_ref[...], gq_ref[...]).astype(BF16)
    qa = jnp.dot(zq, wqa_ref[...], preferred_element_type=F32)
    qb = jnp.dot(zq, wqb_ref[...], preferred_element_type=F32)
    ckvr = ckvr_ref[...]
    zkv = _rms(ckvr[:, :KV_LORA], gkv_ref[...]).astype(BF16)
    kn = jnp.dot(zkv, wk_ref[...], preferred_element_type=F32)
    v_ref[...] = jnp.dot(zkv, wv_ref[...], preferred_element_type=F32).astype(BF16)
    kr = ckvr[:, KV_LORA:].astype(BF16)
    k_rope = (jnp.dot(kr, pa_ref[...], preferred_element_type=F32) * cos
              + jnp.dot(kr, pb_ref[...], preferred_element_type=F32) * sin)
    scale = math.log2(math.e) / math.sqrt(QK_NOPE + QK_ROPE)
    for h in range(MLA_HEADS):
        sl = slice(h * MLA_SLOT, (h + 1) * MLA_SLOT)
        q_ref[:, sl] = ((qa[:, sl] * cos + qb[:, sl] * sin) * scale).astype(BF16)
        k_ref[:, sl] = (kn[:, sl] + k_rope).astype(BF16)


def _mla_prep(c_q, c_kvr, cos_sin, p, tm):
    n = c_q.shape[0]
    full = lambda w: pl.BlockSpec(w.shape, lambda i: (0, 0))
    row = lambda c: pl.BlockSpec((tm, c), lambda i: (i, 0))
    ws = [p["g_qa"], p["g_kva"], p["w_qa"], p["w_qb"], p["w_k"], p["w_v"], p["p_a"], p["p_b"],
          p["place"], p["one"]]
    hq = MLA_HEADS * MLA_SLOT
    return pl.pallas_call(
        _mla_prep_kernel,
        grid=(n // tm,),
        in_specs=[row(c_q.shape[1]), row(c_kvr.shape[1]), row(cos_sin.shape[1])] + [full(w) for w in ws],
        out_specs=[row(hq), row(hq), row(MLA_HEADS * V_HEAD)],
        out_shape=[jax.ShapeDtypeStruct((n, hq), BF16), jax.ShapeDtypeStruct((n, hq), BF16),
                   jax.ShapeDtypeStruct((n, MLA_HEADS * V_HEAD), BF16)],
        compiler_params=_cparams("parallel"),
        name="mla_prep",
    )(c_q, c_kvr, cos_sin, *ws)


ATT_TILE = 512
ATT_HEADS = 4


def _attn_kernel(q_ref, k_ref, v_ref, o_ref):
    qi = pl.program_id(2)
    t = ATT_TILE
    heads = range(ATT_HEADS)
    lo_v = lax.broadcasted_iota(jnp.int32, (t, 2 * V_HEAD), 1) < V_HEAD
    nt = (((1,), (1,)), ((), ()))
    den_lane = (V_HEAD, 0)

    def augment(h, vb):
        lane_v = lax.broadcasted_iota(jnp.int32, vb.shape, 1)
        own = (lane_v < V_HEAD) if h % 2 == 0 else (lane_v >= V_HEAD)
        return jnp.where(own, vb, jnp.where(lane_v == den_lane[h % 2], 1.0, 0.0).astype(BF16))

    def update(h, m, acc, k_rows, mask):
        q = q_ref[0, :, h * MLA_SLOT:(h + 1) * MLA_SLOT]
        kb = k_ref[0, k_rows, h * MLA_SLOT:(h + 1) * MLA_SLOT]
        vb = v_ref[0, k_rows, (h // 2) * 2 * V_HEAD:(h // 2 + 1) * 2 * V_HEAD]
        s = lax.dot_general(q, kb, nt, preferred_element_type=F32)
        if mask is not None:
            s = jnp.where(mask, s, -jnp.inf)
        m_new = jnp.maximum(m, jnp.max(s, axis=-1, keepdims=True))
        pr = jnp.exp2((s - m_new[:, :1]).astype(BF16))
        return m_new, acc * jnp.exp2(m - m_new) + jnp.dot(pr, augment(h, vb), preferred_element_type=F32)

    def full_block(j, carry):
        rows = pl.ds(pl.multiple_of(j * t, t), t)
        return tuple(update(h, *carry[h], rows, None) for h in heads)

    init1 = (jnp.full((t, 2 * V_HEAD), -jnp.inf, F32), jnp.zeros((t, 2 * V_HEAD), F32))
    carry = lax.fori_loop(0, qi, full_block, tuple(init1 for _ in heads))

    causal = lax.broadcasted_iota(jnp.int32, (t, t), 1) <= lax.broadcasted_iota(jnp.int32, (t, t), 0)
    diag_rows = pl.ds(pl.multiple_of(qi * t, t), t)
    accs = [update(h, *carry[h], diag_rows, causal)[1] for h in heads]
    for p in range(ATT_HEADS // 2):
        acc0, acc1 = accs[2 * p], accs[2 * p + 1]
        den0 = acc0[:, den_lane[0]:den_lane[0] + 1]
        den1 = acc1[:, den_lane[1]:den_lane[1] + 1]
        o_ref[0, :, p * 2 * V_HEAD:(p + 1) * 2 * V_HEAD] = (
            jnp.where(lo_v, acc0 / den0, acc1 / den1).astype(o_ref.dtype))


def _mla_attn(q, k, v, batch, seq):
    t = ATT_TILE
    g = ATT_HEADS
    return pl.pallas_call(
        _attn_kernel,
        grid=(batch, MLA_HEADS // g, seq // t),
        in_specs=[pl.BlockSpec((1, t, g * MLA_SLOT), lambda b, hp, i: (b, i, hp)),
                  pl.BlockSpec((1, seq, g * MLA_SLOT), lambda b, hp, i: (b, 0, hp)),
                  pl.BlockSpec((1, seq, g * V_HEAD), lambda b, hp, i: (b, 0, hp))],
        out_specs=pl.BlockSpec((1, t, g * V_HEAD), lambda b, hp, i: (b, i, hp)),
        out_shape=jax.ShapeDtypeStruct((batch, seq, MLA_HEADS * V_HEAD), BF16),
        compiler_params=_cparams("parallel", "parallel", "arbitrary"),
        name="mla_attn",
    )(q, k, v)


ROUTE_W = 128


def _merge_kernel(x_ref, yrw_ref, ymla_ref, gate_ref, wbr_ref, wbm_ref, wo_ref, fg_ref,
                  wr_hi_ref, wr_lo_ref, br_ref, x1_ref, h2p_ref, route_ref, hist_ref):
    d = x_ref.shape[1]
    a = jnp.dot(yrw_ref[...], wbr_ref[...], preferred_element_type=F32)
    b = jnp.dot(ymla_ref[...], wbm_ref[...], preferred_element_type=F32)
    merged = gate_ref[:, :d].astype(F32) * a + gate_ref[:, d:].astype(F32) * b
    x1 = x_ref[...] + jnp.dot(merged.astype(BF16), wo_ref[...], preferred_element_type=F32)
    x1_ref[...] = x1
    h2 = _rms(x1, fg_ref[...])
    _slab_store(h2p_ref, _pack_rows(h2))

    h_hi = h2.astype(BF16)
    h_lo = (h2 - h_hi.astype(F32)).astype(BF16)
    logits = (jnp.dot(h_hi, wr_hi_ref[...], preferred_element_type=F32)
              + jnp.dot(h_lo, wr_hi_ref[...], preferred_element_type=F32)
              + jnp.dot(h_hi, wr_lo_ref[...], preferred_element_type=F32)) + br_ref[...]

    lane = lax.broadcasted_iota(jnp.int32, logits.shape, 1)
    big = jnp.int32(ROUTE_W)
    neg = -jnp.inf

    def first_argmax(vals, vmax):
        return jnp.min(jnp.where(vals == vmax, lane, big), axis=-1, keepdims=True)

    grp = jnp.where(lane < N_GROUPS, logits, neg)
    g_max = jnp.max(grp, axis=-1, keepdims=True)
    g_den = jnp.sum(jnp.exp(grp - g_max), axis=-1, keepdims=True)
    g_sel = first_argmax(grp, g_max)
    gate_g = 1.0 / g_den
    lo = N_GROUPS + g_sel * EXPERTS_PER_GROUP
    fine = jnp.where((lane >= lo) & (lane < lo + EXPERTS_PER_GROUP), logits, neg)
    v1 = jnp.max(fine, axis=-1, keepdims=True)
    i1 = first_argmax(fine, v1)
    fine2 = jnp.where(lane == i1, neg, fine)
    v2 = jnp.max(fine2, axis=-1, keepdims=True)
    i2 = first_argmax(fine2, v2)
    e2 = jnp.exp(v2 - v1)
    den = 1.0 + e2
    w1 = gate_g / den
    w2 = gate_g * e2 / den
    route = jnp.where(lane == 0, (i1 - N_GROUPS).astype(F32),
                      jnp.where(lane == 1, (i2 - N_GROUPS).astype(F32),
                                jnp.where(lane == 2, w1, jnp.where(lane == 3, w2, 0.0))))
    route_ref[...] = route
    chosen = jnp.where((lane == i1) | (lane == i2), 1.0, 0.0)
    hist_ref[0] = jnp.broadcast_to(jnp.sum(chosen, axis=0, keepdims=True), hist_ref.shape[1:])


def _merge(x2, y_rw, y_mla, gates, p, tm):
    n, d = x2.shape
    full = lambda w: pl.BlockSpec(w.shape, lambda i: (0, 0))
    row = lambda c: pl.BlockSpec((tm, c), lambda i: (i, 0))
    ws = [p["w_br"], p["w_bm"], p["w_out"], p["ffn_g"], p["wr_hi"], p["wr_lo"], p["b_route"]]
    return pl.pallas_call(
        _merge_kernel,
        grid=(n // tm,),
        in_specs=[row(d), row(y_rw.shape[1]), row(y_mla.shape[1]), row(2 * d)] + [full(w) for w in ws],
        out_specs=[row(d), pl.BlockSpec((tm * SLABS, LANES), lambda i: (i, 0)), row(ROUTE_W),
                   pl.BlockSpec((1, 8, ROUTE_W), lambda i: (i, 0, 0))],
        out_shape=[jax.ShapeDtypeStruct((n, d), F32), jax.ShapeDtypeStruct((n * SLABS, LANES), jnp.uint32),
                   jax.ShapeDtypeStruct((n, ROUTE_W), F32),
                   jax.ShapeDtypeStruct((n // tm, 8, ROUTE_W), F32)],
        compiler_params=_cparams("parallel"),
        name="merge_route",
    )(x2, y_rw, y_mla, gates, *ws)


def _plan_kernel(route_ref, base_ref, dest_ref):
    tm = route_ref.shape[0]
    route = route_ref[...]
    lane = lax.broadcasted_iota(jnp.int32, route.shape, 1).astype(F32)
    pick = [lane == route[:, k:k + 1] for k in range(TOP_K)]
    both = jnp.where(pick[0] | pick[1], 1.0, 0.0).astype(BF16)
    r = lax.broadcasted_iota(jnp.int32, (tm, tm), 0)
    c = lax.broadcasted_iota(jnp.int32, (tm, tm), 1)
    earlier = jnp.where(r > c, 1.0, 0.0).astype(BF16)
    offs = jnp.dot(earlier, both, preferred_element_type=F32) + base_ref[0]
    rows = [jnp.sum(jnp.where(pk, offs, 0.0), axis=-1, keepdims=True) for pk in pick]
    dest_ref[...] = jnp.where(lane == 0.0, rows[0], jnp.where(lane == 1.0, rows[1], 0.0)).astype(jnp.int32)


def _plan(route, base, tm):
    n = route.shape[0]
    return pl.pallas_call(
        _plan_kernel,
        grid=(n // tm,),
        in_specs=[pl.BlockSpec((tm, ROUTE_W), lambda i: (i, 0)),
                  pl.BlockSpec((1, 1, ROUTE_W), lambda i: (i, 0, 0))],
        out_specs=pl.BlockSpec((tm, ROUTE_W), lambda i: (i, 0)),
        out_shape=jax.ShapeDtypeStruct((n, ROUTE_W), jnp.int32),
        compiler_params=_cparams("parallel"),
        name="route_plan",
    )(route, base)


DISPATCH_TILE = 1024
ROW_DMA_UNROLL = 8


def _dispatch_kernel(pad_end_ref, dest_ref, h_ref, xs_ref, zbuf, sem, zsem):
    tm = h_ref.shape[0] // SLABS

    @pl.when(pl.program_id(0) == 0)
    def _():
        zbuf[...] = jnp.zeros_like(zbuf)

        def tail(e):
            first = pl.multiple_of((pad_end_ref[e] - EXPERT_BLOCK) * SLABS, EXPERT_BLOCK * SLABS)
            return pltpu.make_async_copy(zbuf, xs_ref.at[pl.ds(first, EXPERT_BLOCK * SLABS), :], zsem)

        def region_rows(e):
            return pad_end_ref[e] - (pad_end_ref[e - 1] if e else 0)

        n_blocks = xs_ref.shape[0] // (EXPERT_BLOCK * SLABS)
        used_rows = pad_end_ref[N_EXPERTS - 1]

        def spare(b):
            return pltpu.make_async_copy(zbuf, xs_ref.at[pl.ds(b * EXPERT_BLOCK * SLABS, EXPERT_BLOCK * SLABS), :],
                                         zsem)

        spare_blocks = range(n_blocks - N_EXPERTS, n_blocks)
        for e in range(N_EXPERTS):
            @pl.when(region_rows(e) > 0)
            def _(e=e):
                tail(e).start()
        for b in spare_blocks:
            @pl.when(b * EXPERT_BLOCK >= used_rows)
            def _(b=b):
                spare(b).start()
        for e in range(N_EXPERTS):
            @pl.when(region_rows(e) > 0)
            def _(e=e):
                tail(e).wait()
        for b in spare_blocks:
            @pl.when(b * EXPERT_BLOCK >= used_rows)
            def _(b=b):
                spare(b).wait()

    def start(t, _):
        for k in range(TOP_K):
            pltpu.make_async_copy(_slab_rows(h_ref, t), _slab_rows(xs_ref, dest_ref[0, 0, TOP_K * t + k]),
                                  sem).start(priority=k % 2)
        return 0

    lax.fori_loop(0, tm, start, 0, unroll=ROW_DMA_UNROLL)
    all_rows = xs_ref.at[pl.ds(0, TOP_K * tm * SLABS), :]
    pltpu.make_async_copy(all_rows, all_rows, sem).wait()


def _dispatch(h2, dest, pad_end, p_rows):
    n = h2.shape[0] // SLABS
    tm = DISPATCH_TILE
    dest3 = dest.reshape(n // tm, 1, TOP_K * tm)
    grid_spec = pltpu.PrefetchScalarGridSpec(
        num_scalar_prefetch=1,
        grid=(n // tm,),
        in_specs=[pl.BlockSpec((1, 1, TOP_K * tm), lambda i, pe: (i, 0, 0), memory_space=pltpu.SMEM),
                  pl.BlockSpec((tm * SLABS, LANES), lambda i, pe: (i, 0))],
        out_specs=pl.BlockSpec(memory_space=pl.ANY),
        scratch_shapes=[pltpu.VMEM((EXPERT_BLOCK * SLABS, LANES), jnp.uint32),
                        pltpu.SemaphoreType.DMA(()), pltpu.SemaphoreType.DMA(())],
    )
    return pl.pallas_call(
        _dispatch_kernel,
        grid_spec=grid_spec,
        out_shape=jax.ShapeDtypeStruct((p_rows * SLABS, LANES), jnp.uint32),
        compiler_params=_cparams("arbitrary"),
        name="dispatch",
    )(pad_end, dest3, h2)


def _expert_kernel(blk_e_ref, n_used_ref, x_ref, wgu_ref, wd_ref, y_ref):
    del blk_e_ref

    @pl.when(pl.program_id(0) < n_used_ref[0])
    def _():
        x = _unpack_rows(_slab_load(x_ref)).astype(BF16)
        h = jnp.dot(x, wgu_ref[0], preferred_element_type=F32)
        gt = h[:, :D_EXPERT]
        up = h[:, D_EXPERT:]
        act = (gt * jax.nn.sigmoid(gt) * up).astype(BF16)
        _slab_store(y_ref, _pack_rows(jnp.dot(act, wd_ref[0], preferred_element_type=F32)))

    @pl.when(pl.program_id(0) >= n_used_ref[0])
    def _():
        y_ref[...] = jnp.zeros_like(y_ref)


def _experts(xs, blk_expert, n_used, w_gu, w_down):
    p_rows = xs.shape[0] // SLABS
    d = 2 * SLABS * LANES
    n_blocks = p_rows // EXPERT_BLOCK
    rows = pl.BlockSpec((EXPERT_BLOCK * SLABS, LANES), lambda i, be, nu: (i, 0))
    grid_spec = pltpu.PrefetchScalarGridSpec(
        num_scalar_prefetch=2,
        grid=(n_blocks,),
        in_specs=[rows,
                  pl.BlockSpec((1, d, 2 * D_EXPERT), lambda i, be, nu: (be[i], 0, 0)),
                  pl.BlockSpec((1, D_EXPERT, d), lambda i, be, nu: (be[i], 0, 0))],
        out_specs=rows,
    )
    return pl.pallas_call(
        _expert_kernel,
        grid_spec=grid_spec,
        out_shape=jax.ShapeDtypeStruct(xs.shape, jnp.uint32),
        compiler_params=_cparams("arbitrary"),
        name="experts",
    )(blk_expert, n_used, xs, w_gu, w_down)


COMBINE_TILE = 512


def _combine_kernel(dest_ref, dest_next_ref, x1_ref, route_ref, g_ref, yb_ref, o_ref,
                    buf00, buf01, buf10, buf11, sems, *, final_norm):
    tm = x1_ref.shape[0]
    i = pl.program_id(0)
    bufs = ((buf00, buf01), (buf10, buf11))

    def issue(d_ref, slot):
        def start(t, _):
            for k in range(TOP_K):
                pltpu.make_async_copy(_slab_rows(yb_ref, d_ref[0, 0, TOP_K * t + k]),
                                      _slab_rows(bufs[slot][k], t), sems.at[slot]).start(priority=k % 2)
            return 0

        lax.fori_loop(0, tm, start, 0, unroll=ROW_DMA_UNROLL)

    @pl.when(i == 0)
    def _():
        issue(dest_ref, 0)

    for slot in range(2):
        @pl.when((i % 2 == slot) & (i + 1 < pl.num_programs(0)))
        def _(slot=slot):
            issue(dest_next_ref, 1 - slot)

    for slot in range(2):
        @pl.when(i % 2 == slot)
        def _(slot=slot):
            for b in bufs[slot]:
                pltpu.make_async_copy(b, b, sems.at[slot]).wait()
            route = route_ref[...]
            x2 = (x1_ref[...] + route[:, 2:3] * _unpack_rows(_slab_load(bufs[slot][0]))
                  + route[:, 3:4] * _unpack_rows(_slab_load(bufs[slot][1])))
            o_ref[...] = _rms(x2, g_ref[...]) if final_norm else x2


def _combine(x1, route, dest, yb, final_g, final_norm):
    n, d = x1.shape
    tm = COMBINE_TILE
    n_tiles = n // tm
    dest3 = dest.reshape(n_tiles, 1, TOP_K * tm)
    return pl.pallas_call(
        functools.partial(_combine_kernel, final_norm=final_norm),
        grid=(n_tiles,),
        in_specs=[pl.BlockSpec((1, 1, TOP_K * tm), lambda i: (i, 0, 0), memory_space=pltpu.SMEM),
                  pl.BlockSpec((1, 1, TOP_K * tm), lambda i: (jnp.minimum(i + 1, n_tiles - 1), 0, 0),
                               memory_space=pltpu.SMEM),
                  pl.BlockSpec((tm, d), lambda i: (i, 0)),
                  pl.BlockSpec((tm, ROUTE_W), lambda i: (i, 0)),
                  pl.BlockSpec((1, d), lambda i: (0, 0)),
                  pl.BlockSpec(memory_space=pl.ANY)],
        out_specs=pl.BlockSpec((tm, d), lambda i: (i, 0)),
        out_shape=jax.ShapeDtypeStruct((n, d), F32),
        scratch_shapes=[pltpu.VMEM((tm * SLABS, LANES), jnp.uint32) for _ in range(2 * TOP_K)]
                       + [pltpu.SemaphoreType.DMA((2,))],
        compiler_params=_cparams("arbitrary"),
        name="combine",
    )(dest3, dest3, x1, route, final_g, yb)


def _rwkv_params(rw_mu, rw_w0, rw_w_up, rw_a0, rw_a_up, rw_g_up, rw_k_k, rw_k_a, rw_r_k, rw_gn_w, rw_gn_b):
    row = lambda v: v.reshape(1, -1).astype(F32)
    zeros = jnp.zeros((A_LORA, RW_DIM), F32)
    return {
        "mu_r": row(rw_mu[:RW_DIM]), "mu_k": row(rw_mu[RW_DIM:2 * RW_DIM]),
        "mu_v": row(rw_mu[2 * RW_DIM:3 * RW_DIM]), "mu_l": row(rw_mu[3 * RW_DIM:]),
        "w0": row(rw_w0), "a0": row(rw_a0), "k_k": row(rw_k_k), "k_a": row(rw_k_a),
        "r_k": row(rw_r_k), "gn_w": row(rw_gn_w), "gn_b": row(rw_gn_b),
        "w_up": jnp.concatenate([rw_w_up, zeros], axis=0).astype(F32),
        "a_up": jnp.concatenate([zeros, rw_a_up], axis=0).astype(BF16),
        "g_up": rw_g_up.astype(BF16),
    }


def _mla_params(g_qa, w_q_up, g_kva, w_kv_up):
    half = QK_ROPE // 2
    pad = MLA_SLOT - QK_NOPE - QK_ROPE
    wq = w_q_up.reshape(Q_LORA, MLA_HEADS, QK_NOPE + QK_ROPE)
    q_nope, q_r1, q_r2 = wq[..., :QK_NOPE], wq[..., QK_NOPE:QK_NOPE + half], wq[..., QK_NOPE + half:]
    zq = lambda w: jnp.zeros((Q_LORA, MLA_HEADS, w), F32)
    w_qa = jnp.concatenate([q_nope, q_r1, q_r2, zq(pad)], axis=-1).reshape(Q_LORA, -1)
    w_qb = jnp.concatenate([zq(QK_NOPE), -q_r2, q_r1, zq(pad)], axis=-1).reshape(Q_LORA, -1)
    wkv = w_kv_up.reshape(KV_LORA, MLA_HEADS, QK_NOPE + V_HEAD)
    w_k = jnp.concatenate([wkv[..., :QK_NOPE], jnp.zeros((KV_LORA, MLA_HEADS, MLA_SLOT - QK_NOPE), F32)],
                          axis=-1).reshape(KV_LORA, -1)
    w_v = wkv[..., QK_NOPE:].reshape(KV_LORA, -1)
    eye = jnp.eye(half, dtype=F32)
    z = jnp.zeros((half, half), F32)
    zl = jnp.zeros((QK_ROPE, QK_NOPE), F32)
    zr = jnp.zeros((QK_ROPE, pad), F32)
    p_a = jnp.concatenate([zl, jnp.concatenate([eye, z], 0), jnp.concatenate([z, eye], 0), zr], axis=1)
    p_b = jnp.concatenate([zl, jnp.concatenate([z, -eye], 0), jnp.concatenate([eye, z], 0), zr], axis=1)
    place_half = jnp.concatenate([jnp.zeros((half, QK_NOPE), F32), eye, eye, jnp.zeros((half, pad), F32)], axis=1)
    zh = jnp.zeros_like(place_half)
    place = jnp.concatenate([jnp.concatenate([place_half, zh], 1), jnp.concatenate([zh, place_half], 1)], 0)
    one = jnp.concatenate([jnp.ones((1, QK_NOPE), F32), jnp.zeros((1, MLA_SLOT - QK_NOPE), F32)], axis=1)
    return {"g_qa": g_qa.reshape(1, -1), "g_kva": g_kva.reshape(1, -1),
            "w_qa": w_qa.astype(BF16), "w_qb": w_qb.astype(BF16), "w_k": w_k.astype(BF16),
            "w_v": w_v.astype(BF16), "p_a": p_a.astype(BF16), "p_b": p_b.astype(BF16),
            "place": place.astype(BF16), "one": one}


def _rope_cos_sin(positions):
    inv_freq = ROPE_THETA ** (-jnp.arange(0, QK_ROPE, 2, dtype=F32) / QK_ROPE)
    ang = positions.astype(F32).reshape(-1, 1) * inv_freq
    return jnp.concatenate([jnp.cos(ang), jnp.sin(ang)], axis=1)


def _block_layout(hist, n_assign):
    tile_counts = hist[:, 0, N_GROUPS:N_GROUPS + N_EXPERTS].astype(jnp.int32)
    counts = jnp.sum(tile_counts, axis=0)
    padded = (counts + EXPERT_BLOCK - 1) // EXPERT_BLOCK * EXPERT_BLOCK
    pad_end = jnp.cumsum(padded)
    pad_start = pad_end - padded
    tile_base = jnp.cumsum(tile_counts, axis=0) - tile_counts + pad_start[None, :]
    base = jnp.pad(tile_base.astype(F32), ((0, 0), (0, ROUTE_W - N_EXPERTS)))[:, None, :]
    n_blocks = -(-n_assign // EXPERT_BLOCK) + N_EXPERTS
    blk_row = jnp.arange(n_blocks, dtype=jnp.int32) * EXPERT_BLOCK
    blk_expert = jnp.minimum(jnp.sum((pad_end[None, :] <= blk_row[:, None]).astype(jnp.int32), axis=1),
                             N_EXPERTS - 1)
    n_used = (pad_end[-1] // EXPERT_BLOCK).astype(jnp.int32).reshape(1)
    return base, blk_expert, n_used, n_blocks, pad_end.astype(jnp.int32)


def kernel(x, positions, mix_norm_g, w_in, rw_mu, rw_w0, rw_w_up, rw_a0, rw_a_up, rw_g_up, rw_k_k, rw_k_a, rw_r_k, rw_gn_w, rw_gn_b, mla_g_qa, mla_w_q_up, mla_g_kva, mla_w_kv_up, w_branch_rw, w_branch_mla, w_out, ffn_norm_g, moe_w_group, moe_b_group, moe_w_router, moe_b_router, moe_w_gu, moe_w_down, final_norm_g):
    batch, seq, d = x.shape
    assert d == 2 * SLABS * LANES
    n = batch * seq
    depth = w_in.shape[0]
    rw_cols = 3 * RW_DIM + W_LORA + A_LORA + G_LORA
    mla_cols = Q_LORA + KV_LORA + QK_ROPE
    cos_sin = _rope_cos_sin(positions)
    x2 = x.reshape(n, d)

    for l in range(depth):
        wl = w_in[l].astype(BF16)
        c_rw, c_q, c_kvr, gates = _in_proj(
            x2, mix_norm_g[l].reshape(1, d), wl[:, :rw_cols], wl[:, rw_cols:rw_cols + Q_LORA],
            wl[:, rw_cols + Q_LORA:rw_cols + mla_cols], wl[:, rw_cols + mla_cols:], tm=ROW_TILE)

        rp = _rwkv_params(rw_mu[l], rw_w0[l], rw_w_up[l], rw_a0[l], rw_a_up[l], rw_g_up[l], rw_k_k[l],
                          rw_k_a[l], rw_r_k[l], rw_gn_w[l], rw_gn_b[l])
        y_rw = _rwkv(c_rw.reshape(batch, seq, rw_cols), rp, batch, seq).reshape(n, RW_DIM)

        mp = _mla_params(mla_g_qa[l], mla_w_q_up[l], mla_g_kva[l], mla_w_kv_up[l])
        q, k, v = _mla_prep(c_q, c_kvr, cos_sin, mp, tm=ROW_TILE)
        y_mla = _mla_attn(q.reshape(batch, seq, -1), k.reshape(batch, seq, -1),
                          v.reshape(batch, seq, -1), batch, seq).reshape(n, MLA_HEADS * V_HEAD)

        w_route = jnp.concatenate(
            [moe_w_group[l], moe_w_router[l], jnp.zeros((d, ROUTE_W - N_GROUPS - N_EXPERTS), F32)], axis=1)
        b_route = jnp.concatenate(
            [moe_b_group[l], moe_b_router[l], jnp.zeros((ROUTE_W - N_GROUPS - N_EXPERTS,), F32)]).reshape(1, -1)
        wr_hi = w_route.astype(BF16)
        wr_lo = (w_route - wr_hi.astype(F32)).astype(BF16)
        mparams = {"w_br": w_branch_rw[l].astype(BF16), "w_bm": w_branch_mla[l].astype(BF16),
                   "w_out": w_out[l].astype(BF16), "ffn_g": ffn_norm_g[l].reshape(1, d),
                   "wr_hi": wr_hi, "wr_lo": wr_lo, "b_route": b_route}
        x1, h2p, route, hist = _merge(x2, y_rw, y_mla, gates, mparams, tm=ROW_TILE)

        base, blk_expert, n_used, n_blocks, pad_end = _block_layout(hist, n * TOP_K)
        dest = _plan(route, base, tm=ROW_TILE)[:, :TOP_K]
        xs = _dispatch(h2p, dest, pad_end, n_blocks * EXPERT_BLOCK)
        yb = _experts(xs, blk_expert, n_used, moe_w_gu[l].astype(BF16), moe_w_down[l].astype(BF16))
        x2 = _combine(x1, route, dest, yb, final_norm_g.reshape(1, d), final_norm=(l == depth - 1))

    return x2.reshape(batch, seq, d)
```

```python
import functools
import math

import jax
import jax.numpy as jnp
from jax import lax
from jax.experimental import pallas as pl
from jax.experimental.pallas import tpu as pltpu

F32 = jnp.float32
BF16 = jnp.bfloat16

RW_HEADS = 8
RW_HEAD_DIM = 64
RW_DIM = RW_HEADS * RW_HEAD_DIM
W_LORA = 64
A_LORA = 64
G_LORA = 128
GN_EPS = 64e-5
MLA_HEADS = 8
QK_NOPE = 64
QK_ROPE = 32
V_HEAD = 64
Q_LORA = 384
KV_LORA = 256
ROPE_THETA = 10000.0
N_GROUPS = 4
EXPERTS_PER_GROUP = 8
N_EXPERTS = N_GROUPS * EXPERTS_PER_GROUP
TOP_K = 2
D_EXPERT = 256
EXPERT_BLOCK = 512
NORM_EPS = 1e-6

LANES = 128
HEAD_PAIR = 2 * RW_HEAD_DIM
VMEM_LIMIT = 48 * 1024 * 1024
ROW_TILE = 512


def _cparams(*sem):
    return pltpu.CompilerParams(dimension_semantics=sem, vmem_limit_bytes=VMEM_LIMIT)


def _mm(a, b, dims=((1,), (0,))):
    return lax.dot_general(a.astype(BF16), b.astype(BF16), (dims, ((), ())), preferred_element_type=F32)


def _mm_sel(sel_bf16, x, dims=((1,), (0,))):
    dn = (dims, ((), ()))
    hi = x.astype(BF16)
    lo = (x - hi.astype(F32)).astype(BF16)
    return (lax.dot_general(sel_bf16, hi, dn, preferred_element_type=F32)
            + lax.dot_general(sel_bf16, lo, dn, preferred_element_type=F32))


def _seg_sum(x, seg_bf16):
    return jnp.dot(x.astype(BF16), seg_bf16, preferred_element_type=F32)


def _rms(x, g):
    return x * lax.rsqrt(jnp.mean(x * x, axis=-1, keepdims=True) + NORM_EPS) * g


def _pack_rows(x):
    half = x.shape[1] // 2
    bits = lambda v: lax.bitcast_convert_type(v.astype(BF16).astype(F32), jnp.uint32)
    return bits(x[:, :half]) | (bits(x[:, half:]) >> 16)


def _unpack_rows(p):
    hi = lax.bitcast_convert_type(p & jnp.uint32(0xFFFF0000), F32)
    lo = lax.bitcast_convert_type(p << 16, F32)
    return jnp.concatenate([hi, lo], axis=1)


SLABS = 4


def _slab_rows(ref, r):
    return ref.at[pl.ds(pl.multiple_of(r * SLABS, SLABS), SLABS), :]


def _slab_load(ref):
    rows = ref.shape[0] // SLABS
    return jnp.concatenate([ref[pl.ds(j, rows, stride=SLABS), :] for j in range(SLABS)], axis=1)


def _slab_store(ref, x):
    rows = ref.shape[0] // SLABS
    for j in range(SLABS):
        ref[pl.ds(j, rows, stride=SLABS), :] = x[:, j * LANES:(j + 1) * LANES]


def _in_proj_kernel(x_ref, g_ref, wrw_ref, wq_ref, wkvr_ref, wg_ref,
                    crw_ref, cq_ref, ckvr_ref, gate_ref):
    hb = _rms(x_ref[...], g_ref[...]).astype(BF16)
    crw_ref[...] = jnp.dot(hb, wrw_ref[...], preferred_element_type=F32)
    cq_ref[...] = jnp.dot(hb, wq_ref[...], preferred_element_type=F32)
    ckvr_ref[...] = jnp.dot(hb, wkvr_ref[...], preferred_element_type=F32)
    gate_ref[...] = jax.nn.sigmoid(jnp.dot(hb, wg_ref[...], preferred_element_type=F32)).astype(BF16)


def _in_proj(x2, g, w_rw, w_q, w_kvr, w_gate, tm):
    n, d = x2.shape
    full = lambda w: pl.BlockSpec(w.shape, lambda i: (0, 0))
    row = lambda c: pl.BlockSpec((tm, c), lambda i: (i, 0))
    return pl.pallas_call(
        _in_proj_kernel,
        grid=(n // tm,),
        in_specs=[row(d), full(g), full(w_rw), full(w_q), full(w_kvr), full(w_gate)],
        out_specs=[row(w_rw.shape[1]), row(w_q.shape[1]), row(w_kvr.shape[1]), row(w_gate.shape[1])],
        out_shape=[jax.ShapeDtypeStruct((n, w_rw.shape[1]), F32),
                   jax.ShapeDtypeStruct((n, w_q.shape[1]), F32),
                   jax.ShapeDtypeStruct((n, w_kvr.shape[1]), F32),
                   jax.ShapeDtypeStruct((n, w_gate.shape[1]), BF16)],
        compiler_params=_cparams("parallel"),
        name="in_proj",
    )(x2, g, w_rw, w_q, w_kvr, w_gate)


RW_CHUNK = 64
RW_TILE = 1024
RW_GROUP = 16

def _token_shift(cur, halo_ref, first):
    prev_row = jnp.where(first, 0.0, halo_ref[0, 7:8, :])
    rolled = pltpu.roll(cur, 1, 0)
    row = lax.broadcasted_iota(jnp.int32, cur.shape, 0)
    return jnp.where(row == 0, prev_row, rolled)


def _rwkv_kernel(r_ref, k_ref, v_ref, l_ref, hr_ref, hk_ref, hv_ref, hl_ref,
                 mur_ref, muk_ref, muv_ref, mul_ref, w0_ref, a0_ref, kk_ref, ka_ref, rk_ref,
                 gnw_ref, gnb_ref, wup_ref, aup_ref, gup_ref, y_ref, st_ref):
    i = pl.program_id(2)
    first = i == 0

    @pl.when(first)
    def _():
        st_ref[...] = jnp.zeros_like(st_ref)

    def mixed(c_ref, h_ref, mu_ref):
        cur = c_ref[0]
        return cur + (_token_shift(cur, h_ref, first) - cur) * mu_ref[...]

    zr = mixed(r_ref, hr_ref, mur_ref)
    zk = mixed(k_ref, hk_ref, muk_ref)
    zv = mixed(v_ref, hv_ref, muv_ref)
    zl = mixed(l_ref, hl_ref, mul_ref)
    z_wa = zl[:, :LANES]
    z_g = zl[:, LANES:]

    lane = lax.broadcasted_iota(jnp.int32, (LANES, LANES), 1)
    sub = lax.broadcasted_iota(jnp.int32, (LANES, LANES), 0)
    same_head = (lane // RW_HEAD_DIM) == (sub // RW_HEAD_DIM)
    seg = jnp.where(same_head, 1.0, 0.0).astype(BF16)

    w = w0_ref[...] + _mm(jnp.tanh(z_wa), wup_ref[...])
    u = -w
    softplus = jnp.maximum(u, 0.0) + jnp.log(1.0 + jnp.exp(-jnp.abs(u)))
    log_decay = -jnp.exp(-softplus - 0.5)
    a = jax.nn.sigmoid(a0_ref[...] + _mm(z_wa, aup_ref[...]))
    g = _mm(jax.nn.sigmoid(z_g), gup_ref[...])

    kk = zk * kk_ref[...]
    kk = kk / jnp.maximum(jnp.sqrt(_seg_sum(kk * kk, seg)), 1e-12)
    k2 = zk * (1.0 + (a - 1.0) * ka_ref[...])
    bonus = _seg_sum(zr * k2 * rk_ref[...], seg) * zv
    kka = kk * a

    c = RW_CHUNK
    crow = lax.broadcasted_iota(jnp.int32, (c, c), 0)
    ccol = lax.broadcasted_iota(jnp.int32, (c, c), 1)
    cum_sel = jnp.where(crow >= ccol, 1.0, 0.0).astype(BF16)
    tril_incl = sub >= lane
    tril_strict = sub > lane
    eye_l = jnp.where(lane == sub, 1.0, 0.0).astype(F32)
    lo_half = lax.broadcasted_iota(jnp.int32, (c, LANES), 1) < RW_HEAD_DIM
    nt = ((1,), (1,))
    tn = ((0,), (0,))
    zeros_blk = jnp.zeros((2 * c, LANES), BF16)
    zeros_half = jnp.zeros((c, LANES), BF16)

    def stack(t):
        tb = t.astype(BF16)
        return jnp.concatenate([jnp.where(lo_half, tb, zeros_half), jnp.where(lo_half, zeros_half, tb)], axis=0)

    tril_incl2 = jnp.concatenate([tril_incl, tril_incl], axis=1)

    def chunk_group(ids):
        chunks = range(len(ids))
        x_a, x_b, x_k, x_r, x_v, x_bh, x_kh, w_tot, r_dec = [], [], [], [], [], [], [], [], []
        for ci in ids:
            sl = slice(ci * c, (ci + 1) * c)
            ld = log_decay[sl]
            cum = _mm_sel(cum_sel, ld)
            tot = cum[c - 1:c, :]
            e_neg = jnp.exp(-cum)
            e_rest = jnp.exp(tot - cum)
            x_a.append(stack(-kk[sl] * jnp.exp(cum - ld)))
            x_b.append(stack(kka[sl] * e_neg))
            x_k.append(stack(k2[sl] * e_neg))
            r_dec.append(zr[sl] * jnp.exp(cum))
            x_r.append(stack(r_dec[-1]))
            x_v.append(stack(zv[sl]))
            x_bh.append(stack(kka[sl] * e_rest))
            x_kh.append(stack(k2[sl] * e_rest))
            w_tot.append(jnp.exp(tot))

        inter = [_mm(jnp.concatenate([x_a[i], x_r[i]], axis=0),
                     jnp.concatenate([x_b[i], x_k[i]], axis=0), nt) for i in chunks]
        inter = [m.astype(BF16) for m in inter]
        zeros_sq_b = jnp.zeros((LANES, LANES), BF16)
        a_ab = [jnp.where(tril_strict, m[:2 * c, :2 * c], zeros_sq_b) for m in inter]
        a_ak = [jnp.where(tril_strict, m[:2 * c, 2 * c:], zeros_sq_b) for m in inter]
        a_r = [jnp.where(tril_incl2, m[2 * c:], jnp.concatenate([zeros_sq_b, zeros_sq_b], axis=1))
               for m in inter]
        w_ak = [_mm(a_ak[i], x_v[i]).astype(BF16) for i in chunks]

        t_inv = [eye_l + m.astype(F32) for m in a_ab]
        pw = a_ab
        for _ in range(int(math.log2(c)) - 1):
            pw = [_mm(m, m).astype(BF16) for m in pw]
            t_inv = [t_inv[i] + _mm(t_inv[i], pw[i]) for i in chunks]

        solved = [_mm(t_inv[i], jnp.concatenate([x_a[i], w_ak[i]], axis=1)).astype(BF16)
                  for i in chunks]
        rhs = [jnp.concatenate([solved[i], jnp.concatenate([zeros_blk, x_v[i]], axis=1)], axis=0)
               for i in chunks]
        out = [_mm(a_r[i], rhs[i]) for i in chunks]
        carry = [_mm(jnp.concatenate([x_bh[i], x_kh[i]], axis=0), rhs[i], tn) for i in chunks]
        q_hat, y_loc = [], []
        for i in chunks:
            q_hat.append(r_dec[i] + out[i][:c, :LANES] + out[i][c:, :LANES])
            y_loc.append(out[i][:c, LANES:] + out[i][c:, LANES:])
        trans = [jnp.concatenate([eye_l * w_tot[i] + carry[i][:, :LANES], carry[i][:, LANES:]], axis=1)
                 for i in chunks]
        return q_hat, y_loc, trans

    n_chunks = RW_TILE // c
    groups = [chunk_group(range(g, g + RW_GROUP)) for g in range(0, n_chunks, RW_GROUP)]
    q_hat, y_loc, trans = (sum((g[j] for g in groups), []) for j in range(3))
    chunks = range(n_chunks)

    zeros_sq = jnp.zeros((LANES, LANES), F32)

    def compose(later, earlier):
        return _mm(later[:, :LANES], earlier) + jnp.concatenate([zeros_sq, later[:, LANES:]], axis=1)

    scan = list(trans)
    dist = 1
    while dist < len(scan):
        scan = [scan[i] if i < dist else compose(scan[i], scan[i - dist]) for i in range(len(scan))]
        dist *= 2
    prefix = [None] + scan
    q_pre = [None] + [_mm(q_hat[i], prefix[i]) for i in chunks[1:]]
    q_m = jnp.concatenate([q_hat[0]] + [q_pre[i][:, :LANES] for i in chunks[1:]], axis=0)
    y_off = jnp.concatenate([y_loc[0]] + [y_loc[i] + q_pre[i][:, LANES:] for i in chunks[1:]], axis=0)
    state = st_ref[...]
    y = _mm(q_m, state) + y_off
    st_ref[...] = _mm(prefix[-1][:, :LANES], state) + prefix[-1][:, LANES:]

    inv_n = 1.0 / RW_HEAD_DIM
    mean = _seg_sum(y, seg) * inv_n
    d = y - mean
    var = _seg_sum(d * d, seg) * inv_n
    yn = d * lax.rsqrt(var + GN_EPS) * gnw_ref[...] + gnb_ref[...]
    y_ref[0] = ((yn + bonus) * g).astype(y_ref.dtype)


def _rwkv(c_rw, p, batch, seq):
    ts = RW_TILE
    n_pairs = RW_DIM // HEAD_PAIR
    lora_blk = (3 * RW_DIM) // (2 * LANES)
    halo = ts // 8

    def col(off):
        return pl.BlockSpec((1, ts, LANES), lambda b, pp, i, off=off: (b, i, off + pp))

    def col_halo(off):
        return pl.BlockSpec((1, 8, LANES),
                            lambda b, pp, i, off=off: (b, jnp.maximum(i * halo - 1, 0), off + pp))

    vec = pl.BlockSpec((1, LANES), lambda b, pp, i: (0, pp))
    lora_w = pl.BlockSpec((LANES, LANES), lambda b, pp, i: (0, pp))
    in_specs = [
        col(0), col(n_pairs), col(2 * n_pairs),
        pl.BlockSpec((1, ts, 2 * LANES), lambda b, pp, i: (b, i, lora_blk)),
        col_halo(0), col_halo(n_pairs), col_halo(2 * n_pairs),
        pl.BlockSpec((1, 8, 2 * LANES), lambda b, pp, i: (b, jnp.maximum(i * halo - 1, 0), lora_blk)),
        vec, vec, vec, pl.BlockSpec((1, 2 * LANES), lambda b, pp, i: (0, 0)),
        vec, vec, vec, vec, vec, vec, vec, lora_w, lora_w, lora_w,
    ]
    return pl.pallas_call(
        _rwkv_kernel,
        grid=(batch, n_pairs, seq // ts),
        in_specs=in_specs,
        out_specs=pl.BlockSpec((1, ts, LANES), lambda b, pp, i: (b, i, pp)),
        out_shape=jax.ShapeDtypeStruct((batch, seq, RW_DIM), BF16),
        scratch_shapes=[pltpu.VMEM((LANES, LANES), F32)],
        compiler_params=_cparams("parallel", "parallel", "arbitrary"),
        name="rwkv",
    )(c_rw, c_rw, c_rw, c_rw, c_rw, c_rw, c_rw, c_rw,
      p["mu_r"], p["mu_k"], p["mu_v"], p["mu_l"], p["w0"], p["a0"], p["k_k"], p["k_a"], p["r_k"],
      p["gn_w"], p["gn_b"], p["w_up"], p["a_up"], p["g_up"])


MLA_SLOT = 128


def _mla_prep_kernel(cq_ref, ckvr_ref, cs_ref, gq_ref, gkv_ref,
                     wqa_ref, wqb_ref, wk_ref, wv_ref, pa_ref, pb_ref, place_ref, one_ref,
                     q_ref, k_ref, v_ref):
    cs = cs_ref[...]
    cs_hi = cs.astype(BF16)
    cs_lo = (cs - cs_hi.astype(F32)).astype(BF16)
    tables = (jnp.dot(cs_hi, place_ref[...], preferred_element_type=F32)
              + jnp.dot(cs_lo, place_ref[...], preferred_element_type=F32))
    cos = tables[:, :MLA_SLOT] + one_ref[...]
    sin = tables[:, MLA_SLOT:]
    zq = _rms(cq_ref[...], gq_ref[...]).astype(BF16)
    qa = jnp.dot(zq, wqa_ref[...], preferred_element_type=F32)
    qb = jnp.dot(zq, wqb_ref[...], preferred_element_type=F32)
    ckvr = ckvr_ref[...]
    zkv = _rms(ckvr[:, :KV_LORA], gkv_ref[...]).astype(BF16)
    kn = jnp.dot(zkv, wk_ref[...], preferred_element_type=F32)
    v_ref[...] = jnp.dot(zkv, wv_ref[...], preferred_element_type=F32).astype(BF16)
    kr = ckvr[:, KV_LORA:].astype(BF16)
    k_rope = (jnp.dot(kr, pa_ref[...], preferred_element_type=F32) * cos
              + jnp.dot(kr, pb_ref[...], preferred_element_type=F32) * sin)
    scale = math.log2(math.e) / math.sqrt(QK_NOPE + QK_ROPE)
    for h in range(MLA_HEADS):
        sl = slice(h * MLA_SLOT, (h + 1) * MLA_SLOT)
        q_ref[:, sl] = ((qa[:, sl] * cos + qb[:, sl] * sin) * scale).astype(BF16)
        k_ref[:, sl] = (kn[:, sl] + k_rope).astype(BF16)


def _mla_prep(c_q, c_kvr, cos_sin, p, tm):
    n = c_q.shape[0]
    full = lambda w: pl.BlockSpec(w.shape, lambda i: (0, 0))
    row = lambda c: pl.BlockSpec((tm, c), lambda i: (i, 0))
    ws = [p["g_qa"], p["g_kva"], p["w_qa"], p["w_qb"], p["w_k"], p["w_v"], p["p_a"], p["p_b"],
          p["place"], p["one"]]
    hq = MLA_HEADS * MLA_SLOT
    return pl.pallas_call(
        _mla_prep_kernel,
        grid=(n // tm,),
        in_specs=[row(c_q.shape[1]), row(c_kvr.shape[1]), row(cos_sin.shape[1])] + [full(w) for w in ws],
        out_specs=[row(hq), row(hq), row(MLA_HEADS * V_HEAD)],
        out_shape=[jax.ShapeDtypeStruct((n, hq), BF16), jax.ShapeDtypeStruct((n, hq), BF16),
                   jax.ShapeDtypeStruct((n, MLA_HEADS * V_HEAD), BF16)],
        compiler_params=_cparams("parallel"),
        name="mla_prep",
    )(c_q, c_kvr, cos_sin, *ws)


ATT_TILE = 512
ATT_HEADS = 4


def _attn_kernel(q_ref, k_ref, v_ref, o_ref):
    qi = pl.program_id(2)
    t = ATT_TILE
    heads = range(ATT_HEADS)
    lo_v = lax.broadcasted_iota(jnp.int32, (t, 2 * V_HEAD), 1) < V_HEAD
    nt = (((1,), (1,)), ((), ()))
    den_lane = (V_HEAD, 0)

    def augment(h, vb):
        lane_v = lax.broadcasted_iota(jnp.int32, vb.shape, 1)
        own = (lane_v < V_HEAD) if h % 2 == 0 else (lane_v >= V_HEAD)
        return jnp.where(own, vb, jnp.where(lane_v == den_lane[h % 2], 1.0, 0.0).astype(BF16))

    def update(h, m, acc, k_rows, mask):
        q = q_ref[0, :, h * MLA_SLOT:(h + 1) * MLA_SLOT]
        kb = k_ref[0, k_rows, h * MLA_SLOT:(h + 1) * MLA_SLOT]
        vb = v_ref[0, k_rows, (h // 2) * 2 * V_HEAD:(h // 2 + 1) * 2 * V_HEAD]
        s = lax.dot_general(q, kb, nt, preferred_element_type=F32)
        if mask is not None:
            s = jnp.where(mask, s, -jnp.inf)
        m_new = jnp.maximum(m, jnp.max(s, axis=-1, keepdims=True))
        pr = jnp.exp2((s - m_new[:, :1]).astype(BF16))
        return m_new, acc * jnp.exp2(m - m_new) + jnp.dot(pr, augment(h, vb), preferred_element_type=F32)

    def full_block(j, carry):
        rows = pl.ds(pl.multiple_of(j * t, t), t)
        return tuple(update(h, *carry[h], rows, None) for h in heads)

    init1 = (jnp.full((t, 2 * V_HEAD), -jnp.inf, F32), jnp.zeros((t, 2 * V_HEAD), F32))
    carry = lax.fori_loop(0, qi, full_block, tuple(init1 for _ in heads))

    causal = lax.broadcasted_iota(jnp.int32, (t, t), 1) <= lax.broadcasted_iota(jnp.int32, (t, t), 0)
    diag_rows = pl.ds(pl.multiple_of(qi * t, t), t)
    accs = [update(h, *carry[h], diag_rows, causal)[1] for h in heads]
    for p in range(ATT_HEADS // 2):
        acc0, acc1 = accs[2 * p], accs[2 * p + 1]
        den0 = acc0[:, den_lane[0]:den_lane[0] + 1]
        den1 = acc1[:, den_lane[1]:den_lane[1] + 1]
        o_ref[0, :, p * 2 * V_HEAD:(p + 1) * 2 * V_HEAD] = (
            jnp.where(lo_v, acc0 / den0, acc1 / den1).astype(o_ref.dtype))


def _mla_attn(q, k, v, batch, seq):
    t = ATT_TILE
    g = ATT_HEADS
    return pl.pallas_call(
        _attn_kernel,
        grid=(batch, MLA_HEADS // g, seq // t),
        in_specs=[pl.BlockSpec((1, t, g * MLA_SLOT), lambda b, hp, i: (b, i, hp)),
                  pl.BlockSpec((1, seq, g * MLA_SLOT), lambda b, hp, i: (b, 0, hp)),
                  pl.BlockSpec((1, seq, g * V_HEAD), lambda b, hp, i: (b, 0, hp))],
        out_specs=pl.BlockSpec((1, t, g * V_HEAD), lambda b, hp, i: (b, i, hp)),
        out_shape=jax.ShapeDtypeStruct((batch, seq, MLA_HEADS * V_HEAD), BF16),
        compiler_params=_cparams("parallel", "parallel", "arbitrary"),
        name="mla_attn",
    )(q, k, v)


ROUTE_W = 128


def _merge_kernel(x_ref, yrw_ref, ymla_ref, gate_ref, wbr_ref, wbm_ref, wo_ref, fg_ref,
                  wr_ref, br_ref, x1_ref, h2p_ref, route_ref, hist_ref):
    d = x_ref.shape[1]
    a = jnp.dot(yrw_ref[...], wbr_ref[...], preferred_element_type=F32)
    b = jnp.dot(ymla_ref[...], wbm_ref[...], preferred_element_type=F32)
    merged = gate_ref[:, :d].astype(F32) * a + gate_ref[:, d:].astype(F32) * b
    x1 = x_ref[...] + jnp.dot(merged.astype(BF16), wo_ref[...], preferred_element_type=F32)
    x1_ref[...] = x1
    h2 = _rms(x1, fg_ref[...])
    _slab_store(h2p_ref, _pack_rows(h2))

    h_hi = h2.astype(BF16)
    h_lo = (h2 - h_hi.astype(F32)).astype(BF16)
    both = jnp.dot(h_hi, wr_ref[...], preferred_element_type=F32)
    logits = (both[:, :ROUTE_W] + both[:, ROUTE_W:]
              + jnp.dot(h_lo, wr_ref[:, :ROUTE_W], preferred_element_type=F32)) + br_ref[...]

    lane = lax.broadcasted_iota(jnp.int32, logits.shape, 1)
    big = jnp.int32(ROUTE_W)
    neg = -jnp.inf

    def first_argmax(vals, vmax):
        return jnp.min(jnp.where(vals == vmax, lane, big), axis=-1, keepdims=True)

    grp = jnp.where(lane < N_GROUPS, logits, neg)
    g_max = jnp.max(grp, axis=-1, keepdims=True)
    g_den = jnp.sum(jnp.exp(grp - g_max), axis=-1, keepdims=True)
    g_sel = first_argmax(grp, g_max)
    gate_g = 1.0 / g_den
    lo = N_GROUPS + g_sel * EXPERTS_PER_GROUP
    fine = jnp.where((lane >= lo) & (lane < lo + EXPERTS_PER_GROUP), logits, neg)
    v1 = jnp.max(fine, axis=-1, keepdims=True)
    i1 = first_argmax(fine, v1)
    fine2 = jnp.where(lane == i1, neg, fine)
    v2 = jnp.max(fine2, axis=-1, keepdims=True)
    i2 = first_argmax(fine2, v2)
    e2 = jnp.exp(v2 - v1)
    den = 1.0 + e2
    w1 = gate_g / den
    w2 = gate_g * e2 / den
    route = jnp.where(lane == 0, (i1 - N_GROUPS).astype(F32),
                      jnp.where(lane == 1, (i2 - N_GROUPS).astype(F32),
                                jnp.where(lane == 2, w1, jnp.where(lane == 3, w2, 0.0))))
    route_ref[...] = route
    chosen = jnp.where((lane == i1) | (lane == i2), 1.0, 0.0)
    hist_ref[0] = jnp.broadcast_to(jnp.sum(chosen, axis=0, keepdims=True), hist_ref.shape[1:])


def _merge(x2, y_rw, y_mla, gates, p, tm):
    n, d = x2.shape
    full = lambda w: pl.BlockSpec(w.shape, lambda i: (0, 0))
    row = lambda c: pl.BlockSpec((tm, c), lambda i: (i, 0))
    ws = [p["w_br"], p["w_bm"], p["w_out"], p["ffn_g"], p["w_route"], p["b_route"]]
    return pl.pallas_call(
        _merge_kernel,
        grid=(n // tm,),
        in_specs=[row(d), row(y_rw.shape[1]), row(y_mla.shape[1]), row(2 * d)] + [full(w) for w in ws],
        out_specs=[row(d), pl.BlockSpec((tm * SLABS, LANES), lambda i: (i, 0)), row(ROUTE_W),
                   pl.BlockSpec((1, 8, ROUTE_W), lambda i: (i, 0, 0))],
        out_shape=[jax.ShapeDtypeStruct((n, d), F32), jax.ShapeDtypeStruct((n * SLABS, LANES), jnp.uint32),
                   jax.ShapeDtypeStruct((n, ROUTE_W), F32),
                   jax.ShapeDtypeStruct((n // tm, 8, ROUTE_W), F32)],
        compiler_params=_cparams("parallel"),
        name="merge_route",
    )(x2, y_rw, y_mla, gates, *ws)


def _plan_kernel(route_ref, base_ref, dest_ref):
    tm = route_ref.shape[0]
    route = route_ref[...]
    lane = lax.broadcasted_iota(jnp.int32, route.shape, 1).astype(F32)
    pick = [lane == route[:, k:k + 1] for k in range(TOP_K)]
    both = jnp.where(pick[0] | pick[1], 1.0, 0.0).astype(BF16)
    r = lax.broadcasted_iota(jnp.int32, (tm, tm), 0)
    c = lax.broadcasted_iota(jnp.int32, (tm, tm), 1)
    earlier = jnp.where(r > c, 1.0, 0.0).astype(BF16)
    offs = jnp.dot(earlier, both, preferred_element_type=F32) + base_ref[0]
    rows = [jnp.sum(jnp.where(pk, offs, 0.0), axis=-1, keepdims=True) for pk in pick]
    dest_ref[...] = jnp.where(lane == 0.0, rows[0], jnp.where(lane == 1.0, rows[1], 0.0)).astype(jnp.int32)


def _plan(route, base, tm):
    n = route.shape[0]
    return pl.pallas_call(
        _plan_kernel,
        grid=(n // tm,),
        in_specs=[pl.BlockSpec((tm, ROUTE_W), lambda i: (i, 0)),
                  pl.BlockSpec((1, 1, ROUTE_W), lambda i: (i, 0, 0))],
        out_specs=pl.BlockSpec((tm, ROUTE_W), lambda i: (i, 0)),
        out_shape=jax.ShapeDtypeStruct((n, ROUTE_W), jnp.int32),
        compiler_params=_cparams("parallel"),
        name="route_plan",
    )(route, base)


DISPATCH_TILE = 1024
ROW_DMA_UNROLL = 8


def _dispatch_kernel(pad_end_ref, dest_ref, h_ref, xs_ref, zbuf, sem, zsem):
    tm = h_ref.shape[0] // SLABS

    @pl.when(pl.program_id(0) == 0)
    def _():
        zbuf[...] = jnp.zeros_like(zbuf)

        def tail(e):
            first = pl.multiple_of((pad_end_ref[e] - EXPERT_BLOCK) * SLABS, EXPERT_BLOCK * SLABS)
            return pltpu.make_async_copy(zbuf, xs_ref.at[pl.ds(first, EXPERT_BLOCK * SLABS), :], zsem)

        def region_rows(e):
            return pad_end_ref[e] - (pad_end_ref[e - 1] if e else 0)

        n_blocks = xs_ref.shape[0] // (EXPERT_BLOCK * SLABS)
        used_rows = pad_end_ref[N_EXPERTS - 1]

        def spare(b):
            return pltpu.make_async_copy(zbuf, xs_ref.at[pl.ds(b * EXPERT_BLOCK * SLABS, EXPERT_BLOCK * SLABS), :],
                                         zsem)

        spare_blocks = range(n_blocks - N_EXPERTS, n_blocks)
        for e in range(N_EXPERTS):
            @pl.when(region_rows(e) > 0)
            def _(e=e):
                tail(e).start()
        for b in spare_blocks:
            @pl.when(b * EXPERT_BLOCK >= used_rows)
            def _(b=b):
                spare(b).start()
        for e in range(N_EXPERTS):
            @pl.when(region_rows(e) > 0)
            def _(e=e):
                tail(e).wait()
        for b in spare_blocks:
            @pl.when(b * EXPERT_BLOCK >= used_rows)
            def _(b=b):
                spare(b).wait()

    def start(t, _):
        for k in range(TOP_K):
            pltpu.make_async_copy(_slab_rows(h_ref, t), _slab_rows(xs_ref, dest_ref[0, 0, TOP_K * t + k]),
                                  sem).start(priority=k % 2)
        return 0

    lax.fori_loop(0, tm, start, 0, unroll=ROW_DMA_UNROLL)
    all_rows = xs_ref.at[pl.ds(0, TOP_K * tm * SLABS), :]
    pltpu.make_async_copy(all_rows, all_rows, sem).wait()


def _dispatch(h2, dest, pad_end, p_rows):
    n = h2.shape[0] // SLABS
    tm = DISPATCH_TILE
    dest3 = dest.reshape(n // tm, 1, TOP_K * tm)
    grid_spec = pltpu.PrefetchScalarGridSpec(
        num_scalar_prefetch=1,
        grid=(n // tm,),
        in_specs=[pl.BlockSpec((1, 1, TOP_K * tm), lambda i, pe: (i, 0, 0), memory_space=pltpu.SMEM),
                  pl.BlockSpec((tm * SLABS, LANES), lambda i, pe: (i, 0))],
        out_specs=pl.BlockSpec(memory_space=pl.ANY),
        scratch_shapes=[pltpu.VMEM((EXPERT_BLOCK * SLABS, LANES), jnp.uint32),
                        pltpu.SemaphoreType.DMA(()), pltpu.SemaphoreType.DMA(())],
    )
    return pl.pallas_call(
        _dispatch_kernel,
        grid_spec=grid_spec,
        out_shape=jax.ShapeDtypeStruct((p_rows * SLABS, LANES), jnp.uint32),
        compiler_params=_cparams("arbitrary"),
        name="dispatch",
    )(pad_end, dest3, h2)


def _expert_kernel(blk_e_ref, n_used_ref, x_ref, wgu_ref, wd_ref, y_ref):
    del blk_e_ref

    @pl.when(pl.program_id(0) < n_used_ref[0])
    def _():
        x = _unpack_rows(_slab_load(x_ref)).astype(BF16)
        h = jnp.dot(x, wgu_ref[0], preferred_element_type=F32)
        gt = h[:, :D_EXPERT]
        up = h[:, D_EXPERT:]
        act = (gt * jax.nn.sigmoid(gt) * up).astype(BF16)
        _slab_store(y_ref, _pack_rows(jnp.dot(act, wd_ref[0], preferred_element_type=F32)))

    @pl.when(pl.program_id(0) >= n_used_ref[0])
    def _():
        y_ref[...] = jnp.zeros_like(y_ref)


def _experts(xs, blk_expert, n_used, w_gu, w_down):
    p_rows = xs.shape[0] // SLABS
    d = 2 * SLABS * LANES
    n_blocks = p_rows // EXPERT_BLOCK
    rows = pl.BlockSpec((EXPERT_BLOCK * SLABS, LANES), lambda i, be, nu: (i, 0))
    grid_spec = pltpu.PrefetchScalarGridSpec(
        num_scalar_prefetch=2,
        grid=(n_blocks,),
        in_specs=[rows,
                  pl.BlockSpec((1, d, 2 * D_EXPERT), lambda i, be, nu: (be[i], 0, 0)),
                  pl.BlockSpec((1, D_EXPERT, d), lambda i, be, nu: (be[i], 0, 0))],
        out_specs=rows,
    )
    return pl.pallas_call(
        _expert_kernel,
        grid_spec=grid_spec,
        out_shape=jax.ShapeDtypeStruct(xs.shape, jnp.uint32),
        compiler_params=_cparams("arbitrary"),
        name="experts",
    )(blk_expert, n_used, xs, w_gu, w_down)


COMBINE_TILE = 512


def _combine_kernel(dest_ref, dest_next_ref, x1_ref, route_ref, g_ref, yb_ref, o_ref,
                    buf00, buf01, buf10, buf11, sems, *, final_norm):
    tm = x1_ref.shape[0]
    i = pl.program_id(0)
    bufs = ((buf00, buf01), (buf10, buf11))

    def issue(d_ref, slot):
        def start(t, _):
            for k in range(TOP_K):
                pltpu.make_async_copy(_slab_rows(yb_ref, d_ref[0, 0, TOP_K * t + k]),
                                      _slab_rows(bufs[slot][k], t), sems.at[slot]).start(priority=k % 2)
            return 0

        lax.fori_loop(0, tm, start, 0, unroll=ROW_DMA_UNROLL)

    @pl.when(i == 0)
    def _():
        issue(dest_ref, 0)

    for slot in range(2):
        @pl.when((i % 2 == slot) & (i + 1 < pl.num_programs(0)))
        def _(slot=slot):
            issue(dest_next_ref, 1 - slot)

    for slot in range(2):
        @pl.when(i % 2 == slot)
        def _(slot=slot):
            for b in bufs[slot]:
                pltpu.make_async_copy(b, b, sems.at[slot]).wait()
            route = route_ref[...]
            x2 = (x1_ref[...] + route[:, 2:3] * _unpack_rows(_slab_load(bufs[slot][0]))
                  + route[:, 3:4] * _unpack_rows(_slab_load(bufs[slot][1])))
            o_ref[...] = _rms(x2, g_ref[...]) if final_norm else x2


def _combine(x1, route, dest, yb, final_g, final_norm):
    n, d = x1.shape
    tm = COMBINE_TILE
    n_tiles = n // tm
    dest3 = dest.reshape(n_tiles, 1, TOP_K * tm)
    return pl.pallas_call(
        functools.partial(_combine_kernel, final_norm=final_norm),
        grid=(n_tiles,),
        in_specs=[pl.BlockSpec((1, 1, TOP_K * tm), lambda i: (i, 0, 0), memory_space=pltpu.SMEM),
                  pl.BlockSpec((1, 1, TOP_K * tm), lambda i: (jnp.minimum(i + 1, n_tiles - 1), 0, 0),
                               memory_space=pltpu.SMEM),
                  pl.BlockSpec((tm, d), lambda i: (i, 0)),
                  pl.BlockSpec((tm, ROUTE_W), lambda i: (i, 0)),
                  pl.BlockSpec((1, d), lambda i: (0, 0)),
                  pl.BlockSpec(memory_space=pl.ANY)],
        out_specs=pl.BlockSpec((tm, d), lambda i: (i, 0)),
        out_shape=jax.ShapeDtypeStruct((n, d), F32),
        scratch_shapes=[pltpu.VMEM((tm * SLABS, LANES), jnp.uint32) for _ in range(2 * TOP_K)]
                       + [pltpu.SemaphoreType.DMA((2,))],
        compiler_params=_cparams("arbitrary"),
        name="combine",
    )(dest3, dest3, x1, route, final_g, yb)


def _rwkv_params(rw_mu, rw_w0, rw_w_up, rw_a0, rw_a_up, rw_g_up, rw_k_k, rw_k_a, rw_r_k, rw_gn_w, rw_gn_b):
    row = lambda v: v.reshape(1, -1).astype(F32)
    zeros = jnp.zeros((A_LORA, RW_DIM), F32)
    return {
        "mu_r": row(rw_mu[:RW_DIM]), "mu_k": row(rw_mu[RW_DIM:2 * RW_DIM]),
        "mu_v": row(rw_mu[2 * RW_DIM:3 * RW_DIM]), "mu_l": row(rw_mu[3 * RW_DIM:]),
        "w0": row(rw_w0), "a0": row(rw_a0), "k_k": row(rw_k_k), "k_a": row(rw_k_a),
        "r_k": row(rw_r_k), "gn_w": row(rw_gn_w), "gn_b": row(rw_gn_b),
        "w_up": jnp.concatenate([rw_w_up, zeros], axis=0).astype(BF16),
        "a_up": jnp.concatenate([zeros, rw_a_up], axis=0).astype(BF16),
        "g_up": rw_g_up.astype(BF16),
    }


def _mla_params(g_qa, w_q_up, g_kva, w_kv_up):
    half = QK_ROPE // 2
    pad = MLA_SLOT - QK_NOPE - QK_ROPE
    wq = w_q_up.reshape(Q_LORA, MLA_HEADS, QK_NOPE + QK_ROPE)
    q_nope, q_r1, q_r2 = wq[..., :QK_NOPE], wq[..., QK_NOPE:QK_NOPE + half], wq[..., QK_NOPE + half:]
    zq = lambda w: jnp.zeros((Q_LORA, MLA_HEADS, w), F32)
    w_qa = jnp.concatenate([q_nope, q_r1, q_r2, zq(pad)], axis=-1).reshape(Q_LORA, -1)
    w_qb = jnp.concatenate([zq(QK_NOPE), -q_r2, q_r1, zq(pad)], axis=-1).reshape(Q_LORA, -1)
    wkv = w_kv_up.reshape(KV_LORA, MLA_HEADS, QK_NOPE + V_HEAD)
    w_k = jnp.concatenate([wkv[..., :QK_NOPE], jnp.zeros((KV_LORA, MLA_HEADS, MLA_SLOT - QK_NOPE), F32)],
                          axis=-1).reshape(KV_LORA, -1)
    w_v = wkv[..., QK_NOPE:].reshape(KV_LORA, -1)
    eye = jnp.eye(half, dtype=F32)
    z = jnp.zeros((half, half), F32)
    zl = jnp.zeros((QK_ROPE, QK_NOPE), F32)
    zr = jnp.zeros((QK_ROPE, pad), F32)
    p_a = jnp.concatenate([zl, jnp.concatenate([eye, z], 0), jnp.concatenate([z, eye], 0), zr], axis=1)
    p_b = jnp.concatenate([zl, jnp.concatenate([z, -eye], 0), jnp.concatenate([eye, z], 0), zr], axis=1)
    place_half = jnp.concatenate([jnp.zeros((half, QK_NOPE), F32), eye, eye, jnp.zeros((half, pad), F32)], axis=1)
    zh = jnp.zeros_like(place_half)
    place = jnp.concatenate([jnp.concatenate([place_half, zh], 1), jnp.concatenate([zh, place_half], 1)], 0)
    one = jnp.concatenate([jnp.ones((1, QK_NOPE), F32), jnp.zeros((1, MLA_SLOT - QK_NOPE), F32)], axis=1)
    return {"g_qa": g_qa.reshape(1, -1), "g_kva": g_kva.reshape(1, -1),
            "w_qa": w_qa.astype(BF16), "w_qb": w_qb.astype(BF16), "w_k": w_k.astype(BF16),
            "w_v": w_v.astype(BF16), "p_a": p_a.astype(BF16), "p_b": p_b.astype(BF16),
            "place": place.astype(BF16), "one": one}


def _rope_cos_sin(positions):
    inv_freq = ROPE_THETA ** (-jnp.arange(0, QK_ROPE, 2, dtype=F32) / QK_ROPE)
    ang = positions.astype(F32).reshape(-1, 1) * inv_freq
    return jnp.concatenate([jnp.cos(ang), jnp.sin(ang)], axis=1)


def _block_layout(hist, n_assign):
    tile_counts = hist[:, 0, N_GROUPS:N_GROUPS + N_EXPERTS].astype(jnp.int32)
    counts = jnp.sum(tile_counts, axis=0)
    padded = (counts + EXPERT_BLOCK - 1) // EXPERT_BLOCK * EXPERT_BLOCK
    pad_end = jnp.cumsum(padded)
    pad_start = pad_end - padded
    tile_base = jnp.cumsum(tile_counts, axis=0) - tile_counts + pad_start[None, :]
    base = jnp.pad(tile_base.astype(F32), ((0, 0), (0, ROUTE_W - N_EXPERTS)))[:, None, :]
    n_blocks = -(-n_assign // EXPERT_BLOCK) + N_EXPERTS
    blk_row = jnp.arange(n_blocks, dtype=jnp.int32) * EXPERT_BLOCK
    blk_expert = jnp.minimum(jnp.sum((pad_end[None, :] <= blk_row[:, None]).astype(jnp.int32), axis=1),
                             N_EXPERTS - 1)
    n_used = (pad_end[-1] // EXPERT_BLOCK).astype(jnp.int32).reshape(1)
    return base, blk_expert, n_used, n_blocks, pad_end.astype(jnp.int32)


def kernel(x, positions, mix_norm_g, w_in, rw_mu, rw_w0, rw_w_up, rw_a0, rw_a_up, rw_g_up, rw_k_k, rw_k_a, rw_r_k, rw_gn_w, rw_gn_b, mla_g_qa, mla_w_q_up, mla_g_kva, mla_w_kv_up, w_branch_rw, w_branch_mla, w_out, ffn_norm_g, moe_w_group, moe_b_group, moe_w_router, moe_b_router, moe_w_gu, moe_w_down, final_norm_g):
    batch, seq, d = x.shape
    assert d == 2 * SLABS * LANES
    n = batch * seq
    depth = w_in.shape[0]
    rw_cols = 3 * RW_DIM + W_LORA + A_LORA + G_LORA
    mla_cols = Q_LORA + KV_LORA + QK_ROPE
    cos_sin = _rope_cos_sin(positions)
    x2 = x.reshape(n, d)

    for l in range(depth):
        wl = w_in[l].astype(BF16)
        c_rw, c_q, c_kvr, gates = _in_proj(
            x2, mix_norm_g[l].reshape(1, d), wl[:, :rw_cols], wl[:, rw_cols:rw_cols + Q_LORA],
            wl[:, rw_cols + Q_LORA:rw_cols + mla_cols], wl[:, rw_cols + mla_cols:], tm=ROW_TILE)

        rp = _rwkv_params(rw_mu[l], rw_w0[l], rw_w_up[l], rw_a0[l], rw_a_up[l], rw_g_up[l], rw_k_k[l],
                          rw_k_a[l], rw_r_k[l], rw_gn_w[l], rw_gn_b[l])
        y_rw = _rwkv(c_rw.reshape(batch, seq, rw_cols), rp, batch, seq).reshape(n, RW_DIM)

        mp = _mla_params(mla_g_qa[l], mla_w_q_up[l], mla_g_kva[l], mla_w_kv_up[l])
        q, k, v = _mla_prep(c_q, c_kvr, cos_sin, mp, tm=ROW_TILE)
        y_mla = _mla_attn(q.reshape(batch, seq, -1), k.reshape(batch, seq, -1),
                          v.reshape(batch, seq, -1), batch, seq).reshape(n, MLA_HEADS * V_HEAD)

        w_route = jnp.concatenate(
            [moe_w_group[l], moe_w_router[l], jnp.zeros((d, ROUTE_W - N_GROUPS - N_EXPERTS), F32)], axis=1)
        b_route = jnp.concatenate(
            [moe_b_group[l], moe_b_router[l], jnp.zeros((ROUTE_W - N_GROUPS - N_EXPERTS,), F32)]).reshape(1, -1)
        wr_hi = w_route.astype(BF16)
        wr_lo = (w_route - wr_hi.astype(F32)).astype(BF16)
        mparams = {"w_br": w_branch_rw[l].astype(BF16), "w_bm": w_branch_mla[l].astype(BF16),
                   "w_out": w_out[l].astype(BF16), "ffn_g": ffn_norm_g[l].reshape(1, d),
                   "w_route": jnp.concatenate([wr_hi, wr_lo], axis=1), "b_route": b_route}
        x1, h2p, route, hist = _merge(x2, y_rw, y_mla, gates, mparams, tm=ROW_TILE)

        base, blk_expert, n_used, n_blocks, pad_end = _block_layout(hist, n * TOP_K)
        dest = _plan(route, base, tm=ROW_TILE)[:, :TOP_K]
        xs = _dispatch(h2p, dest, pad_end, n_blocks * EXPERT_BLOCK)
        yb = _experts(xs, blk_expert, n_used, moe_w_gu[l].astype(BF16), moe_w_down[l].astype(BF16))
        x2 = _combine(x1, route, dest, yb, final_norm_g.reshape(1, d), final_norm=(l == depth - 1))

    return x2.reshape(batch, seq, d)
```

```python
import functools
import math

import jax
import jax.numpy as jnp
from jax import lax
from jax.experimental import pallas as pl
from jax.experimental.pallas import tpu as pltpu

F32 = jnp.float32
BF16 = jnp.bfloat16

RW_HEADS = 8
RW_HEAD_DIM = 64
RW_DIM = RW_HEADS * RW_HEAD_DIM
W_LORA = 64
A_LORA = 64
G_LORA = 128
GN_EPS = 64e-5
MLA_HEADS = 8
QK_NOPE = 64
QK_ROPE = 32
V_HEAD = 64
Q_LORA = 384
KV_LORA = 256
ROPE_THETA = 10000.0
N_GROUPS = 4
EXPERTS_PER_GROUP = 8
N_EXPERTS = N_GROUPS * EXPERTS_PER_GROUP
TOP_K = 2
D_EXPERT = 256
EXPERT_BLOCK = 512
NORM_EPS = 1e-6

LANES = 128
HEAD_PAIR = 2 * RW_HEAD_DIM
VMEM_LIMIT = 48 * 1024 * 1024
ROW_TILE = 512


def _cparams(*sem):
    return pltpu.CompilerParams(dimension_semantics=sem, vmem_limit_bytes=VMEM_LIMIT)


def _mm(a, b, dims=((1,), (0,))):
    return lax.dot_general(a.astype(BF16), b.astype(BF16), (dims, ((), ())), preferred_element_type=F32)


def _mm_sel(sel_bf16, x, dims=((1,), (0,))):
    dn = (dims, ((), ()))
    hi = x.astype(BF16)
    lo = (x - hi.astype(F32)).astype(BF16)
    return (lax.dot_general(sel_bf16, hi, dn, preferred_element_type=F32)
            + lax.dot_general(sel_bf16, lo, dn, preferred_element_type=F32))


def _seg_sum(x, seg_bf16):
    return jnp.dot(x.astype(BF16), seg_bf16, preferred_element_type=F32)


def _rms(x, g):
    return x * lax.rsqrt(jnp.mean(x * x, axis=-1, keepdims=True) + NORM_EPS) * g


def _pack_rows(x):
    half = x.shape[1] // 2
    bits = lambda v: lax.bitcast_convert_type(v.astype(BF16).astype(F32), jnp.uint32)
    return bits(x[:, :half]) | (bits(x[:, half:]) >> 16)


def _unpack_rows(p):
    hi = lax.bitcast_convert_type(p & jnp.uint32(0xFFFF0000), F32)
    lo = lax.bitcast_convert_type(p << 16, F32)
    return jnp.concatenate([hi, lo], axis=1)


SLABS = 4


def _slab_rows(ref, r):
    return ref.at[pl.ds(pl.multiple_of(r * SLABS, SLABS), SLABS), :]


def _slab_load(ref):
    rows = ref.shape[0] // SLABS
    return jnp.concatenate([ref[pl.ds(j, rows, stride=SLABS), :] for j in range(SLABS)], axis=1)


def _slab_store(ref, x):
    rows = ref.shape[0] // SLABS
    for j in range(SLABS):
        ref[pl.ds(j, rows, stride=SLABS), :] = x[:, j * LANES:(j + 1) * LANES]


def _in_proj_kernel(x_ref, g_ref, wrw_ref, wq_ref, wkvr_ref, wg_ref,
                    crw_ref, cq_ref, ckvr_ref, gate_ref):
    hb = _rms(x_ref[...], g_ref[...]).astype(BF16)
    crw_ref[...] = jnp.dot(hb, wrw_ref[...], preferred_element_type=F32)
    cq_ref[...] = jnp.dot(hb, wq_ref[...], preferred_element_type=F32)
    ckvr_ref[...] = jnp.dot(hb, wkvr_ref[...], preferred_element_type=F32)
    gate_ref[...] = jax.nn.sigmoid(jnp.dot(hb, wg_ref[...], preferred_element_type=F32)).astype(BF16)


def _in_proj(x2, g, w_rw, w_q, w_kvr, w_gate, tm):
    n, d = x2.shape
    full = lambda w: pl.BlockSpec(w.shape, lambda i: (0, 0))
    row = lambda c: pl.BlockSpec((tm, c), lambda i: (i, 0))
    return pl.pallas_call(
        _in_proj_kernel,
        grid=(n // tm,),
        in_specs=[row(d), full(g), full(w_rw), full(w_q), full(w_kvr), full(w_gate)],
        out_specs=[row(w_rw.shape[1]), row(w_q.shape[1]), row(w_kvr.shape[1]), row(w_gate.shape[1])],
        out_shape=[jax.ShapeDtypeStruct((n, w_rw.shape[1]), F32),
                   jax.ShapeDtypeStruct((n, w_q.shape[1]), F32),
                   jax.ShapeDtypeStruct((n, w_kvr.shape[1]), F32),
                   jax.ShapeDtypeStruct((n, w_gate.shape[1]), BF16)],
        compiler_params=_cparams("parallel"),
        name="in_proj",
    )(x2, g, w_rw, w_q, w_kvr, w_gate)


RW_CHUNK = 64
RW_TILE = 1024
RW_GROUP = 16

def _token_shift(cur, halo_ref, first):
    prev_row = jnp.where(first, 0.0, halo_ref[0, 7:8, :])
    rolled = pltpu.roll(cur, 1, 0)
    row = lax.broadcasted_iota(jnp.int32, cur.shape, 0)
    return jnp.where(row == 0, prev_row, rolled)


def _rwkv_kernel(r_ref, k_ref, v_ref, l_ref, hr_ref, hk_ref, hv_ref, hl_ref,
                 mur_ref, muk_ref, muv_ref, mul_ref, w0_ref, a0_ref, kk_ref, ka_ref, rk_ref,
                 gnw_ref, gnb_ref, wup_ref, aup_ref, gup_ref, y_ref, st_ref):
    i = pl.program_id(2)
    first = i == 0

    @pl.when(first)
    def _():
        st_ref[...] = jnp.zeros_like(st_ref)

    def mixed(c_ref, h_ref, mu_ref):
        cur = c_ref[0]
        return cur + (_token_shift(cur, h_ref, first) - cur) * mu_ref[...]

    zr = mixed(r_ref, hr_ref, mur_ref)
    zk = mixed(k_ref, hk_ref, muk_ref)
    zv = mixed(v_ref, hv_ref, muv_ref)
    zl = mixed(l_ref, hl_ref, mul_ref)
    z_wa = zl[:, :LANES]
    z_g = zl[:, LANES:]

    lane = lax.broadcasted_iota(jnp.int32, (LANES, LANES), 1)
    sub = lax.broadcasted_iota(jnp.int32, (LANES, LANES), 0)
    same_head = (lane // RW_HEAD_DIM) == (sub // RW_HEAD_DIM)
    seg = jnp.where(same_head, 1.0, 0.0).astype(BF16)

    w = w0_ref[...] + _mm(jnp.tanh(z_wa), wup_ref[...])
    u = -w
    softplus = jnp.maximum(u, 0.0) + jnp.log(1.0 + jnp.exp(-jnp.abs(u)))
    log_decay = -jnp.exp(-softplus - 0.5)
    a = jax.nn.sigmoid(a0_ref[...] + _mm(z_wa, aup_ref[...]))
    g = _mm(jax.nn.sigmoid(z_g), gup_ref[...])

    kk = zk * kk_ref[...]
    kk = kk / jnp.maximum(jnp.sqrt(_seg_sum(kk * kk, seg)), 1e-12)
    k2 = zk * (1.0 + (a - 1.0) * ka_ref[...])
    bonus = _seg_sum(zr * k2 * rk_ref[...], seg) * zv
    kka = kk * a

    c = RW_CHUNK
    crow = lax.broadcasted_iota(jnp.int32, (c, c), 0)
    ccol = lax.broadcasted_iota(jnp.int32, (c, c), 1)
    cum_sel = jnp.where(crow >= ccol, 1.0, 0.0).astype(BF16)
    tril_incl = sub >= lane
    tril_strict = sub > lane
    eye_l = jnp.where(lane == sub, 1.0, 0.0).astype(F32)
    lo_half = lax.broadcasted_iota(jnp.int32, (c, LANES), 1) < RW_HEAD_DIM
    nt = ((1,), (1,))
    tn = ((0,), (0,))
    zeros_blk = jnp.zeros((2 * c, LANES), BF16)
    zeros_half = jnp.zeros((c, LANES), BF16)

    def stack(t):
        tb = t.astype(BF16)
        return jnp.concatenate([jnp.where(lo_half, tb, zeros_half), jnp.where(lo_half, zeros_half, tb)], axis=0)

    tril_incl2 = jnp.concatenate([tril_incl, tril_incl], axis=1)

    def chunk_group(ids):
        chunks = range(len(ids))
        x_a, x_b, x_k, x_r, x_v, x_bh, x_kh, w_tot, r_dec = [], [], [], [], [], [], [], [], []
        for ci in ids:
            sl = slice(ci * c, (ci + 1) * c)
            ld = log_decay[sl]
            cum = _mm_sel(cum_sel, ld)
            tot = cum[c - 1:c, :]
            e_neg = jnp.exp(-cum)
            e_rest = jnp.exp(tot - cum)
            x_a.append(stack(-kk[sl] * jnp.exp(cum - ld)))
            x_b.append(stack(kka[sl] * e_neg))
            x_k.append(stack(k2[sl] * e_neg))
            r_dec.append(zr[sl] * jnp.exp(cum))
            x_r.append(stack(r_dec[-1]))
            x_v.append(stack(zv[sl]))
            x_bh.append(stack(kka[sl] * e_rest))
            x_kh.append(stack(k2[sl] * e_rest))
            w_tot.append(jnp.exp(tot))

        inter = [_mm(jnp.concatenate([x_a[i], x_r[i]], axis=0),
                     jnp.concatenate([x_b[i], x_k[i]], axis=0), nt) for i in chunks]
        inter = [m.astype(BF16) for m in inter]
        zeros_sq_b = jnp.zeros((LANES, LANES), BF16)
        a_ab = [jnp.where(tril_strict, m[:2 * c, :2 * c], zeros_sq_b) for m in inter]
        a_ak = [jnp.where(tril_strict, m[:2 * c, 2 * c:], zeros_sq_b) for m in inter]
        a_r = [jnp.where(tril_incl2, m[2 * c:], jnp.concatenate([zeros_sq_b, zeros_sq_b], axis=1))
               for m in inter]
        w_ak = [_mm(a_ak[i], x_v[i]).astype(BF16) for i in chunks]

        t_inv = [eye_l + m.astype(F32) for m in a_ab]
        pw = a_ab
        for _ in range(int(math.log2(c)) - 1):
            pw = [_mm(m, m).astype(BF16) for m in pw]
            t_inv = [t_inv[i] + _mm(t_inv[i], pw[i]) for i in chunks]

        solved = [_mm(t_inv[i], jnp.concatenate([x_a[i], w_ak[i]], axis=1)).astype(BF16)
                  for i in chunks]
        rhs = [jnp.concatenate([solved[i], jnp.concatenate([zeros_blk, x_v[i]], axis=1)], axis=0)
               for i in chunks]
        out = [_mm(a_r[i], rhs[i]) for i in chunks]
        carry = [_mm(jnp.concatenate([x_bh[i], x_kh[i]], axis=0), rhs[i], tn) for i in chunks]
        q_hat, y_loc = [], []
        for i in chunks:
            q_hat.append(r_dec[i] + out[i][:c, :LANES] + out[i][c:, :LANES])
            y_loc.append(out[i][:c, LANES:] + out[i][c:, LANES:])
        trans = [jnp.concatenate([eye_l * w_tot[i] + carry[i][:, :LANES], carry[i][:, LANES:]], axis=1)
                 for i in chunks]
        return q_hat, y_loc, trans

    n_chunks = RW_TILE // c
    groups = [chunk_group(range(g, g + RW_GROUP)) for g in range(0, n_chunks, RW_GROUP)]
    q_hat, y_loc, trans = (sum((g[j] for g in groups), []) for j in range(3))
    chunks = range(n_chunks)

    zeros_sq = jnp.zeros((LANES, LANES), F32)

    def compose(later, earlier):
        return _mm(later[:, :LANES], earlier) + jnp.concatenate([zeros_sq, later[:, LANES:]], axis=1)

    scan = list(trans)
    dist = 1
    while dist < len(scan):
        scan = [scan[i] if i < dist else compose(scan[i], scan[i - dist]) for i in range(len(scan))]
        dist *= 2
    prefix = [None] + scan
    q_pre = [None] + [_mm(q_hat[i], prefix[i]) for i in chunks[1:]]
    q_m = jnp.concatenate([q_hat[0]] + [q_pre[i][:, :LANES] for i in chunks[1:]], axis=0)
    y_off = jnp.concatenate([y_loc[0]] + [y_loc[i] + q_pre[i][:, LANES:] for i in chunks[1:]], axis=0)
    state = st_ref[...]
    y = _mm(q_m, state) + y_off
    st_ref[...] = _mm(prefix[-1][:, :LANES], state) + prefix[-1][:, LANES:]

    inv_n = 1.0 / RW_HEAD_DIM
    mean = _seg_sum(y, seg) * inv_n
    d = y - mean
    var = _seg_sum(d * d, seg) * inv_n
    yn = d * lax.rsqrt(var + GN_EPS) * gnw_ref[...] + gnb_ref[...]
    y_ref[0] = ((yn + bonus) * g).astype(y_ref.dtype)


def _rwkv(c_rw, p, batch, seq):
    ts = RW_TILE
    n_pairs = RW_DIM // HEAD_PAIR
    lora_blk = (3 * RW_DIM) // (2 * LANES)
    halo = ts // 8

    def col(off):
        return pl.BlockSpec((1, ts, LANES), lambda b, pp, i, off=off: (b, i, off + pp))

    def col_halo(off):
        return pl.BlockSpec((1, 8, LANES),
                            lambda b, pp, i, off=off: (b, jnp.maximum(i * halo - 1, 0), off + pp))

    vec = pl.BlockSpec((1, LANES), lambda b, pp, i: (0, pp))
    lora_w = pl.BlockSpec((LANES, LANES), lambda b, pp, i: (0, pp))
    in_specs = [
        col(0), col(n_pairs), col(2 * n_pairs),
        pl.BlockSpec((1, ts, 2 * LANES), lambda b, pp, i: (b, i, lora_blk)),
        col_halo(0), col_halo(n_pairs), col_halo(2 * n_pairs),
        pl.BlockSpec((1, 8, 2 * LANES), lambda b, pp, i: (b, jnp.maximum(i * halo - 1, 0), lora_blk)),
        vec, vec, vec, pl.BlockSpec((1, 2 * LANES), lambda b, pp, i: (0, 0)),
        vec, vec, vec, vec, vec, vec, vec, lora_w, lora_w, lora_w,
    ]
    return pl.pallas_call(
        _rwkv_kernel,
        grid=(batch, n_pairs, seq // ts),
        in_specs=in_specs,
        out_specs=pl.BlockSpec((1, ts, LANES), lambda b, pp, i: (b, i, pp)),
        out_shape=jax.ShapeDtypeStruct((batch, seq, RW_DIM), BF16),
        scratch_shapes=[pltpu.VMEM((LANES, LANES), F32)],
        compiler_params=_cparams("parallel", "parallel", "arbitrary"),
        name="rwkv",
    )(c_rw, c_rw, c_rw, c_rw, c_rw, c_rw, c_rw, c_rw,
      p["mu_r"], p["mu_k"], p["mu_v"], p["mu_l"], p["w0"], p["a0"], p["k_k"], p["k_a"], p["r_k"],
      p["gn_w"], p["gn_b"], p["w_up"], p["a_up"], p["g_up"])


MLA_SLOT = 128


def _mla_prep_kernel(cq_ref, ckvr_ref, cs_ref, gq_ref, gkv_ref,
                     wqa_ref, wqb_ref, wk_ref, wv_ref, pa_ref, pb_ref, place_ref, one_ref,
                     qt_ref, k_ref, vt_ref):
    cs = cs_ref[...]
    cs_hi = cs.astype(BF16)
    cs_lo = (cs - cs_hi.astype(F32)).astype(BF16)
    tables = (jnp.dot(cs_hi, place_ref[...], preferred_element_type=F32)
              + jnp.dot(cs_lo, place_ref[...], preferred_element_type=F32))
    cos = tables[:, :MLA_SLOT] + one_ref[...]
    sin = tables[:, MLA_SLOT:]
    zq = _rms(cq_ref[...], gq_ref[...]).astype(BF16)
    qa = jnp.dot(zq, wqa_ref[...], preferred_element_type=F32)
    qb = jnp.dot(zq, wqb_ref[...], preferred_element_type=F32)
    ckvr = ckvr_ref[...]
    zkv = _rms(ckvr[:, :KV_LORA], gkv_ref[...]).astype(BF16)
    kn = jnp.dot(zkv, wk_ref[...], preferred_element_type=F32)
    v = jnp.dot(zkv, wv_ref[...], preferred_element_type=F32)
    for blk in range(v.shape[1] // LANES):
        vt_ref[0, blk * LANES:(blk + 1) * LANES, :] = v[:, blk * LANES:(blk + 1) * LANES].T.astype(BF16)
    kr = ckvr[:, KV_LORA:].astype(BF16)
    k_rope = (jnp.dot(kr, pa_ref[...], preferred_element_type=F32) * cos
              + jnp.dot(kr, pb_ref[...], preferred_element_type=F32) * sin)
    scale = math.log2(math.e) / math.sqrt(QK_NOPE + QK_ROPE)
    for h in range(MLA_HEADS):
        sl = slice(h * MLA_SLOT, (h + 1) * MLA_SLOT)
        qt_ref[0, sl, :] = ((qa[:, sl] * cos + qb[:, sl] * sin) * scale).T.astype(BF16)
        k_ref[:, sl] = (kn[:, sl] + k_rope).astype(BF16)


def _mla_prep(c_q, c_kvr, cos_sin, p, tm, batch, seq):
    n = c_q.shape[0]
    full = lambda w: pl.BlockSpec(w.shape, lambda i: (0, 0))
    row = lambda c: pl.BlockSpec((tm, c), lambda i: (i, 0))
    per_seq = seq // tm
    col = lambda r: pl.BlockSpec((1, r, tm), lambda i: (i // per_seq, 0, i % per_seq))
    ws = [p["g_qa"], p["g_kva"], p["w_qa"], p["w_qb"], p["w_k"], p["w_v"], p["p_a"], p["p_b"],
          p["place"], p["one"]]
    hq = MLA_HEADS * MLA_SLOT
    hv = MLA_HEADS * V_HEAD
    return pl.pallas_call(
        _mla_prep_kernel,
        grid=(n // tm,),
        in_specs=[row(c_q.shape[1]), row(c_kvr.shape[1]), row(cos_sin.shape[1])] + [full(w) for w in ws],
        out_specs=[col(hq), row(hq), col(hv)],
        out_shape=[jax.ShapeDtypeStruct((batch, hq, seq), BF16), jax.ShapeDtypeStruct((n, hq), BF16),
                   jax.ShapeDtypeStruct((batch, hv, seq), BF16)],
        compiler_params=_cparams("parallel"),
        name="mla_prep",
    )(c_q, c_kvr, cos_sin, *ws)


ATT_TILE = 512
ATT_HEADS = 4


def _attn_kernel(qt_ref, k_ref, vt_ref, o_ref):
    qi = pl.program_id(2)
    t = ATT_TILE
    heads = range(ATT_HEADS)
    den_row = (V_HEAD, 0)

    def augment(h, vt):
        r = lax.broadcasted_iota(jnp.int32, vt.shape, 0)
        own = (r < V_HEAD) if h % 2 == 0 else (r >= V_HEAD)
        return jnp.where(own, vt, jnp.where(r == den_row[h % 2], 1.0, 0.0).astype(BF16))

    def block(j, carry, mask):
        keys = pl.ds(pl.multiple_of(j * t, t), t)
        sts = []
        for h in heads:
            qt = qt_ref[0, h * MLA_SLOT:(h + 1) * MLA_SLOT, :]
            kb = k_ref[0, keys, h * MLA_SLOT:(h + 1) * MLA_SLOT]
            st = jnp.dot(kb, qt, preferred_element_type=F32)
            sts.append(st if mask is None else jnp.where(mask, st, -jnp.inf))
        m_new = [jnp.maximum(carry[h][0], jnp.max(sts[h], axis=0, keepdims=True)) for h in heads]
        pts = [jnp.exp2((sts[h] - m_new[h]).astype(BF16)) for h in heads]
        out = []
        for h in heads:
            m, acc = carry[h]
            vt = vt_ref[0, (h // 2) * 2 * V_HEAD:(h // 2 + 1) * 2 * V_HEAD, keys]
            acc = acc * jnp.exp2(m - m_new[h]) + jnp.dot(augment(h, vt), pts[h], preferred_element_type=F32)
            out.append((m_new[h], acc))
        return tuple(out)

    init1 = (jnp.full((1, t), -jnp.inf, F32), jnp.zeros((2 * V_HEAD, t), F32))
    carry = lax.fori_loop(0, qi, lambda j, c: block(j, c, None), tuple(init1 for _ in heads))

    causal = lax.broadcasted_iota(jnp.int32, (t, t), 0) <= lax.broadcasted_iota(jnp.int32, (t, t), 1)
    accs = [mc[1] for mc in block(qi, carry, causal)]
    lo_rows = lax.broadcasted_iota(jnp.int32, (2 * V_HEAD, t), 0) < V_HEAD
    for p in range(ATT_HEADS // 2):
        acc0, acc1 = accs[2 * p], accs[2 * p + 1]
        den0 = acc0[den_row[0]:den_row[0] + 1, :]
        den1 = acc1[den_row[1]:den_row[1] + 1, :]
        out_t = jnp.where(lo_rows, acc0 / den0, acc1 / den1)
        o_ref[0, :, p * 2 * V_HEAD:(p + 1) * 2 * V_HEAD] = out_t.T.astype(o_ref.dtype)


def _mla_attn(qt, k, vt, batch, seq):
    t = ATT_TILE
    g = ATT_HEADS
    return pl.pallas_call(
        _attn_kernel,
        grid=(batch, MLA_HEADS // g, seq // t),
        in_specs=[pl.BlockSpec((1, g * MLA_SLOT, t), lambda b, hp, i: (b, hp, i)),
                  pl.BlockSpec((1, seq, g * MLA_SLOT), lambda b, hp, i: (b, 0, hp)),
                  pl.BlockSpec((1, g * V_HEAD, seq), lambda b, hp, i: (b, hp, 0))],
        out_specs=pl.BlockSpec((1, t, g * V_HEAD), lambda b, hp, i: (b, i, hp)),
        out_shape=jax.ShapeDtypeStruct((batch, seq, MLA_HEADS * V_HEAD), BF16),
        compiler_params=_cparams("parallel", "parallel", "arbitrary"),
        name="mla_attn",
    )(qt, k, vt)


ROUTE_W = 128


def _merge_kernel(x_ref, yrw_ref, ymla_ref, gate_ref, wbr_ref, wbm_ref, wo_ref, fg_ref,
                  wr_ref, br_ref, x1_ref, h2p_ref, route_ref, hist_ref):
    d = x_ref.shape[1]
    a = jnp.dot(yrw_ref[...], wbr_ref[...], preferred_element_type=F32)
    b = jnp.dot(ymla_ref[...], wbm_ref[...], preferred_element_type=F32)
    merged = gate_ref[:, :d].astype(F32) * a + gate_ref[:, d:].astype(F32) * b
    x1 = x_ref[...] + jnp.dot(merged.astype(BF16), wo_ref[...], preferred_element_type=F32)
    x1_ref[...] = x1
    h2 = _rms(x1, fg_ref[...])
    _slab_store(h2p_ref, _pack_rows(h2))

    h_hi = h2.astype(BF16)
    h_lo = (h2 - h_hi.astype(F32)).astype(BF16)
    both = jnp.dot(h_hi, wr_ref[...], preferred_element_type=F32)
    logits = (both[:, :ROUTE_W] + both[:, ROUTE_W:]
              + jnp.dot(h_lo, wr_ref[:, :ROUTE_W], preferred_element_type=F32)) + br_ref[...]

    lane = lax.broadcasted_iota(jnp.int32, logits.shape, 1)
    big = jnp.int32(ROUTE_W)
    neg = -jnp.inf

    def first_argmax(vals, vmax):
        return jnp.min(jnp.where(vals == vmax, lane, big), axis=-1, keepdims=True)

    grp = jnp.where(lane < N_GROUPS, logits, neg)
    g_max = jnp.max(grp, axis=-1, keepdims=True)
    g_den = jnp.sum(jnp.exp(grp - g_max), axis=-1, keepdims=True)
    g_sel = first_argmax(grp, g_max)
    gate_g = 1.0 / g_den
    lo = N_GROUPS + g_sel * EXPERTS_PER_GROUP
    fine = jnp.where((lane >= lo) & (lane < lo + EXPERTS_PER_GROUP), logits, neg)
    v1 = jnp.max(fine, axis=-1, keepdims=True)
    i1 = first_argmax(fine, v1)
    fine2 = jnp.where(lane == i1, neg, fine)
    v2 = jnp.max(fine2, axis=-1, keepdims=True)
    i2 = first_argmax(fine2, v2)
    e2 = jnp.exp(v2 - v1)
    den = 1.0 + e2
    w1 = gate_g / den
    w2 = gate_g * e2 / den
    route = jnp.where(lane == 0, (i1 - N_GROUPS).astype(F32),
                      jnp.where(lane == 1, (i2 - N_GROUPS).astype(F32),
                                jnp.where(lane == 2, w1, jnp.where(lane == 3, w2, 0.0))))
    route_ref[...] = route
    chosen = jnp.where((lane == i1) | (lane == i2), 1.0, 0.0)
    hist_ref[0] = jnp.broadcast_to(jnp.sum(chosen, axis=0, keepdims=True), hist_ref.shape[1:])


def _merge(x2, y_rw, y_mla, gates, p, tm):
    n, d = x2.shape
    full = lambda w: pl.BlockSpec(w.shape, lambda i: (0, 0))
    row = lambda c: pl.BlockSpec((tm, c), lambda i: (i, 0))
    ws = [p["w_br"], p["w_bm"], p["w_out"], p["ffn_g"], p["w_route"], p["b_route"]]
    return pl.pallas_call(
        _merge_kernel,
        grid=(n // tm,),
        in_specs=[row(d), row(y_rw.shape[1]), row(y_mla.shape[1]), row(2 * d)] + [full(w) for w in ws],
        out_specs=[row(d), pl.BlockSpec((tm * SLABS, LANES), lambda i: (i, 0)), row(ROUTE_W),
                   pl.BlockSpec((1, 8, ROUTE_W), lambda i: (i, 0, 0))],
        out_shape=[jax.ShapeDtypeStruct((n, d), F32), jax.ShapeDtypeStruct((n * SLABS, LANES), jnp.uint32),
                   jax.ShapeDtypeStruct((n, ROUTE_W), F32),
                   jax.ShapeDtypeStruct((n // tm, 8, ROUTE_W), F32)],
        compiler_params=_cparams("parallel"),
        name="merge_route",
    )(x2, y_rw, y_mla, gates, *ws)


def _plan_kernel(route_ref, base_ref, dest_ref):
    tm = route_ref.shape[0]
    route = route_ref[...]
    lane = lax.broadcasted_iota(jnp.int32, route.shape, 1).astype(F32)
    pick = [lane == route[:, k:k + 1] for k in range(TOP_K)]
    both = jnp.where(pick[0] | pick[1], 1.0, 0.0).astype(BF16)
    r = lax.broadcasted_iota(jnp.int32, (tm, tm), 0)
    c = lax.broadcasted_iota(jnp.int32, (tm, tm), 1)
    earlier = jnp.where(r > c, 1.0, 0.0).astype(BF16)
    offs = jnp.dot(earlier, both, preferred_element_type=F32) + base_ref[0]
    rows = [jnp.sum(jnp.where(pk, offs, 0.0), axis=-1, keepdims=True) for pk in pick]
    dest_ref[...] = jnp.where(lane == 0.0, rows[0], jnp.where(lane == 1.0, rows[1], 0.0)).astype(jnp.int32)


def _plan(route, base, tm):
    n = route.shape[0]
    return pl.pallas_call(
        _plan_kernel,
        grid=(n // tm,),
        in_specs=[pl.BlockSpec((tm, ROUTE_W), lambda i: (i, 0)),
                  pl.BlockSpec((1, 1, ROUTE_W), lambda i: (i, 0, 0))],
        out_specs=pl.BlockSpec((tm, ROUTE_W), lambda i: (i, 0)),
        out_shape=jax.ShapeDtypeStruct((n, ROUTE_W), jnp.int32),
        compiler_params=_cparams("parallel"),
        name="route_plan",
    )(route, base)


DISPATCH_TILE = 1024
ROW_DMA_UNROLL = 8


def _dispatch_kernel(pad_end_ref, dest_ref, h_ref, xs_ref, zbuf, sem, zsem):
    tm = h_ref.shape[0] // SLABS

    @pl.when(pl.program_id(0) == 0)
    def _():
        zbuf[...] = jnp.zeros_like(zbuf)

        def tail(e):
            first = pl.multiple_of((pad_end_ref[e] - EXPERT_BLOCK) * SLABS, EXPERT_BLOCK * SLABS)
            return pltpu.make_async_copy(zbuf, xs_ref.at[pl.ds(first, EXPERT_BLOCK * SLABS), :], zsem)

        def region_rows(e):
            return pad_end_ref[e] - (pad_end_ref[e - 1] if e else 0)

        n_blocks = xs_ref.shape[0] // (EXPERT_BLOCK * SLABS)
        used_rows = pad_end_ref[N_EXPERTS - 1]

        def spare(b):
            return pltpu.make_async_copy(zbuf, xs_ref.at[pl.ds(b * EXPERT_BLOCK * SLABS, EXPERT_BLOCK * SLABS), :],
                                         zsem)

        spare_blocks = range(n_blocks - N_EXPERTS, n_blocks)
        for e in range(N_EXPERTS):
            @pl.when(region_rows(e) > 0)
            def _(e=e):
                tail(e).start()
        for b in spare_blocks:
            @pl.when(b * EXPERT_BLOCK >= used_rows)
            def _(b=b):
                spare(b).start()
        for e in range(N_EXPERTS):
            @pl.when(region_rows(e) > 0)
            def _(e=e):
                tail(e).wait()
        for b in spare_blocks:
            @pl.when(b * EXPERT_BLOCK >= used_rows)
            def _(b=b):
                spare(b).wait()

    def start(t, _):
        for k in range(TOP_K):
            pltpu.make_async_copy(_slab_rows(h_ref, t), _slab_rows(xs_ref, dest_ref[0, 0, TOP_K * t + k]),
                                  sem).start(priority=k % 2)
        return 0

    lax.fori_loop(0, tm, start, 0, unroll=ROW_DMA_UNROLL)
    all_rows = xs_ref.at[pl.ds(0, TOP_K * tm * SLABS), :]
    pltpu.make_async_copy(all_rows, all_rows, sem).wait()


def _dispatch(h2, dest, pad_end, p_rows):
    n = h2.shape[0] // SLABS
    tm = DISPATCH_TILE
    dest3 = dest.reshape(n // tm, 1, TOP_K * tm)
    grid_spec = pltpu.PrefetchScalarGridSpec(
        num_scalar_prefetch=1,
        grid=(n // tm,),
        in_specs=[pl.BlockSpec((1, 1, TOP_K * tm), lambda i, pe: (i, 0, 0), memory_space=pltpu.SMEM),
                  pl.BlockSpec((tm * SLABS, LANES), lambda i, pe: (i, 0))],
        out_specs=pl.BlockSpec(memory_space=pl.ANY),
        scratch_shapes=[pltpu.VMEM((EXPERT_BLOCK * SLABS, LANES), jnp.uint32),
                        pltpu.SemaphoreType.DMA(()), pltpu.SemaphoreType.DMA(())],
    )
    return pl.pallas_call(
        _dispatch_kernel,
        grid_spec=grid_spec,
        out_shape=jax.ShapeDtypeStruct((p_rows * SLABS, LANES), jnp.uint32),
        compiler_params=_cparams("arbitrary"),
        name="dispatch",
    )(pad_end, dest3, h2)


def _expert_kernel(blk_e_ref, n_used_ref, x_ref, wgu_ref, wd_ref, y_ref):
    del blk_e_ref

    @pl.when(pl.program_id(0) < n_used_ref[0])
    def _():
        x = _unpack_rows(_slab_load(x_ref)).astype(BF16)
        h = jnp.dot(x, wgu_ref[0], preferred_element_type=F32)
        gt = h[:, :D_EXPERT]
        up = h[:, D_EXPERT:]
        act = (gt * jax.nn.sigmoid(gt) * up).astype(BF16)
        _slab_store(y_ref, _pack_rows(jnp.dot(act, wd_ref[0], preferred_element_type=F32)))

    @pl.when(pl.program_id(0) >= n_used_ref[0])
    def _():
        y_ref[...] = jnp.zeros_like(y_ref)


def _experts(xs, blk_expert, n_used, w_gu, w_down):
    p_rows = xs.shape[0] // SLABS
    d = 2 * SLABS * LANES
    n_blocks = p_rows // EXPERT_BLOCK
    rows = pl.BlockSpec((EXPERT_BLOCK * SLABS, LANES), lambda i, be, nu: (i, 0))
    grid_spec = pltpu.PrefetchScalarGridSpec(
        num_scalar_prefetch=2,
        grid=(n_blocks,),
        in_specs=[rows,
                  pl.BlockSpec((1, d, 2 * D_EXPERT), lambda i, be, nu: (be[i], 0, 0)),
                  pl.BlockSpec((1, D_EXPERT, d), lambda i, be, nu: (be[i], 0, 0))],
        out_specs=rows,
    )
    return pl.pallas_call(
        _expert_kernel,
        grid_spec=grid_spec,
        out_shape=jax.ShapeDtypeStruct(xs.shape, jnp.uint32),
        compiler_params=_cparams("arbitrary"),
        name="experts",
    )(blk_expert, n_used, xs, w_gu, w_down)


COMBINE_TILE = 512


def _combine_kernel(dest_ref, dest_next_ref, x1_ref, route_ref, g_ref, yb_ref, o_ref,
                    buf00, buf01, buf10, buf11, sems, *, final_norm):
    tm = x1_ref.shape[0]
    i = pl.program_id(0)
    bufs = ((buf00, buf01), (buf10, buf11))

    def issue(d_ref, slot):
        def start(t, _):
            for k in range(TOP_K):
                pltpu.make_async_copy(_slab_rows(yb_ref, d_ref[0, 0, TOP_K * t + k]),
                                      _slab_rows(bufs[slot][k], t), sems.at[slot]).start(priority=k % 2)
            return 0

        lax.fori_loop(0, tm, start, 0, unroll=ROW_DMA_UNROLL)

    @pl.when(i == 0)
    def _():
        issue(dest_ref, 0)

    for slot in range(2):
        @pl.when((i % 2 == slot) & (i + 1 < pl.num_programs(0)))
        def _(slot=slot):
            issue(dest_next_ref, 1 - slot)

    for slot in range(2):
        @pl.when(i % 2 == slot)
        def _(slot=slot):
            for b in bufs[slot]:
                pltpu.make_async_copy(b, b, sems.at[slot]).wait()
            route = route_ref[...]
            x2 = (x1_ref[...] + route[:, 2:3] * _unpack_rows(_slab_load(bufs[slot][0]))
                  + route[:, 3:4] * _unpack_rows(_slab_load(bufs[slot][1])))
            o_ref[...] = _rms(x2, g_ref[...]) if final_norm else x2


def _combine(x1, route, dest, yb, final_g, final_norm):
    n, d = x1.shape
    tm = COMBINE_TILE
    n_tiles = n // tm
    dest3 = dest.reshape(n_tiles, 1, TOP_K * tm)
    return pl.pallas_call(
        functools.partial(_combine_kernel, final_norm=final_norm),
        grid=(n_tiles,),
        in_specs=[pl.BlockSpec((1, 1, TOP_K * tm), lambda i: (i, 0, 0), memory_space=pltpu.SMEM),
                  pl.BlockSpec((1, 1, TOP_K * tm), lambda i: (jnp.minimum(i + 1, n_tiles - 1), 0, 0),
                               memory_space=pltpu.SMEM),
                  pl.BlockSpec((tm, d), lambda i: (i, 0)),
                  pl.BlockSpec((tm, ROUTE_W), lambda i: (i, 0)),
                  pl.BlockSpec((1, d), lambda i: (0, 0)),
                  pl.BlockSpec(memory_space=pl.ANY)],
        out_specs=pl.BlockSpec((tm, d), lambda i: (i, 0)),
        out_shape=jax.ShapeDtypeStruct((n, d), F32),
        scratch_shapes=[pltpu.VMEM((tm * SLABS, LANES), jnp.uint32) for _ in range(2 * TOP_K)]
                       + [pltpu.SemaphoreType.DMA((2,))],
        compiler_params=_cparams("arbitrary"),
        name="combine",
    )(dest3, dest3, x1, route, final_g, yb)


def _rwkv_params(rw_mu, rw_w0, rw_w_up, rw_a0, rw_a_up, rw_g_up, rw_k_k, rw_k_a, rw_r_k, rw_gn_w, rw_gn_b):
    row = lambda v: v.reshape(1, -1).astype(F32)
    zeros = jnp.zeros((A_LORA, RW_DIM), F32)
    return {
        "mu_r": row(rw_mu[:RW_DIM]), "mu_k": row(rw_mu[RW_DIM:2 * RW_DIM]),
        "mu_v": row(rw_mu[2 * RW_DIM:3 * RW_DIM]), "mu_l": row(rw_mu[3 * RW_DIM:]),
        "w0": row(rw_w0), "a0": row(rw_a0), "k_k": row(rw_k_k), "k_a": row(rw_k_a),
        "r_k": row(rw_r_k), "gn_w": row(rw_gn_w), "gn_b": row(rw_gn_b),
        "w_up": jnp.concatenate([rw_w_up, zeros], axis=0).astype(BF16),
        "a_up": jnp.concatenate([zeros, rw_a_up], axis=0).astype(BF16),
        "g_up": rw_g_up.astype(BF16),
    }


def _mla_params(g_qa, w_q_up, g_kva, w_kv_up):
    half = QK_ROPE // 2
    pad = MLA_SLOT - QK_NOPE - QK_ROPE
    wq = w_q_up.reshape(Q_LORA, MLA_HEADS, QK_NOPE + QK_ROPE)
    q_nope, q_r1, q_r2 = wq[..., :QK_NOPE], wq[..., QK_NOPE:QK_NOPE + half], wq[..., QK_NOPE + half:]
    zq = lambda w: jnp.zeros((Q_LORA, MLA_HEADS, w), F32)
    w_qa = jnp.concatenate([q_nope, q_r1, q_r2, zq(pad)], axis=-1).reshape(Q_LORA, -1)
    w_qb = jnp.concatenate([zq(QK_NOPE), -q_r2, q_r1, zq(pad)], axis=-1).reshape(Q_LORA, -1)
    wkv = w_kv_up.reshape(KV_LORA, MLA_HEADS, QK_NOPE + V_HEAD)
    w_k = jnp.concatenate([wkv[..., :QK_NOPE], jnp.zeros((KV_LORA, MLA_HEADS, MLA_SLOT - QK_NOPE), F32)],
                          axis=-1).reshape(KV_LORA, -1)
    w_v = wkv[..., QK_NOPE:].reshape(KV_LORA, -1)
    eye = jnp.eye(half, dtype=F32)
    z = jnp.zeros((half, half), F32)
    zl = jnp.zeros((QK_ROPE, QK_NOPE), F32)
    zr = jnp.zeros((QK_ROPE, pad), F32)
    p_a = jnp.concatenate([zl, jnp.concatenate([eye, z], 0), jnp.concatenate([z, eye], 0), zr], axis=1)
    p_b = jnp.concatenate([zl, jnp.concatenate([z, -eye], 0), jnp.concatenate([eye, z], 0), zr], axis=1)
    place_half = jnp.concatenate([jnp.zeros((half, QK_NOPE), F32), eye, eye, jnp.zeros((half, pad), F32)], axis=1)
    zh = jnp.zeros_like(place_half)
    place = jnp.concatenate([jnp.concatenate([place_half, zh], 1), jnp.concatenate([zh, place_half], 1)], 0)
    one = jnp.concatenate([jnp.ones((1, QK_NOPE), F32), jnp.zeros((1, MLA_SLOT - QK_NOPE), F32)], axis=1)
    return {"g_qa": g_qa.reshape(1, -1), "g_kva": g_kva.reshape(1, -1),
            "w_qa": w_qa.astype(BF16), "w_qb": w_qb.astype(BF16), "w_k": w_k.astype(BF16),
            "w_v": w_v.astype(BF16), "p_a": p_a.astype(BF16), "p_b": p_b.astype(BF16),
            "place": place.astype(BF16), "one": one}


def _rope_cos_sin(positions):
    inv_freq = ROPE_THETA ** (-jnp.arange(0, QK_ROPE, 2, dtype=F32) / QK_ROPE)
    ang = positions.astype(F32).reshape(-1, 1) * inv_freq
    return jnp.concatenate([jnp.cos(ang), jnp.sin(ang)], axis=1)


def _block_layout(hist, n_assign):
    tile_counts = hist[:, 0, N_GROUPS:N_GROUPS + N_EXPERTS].astype(jnp.int32)
    counts = jnp.sum(tile_counts, axis=0)
    padded = (counts + EXPERT_BLOCK - 1) // EXPERT_BLOCK * EXPERT_BLOCK
    pad_end = jnp.cumsum(padded)
    pad_start = pad_end - padded
    tile_base = jnp.cumsum(tile_counts, axis=0) - tile_counts + pad_start[None, :]
    base = jnp.pad(tile_base.astype(F32), ((0, 0), (0, ROUTE_W - N_EXPERTS)))[:, None, :]
    n_blocks = -(-n_assign // EXPERT_BLOCK) + N_EXPERTS
    blk_row = jnp.arange(n_blocks, dtype=jnp.int32) * EXPERT_BLOCK
    blk_expert = jnp.minimum(jnp.sum((pad_end[None, :] <= blk_row[:, None]).astype(jnp.int32), axis=1),
                             N_EXPERTS - 1)
    n_used = (pad_end[-1] // EXPERT_BLOCK).astype(jnp.int32).reshape(1)
    return base, blk_expert, n_used, n_blocks, pad_end.astype(jnp.int32)


def kernel(x, positions, mix_norm_g, w_in, rw_mu, rw_w0, rw_w_up, rw_a0, rw_a_up, rw_g_up, rw_k_k, rw_k_a, rw_r_k, rw_gn_w, rw_gn_b, mla_g_qa, mla_w_q_up, mla_g_kva, mla_w_kv_up, w_branch_rw, w_branch_mla, w_out, ffn_norm_g, moe_w_group, moe_b_group, moe_w_router, moe_b_router, moe_w_gu, moe_w_down, final_norm_g):
    batch, seq, d = x.shape
    assert d == 2 * SLABS * LANES
    n = batch * seq
    depth = w_in.shape[0]
    rw_cols = 3 * RW_DIM + W_LORA + A_LORA + G_LORA
    mla_cols = Q_LORA + KV_LORA + QK_ROPE
    cos_sin = _rope_cos_sin(positions)
    x2 = x.reshape(n, d)

    for l in range(depth):
        wl = w_in[l].astype(BF16)
        c_rw, c_q, c_kvr, gates = _in_proj(
            x2, mix_norm_g[l].reshape(1, d), wl[:, :rw_cols], wl[:, rw_cols:rw_cols + Q_LORA],
            wl[:, rw_cols + Q_LORA:rw_cols + mla_cols], wl[:, rw_cols + mla_cols:], tm=ROW_TILE)

        rp = _rwkv_params(rw_mu[l], rw_w0[l], rw_w_up[l], rw_a0[l], rw_a_up[l], rw_g_up[l], rw_k_k[l],
                          rw_k_a[l], rw_r_k[l], rw_gn_w[l], rw_gn_b[l])
        y_rw = _rwkv(c_rw.reshape(batch, seq, rw_cols), rp, batch, seq).reshape(n, RW_DIM)

        mp = _mla_params(mla_g_qa[l], mla_w_q_up[l], mla_g_kva[l], mla_w_kv_up[l])
        q_t, k, v_t = _mla_prep(c_q, c_kvr, cos_sin, mp, ROW_TILE, batch, seq)
        y_mla = _mla_attn(q_t, k.reshape(batch, seq, -1), v_t, batch, seq).reshape(n, MLA_HEADS * V_HEAD)

        w_route = jnp.concatenate(
            [moe_w_group[l], moe_w_router[l], jnp.zeros((d, ROUTE_W - N_GROUPS - N_EXPERTS), F32)], axis=1)
        b_route = jnp.concatenate(
            [moe_b_group[l], moe_b_router[l], jnp.zeros((ROUTE_W - N_GROUPS - N_EXPERTS,), F32)]).reshape(1, -1)
        wr_hi = w_route.astype(BF16)
        wr_lo = (w_route - wr_hi.astype(F32)).astype(BF16)
        mparams = {"w_br": w_branch_rw[l].astype(BF16), "w_bm": w_branch_mla[l].astype(BF16),
                   "w_out": w_out[l].astype(BF16), "ffn_g": ffn_norm_g[l].reshape(1, d),
                   "w_route": jnp.concatenate([wr_hi, wr_lo], axis=1), "b_route": b_route}
        x1, h2p, route, hist = _merge(x2, y_rw, y_mla, gates, mparams, tm=ROW_TILE)

        base, blk_expert, n_used, n_blocks, pad_end = _block_layout(hist, n * TOP_K)
        dest = _plan(route, base, tm=ROW_TILE)[:, :TOP_K]
        xs = _dispatch(h2p, dest, pad_end, n_blocks * EXPERT_BLOCK)
        yb = _experts(xs, blk_expert, n_used, moe_w_gu[l].astype(BF16), moe_w_down[l].astype(BF16))
        x2 = _combine(x1, route, dest, yb, final_norm_g.reshape(1, d), final_norm=(l == depth - 1))

    return x2.reshape(batch, seq, d)
```

```python
import functools
import math

import jax
import jax.numpy as jnp
from jax import lax
from jax.experimental import pallas as pl
from jax.experimental.pallas import tpu as pltpu

F32 = jnp.float32
BF16 = jnp.bfloat16

RW_HEADS = 8
RW_HEAD_DIM = 64
RW_DIM = RW_HEADS * RW_HEAD_DIM
W_LORA = 64
A_LORA = 64
G_LORA = 128
GN_EPS = 64e-5
MLA_HEADS = 8
QK_NOPE = 64
QK_ROPE = 32
V_HEAD = 64
Q_LORA = 384
KV_LORA = 256
ROPE_THETA = 10000.0
N_GROUPS = 4
EXPERTS_PER_GROUP = 8
N_EXPERTS = N_GROUPS * EXPERTS_PER_GROUP
TOP_K = 2
D_EXPERT = 256
EXPERT_BLOCK = 512
NORM_EPS = 1e-6

LANES = 128
HEAD_PAIR = 2 * RW_HEAD_DIM
VMEM_LIMIT = 48 * 1024 * 1024
ROW_TILE = 512


def _cparams(*sem):
    return pltpu.CompilerParams(dimension_semantics=sem, vmem_limit_bytes=VMEM_LIMIT)


def _mm(a, b, dims=((1,), (0,))):
    return lax.dot_general(a.astype(BF16), b.astype(BF16), (dims, ((), ())), preferred_element_type=F32)


def _mm_sel(sel_bf16, x, dims=((1,), (0,))):
    dn = (dims, ((), ()))
    hi = x.astype(BF16)
    lo = (x - hi.astype(F32)).astype(BF16)
    return (lax.dot_general(sel_bf16, hi, dn, preferred_element_type=F32)
            + lax.dot_general(sel_bf16, lo, dn, preferred_element_type=F32))


def _seg_sum(x, seg_bf16):
    return jnp.dot(x.astype(BF16), seg_bf16, preferred_element_type=F32)


def _rms(x, g):
    return x * lax.rsqrt(jnp.mean(x * x, axis=-1, keepdims=True) + NORM_EPS) * g


def _pack_rows(x):
    half = x.shape[1] // 2
    bits = lambda v: lax.bitcast_convert_type(v.astype(BF16).astype(F32), jnp.uint32)
    return bits(x[:, :half]) | (bits(x[:, half:]) >> 16)


def _unpack_rows(p):
    hi = lax.bitcast_convert_type(p & jnp.uint32(0xFFFF0000), F32)
    lo = lax.bitcast_convert_type(p << 16, F32)
    return jnp.concatenate([hi, lo], axis=1)


SLABS = 4


def _slab_rows(ref, r):
    return ref.at[pl.ds(pl.multiple_of(r * SLABS, SLABS), SLABS), :]


def _slab_load(ref):
    rows = ref.shape[0] // SLABS
    return jnp.concatenate([ref[pl.ds(j, rows, stride=SLABS), :] for j in range(SLABS)], axis=1)


def _slab_store(ref, x):
    rows = ref.shape[0] // SLABS
    for j in range(SLABS):
        ref[pl.ds(j, rows, stride=SLABS), :] = x[:, j * LANES:(j + 1) * LANES]


def _in_proj_kernel(x_ref, g_ref, wrw_ref, wq_ref, wkvr_ref, wg_ref,
                    crw_ref, cq_ref, ckvr_ref, gate_ref):
    hb = _rms(x_ref[...], g_ref[...]).astype(BF16)
    crw_ref[...] = jnp.dot(hb, wrw_ref[...], preferred_element_type=F32)
    cq_ref[...] = jnp.dot(hb, wq_ref[...], preferred_element_type=F32)
    ckvr_ref[...] = jnp.dot(hb, wkvr_ref[...], preferred_element_type=F32)
    gate_ref[...] = jax.nn.sigmoid(jnp.dot(hb, wg_ref[...], preferred_element_type=F32)).astype(BF16)


def _in_proj(x2, g, w_rw, w_q, w_kvr, w_gate, tm):
    n, d = x2.shape
    full = lambda w: pl.BlockSpec(w.shape, lambda i: (0, 0))
    row = lambda c: pl.BlockSpec((tm, c), lambda i: (i, 0))
    return pl.pallas_call(
        _in_proj_kernel,
        grid=(n // tm,),
        in_specs=[row(d), full(g), full(w_rw), full(w_q), full(w_kvr), full(w_gate)],
        out_specs=[row(w_rw.shape[1]), row(w_q.shape[1]), row(w_kvr.shape[1]), row(w_gate.shape[1])],
        out_shape=[jax.ShapeDtypeStruct((n, w_rw.shape[1]), F32),
                   jax.ShapeDtypeStruct((n, w_q.shape[1]), F32),
                   jax.ShapeDtypeStruct((n, w_kvr.shape[1]), F32),
                   jax.ShapeDtypeStruct((n, w_gate.shape[1]), BF16)],
        compiler_params=_cparams("parallel"),
        name="in_proj",
    )(x2, g, w_rw, w_q, w_kvr, w_gate)


RW_CHUNK = 64
RW_TILE = 1024
RW_GROUP = 16

def _token_shift(cur, halo_ref, first):
    prev_row = jnp.where(first, 0.0, halo_ref[0, 7:8, :])
    rolled = pltpu.roll(cur, 1, 0)
    row = lax.broadcasted_iota(jnp.int32, cur.shape, 0)
    return jnp.where(row == 0, prev_row, rolled)


def _rwkv_kernel(r_ref, k_ref, v_ref, l_ref, hr_ref, hk_ref, hv_ref, hl_ref,
                 mur_ref, muk_ref, muv_ref, mul_ref, w0_ref, a0_ref, kk_ref, ka_ref, rk_ref,
                 gnw_ref, gnb_ref, wup_ref, aup_ref, gup_ref, y_ref, st_ref):
    i = pl.program_id(2)
    first = i == 0

    @pl.when(first)
    def _():
        st_ref[...] = jnp.zeros_like(st_ref)

    def mixed(c_ref, h_ref, mu_ref):
        cur = c_ref[0]
        return cur + (_token_shift(cur, h_ref, first) - cur) * mu_ref[...]

    zr = mixed(r_ref, hr_ref, mur_ref)
    zk = mixed(k_ref, hk_ref, muk_ref)
    zv = mixed(v_ref, hv_ref, muv_ref)
    zl = mixed(l_ref, hl_ref, mul_ref)
    z_wa = zl[:, :LANES]
    z_g = zl[:, LANES:]

    lane = lax.broadcasted_iota(jnp.int32, (LANES, LANES), 1)
    sub = lax.broadcasted_iota(jnp.int32, (LANES, LANES), 0)
    same_head = (lane // RW_HEAD_DIM) == (sub // RW_HEAD_DIM)
    seg = jnp.where(same_head, 1.0, 0.0).astype(BF16)

    w = w0_ref[...] + _mm(jnp.tanh(z_wa), wup_ref[...])
    u = -w
    softplus = jnp.maximum(u, 0.0) + jnp.log(1.0 + jnp.exp(-jnp.abs(u)))
    log_decay = -jnp.exp(-softplus - 0.5)
    a = jax.nn.sigmoid(a0_ref[...] + _mm(z_wa, aup_ref[...]))
    g = _mm(jax.nn.sigmoid(z_g), gup_ref[...])

    kk = zk * kk_ref[...]
    kk = kk / jnp.maximum(jnp.sqrt(_seg_sum(kk * kk, seg)), 1e-12)
    k2 = zk * (1.0 + (a - 1.0) * ka_ref[...])
    bonus = _seg_sum(zr * k2 * rk_ref[...], seg) * zv
    kka = kk * a

    c = RW_CHUNK
    crow = lax.broadcasted_iota(jnp.int32, (c, c), 0)
    ccol = lax.broadcasted_iota(jnp.int32, (c, c), 1)
    cum_sel = jnp.where(crow >= ccol, 1.0, 0.0).astype(BF16)
    tril_incl = sub >= lane
    tril_strict = sub > lane
    eye_l = jnp.where(lane == sub, 1.0, 0.0).astype(F32)
    lo_half = lax.broadcasted_iota(jnp.int32, (c, LANES), 1) < RW_HEAD_DIM
    nt = ((1,), (1,))
    tn = ((0,), (0,))
    zeros_blk = jnp.zeros((2 * c, LANES), BF16)
    zeros_half = jnp.zeros((c, LANES), BF16)

    def stack(t):
        tb = t.astype(BF16)
        return jnp.concatenate([jnp.where(lo_half, tb, zeros_half), jnp.where(lo_half, zeros_half, tb)], axis=0)

    tril_incl2 = jnp.concatenate([tril_incl, tril_incl], axis=1)

    def chunk_group(ids):
        chunks = range(len(ids))
        x_a, x_b, x_k, x_r, x_v, x_bh, x_kh, w_tot, r_dec = [], [], [], [], [], [], [], [], []
        for ci in ids:
            sl = slice(ci * c, (ci + 1) * c)
            ld = log_decay[sl]
            cum = _mm_sel(cum_sel, ld)
            tot = cum[c - 1:c, :]
            e_neg = jnp.exp(-cum)
            e_rest = jnp.exp(tot - cum)
            x_a.append(stack(-kk[sl] * jnp.exp(cum - ld)))
            x_b.append(stack(kka[sl] * e_neg))
            x_k.append(stack(k2[sl] * e_neg))
            r_dec.append(zr[sl] * jnp.exp(cum))
            x_r.append(stack(r_dec[-1]))
            x_v.append(stack(zv[sl]))
            x_bh.append(stack(kka[sl] * e_rest))
            x_kh.append(stack(k2[sl] * e_rest))
            w_tot.append(jnp.exp(tot))

        inter = [_mm(jnp.concatenate([x_a[i], x_r[i]], axis=0),
                     jnp.concatenate([x_b[i], x_k[i]], axis=0), nt) for i in chunks]
        inter = [m.astype(BF16) for m in inter]
        zeros_sq_b = jnp.zeros((LANES, LANES), BF16)
        a_ab = [jnp.where(tril_strict, m[:2 * c, :2 * c], zeros_sq_b) for m in inter]
        a_ak = [jnp.where(tril_strict, m[:2 * c, 2 * c:], zeros_sq_b) for m in inter]
        a_r = [jnp.where(tril_incl2, m[2 * c:], jnp.concatenate([zeros_sq_b, zeros_sq_b], axis=1))
               for m in inter]
        w_ak = [_mm(a_ak[i], x_v[i]).astype(BF16) for i in chunks]

        t_inv = [eye_l + m.astype(F32) for m in a_ab]
        pw = a_ab
        for _ in range(int(math.log2(c)) - 1):
            pw = [_mm(m, m).astype(BF16) for m in pw]
            t_inv = [t_inv[i] + _mm(t_inv[i], pw[i]) for i in chunks]

        solved = [_mm(t_inv[i], jnp.concatenate([x_a[i], w_ak[i]], axis=1)).astype(BF16)
                  for i in chunks]
        rhs = [jnp.concatenate([solved[i], jnp.concatenate([zeros_blk, x_v[i]], axis=1)], axis=0)
               for i in chunks]
        out = [_mm(a_r[i], rhs[i]) for i in chunks]
        carry = [_mm(jnp.concatenate([x_bh[i], x_kh[i]], axis=0), rhs[i], tn) for i in chunks]
        q_hat, y_loc = [], []
        for i in chunks:
            q_hat.append(r_dec[i] + out[i][:c, :LANES] + out[i][c:, :LANES])
            y_loc.append(out[i][:c, LANES:] + out[i][c:, LANES:])
        trans = [jnp.concatenate([eye_l * w_tot[i] + carry[i][:, :LANES], carry[i][:, LANES:]], axis=1)
                 for i in chunks]
        return q_hat, y_loc, trans

    n_chunks = RW_TILE // c
    groups = [chunk_group(range(g, g + RW_GROUP)) for g in range(0, n_chunks, RW_GROUP)]
    q_hat, y_loc, trans = (sum((g[j] for g in groups), []) for j in range(3))
    chunks = range(n_chunks)

    zeros_sq = jnp.zeros((LANES, LANES), F32)

    def compose(later, earlier):
        return _mm(later[:, :LANES], earlier) + jnp.concatenate([zeros_sq, later[:, LANES:]], axis=1)

    scan = list(trans)
    dist = 1
    while dist < len(scan):
        scan = [scan[i] if i < dist else compose(scan[i], scan[i - dist]) for i in range(len(scan))]
        dist *= 2
    prefix = [None] + scan
    q_pre = [None] + [_mm(q_hat[i], prefix[i]) for i in chunks[1:]]
    q_m = jnp.concatenate([q_hat[0]] + [q_pre[i][:, :LANES] for i in chunks[1:]], axis=0)
    y_off = jnp.concatenate([y_loc[0]] + [y_loc[i] + q_pre[i][:, LANES:] for i in chunks[1:]], axis=0)
    state = st_ref[...]
    y = _mm(q_m, state) + y_off
    st_ref[...] = _mm(prefix[-1][:, :LANES], state) + prefix[-1][:, LANES:]

    inv_n = 1.0 / RW_HEAD_DIM
    mean = _seg_sum(y, seg) * inv_n
    d = y - mean
    var = _seg_sum(d * d, seg) * inv_n
    yn = d * lax.rsqrt(var + GN_EPS) * gnw_ref[...] + gnb_ref[...]
    y_ref[0] = ((yn + bonus) * g).astype(y_ref.dtype)


def _rwkv(c_rw, p, batch, seq):
    ts = RW_TILE
    n_pairs = RW_DIM // HEAD_PAIR
    lora_blk = (3 * RW_DIM) // (2 * LANES)
    halo = ts // 8

    def col(off):
        return pl.BlockSpec((1, ts, LANES), lambda b, pp, i, off=off: (b, i, off + pp))

    def col_halo(off):
        return pl.BlockSpec((1, 8, LANES),
                            lambda b, pp, i, off=off: (b, jnp.maximum(i * halo - 1, 0), off + pp))

    vec = pl.BlockSpec((1, LANES), lambda b, pp, i: (0, pp))
    lora_w = pl.BlockSpec((LANES, LANES), lambda b, pp, i: (0, pp))
    in_specs = [
        col(0), col(n_pairs), col(2 * n_pairs),
        pl.BlockSpec((1, ts, 2 * LANES), lambda b, pp, i: (b, i, lora_blk)),
        col_halo(0), col_halo(n_pairs), col_halo(2 * n_pairs),
        pl.BlockSpec((1, 8, 2 * LANES), lambda b, pp, i: (b, jnp.maximum(i * halo - 1, 0), lora_blk)),
        vec, vec, vec, pl.BlockSpec((1, 2 * LANES), lambda b, pp, i: (0, 0)),
        vec, vec, vec, vec, vec, vec, vec, lora_w, lora_w, lora_w,
    ]
    return pl.pallas_call(
        _rwkv_kernel,
        grid=(batch, n_pairs, seq // ts),
        in_specs=in_specs,
        out_specs=pl.BlockSpec((1, ts, LANES), lambda b, pp, i: (b, i, pp)),
        out_shape=jax.ShapeDtypeStruct((batch, seq, RW_DIM), BF16),
        scratch_shapes=[pltpu.VMEM((LANES, LANES), F32)],
        compiler_params=_cparams("parallel", "parallel", "arbitrary"),
        name="rwkv",
    )(c_rw, c_rw, c_rw, c_rw, c_rw, c_rw, c_rw, c_rw,
      p["mu_r"], p["mu_k"], p["mu_v"], p["mu_l"], p["w0"], p["a0"], p["k_k"], p["k_a"], p["r_k"],
      p["gn_w"], p["gn_b"], p["w_up"], p["a_up"], p["g_up"])


MLA_SLOT = 128


def _mla_prep_kernel(cq_ref, ckvr_ref, cs_ref, gq_ref, gkv_ref,
                     wqa_ref, wqb_ref, wk_ref, wv_ref, pa_ref, pb_ref, place_ref, one_ref,
                     qt_ref, k_ref, vt_ref):
    cs = cs_ref[...]
    cs_hi = cs.astype(BF16)
    cs_lo = (cs - cs_hi.astype(F32)).astype(BF16)
    tables = (jnp.dot(cs_hi, place_ref[...], preferred_element_type=F32)
              + jnp.dot(cs_lo, place_ref[...], preferred_element_type=F32))
    cos = tables[:, :MLA_SLOT] + one_ref[...]
    sin = tables[:, MLA_SLOT:]
    zq = _rms(cq_ref[...], gq_ref[...]).astype(BF16)
    qa = jnp.dot(zq, wqa_ref[...], preferred_element_type=F32)
    qb = jnp.dot(zq, wqb_ref[...], preferred_element_type=F32)
    ckvr = ckvr_ref[...]
    zkv = _rms(ckvr[:, :KV_LORA], gkv_ref[...]).astype(BF16)
    kn = jnp.dot(zkv, wk_ref[...], preferred_element_type=F32)
    v = jnp.dot(zkv, wv_ref[...], preferred_element_type=F32)
    for blk in range(v.shape[1] // LANES):
        vt_ref[0, blk * LANES:(blk + 1) * LANES, :] = v[:, blk * LANES:(blk + 1) * LANES].T.astype(BF16)
    kr = ckvr[:, KV_LORA:].astype(BF16)
    k_rope = (jnp.dot(kr, pa_ref[...], preferred_element_type=F32) * cos
              + jnp.dot(kr, pb_ref[...], preferred_element_type=F32) * sin)
    scale = math.log2(math.e) / math.sqrt(QK_NOPE + QK_ROPE)
    for h in range(MLA_HEADS):
        sl = slice(h * MLA_SLOT, (h + 1) * MLA_SLOT)
        qt_ref[0, sl, :] = ((qa[:, sl] * cos + qb[:, sl] * sin) * scale).T.astype(BF16)
        k_ref[:, sl] = (kn[:, sl] + k_rope).astype(BF16)


def _mla_prep(c_q, c_kvr, cos_sin, p, tm, batch, seq):
    n = c_q.shape[0]
    full = lambda w: pl.BlockSpec(w.shape, lambda i: (0, 0))
    row = lambda c: pl.BlockSpec((tm, c), lambda i: (i, 0))
    per_seq = seq // tm
    col = lambda r: pl.BlockSpec((1, r, tm), lambda i: (i // per_seq, 0, i % per_seq))
    ws = [p["g_qa"], p["g_kva"], p["w_qa"], p["w_qb"], p["w_k"], p["w_v"], p["p_a"], p["p_b"],
          p["place"], p["one"]]
    hq = MLA_HEADS * MLA_SLOT
    hv = MLA_HEADS * V_HEAD
    return pl.pallas_call(
        _mla_prep_kernel,
        grid=(n // tm,),
        in_specs=[row(c_q.shape[1]), row(c_kvr.shape[1]), row(cos_sin.shape[1])] + [full(w) for w in ws],
        out_specs=[col(hq), row(hq), col(hv)],
        out_shape=[jax.ShapeDtypeStruct((batch, hq, seq), BF16), jax.ShapeDtypeStruct((n, hq), BF16),
                   jax.ShapeDtypeStruct((batch, hv, seq), BF16)],
        compiler_params=_cparams("parallel"),
        name="mla_prep",
    )(c_q, c_kvr, cos_sin, *ws)


ATT_TILE = 512
ATT_HEADS = 4


def _attn_kernel(qt_ref, k_ref, vt_ref, o_ref):
    qi = pl.program_id(2)
    t = ATT_TILE
    heads = range(ATT_HEADS)
    den_row = (V_HEAD, 0)

    def augment(h, vt):
        r = lax.broadcasted_iota(jnp.int32, vt.shape, 0)
        own = (r < V_HEAD) if h % 2 == 0 else (r >= V_HEAD)
        return jnp.where(own, vt, jnp.where(r == den_row[h % 2], 1.0, 0.0).astype(BF16))

    def block(j, carry, mask):
        keys = pl.ds(pl.multiple_of(j * t, t), t)
        sts = []
        for h in heads:
            qt = qt_ref[0, h * MLA_SLOT:(h + 1) * MLA_SLOT, :]
            kb = k_ref[0, keys, h * MLA_SLOT:(h + 1) * MLA_SLOT]
            st = jnp.dot(kb, qt, preferred_element_type=F32)
            sts.append(st if mask is None else jnp.where(mask, st, -jnp.inf))
        m_new = [jnp.maximum(carry[h][0], jnp.max(sts[h], axis=0, keepdims=True)) for h in heads]
        pts = [jnp.exp2((sts[h] - m_new[h]).astype(BF16)) for h in heads]
        out = []
        for h in heads:
            m, acc = carry[h]
            vt = vt_ref[0, (h // 2) * 2 * V_HEAD:(h // 2 + 1) * 2 * V_HEAD, keys]
            acc = acc * jnp.exp2(m - m_new[h]) + jnp.dot(augment(h, vt), pts[h], preferred_element_type=F32)
            out.append((m_new[h], acc))
        return tuple(out)

    init1 = (jnp.full((1, t), -jnp.inf, F32), jnp.zeros((2 * V_HEAD, t), F32))
    carry = lax.fori_loop(0, qi, lambda j, c: block(j, c, None), tuple(init1 for _ in heads))

    causal = lax.broadcasted_iota(jnp.int32, (t, t), 0) <= lax.broadcasted_iota(jnp.int32, (t, t), 1)
    accs = [mc[1] for mc in block(qi, carry, causal)]
    lo_rows = lax.broadcasted_iota(jnp.int32, (2 * V_HEAD, t), 0) < V_HEAD
    for p in range(ATT_HEADS // 2):
        acc0, acc1 = accs[2 * p], accs[2 * p + 1]
        den0 = acc0[den_row[0]:den_row[0] + 1, :]
        den1 = acc1[den_row[1]:den_row[1] + 1, :]
        out_t = jnp.where(lo_rows, acc0 / den0, acc1 / den1)
        o_ref[0, :, p * 2 * V_HEAD:(p + 1) * 2 * V_HEAD] = out_t.T.astype(o_ref.dtype)


def _mla_attn(qt, k, vt, batch, seq):
    t = ATT_TILE
    g = ATT_HEADS
    return pl.pallas_call(
        _attn_kernel,
        grid=(batch, MLA_HEADS // g, seq // t),
        in_specs=[pl.BlockSpec((1, g * MLA_SLOT, t), lambda b, hp, i: (b, hp, i)),
                  pl.BlockSpec((1, seq, g * MLA_SLOT), lambda b, hp, i: (b, 0, hp)),
                  pl.BlockSpec((1, g * V_HEAD, seq), lambda b, hp, i: (b, hp, 0))],
        out_specs=pl.BlockSpec((1, t, g * V_HEAD), lambda b, hp, i: (b, i, hp)),
        out_shape=jax.ShapeDtypeStruct((batch, seq, MLA_HEADS * V_HEAD), BF16),
        compiler_params=_cparams("parallel", "parallel", "arbitrary"),
        name="mla_attn",
    )(qt, k, vt)


ROUTE_W = 128
ROUTE_FINE0 = 8


def _merge_kernel(x_ref, yrw_ref, ymla_ref, gate_ref, wbr_ref, wbm_ref, wo_ref, fg_ref,
                  wr_ref, br_ref, x1_ref, h2p_ref, route_ref, route_t_ref, hist_ref):
    d = x_ref.shape[1]
    a = jnp.dot(yrw_ref[...], wbr_ref[...], preferred_element_type=F32)
    b = jnp.dot(ymla_ref[...], wbm_ref[...], preferred_element_type=F32)
    merged = gate_ref[:, :d].astype(F32) * a + gate_ref[:, d:].astype(F32) * b
    x1 = x_ref[...] + jnp.dot(merged.astype(BF16), wo_ref[...], preferred_element_type=F32)
    x1_ref[...] = x1
    h2 = _rms(x1, fg_ref[...])
    _slab_store(h2p_ref, _pack_rows(h2))

    h_hi = h2.astype(BF16)
    h_lo = (h2 - h_hi.astype(F32)).astype(BF16)
    both = jnp.dot(h_hi, wr_ref[...], preferred_element_type=F32)
    logits = (both[:, :ROUTE_W] + both[:, ROUTE_W:]
              + jnp.dot(h_lo, wr_ref[:, :ROUTE_W], preferred_element_type=F32)) + br_ref[...]

    tm = logits.shape[0]
    lt = logits.T
    sub = lax.broadcasted_iota(jnp.int32, (EXPERTS_PER_GROUP, tm), 0)
    big = jnp.int32(EXPERTS_PER_GROUP)
    neg = -jnp.inf

    def first_argmax(vals, vmax):
        return jnp.min(jnp.where(vals == vmax, sub, big), axis=0, keepdims=True)

    grp = jnp.where(sub < N_GROUPS, lt[:EXPERTS_PER_GROUP], neg)
    g_max = jnp.max(grp, axis=0, keepdims=True)
    g_den = jnp.sum(jnp.exp(grp - g_max), axis=0, keepdims=True)
    g_sel = first_argmax(grp, g_max)
    gate_g = 1.0 / g_den
    fine = lt[ROUTE_FINE0:ROUTE_FINE0 + EXPERTS_PER_GROUP]
    for g in range(1, N_GROUPS):
        lo = ROUTE_FINE0 + g * EXPERTS_PER_GROUP
        fine = jnp.where(g_sel == g, lt[lo:lo + EXPERTS_PER_GROUP], fine)
    v1 = jnp.max(fine, axis=0, keepdims=True)
    i1 = first_argmax(fine, v1)
    fine2 = jnp.where(sub == i1, neg, fine)
    v2 = jnp.max(fine2, axis=0, keepdims=True)
    i2 = first_argmax(fine2, v2)
    e2 = jnp.exp(v2 - v1)
    den = 1.0 + e2
    w1 = gate_g / den
    w2 = gate_g * e2 / den
    base_e = g_sel * EXPERTS_PER_GROUP
    route_t = jnp.where(sub == 0, (base_e + i1).astype(F32),
                        jnp.where(sub == 1, (base_e + i2).astype(F32),
                                  jnp.where(sub == 2, w1, jnp.where(sub == 3, w2, 0.0))))
    pad = jnp.zeros((ROUTE_W - EXPERTS_PER_GROUP, tm), F32)
    route_t_ref[...] = route_t
    route_ref[...] = jnp.concatenate([route_t, pad], axis=0).T
    picked = (sub == i1) | (sub == i2)
    for g in range(N_GROUPS):
        cnt = jnp.sum(jnp.where(picked & (g_sel == g), 1.0, 0.0), axis=1, keepdims=True)
        hist_ref[0, g * EXPERTS_PER_GROUP:(g + 1) * EXPERTS_PER_GROUP, :] = jnp.broadcast_to(
            cnt, (EXPERTS_PER_GROUP, ROUTE_W))


def _merge(x2, y_rw, y_mla, gates, p, tm):
    n, d = x2.shape
    full = lambda w: pl.BlockSpec(w.shape, lambda i: (0, 0))
    row = lambda c: pl.BlockSpec((tm, c), lambda i: (i, 0))
    ws = [p["w_br"], p["w_bm"], p["w_out"], p["ffn_g"], p["w_route"], p["b_route"]]
    return pl.pallas_call(
        _merge_kernel,
        grid=(n // tm,),
        in_specs=[row(d), row(y_rw.shape[1]), row(y_mla.shape[1]), row(2 * d)] + [full(w) for w in ws],
        out_specs=[row(d), pl.BlockSpec((tm * SLABS, LANES), lambda i: (i, 0)), row(ROUTE_W),
                   pl.BlockSpec((EXPERTS_PER_GROUP, tm), lambda i: (0, i)),
                   pl.BlockSpec((1, N_EXPERTS, ROUTE_W), lambda i: (i, 0, 0))],
        out_shape=[jax.ShapeDtypeStruct((n, d), F32), jax.ShapeDtypeStruct((n * SLABS, LANES), jnp.uint32),
                   jax.ShapeDtypeStruct((n, ROUTE_W), F32),
                   jax.ShapeDtypeStruct((EXPERTS_PER_GROUP, n), F32),
                   jax.ShapeDtypeStruct((n // tm, N_EXPERTS, ROUTE_W), F32)],
        compiler_params=_cparams("parallel"),
        name="merge_route",
    )(x2, y_rw, y_mla, gates, *ws)


def _plan_kernel(route_t_ref, base_ref, dest_ref):
    tm = route_t_ref.shape[1]
    rt = route_t_ref[...]
    expert = lax.broadcasted_iota(jnp.int32, (N_EXPERTS, tm), 0).astype(F32)
    pick = [expert == rt[k:k + 1, :] for k in range(TOP_K)]
    both = jnp.where(pick[0] | pick[1], 1.0, 0.0).astype(BF16)
    r = lax.broadcasted_iota(jnp.int32, (tm, tm), 0)
    c = lax.broadcasted_iota(jnp.int32, (tm, tm), 1)
    earlier = jnp.where(r < c, 1.0, 0.0).astype(BF16)
    offs = jnp.dot(both, earlier, preferred_element_type=F32) + base_ref[0][:, :1]
    rows = [jnp.sum(jnp.where(pk, offs, 0.0), axis=0, keepdims=True) for pk in pick]
    sub = lax.broadcasted_iota(jnp.int32, dest_ref.shape, 0)
    dest_ref[...] = jnp.where(sub == 0, rows[0], jnp.where(sub == 1, rows[1], 0.0)).astype(jnp.int32)


def _plan(route_t, base, tm):
    n = route_t.shape[1]
    return pl.pallas_call(
        _plan_kernel,
        grid=(n // tm,),
        in_specs=[pl.BlockSpec((EXPERTS_PER_GROUP, tm), lambda i: (0, i)),
                  pl.BlockSpec((1, N_EXPERTS, ROUTE_W), lambda i: (i, 0, 0))],
        out_specs=pl.BlockSpec((EXPERTS_PER_GROUP, tm), lambda i: (0, i)),
        out_shape=jax.ShapeDtypeStruct((EXPERTS_PER_GROUP, n), jnp.int32),
        compiler_params=_cparams("parallel"),
        name="route_plan",
    )(route_t, base)


def _dest_tiles(dest_t, tm):
    n = dest_t.shape[1]
    return dest_t[:TOP_K].reshape(TOP_K, n // tm, tm).transpose(1, 0, 2).reshape(n // tm, 1, TOP_K * tm)


DISPATCH_TILE = 1024
ROW_DMA_UNROLL = 8


def _dispatch_kernel(pad_end_ref, dest_ref, h_ref, xs_ref, zbuf, sem, zsem):
    tm = h_ref.shape[0] // SLABS

    @pl.when(pl.program_id(0) == 0)
    def _():
        zbuf[...] = jnp.zeros_like(zbuf)

        def tail(e):
            first = pl.multiple_of((pad_end_ref[e] - EXPERT_BLOCK) * SLABS, EXPERT_BLOCK * SLABS)
            return pltpu.make_async_copy(zbuf, xs_ref.at[pl.ds(first, EXPERT_BLOCK * SLABS), :], zsem)

        def region_rows(e):
            return pad_end_ref[e] - (pad_end_ref[e - 1] if e else 0)

        n_blocks = xs_ref.shape[0] // (EXPERT_BLOCK * SLABS)
        used_rows = pad_end_ref[N_EXPERTS - 1]

        def spare(b):
            return pltpu.make_async_copy(zbuf, xs_ref.at[pl.ds(b * EXPERT_BLOCK * SLABS, EXPERT_BLOCK * SLABS), :],
                                         zsem)

        spare_blocks = range(n_blocks - N_EXPERTS, n_blocks)
        for e in range(N_EXPERTS):
            @pl.when(region_rows(e) > 0)
            def _(e=e):
                tail(e).start()
        for b in spare_blocks:
            @pl.when(b * EXPERT_BLOCK >= used_rows)
            def _(b=b):
                spare(b).start()
        for e in range(N_EXPERTS):
            @pl.when(region_rows(e) > 0)
            def _(e=e):
                tail(e).wait()
        for b in spare_blocks:
            @pl.when(b * EXPERT_BLOCK >= used_rows)
            def _(b=b):
                spare(b).wait()

    def start(t, _):
        for k in range(TOP_K):
            pltpu.make_async_copy(_slab_rows(h_ref, t), _slab_rows(xs_ref, dest_ref[0, 0, k * tm + t]),
                                  sem).start(priority=k % 2)
        return 0

    lax.fori_loop(0, tm, start, 0, unroll=ROW_DMA_UNROLL)
    all_rows = xs_ref.at[pl.ds(0, TOP_K * tm * SLABS), :]
    pltpu.make_async_copy(all_rows, all_rows, sem).wait()


def _dispatch(h2, dest, pad_end, p_rows):
    n = h2.shape[0] // SLABS
    tm = DISPATCH_TILE
    dest3 = _dest_tiles(dest, tm)
    grid_spec = pltpu.PrefetchScalarGridSpec(
        num_scalar_prefetch=1,
        grid=(n // tm,),
        in_specs=[pl.BlockSpec((1, 1, TOP_K * tm), lambda i, pe: (i, 0, 0), memory_space=pltpu.SMEM),
                  pl.BlockSpec((tm * SLABS, LANES), lambda i, pe: (i, 0))],
        out_specs=pl.BlockSpec(memory_space=pl.ANY),
        scratch_shapes=[pltpu.VMEM((EXPERT_BLOCK * SLABS, LANES), jnp.uint32),
                        pltpu.SemaphoreType.DMA(()), pltpu.SemaphoreType.DMA(())],
    )
    return pl.pallas_call(
        _dispatch_kernel,
        grid_spec=grid_spec,
        out_shape=jax.ShapeDtypeStruct((p_rows * SLABS, LANES), jnp.uint32),
        compiler_params=_cparams("arbitrary"),
        name="dispatch",
    )(pad_end, dest3, h2)


def _expert_kernel(blk_e_ref, n_used_ref, x_ref, wgu_ref, wd_ref, y_ref):
    del blk_e_ref

    @pl.when(pl.program_id(0) < n_used_ref[0])
    def _():
        x = _unpack_rows(_slab_load(x_ref)).astype(BF16)
        h = jnp.dot(x, wgu_ref[0], preferred_element_type=F32)
        gt = h[:, :D_EXPERT]
        up = h[:, D_EXPERT:]
        act = (gt * jax.nn.sigmoid(gt) * up).astype(BF16)
        _slab_store(y_ref, _pack_rows(jnp.dot(act, wd_ref[0], preferred_element_type=F32)))

    @pl.when(pl.program_id(0) >= n_used_ref[0])
    def _():
        y_ref[...] = jnp.zeros_like(y_ref)


def _experts(xs, blk_expert, n_used, w_gu, w_down):
    p_rows = xs.shape[0] // SLABS
    d = 2 * SLABS * LANES
    n_blocks = p_rows // EXPERT_BLOCK
    rows = pl.BlockSpec((EXPERT_BLOCK * SLABS, LANES), lambda i, be, nu: (i, 0))
    grid_spec = pltpu.PrefetchScalarGridSpec(
        num_scalar_prefetch=2,
        grid=(n_blocks,),
        in_specs=[rows,
                  pl.BlockSpec((1, d, 2 * D_EXPERT), lambda i, be, nu: (be[i], 0, 0)),
                  pl.BlockSpec((1, D_EXPERT, d), lambda i, be, nu: (be[i], 0, 0))],
        out_specs=rows,
    )
    return pl.pallas_call(
        _expert_kernel,
        grid_spec=grid_spec,
        out_shape=jax.ShapeDtypeStruct(xs.shape, jnp.uint32),
        compiler_params=_cparams("arbitrary"),
        name="experts",
    )(blk_expert, n_used, xs, w_gu, w_down)


COMBINE_TILE = 512


def _combine_kernel(dest_ref, dest_next_ref, x1_ref, route_ref, g_ref, yb_ref, o_ref,
                    buf00, buf01, buf10, buf11, sems, *, final_norm):
    tm = x1_ref.shape[0]
    i = pl.program_id(0)
    bufs = ((buf00, buf01), (buf10, buf11))

    def issue(d_ref, slot):
        def start(t, _):
            for k in range(TOP_K):
                pltpu.make_async_copy(_slab_rows(yb_ref, d_ref[0, 0, k * tm + t]),
                                      _slab_rows(bufs[slot][k], t), sems.at[slot]).start(priority=k % 2)
            return 0

        lax.fori_loop(0, tm, start, 0, unroll=ROW_DMA_UNROLL)

    @pl.when(i == 0)
    def _():
        issue(dest_ref, 0)

    for slot in range(2):
        @pl.when((i % 2 == slot) & (i + 1 < pl.num_programs(0)))
        def _(slot=slot):
            issue(dest_next_ref, 1 - slot)

    for slot in range(2):
        @pl.when(i % 2 == slot)
        def _(slot=slot):
            for b in bufs[slot]:
                pltpu.make_async_copy(b, b, sems.at[slot]).wait()
            route = route_ref[...]
            x2 = (x1_ref[...] + route[:, 2:3] * _unpack_rows(_slab_load(bufs[slot][0]))
                  + route[:, 3:4] * _unpack_rows(_slab_load(bufs[slot][1])))
            o_ref[...] = _rms(x2, g_ref[...]) if final_norm else x2


def _combine(x1, route, dest, yb, final_g, final_norm):
    n, d = x1.shape
    tm = COMBINE_TILE
    n_tiles = n // tm
    dest3 = _dest_tiles(dest, tm)
    return pl.pallas_call(
        functools.partial(_combine_kernel, final_norm=final_norm),
        grid=(n_tiles,),
        in_specs=[pl.BlockSpec((1, 1, TOP_K * tm), lambda i: (i, 0, 0), memory_space=pltpu.SMEM),
                  pl.BlockSpec((1, 1, TOP_K * tm), lambda i: (jnp.minimum(i + 1, n_tiles - 1), 0, 0),
                               memory_space=pltpu.SMEM),
                  pl.BlockSpec((tm, d), lambda i: (i, 0)),
                  pl.BlockSpec((tm, ROUTE_W), lambda i: (i, 0)),
                  pl.BlockSpec((1, d), lambda i: (0, 0)),
                  pl.BlockSpec(memory_space=pl.ANY)],
        out_specs=pl.BlockSpec((tm, d), lambda i: (i, 0)),
        out_shape=jax.ShapeDtypeStruct((n, d), F32),
        scratch_shapes=[pltpu.VMEM((tm * SLABS, LANES), jnp.uint32) for _ in range(2 * TOP_K)]
                       + [pltpu.SemaphoreType.DMA((2,))],
        compiler_params=_cparams("arbitrary"),
        name="combine",
    )(dest3, dest3, x1, route, final_g, yb)


def _rwkv_params(rw_mu, rw_w0, rw_w_up, rw_a0, rw_a_up, rw_g_up, rw_k_k, rw_k_a, rw_r_k, rw_gn_w, rw_gn_b):
    row = lambda v: v.reshape(1, -1).astype(F32)
    zeros = jnp.zeros((A_LORA, RW_DIM), F32)
    return {
        "mu_r": row(rw_mu[:RW_DIM]), "mu_k": row(rw_mu[RW_DIM:2 * RW_DIM]),
        "mu_v": row(rw_mu[2 * RW_DIM:3 * RW_DIM]), "mu_l": row(rw_mu[3 * RW_DIM:]),
        "w0": row(rw_w0), "a0": row(rw_a0), "k_k": row(rw_k_k), "k_a": row(rw_k_a),
        "r_k": row(rw_r_k), "gn_w": row(rw_gn_w), "gn_b": row(rw_gn_b),
        "w_up": jnp.concatenate([rw_w_up, zeros], axis=0).astype(BF16),
        "a_up": jnp.concatenate([zeros, rw_a_up], axis=0).astype(BF16),
        "g_up": rw_g_up.astype(BF16),
    }


def _mla_params(g_qa, w_q_up, g_kva, w_kv_up):
    half = QK_ROPE // 2
    pad = MLA_SLOT - QK_NOPE - QK_ROPE
    wq = w_q_up.reshape(Q_LORA, MLA_HEADS, QK_NOPE + QK_ROPE)
    q_nope, q_r1, q_r2 = wq[..., :QK_NOPE], wq[..., QK_NOPE:QK_NOPE + half], wq[..., QK_NOPE + half:]
    zq = lambda w: jnp.zeros((Q_LORA, MLA_HEADS, w), F32)
    w_qa = jnp.concatenate([q_nope, q_r1, q_r2, zq(pad)], axis=-1).reshape(Q_LORA, -1)
    w_qb = jnp.concatenate([zq(QK_NOPE), -q_r2, q_r1, zq(pad)], axis=-1).reshape(Q_LORA, -1)
    wkv = w_kv_up.reshape(KV_LORA, MLA_HEADS, QK_NOPE + V_HEAD)
    w_k = jnp.concatenate([wkv[..., :QK_NOPE], jnp.zeros((KV_LORA, MLA_HEADS, MLA_SLOT - QK_NOPE), F32)],
                          axis=-1).reshape(KV_LORA, -1)
    w_v = wkv[..., QK_NOPE:].reshape(KV_LORA, -1)
    eye = jnp.eye(half, dtype=F32)
    z = jnp.zeros((half, half), F32)
    zl = jnp.zeros((QK_ROPE, QK_NOPE), F32)
    zr = jnp.zeros((QK_ROPE, pad), F32)
    p_a = jnp.concatenate([zl, jnp.concatenate([eye, z], 0), jnp.concatenate([z, eye], 0), zr], axis=1)
    p_b = jnp.concatenate([zl, jnp.concatenate([z, -eye], 0), jnp.concatenate([eye, z], 0), zr], axis=1)
    place_half = jnp.concatenate([jnp.zeros((half, QK_NOPE), F32), eye, eye, jnp.zeros((half, pad), F32)], axis=1)
    zh = jnp.zeros_like(place_half)
    place = jnp.concatenate([jnp.concatenate([place_half, zh], 1), jnp.concatenate([zh, place_half], 1)], 0)
    one = jnp.concatenate([jnp.ones((1, QK_NOPE), F32), jnp.zeros((1, MLA_SLOT - QK_NOPE), F32)], axis=1)
    return {"g_qa": g_qa.reshape(1, -1), "g_kva": g_kva.reshape(1, -1),
            "w_qa": w_qa.astype(BF16), "w_qb": w_qb.astype(BF16), "w_k": w_k.astype(BF16),
            "w_v": w_v.astype(BF16), "p_a": p_a.astype(BF16), "p_b": p_b.astype(BF16),
            "place": place.astype(BF16), "one": one}


def _rope_cos_sin(positions):
    inv_freq = ROPE_THETA ** (-jnp.arange(0, QK_ROPE, 2, dtype=F32) / QK_ROPE)
    ang = positions.astype(F32).reshape(-1, 1) * inv_freq
    return jnp.concatenate([jnp.cos(ang), jnp.sin(ang)], axis=1)


def _block_layout(hist, n_assign):
    tile_counts = hist[:, :, 0].astype(jnp.int32)
    counts = jnp.sum(tile_counts, axis=0)
    padded = (counts + EXPERT_BLOCK - 1) // EXPERT_BLOCK * EXPERT_BLOCK
    pad_end = jnp.cumsum(padded)
    pad_start = pad_end - padded
    tile_base = jnp.cumsum(tile_counts, axis=0) - tile_counts + pad_start[None, :]
    base = jnp.broadcast_to(tile_base.astype(F32)[:, :, None], tile_base.shape + (ROUTE_W,))
    n_blocks = -(-n_assign // EXPERT_BLOCK) + N_EXPERTS
    blk_row = jnp.arange(n_blocks, dtype=jnp.int32) * EXPERT_BLOCK
    blk_expert = jnp.minimum(jnp.sum((pad_end[None, :] <= blk_row[:, None]).astype(jnp.int32), axis=1),
                             N_EXPERTS - 1)
    n_used = (pad_end[-1] // EXPERT_BLOCK).astype(jnp.int32).reshape(1)
    return base, blk_expert, n_used, n_blocks, pad_end.astype(jnp.int32)


def kernel(x, positions, mix_norm_g, w_in, rw_mu, rw_w0, rw_w_up, rw_a0, rw_a_up, rw_g_up, rw_k_k, rw_k_a, rw_r_k, rw_gn_w, rw_gn_b, mla_g_qa, mla_w_q_up, mla_g_kva, mla_w_kv_up, w_branch_rw, w_branch_mla, w_out, ffn_norm_g, moe_w_group, moe_b_group, moe_w_router, moe_b_router, moe_w_gu, moe_w_down, final_norm_g):
    batch, seq, d = x.shape
    assert d == 2 * SLABS * LANES
    n = batch * seq
    depth = w_in.shape[0]
    rw_cols = 3 * RW_DIM + W_LORA + A_LORA + G_LORA
    mla_cols = Q_LORA + KV_LORA + QK_ROPE
    cos_sin = _rope_cos_sin(positions)
    x2 = x.reshape(n, d)

    for l in range(depth):
        wl = w_in[l].astype(BF16)
        c_rw, c_q, c_kvr, gates = _in_proj(
            x2, mix_norm_g[l].reshape(1, d), wl[:, :rw_cols], wl[:, rw_cols:rw_cols + Q_LORA],
            wl[:, rw_cols + Q_LORA:rw_cols + mla_cols], wl[:, rw_cols + mla_cols:], tm=ROW_TILE)

        rp = _rwkv_params(rw_mu[l], rw_w0[l], rw_w_up[l], rw_a0[l], rw_a_up[l], rw_g_up[l], rw_k_k[l],
                          rw_k_a[l], rw_r_k[l], rw_gn_w[l], rw_gn_b[l])
        y_rw = _rwkv(c_rw.reshape(batch, seq, rw_cols), rp, batch, seq).reshape(n, RW_DIM)

        mp = _mla_params(mla_g_qa[l], mla_w_q_up[l], mla_g_kva[l], mla_w_kv_up[l])
        q_t, k, v_t = _mla_prep(c_q, c_kvr, cos_sin, mp, ROW_TILE, batch, seq)
        y_mla = _mla_attn(q_t, k.reshape(batch, seq, -1), v_t, batch, seq).reshape(n, MLA_HEADS * V_HEAD)

        gap, tail = ROUTE_FINE0 - N_GROUPS, ROUTE_W - ROUTE_FINE0 - N_EXPERTS
        w_route = jnp.concatenate(
            [moe_w_group[l], jnp.zeros((d, gap), F32), moe_w_router[l], jnp.zeros((d, tail), F32)], axis=1)
        b_route = jnp.concatenate(
            [moe_b_group[l], jnp.zeros((gap,), F32), moe_b_router[l], jnp.zeros((tail,), F32)]).reshape(1, -1)
        wr_hi = w_route.astype(BF16)
        wr_lo = (w_route - wr_hi.astype(F32)).astype(BF16)
        mparams = {"w_br": w_branch_rw[l].astype(BF16), "w_bm": w_branch_mla[l].astype(BF16),
                   "w_out": w_out[l].astype(BF16), "ffn_g": ffn_norm_g[l].reshape(1, d),
                   "w_route": jnp.concatenate([wr_hi, wr_lo], axis=1), "b_route": b_route}
        x1, h2p, route, route_t, hist = _merge(x2, y_rw, y_mla, gates, mparams, tm=ROW_TILE)

        base, blk_expert, n_used, n_blocks, pad_end = _block_layout(hist, n * TOP_K)
        dest = _plan(route_t, base, tm=ROW_TILE)
        xs = _dispatch(h2p, dest, pad_end, n_blocks * EXPERT_BLOCK)
        yb = _experts(xs, blk_expert, n_used, moe_w_gu[l].astype(BF16), moe_w_down[l].astype(BF16))
        x2 = _combine(x1, route, dest, yb, final_norm_g.reshape(1, d), final_norm=(l == depth - 1))

    return x2.reshape(batch, seq, d)
```

```python
import functools
import math

import jax
import jax.numpy as jnp
from jax import lax
from jax.experimental import pallas as pl
from jax.experimental.pallas import tpu as pltpu

F32 = jnp.float32
BF16 = jnp.bfloat16

RW_HEADS = 8
RW_HEAD_DIM = 64
RW_DIM = RW_HEADS * RW_HEAD_DIM
W_LORA = 64
A_LORA = 64
G_LORA = 128
GN_EPS = 64e-5
MLA_HEADS = 8
QK_NOPE = 64
QK_ROPE = 32
V_HEAD = 64
Q_LORA = 384
KV_LORA = 256
ROPE_THETA = 10000.0
N_GROUPS = 4
EXPERTS_PER_GROUP = 8
N_EXPERTS = N_GROUPS * EXPERTS_PER_GROUP
TOP_K = 2
D_EXPERT = 256
EXPERT_BLOCK = 512
NORM_EPS = 1e-6

LANES = 128
HEAD_PAIR = 2 * RW_HEAD_DIM
VMEM_LIMIT = 48 * 1024 * 1024
ROW_TILE = 512


def _cparams(*sem):
    return pltpu.CompilerParams(dimension_semantics=sem, vmem_limit_bytes=VMEM_LIMIT)


def _mm(a, b, dims=((1,), (0,))):
    return lax.dot_general(a.astype(BF16), b.astype(BF16), (dims, ((), ())), preferred_element_type=F32)


def _mm_sel(sel_bf16, x, dims=((1,), (0,))):
    dn = (dims, ((), ()))
    hi = x.astype(BF16)
    lo = (x - hi.astype(F32)).astype(BF16)
    return (lax.dot_general(sel_bf16, hi, dn, preferred_element_type=F32)
            + lax.dot_general(sel_bf16, lo, dn, preferred_element_type=F32))


def _seg_sum(x, seg_bf16):
    return jnp.dot(x.astype(BF16), seg_bf16, preferred_element_type=F32)


def _rms(x, g):
    return x * lax.rsqrt(jnp.mean(x * x, axis=-1, keepdims=True) + NORM_EPS) * g


def _pack_rows(x):
    half = x.shape[1] // 2
    bits = lambda v: lax.bitcast_convert_type(v.astype(BF16).astype(F32), jnp.uint32)
    return bits(x[:, :half]) | (bits(x[:, half:]) >> 16)


def _unpack_rows(p):
    hi = lax.bitcast_convert_type(p & jnp.uint32(0xFFFF0000), F32)
    lo = lax.bitcast_convert_type(p << 16, F32)
    return jnp.concatenate([hi, lo], axis=1)


SLABS = 4


def _slab_rows(ref, r):
    return ref.at[pl.ds(pl.multiple_of(r * SLABS, SLABS), SLABS), :]


def _slab_load(ref):
    rows = ref.shape[0] // SLABS
    return jnp.concatenate([ref[pl.ds(j, rows, stride=SLABS), :] for j in range(SLABS)], axis=1)


def _slab_store(ref, x):
    rows = ref.shape[0] // SLABS
    for j in range(SLABS):
        ref[pl.ds(j, rows, stride=SLABS), :] = x[:, j * LANES:(j + 1) * LANES]


def _in_proj_kernel(x_ref, g_ref, wrw_ref, wmla_ref, wg_ref, crw_ref, cmla_ref, gate_ref):
    hb = _rms(x_ref[...], g_ref[...]).astype(BF16)
    crw_ref[...] = jnp.dot(hb, wrw_ref[...], preferred_element_type=F32)
    cmla_ref[...] = jnp.dot(hb, wmla_ref[...], preferred_element_type=F32)
    gate_ref[...] = jax.nn.sigmoid(jnp.dot(hb, wg_ref[...], preferred_element_type=F32)).astype(BF16)


def _in_proj(x2, g, w_rw, w_mla, w_gate, tm):
    n, d = x2.shape
    full = lambda w: pl.BlockSpec(w.shape, lambda i: (0, 0))
    row = lambda c: pl.BlockSpec((tm, c), lambda i: (i, 0))
    return pl.pallas_call(
        _in_proj_kernel,
        grid=(n // tm,),
        in_specs=[row(d), full(g), full(w_rw), full(w_mla), full(w_gate)],
        out_specs=[row(w_rw.shape[1]), row(w_mla.shape[1]), row(w_gate.shape[1])],
        out_shape=[jax.ShapeDtypeStruct((n, w_rw.shape[1]), F32),
                   jax.ShapeDtypeStruct((n, w_mla.shape[1]), F32),
                   jax.ShapeDtypeStruct((n, w_gate.shape[1]), BF16)],
        compiler_params=_cparams("parallel"),
        name="in_proj",
    )(x2, g, w_rw, w_mla, w_gate)


RW_CHUNK = 64
RW_TILE = 1024
RW_GROUP = 16
def _token_shift(cur, halo_ref, first):
    prev_row = jnp.where(first, 0.0, halo_ref[0, 7:8, :])
    rolled = pltpu.roll(cur, 1, 0)
    row = lax.broadcasted_iota(jnp.int32, cur.shape, 0)
    return jnp.where(row == 0, prev_row, rolled)


def _rwkv_kernel(r_ref, k_ref, v_ref, l_ref, hr_ref, hk_ref, hv_ref, hl_ref,
                 mur_ref, muk_ref, muv_ref, mul_ref, w0_ref, a0_ref, kk_ref, ka_ref, rk_ref,
                 gnw_ref, gnb_ref, wup_ref, aup_ref, gup_ref, y_ref, st_ref):
    i = pl.program_id(2)
    first = i == 0

    @pl.when(first)
    def _():
        st_ref[...] = jnp.zeros_like(st_ref)

    def mixed(c_ref, h_ref, mu_ref):
        cur = c_ref[0]
        return cur + (_token_shift(cur, h_ref, first) - cur) * mu_ref[...]

    zr = mixed(r_ref, hr_ref, mur_ref)
    zk = mixed(k_ref, hk_ref, muk_ref)
    zv = mixed(v_ref, hv_ref, muv_ref)
    zl = mixed(l_ref, hl_ref, mul_ref)
    z_wa = zl[:, :LANES]
    z_g = zl[:, LANES:]

    lane = lax.broadcasted_iota(jnp.int32, (LANES, LANES), 1)
    sub = lax.broadcasted_iota(jnp.int32, (LANES, LANES), 0)
    same_head = (lane // RW_HEAD_DIM) == (sub // RW_HEAD_DIM)
    seg = jnp.where(same_head, 1.0, 0.0).astype(BF16)

    w = w0_ref[...] + _mm(jnp.tanh(z_wa), wup_ref[...])
    u = -w
    softplus = jnp.maximum(u, 0.0) + jnp.log(1.0 + jnp.exp(-jnp.abs(u)))
    log_decay = -jnp.exp(-softplus - 0.5)
    a = jax.nn.sigmoid(a0_ref[...] + _mm(z_wa, aup_ref[...]))
    g = _mm(jax.nn.sigmoid(z_g), gup_ref[...])

    kk = zk * kk_ref[...]
    kk = kk / jnp.maximum(jnp.sqrt(_seg_sum(kk * kk, seg)), 1e-12)
    k2 = zk * (1.0 + (a - 1.0) * ka_ref[...])
    bonus = _seg_sum(zr * k2 * rk_ref[...], seg) * zv
    kka = kk * a

    c = RW_CHUNK
    crow = lax.broadcasted_iota(jnp.int32, (c, c), 0)
    ccol = lax.broadcasted_iota(jnp.int32, (c, c), 1)
    cum_sel = jnp.where(crow >= ccol, 1.0, 0.0).astype(BF16)
    tril_incl = sub >= lane
    tril_strict = sub > lane
    eye_l = jnp.where(lane == sub, 1.0, 0.0).astype(F32)
    lo_half = lax.broadcasted_iota(jnp.int32, (c, LANES), 1) < RW_HEAD_DIM
    nt = ((1,), (1,))
    tn = ((0,), (0,))
    zeros_blk = jnp.zeros((2 * c, LANES), BF16)
    zeros_half = jnp.zeros((c, LANES), BF16)

    def stack(t):
        tb = t.astype(BF16)
        return jnp.concatenate([jnp.where(lo_half, tb, zeros_half), jnp.where(lo_half, zeros_half, tb)], axis=0)

    tril_incl2 = jnp.concatenate([tril_incl, tril_incl], axis=1)

    def chunk_group(ids):
        chunks = range(len(ids))
        x_a, x_b, x_k, x_r, x_v, x_bh, x_kh, w_tot, r_dec = [], [], [], [], [], [], [], [], []
        for ci in ids:
            sl = slice(ci * c, (ci + 1) * c)
            ld = log_decay[sl]
            cum = _mm_sel(cum_sel, ld)
            tot = cum[c - 1:c, :]
            e_neg = jnp.exp(-cum)
            e_rest = jnp.exp(tot - cum)
            x_a.append(stack(-kk[sl] * jnp.exp(cum - ld)))
            x_b.append(stack(kka[sl] * e_neg))
            x_k.append(stack(k2[sl] * e_neg))
            r_dec.append(zr[sl] * jnp.exp(cum))
            x_r.append(stack(r_dec[-1]))
            x_v.append(stack(zv[sl]))
            x_bh.append(stack(kka[sl] * e_rest))
            x_kh.append(stack(k2[sl] * e_rest))
            w_tot.append(jnp.exp(tot))

        inter = [_mm(jnp.concatenate([x_a[i], x_r[i]], axis=0),
                     jnp.concatenate([x_b[i], x_k[i]], axis=0), nt) for i in chunks]
        inter = [m.astype(BF16) for m in inter]
        zeros_sq_b = jnp.zeros((LANES, LANES), BF16)
        a_ab = [jnp.where(tril_strict, m[:2 * c, :2 * c], zeros_sq_b) for m in inter]
        a_ak = [jnp.where(tril_strict, m[:2 * c, 2 * c:], zeros_sq_b) for m in inter]
        a_r = [jnp.where(tril_incl2, m[2 * c:], jnp.concatenate([zeros_sq_b, zeros_sq_b], axis=1))
               for m in inter]
        w_ak = [_mm(a_ak[i], x_v[i]).astype(BF16) for i in chunks]

        t_inv = [eye_l + m.astype(F32) for m in a_ab]
        pw = a_ab
        for _ in range(int(math.log2(c)) - 1):
            pw = [_mm(m, m).astype(BF16) for m in pw]
            t_inv = [t_inv[i] + _mm(t_inv[i], pw[i]) for i in chunks]

        solved = [_mm(t_inv[i], jnp.concatenate([x_a[i], w_ak[i]], axis=1)).astype(BF16)
                  for i in chunks]
        rhs = [jnp.concatenate([solved[i], jnp.concatenate([zeros_blk, x_v[i]], axis=1)], axis=0)
               for i in chunks]
        out = [_mm(a_r[i], rhs[i]) for i in chunks]
        carry = [_mm(jnp.concatenate([x_bh[i], x_kh[i]], axis=0), rhs[i], tn) for i in chunks]
        q_hat, y_loc = [], []
        for i in chunks:
            q_hat.append(r_dec[i] + out[i][:c, :LANES] + out[i][c:, :LANES])
            y_loc.append(out[i][:c, LANES:] + out[i][c:, LANES:])
        trans = [jnp.concatenate([eye_l * w_tot[i] + carry[i][:, :LANES], carry[i][:, LANES:]], axis=1)
                 for i in chunks]
        return q_hat, y_loc, trans

    n_chunks = RW_TILE // c
    groups = [chunk_group(range(g, g + RW_GROUP)) for g in range(0, n_chunks, RW_GROUP)]
    q_hat, y_loc, trans = (sum((g[j] for g in groups), []) for j in range(3))
    chunks = range(n_chunks)

    zeros_sq = jnp.zeros((LANES, LANES), F32)

    def compose(later, earlier):
        return _mm(later[:, :LANES], earlier) + jnp.concatenate([zeros_sq, later[:, LANES:]], axis=1)

    scan = list(trans)
    dist = 1
    while dist < len(scan):
        scan = [scan[i] if i < dist else compose(scan[i], scan[i - dist]) for i in range(len(scan))]
        dist *= 2
    prefix = [None] + scan
    q_pre = [None] + [_mm(q_hat[i], prefix[i]) for i in chunks[1:]]
    q_m = jnp.concatenate([q_hat[0]] + [q_pre[i][:, :LANES] for i in chunks[1:]], axis=0)
    y_off = jnp.concatenate([y_loc[0]] + [y_loc[i] + q_pre[i][:, LANES:] for i in chunks[1:]], axis=0)
    state = st_ref[...]
    y = _mm(q_m, state) + y_off
    st_ref[...] = _mm(prefix[-1][:, :LANES], state) + prefix[-1][:, LANES:]

    inv_n = 1.0 / RW_HEAD_DIM
    mean = _seg_sum(y, seg) * inv_n
    d = y - mean
    var = _seg_sum(d * d, seg) * inv_n
    yn = d * lax.rsqrt(var + GN_EPS) * gnw_ref[...] + gnb_ref[...]
    y_ref[0] = ((yn + bonus) * g).astype(y_ref.dtype)


def _rwkv(c_rw, p, batch, seq):
    ts = RW_TILE
    n_pairs = RW_DIM // HEAD_PAIR
    lora_blk = (3 * RW_DIM) // (2 * LANES)
    halo = ts // 8

    def col(off):
        return pl.BlockSpec((1, ts, LANES), lambda b, pp, i, off=off: (b, i, off + pp))

    def col_halo(off):
        return pl.BlockSpec((1, 8, LANES),
                            lambda b, pp, i, off=off: (b, jnp.maximum(i * halo - 1, 0), off + pp))

    vec = pl.BlockSpec((1, LANES), lambda b, pp, i: (0, pp))
    lora_w = pl.BlockSpec((LANES, LANES), lambda b, pp, i: (0, pp))
    in_specs = [
        col(0), col(n_pairs), col(2 * n_pairs),
        pl.BlockSpec((1, ts, 2 * LANES), lambda b, pp, i: (b, i, lora_blk)),
        col_halo(0), col_halo(n_pairs), col_halo(2 * n_pairs),
        pl.BlockSpec((1, 8, 2 * LANES), lambda b, pp, i: (b, jnp.maximum(i * halo - 1, 0), lora_blk)),
        vec, vec, vec, pl.BlockSpec((1, 2 * LANES), lambda b, pp, i: (0, 0)),
        vec, vec, vec, vec, vec, vec, vec, lora_w, lora_w, lora_w,
    ]
    return pl.pallas_call(
        _rwkv_kernel,
        grid=(batch, n_pairs, seq // ts),
        in_specs=in_specs,
        out_specs=pl.BlockSpec((1, ts, LANES), lambda b, pp, i: (b, i, pp)),
        out_shape=jax.ShapeDtypeStruct((batch, seq, RW_DIM), BF16),
        scratch_shapes=[pltpu.VMEM((LANES, LANES), F32)],
        compiler_params=_cparams("parallel", "parallel", "arbitrary"),
        name="rwkv",
    )(c_rw, c_rw, c_rw, c_rw, c_rw, c_rw, c_rw, c_rw,
      p["mu_r"], p["mu_k"], p["mu_v"], p["mu_l"], p["w0"], p["a0"], p["k_k"], p["k_a"], p["r_k"],
      p["gn_w"], p["gn_b"], p["w_up"], p["a_up"], p["g_up"])


MLA_SLOT = 128


def _mla_prep_kernel(cmla_ref, cs_ref, gq_ref, gkv_ref,
                     wqa_ref, wqb_ref, wk_ref, wv_ref, pa_ref, pb_ref, place_ref, one_ref,
                     qt_ref, k_ref, vt_ref):
    cs = cs_ref[...]
    cs_hi = cs.astype(BF16)
    cs_lo = (cs - cs_hi.astype(F32)).astype(BF16)
    tables = (jnp.dot(cs_hi, place_ref[...], preferred_element_type=F32)
              + jnp.dot(cs_lo, place_ref[...], preferred_element_type=F32))
    cos = tables[:, :MLA_SLOT] + one_ref[...]
    sin = tables[:, MLA_SLOT:]
    zq = _rms(cmla_ref[:, :Q_LORA], gq_ref[...]).astype(BF16)
    qa = jnp.dot(zq, wqa_ref[...], preferred_element_type=F32)
    qb = jnp.dot(zq, wqb_ref[...], preferred_element_type=F32)
    ckvr = cmla_ref[:, Q_LORA:]
    zkv = _rms(ckvr[:, :KV_LORA], gkv_ref[...]).astype(BF16)
    kn = jnp.dot(zkv, wk_ref[...], preferred_element_type=F32)
    v = jnp.dot(zkv, wv_ref[...], preferred_element_type=F32)
    for blk in range(v.shape[1] // LANES):
        vt_ref[0, blk * LANES:(blk + 1) * LANES, :] = v[:, blk * LANES:(blk + 1) * LANES].astype(BF16).T
    kr = ckvr[:, KV_LORA:].astype(BF16)
    k_rope = (jnp.dot(kr, pa_ref[...], preferred_element_type=F32) * cos
              + jnp.dot(kr, pb_ref[...], preferred_element_type=F32) * sin)
    scale = math.log2(math.e) / math.sqrt(QK_NOPE + QK_ROPE)
    for h in range(MLA_HEADS):
        sl = slice(h * MLA_SLOT, (h + 1) * MLA_SLOT)
        qt_ref[0, sl, :] = ((qa[:, sl] * cos + qb[:, sl] * sin) * scale).astype(BF16).T
        k_ref[:, sl] = (kn[:, sl] + k_rope).astype(BF16)


def _mla_prep(c_mla, cos_sin, p, tm, batch, seq):
    n = c_mla.shape[0]
    full = lambda w: pl.BlockSpec(w.shape, lambda i: (0, 0))
    row = lambda c: pl.BlockSpec((tm, c), lambda i: (i, 0))
    per_seq = seq // tm
    col = lambda r: pl.BlockSpec((1, r, tm), lambda i: (i // per_seq, 0, i % per_seq))
    ws = [p["g_qa"], p["g_kva"], p["w_qa"], p["w_qb"], p["w_k"], p["w_v"], p["p_a"], p["p_b"],
          p["place"], p["one"]]
    hq = MLA_HEADS * MLA_SLOT
    hv = MLA_HEADS * V_HEAD
    return pl.pallas_call(
        _mla_prep_kernel,
        grid=(n // tm,),
        in_specs=[row(c_mla.shape[1]), row(cos_sin.shape[1])] + [full(w) for w in ws],
        out_specs=[col(hq), row(hq), col(hv)],
        out_shape=[jax.ShapeDtypeStruct((batch, hq, seq), BF16), jax.ShapeDtypeStruct((n, hq), BF16),
                   jax.ShapeDtypeStruct((batch, hv, seq), BF16)],
        compiler_params=_cparams("parallel"),
        name="mla_prep",
    )(c_mla, cos_sin, *ws)


ATT_TILE = 512
ATT_HEADS = 4


def _attn_kernel(qt_ref, k_ref, vt_ref, o_ref):
    qi = pl.program_id(2)
    t = ATT_TILE
    heads = range(ATT_HEADS)
    den_row = (V_HEAD, 0)

    def augment(h, vt):
        r = lax.broadcasted_iota(jnp.int32, vt.shape, 0)
        own = (r < V_HEAD) if h % 2 == 0 else (r >= V_HEAD)
        return jnp.where(own, vt, jnp.where(r == den_row[h % 2], 1.0, 0.0).astype(BF16))

    def block(j, carry, mask):
        keys = pl.ds(pl.multiple_of(j * t, t), t)
        sts = []
        for h in heads:
            qt = qt_ref[0, h * MLA_SLOT:(h + 1) * MLA_SLOT, :]
            kb = k_ref[0, keys, h * MLA_SLOT:(h + 1) * MLA_SLOT]
            st = jnp.dot(kb, qt, preferred_element_type=F32)
            sts.append(st if mask is None else jnp.where(mask, st, -jnp.inf))
        m_new = [jnp.maximum(carry[h][0], jnp.max(sts[h], axis=0, keepdims=True)) for h in heads]
        pts = [jnp.exp2((sts[h] - m_new[h]).astype(BF16)) for h in heads]
        out = []
        for h in heads:
            m, acc = carry[h]
            vt = vt_ref[0, (h // 2) * 2 * V_HEAD:(h // 2 + 1) * 2 * V_HEAD, keys]
            acc = acc * jnp.exp2(m - m_new[h]) + jnp.dot(augment(h, vt), pts[h], preferred_element_type=F32)
            out.append((m_new[h], acc))
        return tuple(out)

    init1 = (jnp.full((1, t), -jnp.inf, F32), jnp.zeros((2 * V_HEAD, t), F32))
    carry = lax.fori_loop(0, qi, lambda j, c: block(j, c, None), tuple(init1 for _ in heads))

    causal = lax.broadcasted_iota(jnp.int32, (t, t), 0) <= lax.broadcasted_iota(jnp.int32, (t, t), 1)
    accs = [mc[1] for mc in block(qi, carry, causal)]
    lo_rows = lax.broadcasted_iota(jnp.int32, (2 * V_HEAD, t), 0) < V_HEAD
    for p in range(ATT_HEADS // 2):
        acc0, acc1 = accs[2 * p], accs[2 * p + 1]
        den0 = acc0[den_row[0]:den_row[0] + 1, :]
        den1 = acc1[den_row[1]:den_row[1] + 1, :]
        out_t = jnp.where(lo_rows, acc0 / den0, acc1 / den1)
        o_ref[0, :, p * 2 * V_HEAD:(p + 1) * 2 * V_HEAD] = out_t.T.astype(o_ref.dtype)


def _mla_attn(qt, k, vt, batch, seq):
    t = ATT_TILE
    g = ATT_HEADS
    return pl.pallas_call(
        _attn_kernel,
        grid=(batch, MLA_HEADS // g, seq // t),
        in_specs=[pl.BlockSpec((1, g * MLA_SLOT, t), lambda b, hp, i: (b, hp, i)),
                  pl.BlockSpec((1, seq, g * MLA_SLOT), lambda b, hp, i: (b, 0, hp)),
                  pl.BlockSpec((1, g * V_HEAD, seq), lambda b, hp, i: (b, hp, 0))],
        out_specs=pl.BlockSpec((1, t, g * V_HEAD), lambda b, hp, i: (b, i, hp)),
        out_shape=jax.ShapeDtypeStruct((batch, seq, MLA_HEADS * V_HEAD), BF16),
        compiler_params=_cparams("parallel", "parallel", "arbitrary"),
        name="mla_attn",
    )(qt, k, vt)


ROUTE_W = 128
ROUTE_FINE0 = 8


def _merge_kernel(x_ref, yrw_ref, ymla_ref, gate_ref, wbr_ref, wbm_ref, wo_ref, fg_ref,
                  wr_ref, br_ref, x1_ref, h2p_ref, route_ref, route_t_ref, hist_ref):
    d = x_ref.shape[1]
    a = jnp.dot(yrw_ref[...], wbr_ref[...], preferred_element_type=F32)
    b = jnp.dot(ymla_ref[...], wbm_ref[...], preferred_element_type=F32)
    merged = gate_ref[:, :d].astype(F32) * a + gate_ref[:, d:].astype(F32) * b
    x1 = x_ref[...] + jnp.dot(merged.astype(BF16), wo_ref[...], preferred_element_type=F32)
    x1_ref[...] = x1
    h2 = _rms(x1, fg_ref[...])
    _slab_store(h2p_ref, _pack_rows(h2))

    h_hi = h2.astype(BF16)
    h_lo = (h2 - h_hi.astype(F32)).astype(BF16)
    both = jnp.dot(h_hi, wr_ref[...], preferred_element_type=F32)
    logits = (both[:, :ROUTE_W] + both[:, ROUTE_W:]
              + jnp.dot(h_lo, wr_ref[:, :ROUTE_W], preferred_element_type=F32)) + br_ref[...]

    tm = logits.shape[0]
    lt = logits.T
    sub = lax.broadcasted_iota(jnp.int32, (EXPERTS_PER_GROUP, tm), 0)
    big = jnp.int32(EXPERTS_PER_GROUP)
    neg = -jnp.inf

    def first_argmax(vals, vmax):
        return jnp.min(jnp.where(vals == vmax, sub, big), axis=0, keepdims=True)

    grp = jnp.where(sub < N_GROUPS, lt[:EXPERTS_PER_GROUP], neg)
    g_max = jnp.max(grp, axis=0, keepdims=True)
    g_den = jnp.sum(jnp.exp(grp - g_max), axis=0, keepdims=True)
    g_sel = first_argmax(grp, g_max)
    gate_g = 1.0 / g_den
    fine = lt[ROUTE_FINE0:ROUTE_FINE0 + EXPERTS_PER_GROUP]
    for g in range(1, N_GROUPS):
        lo = ROUTE_FINE0 + g * EXPERTS_PER_GROUP
        fine = jnp.where(g_sel == g, lt[lo:lo + EXPERTS_PER_GROUP], fine)
    v1 = jnp.max(fine, axis=0, keepdims=True)
    i1 = first_argmax(fine, v1)
    fine2 = jnp.where(sub == i1, neg, fine)
    v2 = jnp.max(fine2, axis=0, keepdims=True)
    i2 = first_argmax(fine2, v2)
    e2 = jnp.exp(v2 - v1)
    den = 1.0 + e2
    w1 = gate_g / den
    w2 = gate_g * e2 / den
    base_e = g_sel * EXPERTS_PER_GROUP
    route_t = jnp.where(sub == 0, (base_e + i1).astype(F32),
                        jnp.where(sub == 1, (base_e + i2).astype(F32),
                                  jnp.where(sub == 2, w1, jnp.where(sub == 3, w2, 0.0))))
    pad = jnp.zeros((ROUTE_W - EXPERTS_PER_GROUP, tm), F32)
    route_t_ref[...] = route_t
    route_ref[...] = jnp.concatenate([route_t, pad], axis=0).T
    picked = (sub == i1) | (sub == i2)
    for g in range(N_GROUPS):
        cnt = jnp.sum(jnp.where(picked & (g_sel == g), 1.0, 0.0), axis=1, keepdims=True)
        hist_ref[0, g * EXPERTS_PER_GROUP:(g + 1) * EXPERTS_PER_GROUP, :] = jnp.broadcast_to(
            cnt, (EXPERTS_PER_GROUP, ROUTE_W))


def _merge(x2, y_rw, y_mla, gates, p, tm):
    n, d = x2.shape
    full = lambda w: pl.BlockSpec(w.shape, lambda i: (0, 0))
    row = lambda c: pl.BlockSpec((tm, c), lambda i: (i, 0))
    ws = [p["w_br"], p["w_bm"], p["w_out"], p["ffn_g"], p["w_route"], p["b_route"]]
    return pl.pallas_call(
        _merge_kernel,
        grid=(n // tm,),
        in_specs=[row(d), row(y_rw.shape[1]), row(y_mla.shape[1]), row(2 * d)] + [full(w) for w in ws],
        out_specs=[row(d), pl.BlockSpec((tm * SLABS, LANES), lambda i: (i, 0)), row(ROUTE_W),
                   pl.BlockSpec((EXPERTS_PER_GROUP, tm), lambda i: (0, i)),
                   pl.BlockSpec((1, N_EXPERTS, ROUTE_W), lambda i: (i, 0, 0))],
        out_shape=[jax.ShapeDtypeStruct((n, d), F32), jax.ShapeDtypeStruct((n * SLABS, LANES), jnp.uint32),
                   jax.ShapeDtypeStruct((n, ROUTE_W), F32),
                   jax.ShapeDtypeStruct((EXPERTS_PER_GROUP, n), F32),
                   jax.ShapeDtypeStruct((n // tm, N_EXPERTS, ROUTE_W), F32)],
        compiler_params=_cparams("parallel"),
        name="merge_route",
    )(x2, y_rw, y_mla, gates, *ws)


def _plan_kernel(route_t_ref, base_ref, dest_ref):
    tm = route_t_ref.shape[1]
    rt = route_t_ref[...]
    expert = lax.broadcasted_iota(jnp.int32, (N_EXPERTS, tm), 0).astype(F32)
    pick = [expert == rt[k:k + 1, :] for k in range(TOP_K)]
    both = jnp.where(pick[0] | pick[1], 1.0, 0.0).astype(BF16)
    r = lax.broadcasted_iota(jnp.int32, (tm, tm), 0)
    c = lax.broadcasted_iota(jnp.int32, (tm, tm), 1)
    earlier = jnp.where(r < c, 1.0, 0.0).astype(BF16)
    offs = jnp.dot(both, earlier, preferred_element_type=F32) + base_ref[0][:, :1]
    rows = [jnp.sum(jnp.where(pk, offs, 0.0), axis=0, keepdims=True) for pk in pick]
    sub = lax.broadcasted_iota(jnp.int32, dest_ref.shape, 0)
    dest_ref[...] = jnp.where(sub == 0, rows[0], jnp.where(sub == 1, rows[1], 0.0)).astype(jnp.int32)


def _plan(route_t, base, tm):
    n = route_t.shape[1]
    return pl.pallas_call(
        _plan_kernel,
        grid=(n // tm,),
        in_specs=[pl.BlockSpec((EXPERTS_PER_GROUP, tm), lambda i: (0, i)),
                  pl.BlockSpec((1, N_EXPERTS, ROUTE_W), lambda i: (i, 0, 0))],
        out_specs=pl.BlockSpec((EXPERTS_PER_GROUP, tm), lambda i: (0, i)),
        out_shape=jax.ShapeDtypeStruct((EXPERTS_PER_GROUP, n), jnp.int32),
        compiler_params=_cparams("parallel"),
        name="route_plan",
    )(route_t, base)


def _dest_tiles(dest_t, tm):
    n = dest_t.shape[1]
    return dest_t[:TOP_K].reshape(TOP_K, n // tm, tm).transpose(1, 0, 2).reshape(n // tm, 1, TOP_K * tm)


DISPATCH_TILE = 1024
ROW_DMA_UNROLL = 8


def _dispatch_kernel(pad_end_ref, dest_ref, h_ref, xs_ref, zbuf, sem, zsem):
    tm = h_ref.shape[0] // SLABS

    @pl.when(pl.program_id(0) == 0)
    def _():
        zbuf[...] = jnp.zeros_like(zbuf)

        def tail(e):
            first = pl.multiple_of((pad_end_ref[e] - EXPERT_BLOCK) * SLABS, EXPERT_BLOCK * SLABS)
            return pltpu.make_async_copy(zbuf, xs_ref.at[pl.ds(first, EXPERT_BLOCK * SLABS), :], zsem)

        def region_rows(e):
            return pad_end_ref[e] - (pad_end_ref[e - 1] if e else 0)

        n_blocks = xs_ref.shape[0] // (EXPERT_BLOCK * SLABS)
        used_rows = pad_end_ref[N_EXPERTS - 1]

        def spare(b):
            return pltpu.make_async_copy(zbuf, xs_ref.at[pl.ds(b * EXPERT_BLOCK * SLABS, EXPERT_BLOCK * SLABS), :],
                                         zsem)

        spare_blocks = range(n_blocks - N_EXPERTS, n_blocks)
        for e in range(N_EXPERTS):
            @pl.when(region_rows(e) > 0)
            def _(e=e):
                tail(e).start()
        for b in spare_blocks:
            @pl.when(b * EXPERT_BLOCK >= used_rows)
            def _(b=b):
                spare(b).start()
        for e in range(N_EXPERTS):
            @pl.when(region_rows(e) > 0)
            def _(e=e):
                tail(e).wait()
        for b in spare_blocks:
            @pl.when(b * EXPERT_BLOCK >= used_rows)
            def _(b=b):
                spare(b).wait()

    def start(t, _):
        for k in range(TOP_K):
            pltpu.make_async_copy(_slab_rows(h_ref, t), _slab_rows(xs_ref, dest_ref[0, 0, k * tm + t]),
                                  sem).start(priority=k % 2)
        return 0

    lax.fori_loop(0, tm, start, 0, unroll=ROW_DMA_UNROLL)
    all_rows = xs_ref.at[pl.ds(0, TOP_K * tm * SLABS), :]
    pltpu.make_async_copy(all_rows, all_rows, sem).wait()


def _dispatch(h2, dest, pad_end, p_rows):
    n = h2.shape[0] // SLABS
    tm = DISPATCH_TILE
    dest3 = _dest_tiles(dest, tm)
    grid_spec = pltpu.PrefetchScalarGridSpec(
        num_scalar_prefetch=1,
        grid=(n // tm,),
        in_specs=[pl.BlockSpec((1, 1, TOP_K * tm), lambda i, pe: (i, 0, 0), memory_space=pltpu.SMEM),
                  pl.BlockSpec((tm * SLABS, LANES), lambda i, pe: (i, 0))],
        out_specs=pl.BlockSpec(memory_space=pl.ANY),
        scratch_shapes=[pltpu.VMEM((EXPERT_BLOCK * SLABS, LANES), jnp.uint32),
                        pltpu.SemaphoreType.DMA(()), pltpu.SemaphoreType.DMA(())],
    )
    return pl.pallas_call(
        _dispatch_kernel,
        grid_spec=grid_spec,
        out_shape=jax.ShapeDtypeStruct((p_rows * SLABS, LANES), jnp.uint32),
        compiler_params=_cparams("arbitrary"),
        name="dispatch",
    )(pad_end, dest3, h2)


def _expert_kernel(blk_e_ref, n_used_ref, x_ref, wgu_ref, wd_ref, y_ref):
    del blk_e_ref

    @pl.when(pl.program_id(0) < n_used_ref[0])
    def _():
        x = _unpack_rows(_slab_load(x_ref)).astype(BF16)
        h = jnp.dot(x, wgu_ref[0], preferred_element_type=F32)
        gt = h[:, :D_EXPERT]
        up = h[:, D_EXPERT:]
        act = (gt * jax.nn.sigmoid(gt) * up).astype(BF16)
        _slab_store(y_ref, _pack_rows(jnp.dot(act, wd_ref[0], preferred_element_type=F32)))

    @pl.when(pl.program_id(0) >= n_used_ref[0])
    def _():
        y_ref[...] = jnp.zeros_like(y_ref)


def _experts(xs, blk_expert, n_used, w_gu, w_down):
    p_rows = xs.shape[0] // SLABS
    d = 2 * SLABS * LANES
    n_blocks = p_rows // EXPERT_BLOCK
    rows = pl.BlockSpec((EXPERT_BLOCK * SLABS, LANES), lambda i, be, nu: (i, 0))
    grid_spec = pltpu.PrefetchScalarGridSpec(
        num_scalar_prefetch=2,
        grid=(n_blocks,),
        in_specs=[rows,
                  pl.BlockSpec((1, d, 2 * D_EXPERT), lambda i, be, nu: (be[i], 0, 0)),
                  pl.BlockSpec((1, D_EXPERT, d), lambda i, be, nu: (be[i], 0, 0))],
        out_specs=rows,
    )
    return pl.pallas_call(
        _expert_kernel,
        grid_spec=grid_spec,
        out_shape=jax.ShapeDtypeStruct(xs.shape, jnp.uint32),
        compiler_params=_cparams("arbitrary"),
        name="experts",
    )(blk_expert, n_used, xs, w_gu, w_down)


COMBINE_TILE = 512


def _combine_kernel(dest_ref, dest_next_ref, x1_ref, route_ref, g_ref, yb_ref, o_ref,
                    buf00, buf01, buf10, buf11, sems, *, final_norm):
    tm = x1_ref.shape[0]
    i = pl.program_id(0)
    bufs = ((buf00, buf01), (buf10, buf11))

    def issue(d_ref, slot):
        def start(t, _):
            for k in range(TOP_K):
                pltpu.make_async_copy(_slab_rows(yb_ref, d_ref[0, 0, k * tm + t]),
                                      _slab_rows(bufs[slot][k], t), sems.at[slot]).start(priority=k % 2)
            return 0

        lax.fori_loop(0, tm, start, 0, unroll=ROW_DMA_UNROLL)

    @pl.when(i == 0)
    def _():
        issue(dest_ref, 0)

    for slot in range(2):
        @pl.when((i % 2 == slot) & (i + 1 < pl.num_programs(0)))
        def _(slot=slot):
            issue(dest_next_ref, 1 - slot)

    for slot in range(2):
        @pl.when(i % 2 == slot)
        def _(slot=slot):
            for b in bufs[slot]:
                pltpu.make_async_copy(b, b, sems.at[slot]).wait()
            route = route_ref[...]
            x2 = (x1_ref[...] + route[:, 2:3] * _unpack_rows(_slab_load(bufs[slot][0]))
                  + route[:, 3:4] * _unpack_rows(_slab_load(bufs[slot][1])))
            o_ref[...] = _rms(x2, g_ref[...]) if final_norm else x2


def _combine(x1, route, dest, yb, final_g, final_norm):
    n, d = x1.shape
    tm = COMBINE_TILE
    n_tiles = n // tm
    dest3 = _dest_tiles(dest, tm)
    return pl.pallas_call(
        functools.partial(_combine_kernel, final_norm=final_norm),
        grid=(n_tiles,),
        in_specs=[pl.BlockSpec((1, 1, TOP_K * tm), lambda i: (i, 0, 0), memory_space=pltpu.SMEM),
                  pl.BlockSpec((1, 1, TOP_K * tm), lambda i: (jnp.minimum(i + 1, n_tiles - 1), 0, 0),
                               memory_space=pltpu.SMEM),
                  pl.BlockSpec((tm, d), lambda i: (i, 0)),
                  pl.BlockSpec((tm, ROUTE_W), lambda i: (i, 0)),
                  pl.BlockSpec((1, d), lambda i: (0, 0)),
                  pl.BlockSpec(memory_space=pl.ANY)],
        out_specs=pl.BlockSpec((tm, d), lambda i: (i, 0)),
        out_shape=jax.ShapeDtypeStruct((n, d), F32),
        scratch_shapes=[pltpu.VMEM((tm * SLABS, LANES), jnp.uint32) for _ in range(2 * TOP_K)]
                       + [pltpu.SemaphoreType.DMA((2,))],
        compiler_params=_cparams("arbitrary"),
        name="combine",
    )(dest3, dest3, x1, route, final_g, yb)


def _rwkv_params(rw_mu, rw_w0, rw_w_up, rw_a0, rw_a_up, rw_g_up, rw_k_k, rw_k_a, rw_r_k, rw_gn_w, rw_gn_b):
    row = lambda v: v.reshape(1, -1).astype(F32)
    zeros = jnp.zeros((A_LORA, RW_DIM), F32)
    return {
        "mu_r": row(rw_mu[:RW_DIM]), "mu_k": row(rw_mu[RW_DIM:2 * RW_DIM]),
        "mu_v": row(rw_mu[2 * RW_DIM:3 * RW_DIM]), "mu_l": row(rw_mu[3 * RW_DIM:]),
        "w0": row(rw_w0), "a0": row(rw_a0), "k_k": row(rw_k_k), "k_a": row(rw_k_a),
        "r_k": row(rw_r_k), "gn_w": row(rw_gn_w), "gn_b": row(rw_gn_b),
        "w_up": jnp.concatenate([rw_w_up, zeros], axis=0).astype(BF16),
        "a_up": jnp.concatenate([zeros, rw_a_up], axis=0).astype(BF16),
        "g_up": rw_g_up.astype(BF16),
    }


def _mla_params(g_qa, w_q_up, g_kva, w_kv_up):
    half = QK_ROPE // 2
    pad = MLA_SLOT - QK_NOPE - QK_ROPE
    wq = w_q_up.reshape(Q_LORA, MLA_HEADS, QK_NOPE + QK_ROPE)
    q_nope, q_r1, q_r2 = wq[..., :QK_NOPE], wq[..., QK_NOPE:QK_NOPE + half], wq[..., QK_NOPE + half:]
    zq = lambda w: jnp.zeros((Q_LORA, MLA_HEADS, w), F32)
    w_qa = jnp.concatenate([q_nope, q_r1, q_r2, zq(pad)], axis=-1).reshape(Q_LORA, -1)
    w_qb = jnp.concatenate([zq(QK_NOPE), -q_r2, q_r1, zq(pad)], axis=-1).reshape(Q_LORA, -1)
    wkv = w_kv_up.reshape(KV_LORA, MLA_HEADS, QK_NOPE + V_HEAD)
    w_k = jnp.concatenate([wkv[..., :QK_NOPE], jnp.zeros((KV_LORA, MLA_HEADS, MLA_SLOT - QK_NOPE), F32)],
                          axis=-1).reshape(KV_LORA, -1)
    w_v = wkv[..., QK_NOPE:].reshape(KV_LORA, -1)
    eye = jnp.eye(half, dtype=F32)
    z = jnp.zeros((half, half), F32)
    zl = jnp.zeros((QK_ROPE, QK_NOPE), F32)
    zr = jnp.zeros((QK_ROPE, pad), F32)
    p_a = jnp.concatenate([zl, jnp.concatenate([eye, z], 0), jnp.concatenate([z, eye], 0), zr], axis=1)
    p_b = jnp.concatenate([zl, jnp.concatenate([z, -eye], 0), jnp.concatenate([eye, z], 0), zr], axis=1)
    place_half = jnp.concatenate([jnp.zeros((half, QK_NOPE), F32), eye, eye, jnp.zeros((half, pad), F32)], axis=1)
    zh = jnp.zeros_like(place_half)
    place = jnp.concatenate([jnp.concatenate([place_half, zh], 1), jnp.concatenate([zh, place_half], 1)], 0)
    one = jnp.concatenate([jnp.ones((1, QK_NOPE), F32), jnp.zeros((1, MLA_SLOT - QK_NOPE), F32)], axis=1)
    return {"g_qa": g_qa.reshape(1, -1), "g_kva": g_kva.reshape(1, -1),
            "w_qa": w_qa.astype(BF16), "w_qb": w_qb.astype(BF16), "w_k": w_k.astype(BF16),
            "w_v": w_v.astype(BF16), "p_a": p_a.astype(BF16), "p_b": p_b.astype(BF16),
            "place": place.astype(BF16), "one": one}


def _rope_cos_sin(positions):
    inv_freq = ROPE_THETA ** (-jnp.arange(0, QK_ROPE, 2, dtype=F32) / QK_ROPE)
    ang = positions.astype(F32).reshape(-1, 1) * inv_freq
    return jnp.concatenate([jnp.cos(ang), jnp.sin(ang)], axis=1)


def _block_layout(hist, n_assign):
    tile_counts = hist[:, :, 0].astype(jnp.int32)
    counts = jnp.sum(tile_counts, axis=0)
    padded = (counts + EXPERT_BLOCK - 1) // EXPERT_BLOCK * EXPERT_BLOCK
    pad_end = jnp.cumsum(padded)
    pad_start = pad_end - padded
    tile_base = jnp.cumsum(tile_counts, axis=0) - tile_counts + pad_start[None, :]
    base = jnp.broadcast_to(tile_base.astype(F32)[:, :, None], tile_base.shape + (ROUTE_W,))
    n_blocks = -(-n_assign // EXPERT_BLOCK) + N_EXPERTS
    blk_row = jnp.arange(n_blocks, dtype=jnp.int32) * EXPERT_BLOCK
    blk_expert = jnp.minimum(jnp.sum((pad_end[None, :] <= blk_row[:, None]).astype(jnp.int32), axis=1),
                             N_EXPERTS - 1)
    n_used = (pad_end[-1] // EXPERT_BLOCK).astype(jnp.int32).reshape(1)
    return base, blk_expert, n_used, n_blocks, pad_end.astype(jnp.int32)


def kernel(x, positions, mix_norm_g, w_in, rw_mu, rw_w0, rw_w_up, rw_a0, rw_a_up, rw_g_up, rw_k_k, rw_k_a, rw_r_k, rw_gn_w, rw_gn_b, mla_g_qa, mla_w_q_up, mla_g_kva, mla_w_kv_up, w_branch_rw, w_branch_mla, w_out, ffn_norm_g, moe_w_group, moe_b_group, moe_w_router, moe_b_router, moe_w_gu, moe_w_down, final_norm_g):
    batch, seq, d = x.shape
    assert d == 2 * SLABS * LANES
    n = batch * seq
    depth = w_in.shape[0]
    rw_cols = 3 * RW_DIM + W_LORA + A_LORA + G_LORA
    mla_cols = Q_LORA + KV_LORA + QK_ROPE
    cos_sin = _rope_cos_sin(positions)
    x2 = x.reshape(n, d)

    for l in range(depth):
        cut = (0, rw_cols, rw_cols + mla_cols, w_in.shape[2])
        w_rw, w_mla, w_gate = (w_in[l][:, cut[j]:cut[j + 1]].astype(BF16) for j in range(3))
        c_rw, c_mla, gates = _in_proj(x2, mix_norm_g[l].reshape(1, d), w_rw, w_mla, w_gate, tm=ROW_TILE)

        rp = _rwkv_params(rw_mu[l], rw_w0[l], rw_w_up[l], rw_a0[l], rw_a_up[l], rw_g_up[l], rw_k_k[l],
                          rw_k_a[l], rw_r_k[l], rw_gn_w[l], rw_gn_b[l])
        y_rw = _rwkv(c_rw.reshape(batch, seq, rw_cols), rp, batch, seq).reshape(n, RW_DIM)

        mp = _mla_params(mla_g_qa[l], mla_w_q_up[l], mla_g_kva[l], mla_w_kv_up[l])
        q_t, k, v_t = _mla_prep(c_mla, cos_sin, mp, ROW_TILE, batch, seq)
        y_mla = _mla_attn(q_t, k.reshape(batch, seq, -1), v_t, batch, seq).reshape(n, MLA_HEADS * V_HEAD)

        gap, tail = ROUTE_FINE0 - N_GROUPS, ROUTE_W - ROUTE_FINE0 - N_EXPERTS
        w_route = jnp.concatenate(
            [moe_w_group[l], jnp.zeros((d, gap), F32), moe_w_router[l], jnp.zeros((d, tail), F32)], axis=1)
        b_route = jnp.concatenate(
            [moe_b_group[l], jnp.zeros((gap,), F32), moe_b_router[l], jnp.zeros((tail,), F32)]).reshape(1, -1)
        wr_hi = w_route.astype(BF16)
        wr_lo = (w_route - wr_hi.astype(F32)).astype(BF16)
        mparams = {"w_br": w_branch_rw[l].astype(BF16), "w_bm": w_branch_mla[l].astype(BF16),
                   "w_out": w_out[l].astype(BF16), "ffn_g": ffn_norm_g[l].reshape(1, d),
                   "w_route": jnp.concatenate([wr_hi, wr_lo], axis=1), "b_route": b_route}
        x1, h2p, route, route_t, hist = _merge(x2, y_rw, y_mla, gates, mparams, tm=ROW_TILE)

        base, blk_expert, n_used, n_blocks, pad_end = _block_layout(hist, n * TOP_K)
        dest = _plan(route_t, base, tm=ROW_TILE)
        xs = _dispatch(h2p, dest, pad_end, n_blocks * EXPERT_BLOCK)
        yb = _experts(xs, blk_expert, n_used, moe_w_gu[l].astype(BF16), moe_w_down[l].astype(BF16))
        x2 = _combine(x1, route, dest, yb, final_norm_g.reshape(1, d), final_norm=(l == depth - 1))

    return x2.reshape(batch, seq, d)
```

```python
import functools
import math

import jax
import jax.numpy as jnp
from jax import lax
from jax.experimental import pallas as pl
from jax.experimental.pallas import tpu as pltpu

F32 = jnp.float32
BF16 = jnp.bfloat16

RW_HEADS = 8
RW_HEAD_DIM = 64
RW_DIM = RW_HEADS * RW_HEAD_DIM
W_LORA = 64
A_LORA = 64
G_LORA = 128
GN_EPS = 64e-5
MLA_HEADS = 8
QK_NOPE = 64
QK_ROPE = 32
V_HEAD = 64
Q_LORA = 384
KV_LORA = 256
ROPE_THETA = 10000.0
N_GROUPS = 4
EXPERTS_PER_GROUP = 8
N_EXPERTS = N_GROUPS * EXPERTS_PER_GROUP
TOP_K = 2
D_EXPERT = 256
EXPERT_BLOCK = 512
NORM_EPS = 1e-6

LANES = 128
HEAD_PAIR = 2 * RW_HEAD_DIM
VMEM_LIMIT = 48 * 1024 * 1024
ROW_TILE = 512


def _cparams(*sem):
    return pltpu.CompilerParams(dimension_semantics=sem, vmem_limit_bytes=VMEM_LIMIT)


def _mm(a, b, dims=((1,), (0,))):
    return lax.dot_general(a.astype(BF16), b.astype(BF16), (dims, ((), ())), preferred_element_type=F32)


def _mm_sel(sel_bf16, x, dims=((1,), (0,))):
    dn = (dims, ((), ()))
    hi = x.astype(BF16)
    lo = (x - hi.astype(F32)).astype(BF16)
    return (lax.dot_general(sel_bf16, hi, dn, preferred_element_type=F32)
            + lax.dot_general(sel_bf16, lo, dn, preferred_element_type=F32))


def _seg_sum(x, seg_bf16):
    return jnp.dot(x.astype(BF16), seg_bf16, preferred_element_type=F32)


def _rms(x, g):
    return x * lax.rsqrt(jnp.mean(x * x, axis=-1, keepdims=True) + NORM_EPS) * g


def _pack_rows(x):
    half = x.shape[1] // 2
    bits = lambda v: lax.bitcast_convert_type(v.astype(BF16).astype(F32), jnp.uint32)
    return bits(x[:, :half]) | (bits(x[:, half:]) >> 16)


def _unpack_rows(p):
    hi = lax.bitcast_convert_type(p & jnp.uint32(0xFFFF0000), F32)
    lo = lax.bitcast_convert_type(p << 16, F32)
    return jnp.concatenate([hi, lo], axis=1)


SLABS = 4


def _slab_rows(ref, r):
    return ref.at[pl.ds(pl.multiple_of(r * SLABS, SLABS), SLABS), :]


def _slab_load(ref):
    rows = ref.shape[0] // SLABS
    return jnp.concatenate([ref[pl.ds(j, rows, stride=SLABS), :] for j in range(SLABS)], axis=1)


def _slab_store(ref, x):
    rows = ref.shape[0] // SLABS
    for j in range(SLABS):
        ref[pl.ds(j, rows, stride=SLABS), :] = x[:, j * LANES:(j + 1) * LANES]


def _in_proj_kernel(x_ref, g_ref, wrw_ref, wmla_ref, wg_ref, crw_ref, cmla_ref, gate_ref):
    hb = _rms(x_ref[...], g_ref[...]).astype(BF16)
    crw_ref[...] = jnp.dot(hb, wrw_ref[...], preferred_element_type=F32)
    cmla_ref[...] = jnp.dot(hb, wmla_ref[...], preferred_element_type=F32)
    gate_ref[...] = jax.nn.sigmoid(jnp.dot(hb, wg_ref[...], preferred_element_type=F32)).astype(BF16)


def _in_proj(x2, g, w_rw, w_mla, w_gate, tm):
    n, d = x2.shape
    full = lambda w: pl.BlockSpec(w.shape, lambda i: (0, 0))
    row = lambda c: pl.BlockSpec((tm, c), lambda i: (i, 0))
    return pl.pallas_call(
        _in_proj_kernel,
        grid=(n // tm,),
        in_specs=[row(d), full(g), full(w_rw), full(w_mla), full(w_gate)],
        out_specs=[row(w_rw.shape[1]), row(w_mla.shape[1]), row(w_gate.shape[1])],
        out_shape=[jax.ShapeDtypeStruct((n, w_rw.shape[1]), F32),
                   jax.ShapeDtypeStruct((n, w_mla.shape[1]), F32),
                   jax.ShapeDtypeStruct((n, w_gate.shape[1]), BF16)],
        compiler_params=_cparams("parallel"),
        name="in_proj",
    )(x2, g, w_rw, w_mla, w_gate)


RW_CHUNK = 64
RW_TILE = 1024
def _token_shift(cur, halo_ref, first):
    prev_row = jnp.where(first, 0.0, halo_ref[0, 7:8, :])
    rolled = pltpu.roll(cur, 1, 0)
    row = lax.broadcasted_iota(jnp.int32, cur.shape, 0)
    return jnp.where(row == 0, prev_row, rolled)


def _rwkv_kernel(r_ref, k_ref, v_ref, l_ref, hr_ref, hk_ref, hv_ref, hl_ref,
                 mur_ref, muk_ref, muv_ref, mul_ref, w0_ref, a0_ref, kk_ref, ka_ref, rk_ref,
                 gnw_ref, gnb_ref, wup_ref, aup_ref, gup_ref, y_ref, st_ref):
    i = pl.program_id(2)
    first = i == 0

    @pl.when(first)
    def _():
        st_ref[...] = jnp.zeros_like(st_ref)

    def mixed(c_ref, h_ref, mu_ref):
        cur = c_ref[0]
        return cur + (_token_shift(cur, h_ref, first) - cur) * mu_ref[...]

    zr = mixed(r_ref, hr_ref, mur_ref)
    zk = mixed(k_ref, hk_ref, muk_ref)
    zv = mixed(v_ref, hv_ref, muv_ref)
    zl = mixed(l_ref, hl_ref, mul_ref)
    z_wa = zl[:, :LANES]
    z_g = zl[:, LANES:]

    lane = lax.broadcasted_iota(jnp.int32, (LANES, LANES), 1)
    sub = lax.broadcasted_iota(jnp.int32, (LANES, LANES), 0)
    same_head = (lane // RW_HEAD_DIM) == (sub // RW_HEAD_DIM)
    seg = jnp.where(same_head, 1.0, 0.0).astype(BF16)

    w = w0_ref[...] + _mm(jnp.tanh(z_wa), wup_ref[...])
    u = -w
    softplus = jnp.maximum(u, 0.0) + jnp.log(1.0 + jnp.exp(-jnp.abs(u)))
    log_decay = -jnp.exp(-softplus - 0.5)
    a = jax.nn.sigmoid(a0_ref[...] + _mm(z_wa, aup_ref[...]))
    g = _mm(jax.nn.sigmoid(z_g), gup_ref[...])

    kk = zk * kk_ref[...]
    kk = kk / jnp.maximum(jnp.sqrt(_seg_sum(kk * kk, seg)), 1e-12)
    k2 = zk * (1.0 + (a - 1.0) * ka_ref[...])
    bonus = _seg_sum(zr * k2 * rk_ref[...], seg) * zv
    kka = kk * a

    c = RW_CHUNK
    crow = lax.broadcasted_iota(jnp.int32, (c, c), 0)
    ccol = lax.broadcasted_iota(jnp.int32, (c, c), 1)
    cum_sel = jnp.where(crow >= ccol, 1.0, 0.0).astype(BF16)
    tril_incl = sub >= lane
    tril_strict = sub > lane
    eye_l = jnp.where(lane == sub, 1.0, 0.0).astype(F32)
    lo_half = lax.broadcasted_iota(jnp.int32, (c, LANES), 1) < RW_HEAD_DIM
    nt = ((1,), (1,))
    tn = ((0,), (0,))
    zeros_blk = jnp.zeros((2 * c, LANES), BF16)
    zeros_half = jnp.zeros((c, LANES), BF16)

    def stack(t):
        tb = t.astype(BF16)
        return jnp.concatenate([jnp.where(lo_half, tb, zeros_half), jnp.where(lo_half, zeros_half, tb)], axis=0)

    tril_incl2 = jnp.concatenate([tril_incl, tril_incl], axis=1)

    def chunk_stages(ids):
        chunks = range(len(ids))
        x_a, x_b, x_k, x_r, x_v, x_bh, x_kh, w_tot, r_dec = [], [], [], [], [], [], [], [], []
        for ci in ids:
            sl = slice(ci * c, (ci + 1) * c)
            ld = log_decay[sl]
            cum = _mm_sel(cum_sel, ld)
            tot = cum[c - 1:c, :]
            e_neg = jnp.exp(-cum)
            e_rest = jnp.exp(tot - cum)
            x_a.append(stack(-kk[sl] * jnp.exp(cum - ld)))
            x_b.append(stack(kka[sl] * e_neg))
            x_k.append(stack(k2[sl] * e_neg))
            r_dec.append(zr[sl] * jnp.exp(cum))
            x_r.append(stack(r_dec[-1]))
            x_v.append(stack(zv[sl]))
            x_bh.append(stack(kka[sl] * e_rest))
            x_kh.append(stack(k2[sl] * e_rest))
            w_tot.append(jnp.exp(tot))

        inter = [_mm(jnp.concatenate([x_a[i], x_r[i]], axis=0),
                     jnp.concatenate([x_b[i], x_k[i]], axis=0), nt) for i in chunks]
        inter = [m.astype(BF16) for m in inter]
        zeros_sq_b = jnp.zeros((LANES, LANES), BF16)
        a_ab = [jnp.where(tril_strict, m[:2 * c, :2 * c], zeros_sq_b) for m in inter]
        a_ak = [jnp.where(tril_strict, m[:2 * c, 2 * c:], zeros_sq_b) for m in inter]
        a_r = [jnp.where(tril_incl2, m[2 * c:], jnp.concatenate([zeros_sq_b, zeros_sq_b], axis=1))
               for m in inter]
        w_ak = [_mm(a_ak[i], x_v[i]).astype(BF16) for i in chunks]

        t_inv = [eye_l + m.astype(F32) for m in a_ab]
        pw = a_ab
        for _ in range(int(math.log2(c)) - 1):
            pw = [_mm(m, m).astype(BF16) for m in pw]
            t_inv = [t_inv[i] + _mm(t_inv[i], pw[i]) for i in chunks]

        solved = [_mm(t_inv[i], jnp.concatenate([x_a[i], w_ak[i]], axis=1)).astype(BF16)
                  for i in chunks]
        rhs = [jnp.concatenate([solved[i], jnp.concatenate([zeros_blk, x_v[i]], axis=1)], axis=0)
               for i in chunks]
        out = [_mm(a_r[i], rhs[i]) for i in chunks]
        carry = [_mm(jnp.concatenate([x_bh[i], x_kh[i]], axis=0), rhs[i], tn) for i in chunks]
        q_hat, y_loc = [], []
        for i in chunks:
            q_hat.append(r_dec[i] + out[i][:c, :LANES] + out[i][c:, :LANES])
            y_loc.append(out[i][:c, LANES:] + out[i][c:, LANES:])
        trans = [jnp.concatenate([eye_l * w_tot[i] + carry[i][:, :LANES], carry[i][:, LANES:]], axis=1)
                 for i in chunks]
        return q_hat, y_loc, trans

    n_chunks = RW_TILE // c
    chunks = range(n_chunks)
    q_hat, y_loc, trans = chunk_stages(chunks)

    zeros_sq = jnp.zeros((LANES, LANES), F32)

    def compose(later, earlier):
        return _mm(later[:, :LANES], earlier) + jnp.concatenate([zeros_sq, later[:, LANES:]], axis=1)

    scan = list(trans)
    dist = 1
    while dist < len(scan):
        scan = [scan[i] if i < dist else compose(scan[i], scan[i - dist]) for i in range(len(scan))]
        dist *= 2
    prefix = [None] + scan
    q_pre = [None] + [_mm(q_hat[i], prefix[i]) for i in chunks[1:]]
    q_m = jnp.concatenate([q_hat[0]] + [q_pre[i][:, :LANES] for i in chunks[1:]], axis=0)
    y_off = jnp.concatenate([y_loc[0]] + [y_loc[i] + q_pre[i][:, LANES:] for i in chunks[1:]], axis=0)
    state = st_ref[...]
    y = _mm(q_m, state) + y_off
    st_ref[...] = _mm(prefix[-1][:, :LANES], state) + prefix[-1][:, LANES:]

    inv_n = 1.0 / RW_HEAD_DIM
    mean = _seg_sum(y, seg) * inv_n
    d = y - mean
    var = _seg_sum(d * d, seg) * inv_n
    yn = d * lax.rsqrt(var + GN_EPS) * gnw_ref[...] + gnb_ref[...]
    y_ref[0] = ((yn + bonus) * g).astype(y_ref.dtype)


def _rwkv(c_rw, p, batch, seq):
    ts = RW_TILE
    n_pairs = RW_DIM // HEAD_PAIR
    lora_blk = (3 * RW_DIM) // (2 * LANES)
    halo = ts // 8

    def col(off):
        return pl.BlockSpec((1, ts, LANES), lambda b, pp, i, off=off: (b, i, off + pp))

    def col_halo(off):
        return pl.BlockSpec((1, 8, LANES),
                            lambda b, pp, i, off=off: (b, jnp.maximum(i * halo - 1, 0), off + pp))

    vec = pl.BlockSpec((1, LANES), lambda b, pp, i: (0, pp))
    lora_w = pl.BlockSpec((LANES, LANES), lambda b, pp, i: (0, pp))
    in_specs = [
        col(0), col(n_pairs), col(2 * n_pairs),
        pl.BlockSpec((1, ts, 2 * LANES), lambda b, pp, i: (b, i, lora_blk)),
        col_halo(0), col_halo(n_pairs), col_halo(2 * n_pairs),
        pl.BlockSpec((1, 8, 2 * LANES), lambda b, pp, i: (b, jnp.maximum(i * halo - 1, 0), lora_blk)),
        vec, vec, vec, pl.BlockSpec((1, 2 * LANES), lambda b, pp, i: (0, 0)),
        vec, vec, vec, vec, vec, vec, vec, lora_w, lora_w, lora_w,
    ]
    return pl.pallas_call(
        _rwkv_kernel,
        grid=(batch, n_pairs, seq // ts),
        in_specs=in_specs,
        out_specs=pl.BlockSpec((1, ts, LANES), lambda b, pp, i: (b, i, pp)),
        out_shape=jax.ShapeDtypeStruct((batch, seq, RW_DIM), BF16),
        scratch_shapes=[pltpu.VMEM((LANES, LANES), F32)],
        compiler_params=_cparams("parallel", "parallel", "arbitrary"),
        name="rwkv",
    )(c_rw, c_rw, c_rw, c_rw, c_rw, c_rw, c_rw, c_rw,
      p["mu_r"], p["mu_k"], p["mu_v"], p["mu_l"], p["w0"], p["a0"], p["k_k"], p["k_a"], p["r_k"],
      p["gn_w"], p["gn_b"], p["w_up"], p["a_up"], p["g_up"])


MLA_SLOT = 128


def _mla_prep_kernel(cmla_ref, cs_ref, gq_ref, gkv_ref,
                     wqa_ref, wqb_ref, wk_ref, wv_ref, pa_ref, pb_ref, place_ref, one_ref,
                     qt_ref, k_ref, vt_ref):
    cs = cs_ref[...]
    cs_hi = cs.astype(BF16)
    cs_lo = (cs - cs_hi.astype(F32)).astype(BF16)
    tables = (jnp.dot(cs_hi, place_ref[...], preferred_element_type=F32)
              + jnp.dot(cs_lo, place_ref[...], preferred_element_type=F32))
    cos = tables[:, :MLA_SLOT] + one_ref[...]
    sin = tables[:, MLA_SLOT:]
    zq = _rms(cmla_ref[:, :Q_LORA], gq_ref[...]).astype(BF16)
    qa = jnp.dot(zq, wqa_ref[...], preferred_element_type=F32)
    qb = jnp.dot(zq, wqb_ref[...], preferred_element_type=F32)
    ckvr = cmla_ref[:, Q_LORA:]
    zkv = _rms(ckvr[:, :KV_LORA], gkv_ref[...]).astype(BF16)
    kn = jnp.dot(zkv, wk_ref[...], preferred_element_type=F32)
    v = jnp.dot(zkv, wv_ref[...], preferred_element_type=F32)
    for blk in range(v.shape[1] // LANES):
        vt_ref[0, blk * LANES:(blk + 1) * LANES, :] = v[:, blk * LANES:(blk + 1) * LANES].astype(BF16).T
    kr = ckvr[:, KV_LORA:].astype(BF16)
    k_rope = (jnp.dot(kr, pa_ref[...], preferred_element_type=F32) * cos
              + jnp.dot(kr, pb_ref[...], preferred_element_type=F32) * sin)
    scale = math.log2(math.e) / math.sqrt(QK_NOPE + QK_ROPE)
    for h in range(MLA_HEADS):
        sl = slice(h * MLA_SLOT, (h + 1) * MLA_SLOT)
        qt_ref[0, sl, :] = ((qa[:, sl] * cos + qb[:, sl] * sin) * scale).astype(BF16).T
        k_ref[:, sl] = (kn[:, sl] + k_rope).astype(BF16)


def _mla_prep(c_mla, cos_sin, p, tm, batch, seq):
    n = c_mla.shape[0]
    full = lambda w: pl.BlockSpec(w.shape, lambda i: (0, 0))
    row = lambda c: pl.BlockSpec((tm, c), lambda i: (i, 0))
    per_seq = seq // tm
    col = lambda r: pl.BlockSpec((1, r, tm), lambda i: (i // per_seq, 0, i % per_seq))
    ws = [p["g_qa"], p["g_kva"], p["w_qa"], p["w_qb"], p["w_k"], p["w_v"], p["p_a"], p["p_b"],
          p["place"], p["one"]]
    hq = MLA_HEADS * MLA_SLOT
    hv = MLA_HEADS * V_HEAD
    return pl.pallas_call(
        _mla_prep_kernel,
        grid=(n // tm,),
        in_specs=[row(c_mla.shape[1]), row(cos_sin.shape[1])] + [full(w) for w in ws],
        out_specs=[col(hq), row(hq), col(hv)],
        out_shape=[jax.ShapeDtypeStruct((batch, hq, seq), BF16), jax.ShapeDtypeStruct((n, hq), BF16),
                   jax.ShapeDtypeStruct((batch, hv, seq), BF16)],
        compiler_params=_cparams("parallel"),
        name="mla_prep",
    )(c_mla, cos_sin, *ws)


ATT_TILE = 512
ATT_HEADS = 4


def _attn_kernel(qt_ref, k_ref, vt_ref, o_ref):
    qi = pl.program_id(2)
    t = ATT_TILE
    heads = range(ATT_HEADS)
    den_row = (V_HEAD, 0)

    def augment(h, vt):
        r = lax.broadcasted_iota(jnp.int32, vt.shape, 0)
        own = (r < V_HEAD) if h % 2 == 0 else (r >= V_HEAD)
        return jnp.where(own, vt, jnp.where(r == den_row[h % 2], 1.0, 0.0).astype(BF16))

    def block(j, carry, mask):
        keys = pl.ds(pl.multiple_of(j * t, t), t)
        sts = []
        for h in heads:
            qt = qt_ref[0, h * MLA_SLOT:(h + 1) * MLA_SLOT, :]
            kb = k_ref[0, keys, h * MLA_SLOT:(h + 1) * MLA_SLOT]
            st = jnp.dot(kb, qt, preferred_element_type=F32)
            sts.append(st if mask is None else jnp.where(mask, st, -jnp.inf))
        m_new = [jnp.maximum(carry[h][0], jnp.max(sts[h], axis=0, keepdims=True)) for h in heads]
        pts = [jnp.exp2((sts[h] - m_new[h]).astype(BF16)) for h in heads]
        out = []
        for h in heads:
            m, acc = carry[h]
            vt = vt_ref[0, (h // 2) * 2 * V_HEAD:(h // 2 + 1) * 2 * V_HEAD, keys]
            acc = acc * jnp.exp2(m - m_new[h]) + jnp.dot(augment(h, vt), pts[h], preferred_element_type=F32)
            out.append((m_new[h], acc))
        return tuple(out)

    init1 = (jnp.full((1, t), -jnp.inf, F32), jnp.zeros((2 * V_HEAD, t), F32))
    carry = lax.fori_loop(0, qi, lambda j, c: block(j, c, None), tuple(init1 for _ in heads))

    causal = lax.broadcasted_iota(jnp.int32, (t, t), 0) <= lax.broadcasted_iota(jnp.int32, (t, t), 1)
    accs = [mc[1] for mc in block(qi, carry, causal)]
    lo_rows = lax.broadcasted_iota(jnp.int32, (2 * V_HEAD, t), 0) < V_HEAD
    for p in range(ATT_HEADS // 2):
        acc0, acc1 = accs[2 * p], accs[2 * p + 1]
        den0 = acc0[den_row[0]:den_row[0] + 1, :]
        den1 = acc1[den_row[1]:den_row[1] + 1, :]
        out_t = jnp.where(lo_rows, acc0 / den0, acc1 / den1)
        o_ref[0, :, p * 2 * V_HEAD:(p + 1) * 2 * V_HEAD] = out_t.T.astype(o_ref.dtype)


def _mla_attn(qt, k, vt, batch, seq):
    t = ATT_TILE
    g = ATT_HEADS
    return pl.pallas_call(
        _attn_kernel,
        grid=(batch, MLA_HEADS // g, seq // t),
        in_specs=[pl.BlockSpec((1, g * MLA_SLOT, t), lambda b, hp, i: (b, hp, i)),
                  pl.BlockSpec((1, seq, g * MLA_SLOT), lambda b, hp, i: (b, 0, hp)),
                  pl.BlockSpec((1, g * V_HEAD, seq), lambda b, hp, i: (b, hp, 0))],
        out_specs=pl.BlockSpec((1, t, g * V_HEAD), lambda b, hp, i: (b, i, hp)),
        out_shape=jax.ShapeDtypeStruct((batch, seq, MLA_HEADS * V_HEAD), BF16),
        compiler_params=_cparams("parallel", "parallel", "arbitrary"),
        name="mla_attn",
    )(qt, k, vt)


ROUTE_W = 128
ROUTE_FINE0 = 8


def _merge_kernel(x_ref, yrw_ref, ymla_ref, gate_ref, wbr_ref, wbm_ref, wo_ref, fg_ref,
                  wr_ref, br_ref, x1_ref, h2p_ref, route_ref, route_t_ref, hist_ref):
    d = x_ref.shape[1]
    a = jnp.dot(yrw_ref[...], wbr_ref[...], preferred_element_type=F32)
    b = jnp.dot(ymla_ref[...], wbm_ref[...], preferred_element_type=F32)
    merged = gate_ref[:, :d].astype(F32) * a + gate_ref[:, d:].astype(F32) * b
    x1 = x_ref[...] + jnp.dot(merged.astype(BF16), wo_ref[...], preferred_element_type=F32)
    x1_ref[...] = x1
    h2 = _rms(x1, fg_ref[...])
    _slab_store(h2p_ref, _pack_rows(h2))

    h_hi = h2.astype(BF16)
    h_lo = (h2 - h_hi.astype(F32)).astype(BF16)
    both = jnp.dot(h_hi, wr_ref[...], preferred_element_type=F32)
    logits = (both[:, :ROUTE_W] + both[:, ROUTE_W:]
              + jnp.dot(h_lo, wr_ref[:, :ROUTE_W], preferred_element_type=F32)) + br_ref[...]

    tm = logits.shape[0]
    lt = logits.T
    sub = lax.broadcasted_iota(jnp.int32, (EXPERTS_PER_GROUP, tm), 0)
    big = jnp.int32(EXPERTS_PER_GROUP)
    neg = -jnp.inf

    def first_argmax(vals, vmax):
        return jnp.min(jnp.where(vals == vmax, sub, big), axis=0, keepdims=True)

    grp = jnp.where(sub < N_GROUPS, lt[:EXPERTS_PER_GROUP], neg)
    g_max = jnp.max(grp, axis=0, keepdims=True)
    g_den = jnp.sum(jnp.exp(grp - g_max), axis=0, keepdims=True)
    g_sel = first_argmax(grp, g_max)
    gate_g = 1.0 / g_den
    fine = lt[ROUTE_FINE0:ROUTE_FINE0 + EXPERTS_PER_GROUP]
    for g in range(1, N_GROUPS):
        lo = ROUTE_FINE0 + g * EXPERTS_PER_GROUP
        fine = jnp.where(g_sel == g, lt[lo:lo + EXPERTS_PER_GROUP], fine)
    v1 = jnp.max(fine, axis=0, keepdims=True)
    i1 = first_argmax(fine, v1)
    fine2 = jnp.where(sub == i1, neg, fine)
    v2 = jnp.max(fine2, axis=0, keepdims=True)
    i2 = first_argmax(fine2, v2)
    e2 = jnp.exp(v2 - v1)
    den = 1.0 + e2
    w1 = gate_g / den
    w2 = gate_g * e2 / den
    base_e = g_sel * EXPERTS_PER_GROUP
    route_t = jnp.where(sub == 0, (base_e + i1).astype(F32),
                        jnp.where(sub == 1, (base_e + i2).astype(F32),
                                  jnp.where(sub == 2, w1, jnp.where(sub == 3, w2, 0.0))))
    pad = jnp.zeros((ROUTE_W - EXPERTS_PER_GROUP, tm), F32)
    route_t_ref[...] = route_t
    route_ref[...] = jnp.concatenate([route_t, pad], axis=0).T
    picked = (sub == i1) | (sub == i2)
    for g in range(N_GROUPS):
        cnt = jnp.sum(jnp.where(picked & (g_sel == g), 1.0, 0.0), axis=1, keepdims=True)
        hist_ref[0, g * EXPERTS_PER_GROUP:(g + 1) * EXPERTS_PER_GROUP, :] = jnp.broadcast_to(
            cnt, (EXPERTS_PER_GROUP, ROUTE_W))


def _merge(x2, y_rw, y_mla, gates, p, tm):
    n, d = x2.shape
    full = lambda w: pl.BlockSpec(w.shape, lambda i: (0, 0))
    row = lambda c: pl.BlockSpec((tm, c), lambda i: (i, 0))
    ws = [p["w_br"], p["w_bm"], p["w_out"], p["ffn_g"], p["w_route"], p["b_route"]]
    return pl.pallas_call(
        _merge_kernel,
        grid=(n // tm,),
        in_specs=[row(d), row(y_rw.shape[1]), row(y_mla.shape[1]), row(2 * d)] + [full(w) for w in ws],
        out_specs=[row(d), pl.BlockSpec((tm * SLABS, LANES), lambda i: (i, 0)), row(ROUTE_W),
                   pl.BlockSpec((EXPERTS_PER_GROUP, tm), lambda i: (0, i)),
                   pl.BlockSpec((1, N_EXPERTS, ROUTE_W), lambda i: (i, 0, 0))],
        out_shape=[jax.ShapeDtypeStruct((n, d), F32), jax.ShapeDtypeStruct((n * SLABS, LANES), jnp.uint32),
                   jax.ShapeDtypeStruct((n, ROUTE_W), F32),
                   jax.ShapeDtypeStruct((EXPERTS_PER_GROUP, n), F32),
                   jax.ShapeDtypeStruct((n // tm, N_EXPERTS, ROUTE_W), F32)],
        compiler_params=_cparams("parallel"),
        name="merge_route",
    )(x2, y_rw, y_mla, gates, *ws)


def _plan_kernel(route_t_ref, base_ref, dest_ref):
    tm = route_t_ref.shape[1]
    rt = route_t_ref[...]
    expert = lax.broadcasted_iota(jnp.int32, (N_EXPERTS, tm), 0).astype(F32)
    pick = [expert == rt[k:k + 1, :] for k in range(TOP_K)]
    both = jnp.where(pick[0] | pick[1], 1.0, 0.0).astype(BF16)
    r = lax.broadcasted_iota(jnp.int32, (tm, tm), 0)
    c = lax.broadcasted_iota(jnp.int32, (tm, tm), 1)
    earlier = jnp.where(r < c, 1.0, 0.0).astype(BF16)
    offs = jnp.dot(both, earlier, preferred_element_type=F32) + base_ref[0][:, :1]
    rows = [jnp.sum(jnp.where(pk, offs, 0.0), axis=0, keepdims=True) for pk in pick]
    sub = lax.broadcasted_iota(jnp.int32, dest_ref.shape, 0)
    dest_ref[...] = jnp.where(sub == 0, rows[0], jnp.where(sub == 1, rows[1], 0.0)).astype(jnp.int32)


def _plan(route_t, base, tm):
    n = route_t.shape[1]
    return pl.pallas_call(
        _plan_kernel,
        grid=(n // tm,),
        in_specs=[pl.BlockSpec((EXPERTS_PER_GROUP, tm), lambda i: (0, i)),
                  pl.BlockSpec((1, N_EXPERTS, ROUTE_W), lambda i: (i, 0, 0))],
        out_specs=pl.BlockSpec((EXPERTS_PER_GROUP, tm), lambda i: (0, i)),
        out_shape=jax.ShapeDtypeStruct((EXPERTS_PER_GROUP, n), jnp.int32),
        compiler_params=_cparams("parallel"),
        name="route_plan",
    )(route_t, base)


def _dest_tiles(dest_t, tm):
    n = dest_t.shape[1]
    return dest_t[:TOP_K].reshape(TOP_K, n // tm, tm).transpose(1, 0, 2).reshape(n // tm, 1, TOP_K * tm)


DISPATCH_TILE = 1024
ROW_DMA_UNROLL = 8


def _dispatch_kernel(pad_end_ref, dest_ref, h_ref, xs_ref, zbuf, sem, zsem):
    tm = h_ref.shape[0] // SLABS

    @pl.when(pl.program_id(0) == 0)
    def _():
        zbuf[...] = jnp.zeros_like(zbuf)

        def tail(e):
            first = pl.multiple_of((pad_end_ref[e] - EXPERT_BLOCK) * SLABS, EXPERT_BLOCK * SLABS)
            return pltpu.make_async_copy(zbuf, xs_ref.at[pl.ds(first, EXPERT_BLOCK * SLABS), :], zsem)

        def region_rows(e):
            return pad_end_ref[e] - (pad_end_ref[e - 1] if e else 0)

        n_blocks = xs_ref.shape[0] // (EXPERT_BLOCK * SLABS)
        used_rows = pad_end_ref[N_EXPERTS - 1]

        def spare(b):
            return pltpu.make_async_copy(zbuf, xs_ref.at[pl.ds(b * EXPERT_BLOCK * SLABS, EXPERT_BLOCK * SLABS), :],
                                         zsem)

        spare_blocks = range(n_blocks - N_EXPERTS, n_blocks)
        for e in range(N_EXPERTS):
            @pl.when(region_rows(e) > 0)
            def _(e=e):
                tail(e).start()
        for b in spare_blocks:
            @pl.when(b * EXPERT_BLOCK >= used_rows)
            def _(b=b):
                spare(b).start()
        for e in range(N_EXPERTS):
            @pl.when(region_rows(e) > 0)
            def _(e=e):
                tail(e).wait()
        for b in spare_blocks:
            @pl.when(b * EXPERT_BLOCK >= used_rows)
            def _(b=b):
                spare(b).wait()

    def start(t, _):
        for k in range(TOP_K):
            pltpu.make_async_copy(_slab_rows(h_ref, t), _slab_rows(xs_ref, dest_ref[0, 0, k * tm + t]),
                                  sem).start(priority=k % 2)
        return 0

    lax.fori_loop(0, tm, start, 0, unroll=ROW_DMA_UNROLL)
    all_rows = xs_ref.at[pl.ds(0, TOP_K * tm * SLABS), :]
    pltpu.make_async_copy(all_rows, all_rows, sem).wait()


def _dispatch(h2, dest, pad_end, p_rows):
    n = h2.shape[0] // SLABS
    tm = DISPATCH_TILE
    dest3 = _dest_tiles(dest, tm)
    grid_spec = pltpu.PrefetchScalarGridSpec(
        num_scalar_prefetch=1,
        grid=(n // tm,),
        in_specs=[pl.BlockSpec((1, 1, TOP_K * tm), lambda i, pe: (i, 0, 0), memory_space=pltpu.SMEM),
                  pl.BlockSpec((tm * SLABS, LANES), lambda i, pe: (i, 0))],
        out_specs=pl.BlockSpec(memory_space=pl.ANY),
        scratch_shapes=[pltpu.VMEM((EXPERT_BLOCK * SLABS, LANES), jnp.uint32),
                        pltpu.SemaphoreType.DMA(()), pltpu.SemaphoreType.DMA(())],
    )
    return pl.pallas_call(
        _dispatch_kernel,
        grid_spec=grid_spec,
        out_shape=jax.ShapeDtypeStruct((p_rows * SLABS, LANES), jnp.uint32),
        compiler_params=_cparams("arbitrary"),
        name="dispatch",
    )(pad_end, dest3, h2)


def _expert_kernel(blk_e_ref, n_used_ref, x_ref, wgu_ref, wd_ref, y_ref):
    del blk_e_ref

    @pl.when(pl.program_id(0) < n_used_ref[0])
    def _():
        x = _unpack_rows(_slab_load(x_ref)).astype(BF16)
        h = jnp.dot(x, wgu_ref[0], preferred_element_type=F32)
        gt = h[:, :D_EXPERT]
        up = h[:, D_EXPERT:]
        act = (gt * jax.nn.sigmoid(gt) * up).astype(BF16)
        _slab_store(y_ref, _pack_rows(jnp.dot(act, wd_ref[0], preferred_element_type=F32)))

    @pl.when(pl.program_id(0) >= n_used_ref[0])
    def _():
        y_ref[...] = jnp.zeros_like(y_ref)


def _experts(xs, blk_expert, n_used, w_gu, w_down):
    p_rows = xs.shape[0] // SLABS
    d = 2 * SLABS * LANES
    n_blocks = p_rows // EXPERT_BLOCK
    rows = pl.BlockSpec((EXPERT_BLOCK * SLABS, LANES), lambda i, be, nu: (i, 0))
    grid_spec = pltpu.PrefetchScalarGridSpec(
        num_scalar_prefetch=2,
        grid=(n_blocks,),
        in_specs=[rows,
                  pl.BlockSpec((1, d, 2 * D_EXPERT), lambda i, be, nu: (be[i], 0, 0)),
                  pl.BlockSpec((1, D_EXPERT, d), lambda i, be, nu: (be[i], 0, 0))],
        out_specs=rows,
    )
    return pl.pallas_call(
        _expert_kernel,
        grid_spec=grid_spec,
        out_shape=jax.ShapeDtypeStruct(xs.shape, jnp.uint32),
        compiler_params=_cparams("arbitrary"),
        name="experts",
    )(blk_expert, n_used, xs, w_gu, w_down)


COMBINE_TILE = 512


def _combine_kernel(dest_ref, dest_next_ref, x1_ref, route_ref, g_ref, yb_ref, o_ref,
                    buf00, buf01, buf10, buf11, sems, *, final_norm):
    tm = x1_ref.shape[0]
    i = pl.program_id(0)
    bufs = ((buf00, buf01), (buf10, buf11))

    def issue(d_ref, slot):
        def start(t, _):
            for k in range(TOP_K):
                pltpu.make_async_copy(_slab_rows(yb_ref, d_ref[0, 0, k * tm + t]),
                                      _slab_rows(bufs[slot][k], t), sems.at[slot]).start(priority=k % 2)
            return 0

        lax.fori_loop(0, tm, start, 0, unroll=ROW_DMA_UNROLL)

    @pl.when(i == 0)
    def _():
        issue(dest_ref, 0)

    for slot in range(2):
        @pl.when((i % 2 == slot) & (i + 1 < pl.num_programs(0)))
        def _(slot=slot):
            issue(dest_next_ref, 1 - slot)

    for slot in range(2):
        @pl.when(i % 2 == slot)
        def _(slot=slot):
            for b in bufs[slot]:
                pltpu.make_async_copy(b, b, sems.at[slot]).wait()
            route = route_ref[...]
            x2 = (x1_ref[...] + route[:, 2:3] * _unpack_rows(_slab_load(bufs[slot][0]))
                  + route[:, 3:4] * _unpack_rows(_slab_load(bufs[slot][1])))
            o_ref[...] = _rms(x2, g_ref[...]) if final_norm else x2


def _combine(x1, route, dest, yb, final_g, final_norm):
    n, d = x1.shape
    tm = COMBINE_TILE
    n_tiles = n // tm
    dest3 = _dest_tiles(dest, tm)
    return pl.pallas_call(
        functools.partial(_combine_kernel, final_norm=final_norm),
        grid=(n_tiles,),
        in_specs=[pl.BlockSpec((1, 1, TOP_K * tm), lambda i: (i, 0, 0), memory_space=pltpu.SMEM),
                  pl.BlockSpec((1, 1, TOP_K * tm), lambda i: (jnp.minimum(i + 1, n_tiles - 1), 0, 0),
                               memory_space=pltpu.SMEM),
                  pl.BlockSpec((tm, d), lambda i: (i, 0)),
                  pl.BlockSpec((tm, ROUTE_W), lambda i: (i, 0)),
                  pl.BlockSpec((1, d), lambda i: (0, 0)),
                  pl.BlockSpec(memory_space=pl.ANY)],
        out_specs=pl.BlockSpec((tm, d), lambda i: (i, 0)),
        out_shape=jax.ShapeDtypeStruct((n, d), F32),
        scratch_shapes=[pltpu.VMEM((tm * SLABS, LANES), jnp.uint32) for _ in range(2 * TOP_K)]
                       + [pltpu.SemaphoreType.DMA((2,))],
        compiler_params=_cparams("arbitrary"),
        name="combine",
    )(dest3, dest3, x1, route, final_g, yb)


def _rwkv_params(rw_mu, rw_w0, rw_w_up, rw_a0, rw_a_up, rw_g_up, rw_k_k, rw_k_a, rw_r_k, rw_gn_w, rw_gn_b):
    row = lambda v: v.reshape(1, -1).astype(F32)
    zeros = jnp.zeros((A_LORA, RW_DIM), F32)
    return {
        "mu_r": row(rw_mu[:RW_DIM]), "mu_k": row(rw_mu[RW_DIM:2 * RW_DIM]),
        "mu_v": row(rw_mu[2 * RW_DIM:3 * RW_DIM]), "mu_l": row(rw_mu[3 * RW_DIM:]),
        "w0": row(rw_w0), "a0": row(rw_a0), "k_k": row(rw_k_k), "k_a": row(rw_k_a),
        "r_k": row(rw_r_k), "gn_w": row(rw_gn_w), "gn_b": row(rw_gn_b),
        "w_up": jnp.concatenate([rw_w_up, zeros], axis=0).astype(BF16),
        "a_up": jnp.concatenate([zeros, rw_a_up], axis=0).astype(BF16),
        "g_up": rw_g_up.astype(BF16),
    }


def _mla_params(g_qa, w_q_up, g_kva, w_kv_up):
    half = QK_ROPE // 2
    pad = MLA_SLOT - QK_NOPE - QK_ROPE
    wq = w_q_up.reshape(Q_LORA, MLA_HEADS, QK_NOPE + QK_ROPE)
    q_nope, q_r1, q_r2 = wq[..., :QK_NOPE], wq[..., QK_NOPE:QK_NOPE + half], wq[..., QK_NOPE + half:]
    zq = lambda w: jnp.zeros((Q_LORA, MLA_HEADS, w), F32)
    w_qa = jnp.concatenate([q_nope, q_r1, q_r2, zq(pad)], axis=-1).reshape(Q_LORA, -1)
    w_qb = jnp.concatenate([zq(QK_NOPE), -q_r2, q_r1, zq(pad)], axis=-1).reshape(Q_LORA, -1)
    wkv = w_kv_up.reshape(KV_LORA, MLA_HEADS, QK_NOPE + V_HEAD)
    w_k = jnp.concatenate([wkv[..., :QK_NOPE], jnp.zeros((KV_LORA, MLA_HEADS, MLA_SLOT - QK_NOPE), F32)],
                          axis=-1).reshape(KV_LORA, -1)
    w_v = wkv[..., QK_NOPE:].reshape(KV_LORA, -1)
    eye = jnp.eye(half, dtype=F32)
    z = jnp.zeros((half, half), F32)
    zl = jnp.zeros((QK_ROPE, QK_NOPE), F32)
    zr = jnp.zeros((QK_ROPE, pad), F32)
    p_a = jnp.concatenate([zl, jnp.concatenate([eye, z], 0), jnp.concatenate([z, eye], 0), zr], axis=1)
    p_b = jnp.concatenate([zl, jnp.concatenate([z, -eye], 0), jnp.concatenate([eye, z], 0), zr], axis=1)
    place_half = jnp.concatenate([jnp.zeros((half, QK_NOPE), F32), eye, eye, jnp.zeros((half, pad), F32)], axis=1)
    zh = jnp.zeros_like(place_half)
    place = jnp.concatenate([jnp.concatenate([place_half, zh], 1), jnp.concatenate([zh, place_half], 1)], 0)
    one = jnp.concatenate([jnp.ones((1, QK_NOPE), F32), jnp.zeros((1, MLA_SLOT - QK_NOPE), F32)], axis=1)
    return {"g_qa": g_qa.reshape(1, -1), "g_kva": g_kva.reshape(1, -1),
            "w_qa": w_qa.astype(BF16), "w_qb": w_qb.astype(BF16), "w_k": w_k.astype(BF16),
            "w_v": w_v.astype(BF16), "p_a": p_a.astype(BF16), "p_b": p_b.astype(BF16),
            "place": place.astype(BF16), "one": one}


def _rope_cos_sin(positions):
    inv_freq = ROPE_THETA ** (-jnp.arange(0, QK_ROPE, 2, dtype=F32) / QK_ROPE)
    ang = positions.astype(F32).reshape(-1, 1) * inv_freq
    return jnp.concatenate([jnp.cos(ang), jnp.sin(ang)], axis=1)


def _block_layout(hist, n_assign):
    tile_counts = hist[:, :, 0].astype(jnp.int32)
    counts = jnp.sum(tile_counts, axis=0)
    padded = (counts + EXPERT_BLOCK - 1) // EXPERT_BLOCK * EXPERT_BLOCK
    pad_end = jnp.cumsum(padded)
    pad_start = pad_end - padded
    tile_base = jnp.cumsum(tile_counts, axis=0) - tile_counts + pad_start[None, :]
    base = jnp.broadcast_to(tile_base.astype(F32)[:, :, None], tile_base.shape + (ROUTE_W,))
    n_blocks = -(-n_assign // EXPERT_BLOCK) + N_EXPERTS
    blk_row = jnp.arange(n_blocks, dtype=jnp.int32) * EXPERT_BLOCK
    blk_expert = jnp.minimum(jnp.sum((pad_end[None, :] <= blk_row[:, None]).astype(jnp.int32), axis=1),
                             N_EXPERTS - 1)
    n_used = (pad_end[-1] // EXPERT_BLOCK).astype(jnp.int32).reshape(1)
    return base, blk_expert, n_used, n_blocks, pad_end.astype(jnp.int32)


def kernel(x, positions, mix_norm_g, w_in, rw_mu, rw_w0, rw_w_up, rw_a0, rw_a_up, rw_g_up, rw_k_k, rw_k_a, rw_r_k, rw_gn_w, rw_gn_b, mla_g_qa, mla_w_q_up, mla_g_kva, mla_w_kv_up, w_branch_rw, w_branch_mla, w_out, ffn_norm_g, moe_w_group, moe_b_group, moe_w_router, moe_b_router, moe_w_gu, moe_w_down, final_norm_g):
    batch, seq, d = x.shape
    assert d == 2 * SLABS * LANES
    assert seq % RW_TILE == 0 and seq % ATT_TILE == 0 and seq % ROW_TILE == 0
    n = batch * seq
    assert n % DISPATCH_TILE == 0 and n % COMBINE_TILE == 0
    depth = w_in.shape[0]
    rw_cols = 3 * RW_DIM + W_LORA + A_LORA + G_LORA
    mla_cols = Q_LORA + KV_LORA + QK_ROPE
    cos_sin = _rope_cos_sin(positions)
    x2 = x.reshape(n, d)

    for l in range(depth):
        cut = (0, rw_cols, rw_cols + mla_cols, w_in.shape[2])
        w_rw, w_mla, w_gate = (w_in[l][:, cut[j]:cut[j + 1]].astype(BF16) for j in range(3))
        c_rw, c_mla, gates = _in_proj(x2, mix_norm_g[l].reshape(1, d), w_rw, w_mla, w_gate, tm=ROW_TILE)

        rp = _rwkv_params(rw_mu[l], rw_w0[l], rw_w_up[l], rw_a0[l], rw_a_up[l], rw_g_up[l], rw_k_k[l],
                          rw_k_a[l], rw_r_k[l], rw_gn_w[l], rw_gn_b[l])
        y_rw = _rwkv(c_rw.reshape(batch, seq, rw_cols), rp, batch, seq).reshape(n, RW_DIM)

        mp = _mla_params(mla_g_qa[l], mla_w_q_up[l], mla_g_kva[l], mla_w_kv_up[l])
        q_t, k, v_t = _mla_prep(c_mla, cos_sin, mp, ROW_TILE, batch, seq)
        y_mla = _mla_attn(q_t, k.reshape(batch, seq, -1), v_t, batch, seq).reshape(n, MLA_HEADS * V_HEAD)

        gap, tail = ROUTE_FINE0 - N_GROUPS, ROUTE_W - ROUTE_FINE0 - N_EXPERTS
        w_route = jnp.concatenate(
            [moe_w_group[l], jnp.zeros((d, gap), F32), moe_w_router[l], jnp.zeros((d, tail), F32)], axis=1)
        b_route = jnp.concatenate(
            [moe_b_group[l], jnp.zeros((gap,), F32), moe_b_router[l], jnp.zeros((tail,), F32)]).reshape(1, -1)
        wr_hi = w_route.astype(BF16)
        wr_lo = (w_route - wr_hi.astype(F32)).astype(BF16)
        mparams = {"w_br": w_branch_rw[l].astype(BF16), "w_bm": w_branch_mla[l].astype(BF16),
                   "w_out": w_out[l].astype(BF16), "ffn_g": ffn_norm_g[l].reshape(1, d),
                   "w_route": jnp.concatenate([wr_hi, wr_lo], axis=1), "b_route": b_route}
        x1, h2p, route, route_t, hist = _merge(x2, y_rw, y_mla, gates, mparams, tm=ROW_TILE)

        base, blk_expert, n_used, n_blocks, pad_end = _block_layout(hist, n * TOP_K)
        dest = _plan(route_t, base, tm=ROW_TILE)
        xs = _dispatch(h2p, dest, pad_end, n_blocks * EXPERT_BLOCK)
        yb = _experts(xs, blk_expert, n_used, moe_w_gu[l].astype(BF16), moe_w_down[l].astype(BF16))
        x2 = _combine(x1, route, dest, yb, final_norm_g.reshape(1, d), final_norm=(l == depth - 1))

    return x2.reshape(batch, seq, d)
```

```python
import functools
import math

import jax
import jax.numpy as jnp
from jax import lax
from jax.experimental import pallas as pl
from jax.experimental.pallas import tpu as pltpu

F32 = jnp.float32
BF16 = jnp.bfloat16

RW_HEADS = 8
RW_HEAD_DIM = 64
RW_DIM = RW_HEADS * RW_HEAD_DIM
W_LORA = 64
A_LORA = 64
G_LORA = 128
GN_EPS = 64e-5
MLA_HEADS = 8
QK_NOPE = 64
QK_ROPE = 32
V_HEAD = 64
Q_LORA = 384
KV_LORA = 256
ROPE_THETA = 10000.0
N_GROUPS = 4
EXPERTS_PER_GROUP = 8
N_EXPERTS = N_GROUPS * EXPERTS_PER_GROUP
TOP_K = 2
D_EXPERT = 256
EXPERT_BLOCK = 512
NORM_EPS = 1e-6

LANES = 128
HEAD_PAIR = 2 * RW_HEAD_DIM
VMEM_LIMIT = 48 * 1024 * 1024
ROW_TILE = 512


def _cparams(*sem):
    return pltpu.CompilerParams(dimension_semantics=sem, vmem_limit_bytes=VMEM_LIMIT)


def _mm(a, b, dims=((1,), (0,))):
    return lax.dot_general(a.astype(BF16), b.astype(BF16), (dims, ((), ())), preferred_element_type=F32)


def _mm_sel(sel_bf16, x, dims=((1,), (0,))):
    dn = (dims, ((), ()))
    hi = x.astype(BF16)
    lo = (x - hi.astype(F32)).astype(BF16)
    return (lax.dot_general(sel_bf16, hi, dn, preferred_element_type=F32)
            + lax.dot_general(sel_bf16, lo, dn, preferred_element_type=F32))


def _seg_sum(x, seg_bf16):
    return jnp.dot(x.astype(BF16), seg_bf16, preferred_element_type=F32)


def _rms(x, g):
    return x * lax.rsqrt(jnp.mean(x * x, axis=-1, keepdims=True) + NORM_EPS) * g


def _pack_rows(x):
    half = x.shape[1] // 2
    bits = lambda v: lax.bitcast_convert_type(v.astype(BF16).astype(F32), jnp.uint32)
    return bits(x[:, :half]) | (bits(x[:, half:]) >> 16)


def _unpack_rows(p):
    hi = lax.bitcast_convert_type(p & jnp.uint32(0xFFFF0000), F32)
    lo = lax.bitcast_convert_type(p << 16, F32)
    return jnp.concatenate([hi, lo], axis=1)


SLABS = 4


def _slab_rows(ref, r):
    return ref.at[pl.ds(pl.multiple_of(r * SLABS, SLABS), SLABS), :]


def _slab_load(ref):
    rows = ref.shape[0] // SLABS
    return jnp.concatenate([ref[pl.ds(j, rows, stride=SLABS), :] for j in range(SLABS)], axis=1)


def _slab_store(ref, x):
    rows = ref.shape[0] // SLABS
    for j in range(SLABS):
        ref[pl.ds(j, rows, stride=SLABS), :] = x[:, j * LANES:(j + 1) * LANES]


def _in_proj_kernel(x_ref, g_ref, wrw_ref, wmla_ref, wg_ref, crw_ref, cmla_ref, gate_ref):
    hb = _rms(x_ref[...], g_ref[...]).astype(BF16)
    crw_ref[...] = jnp.dot(hb, wrw_ref[...], preferred_element_type=F32)
    cmla_ref[...] = jnp.dot(hb, wmla_ref[...], preferred_element_type=F32)
    gate_ref[...] = jax.nn.sigmoid(jnp.dot(hb, wg_ref[...], preferred_element_type=F32)).astype(BF16)


def _in_proj(x2, g, w_rw, w_mla, w_gate, tm):
    n, d = x2.shape
    full = lambda w: pl.BlockSpec(w.shape, lambda i: (0, 0))
    row = lambda c: pl.BlockSpec((tm, c), lambda i: (i, 0))
    return pl.pallas_call(
        _in_proj_kernel,
        grid=(n // tm,),
        in_specs=[row(d), full(g), full(w_rw), full(w_mla), full(w_gate)],
        out_specs=[row(w_rw.shape[1]), row(w_mla.shape[1]), row(w_gate.shape[1])],
        out_shape=[jax.ShapeDtypeStruct((n, w_rw.shape[1]), F32),
                   jax.ShapeDtypeStruct((n, w_mla.shape[1]), F32),
                   jax.ShapeDtypeStruct((n, w_gate.shape[1]), BF16)],
        compiler_params=_cparams("parallel"),
        name="in_proj",
    )(x2, g, w_rw, w_mla, w_gate)


RW_CHUNK = 64
RW_TILE = 1024
def _token_shift(cur, halo_ref, first):
    prev_row = jnp.where(first, 0.0, halo_ref[0, 7:8, :])
    rolled = pltpu.roll(cur, 1, 0)
    row = lax.broadcasted_iota(jnp.int32, cur.shape, 0)
    return jnp.where(row == 0, prev_row, rolled)


def _rwkv_kernel(r_ref, k_ref, v_ref, l_ref, hr_ref, hk_ref, hv_ref, hl_ref,
                 mur_ref, muk_ref, muv_ref, mul_ref, w0_ref, a0_ref, kk_ref, ka_ref, rk_ref,
                 gnw_ref, gnb_ref, wup_ref, aup_ref, gup_ref, y_ref, st_ref):
    i = pl.program_id(2)
    first = i == 0

    @pl.when(first)
    def _():
        st_ref[...] = jnp.zeros_like(st_ref)

    def mixed(c_ref, h_ref, mu_ref):
        cur = c_ref[0]
        return cur + (_token_shift(cur, h_ref, first) - cur) * mu_ref[...]

    zr = mixed(r_ref, hr_ref, mur_ref)
    zk = mixed(k_ref, hk_ref, muk_ref)
    zv = mixed(v_ref, hv_ref, muv_ref)
    zl = mixed(l_ref, hl_ref, mul_ref)
    z_wa = zl[:, :LANES]
    z_g = zl[:, LANES:]

    lane = lax.broadcasted_iota(jnp.int32, (LANES, LANES), 1)
    sub = lax.broadcasted_iota(jnp.int32, (LANES, LANES), 0)
    same_head = (lane // RW_HEAD_DIM) == (sub // RW_HEAD_DIM)
    seg = jnp.where(same_head, 1.0, 0.0).astype(BF16)

    w = w0_ref[...] + _mm(jnp.tanh(z_wa), wup_ref[...])
    u = -w
    softplus = jnp.maximum(u, 0.0) + jnp.log(1.0 + jnp.exp(-jnp.abs(u)))
    log_decay = -jnp.exp(-softplus - 0.5)
    a = jax.nn.sigmoid(a0_ref[...] + _mm(z_wa, aup_ref[...]))
    g = _mm(jax.nn.sigmoid(z_g), gup_ref[...])

    kk = zk * kk_ref[...]
    kk = kk / jnp.maximum(jnp.sqrt(_seg_sum(kk * kk, seg)), 1e-12)
    k2 = zk * (1.0 + (a - 1.0) * ka_ref[...])
    bonus = _seg_sum(zr * k2 * rk_ref[...], seg) * zv
    kka = kk * a

    c = RW_CHUNK
    crow = lax.broadcasted_iota(jnp.int32, (c, c), 0)
    ccol = lax.broadcasted_iota(jnp.int32, (c, c), 1)
    cum_sel = jnp.where(crow >= ccol, 1.0, 0.0).astype(BF16)
    tril_incl = sub >= lane
    tril_strict = sub > lane
    eye_l = jnp.where(lane == sub, 1.0, 0.0).astype(F32)
    lo_half = lax.broadcasted_iota(jnp.int32, (c, LANES), 1) < RW_HEAD_DIM
    nt = ((1,), (1,))
    tn = ((0,), (0,))
    zeros_blk = jnp.zeros((2 * c, LANES), BF16)
    zeros_half = jnp.zeros((c, LANES), BF16)

    def stack(t):
        tb = t.astype(BF16)
        return jnp.concatenate([jnp.where(lo_half, tb, zeros_half), jnp.where(lo_half, zeros_half, tb)], axis=0)

    tril_incl2 = jnp.concatenate([tril_incl, tril_incl], axis=1)

    def chunk_stages(ids):
        chunks = range(len(ids))
        x_a, x_b, x_k, x_r, x_v, x_bh, x_kh, w_tot, r_dec = [], [], [], [], [], [], [], [], []
        for ci in ids:
            sl = slice(ci * c, (ci + 1) * c)
            ld = log_decay[sl]
            cum = _mm_sel(cum_sel, ld)
            tot = cum[c - 1:c, :]
            e_neg = jnp.exp(-cum)
            e_rest = jnp.exp(tot - cum)
            x_a.append(stack(-kk[sl] * jnp.exp(cum - ld)))
            x_b.append(stack(kka[sl] * e_neg))
            x_k.append(stack(k2[sl] * e_neg))
            r_dec.append(zr[sl] * jnp.exp(cum))
            x_r.append(stack(r_dec[-1]))
            x_v.append(stack(zv[sl]))
            x_bh.append(stack(kka[sl] * e_rest))
            x_kh.append(stack(k2[sl] * e_rest))
            w_tot.append(jnp.exp(tot))

        inter = [_mm(jnp.concatenate([x_a[i], x_r[i]], axis=0),
                     jnp.concatenate([x_b[i], x_k[i]], axis=0), nt) for i in chunks]
        inter = [m.astype(BF16) for m in inter]
        zeros_sq_b = jnp.zeros((LANES, LANES), BF16)
        a_ab = [jnp.where(tril_strict, m[:2 * c, :2 * c], zeros_sq_b) for m in inter]
        a_ak = [jnp.where(tril_strict, m[:2 * c, 2 * c:], zeros_sq_b) for m in inter]
        a_r = [jnp.where(tril_incl2, m[2 * c:], jnp.concatenate([zeros_sq_b, zeros_sq_b], axis=1))
               for m in inter]
        w_ak = [_mm(a_ak[i], x_v[i]).astype(BF16) for i in chunks]

        t_inv = [eye_l + m.astype(F32) for m in a_ab]
        pw = a_ab
        for _ in range(int(math.log2(c)) - 1):
            pw = [_mm(m, m).astype(BF16) for m in pw]
            t_inv = [t_inv[i] + _mm(t_inv[i], pw[i]) for i in chunks]

        solved = [_mm(t_inv[i], jnp.concatenate([x_a[i], w_ak[i]], axis=1)).astype(BF16)
                  for i in chunks]
        rhs = [jnp.concatenate([solved[i], jnp.concatenate([zeros_blk, x_v[i]], axis=1)], axis=0)
               for i in chunks]
        out = [_mm(a_r[i], rhs[i]) for i in chunks]
        carry = [_mm(jnp.concatenate([x_bh[i], x_kh[i]], axis=0), rhs[i], tn) for i in chunks]
        q_hat, y_loc = [], []
        for i in chunks:
            q_hat.append(r_dec[i] + out[i][:c, :LANES] + out[i][c:, :LANES])
            y_loc.append(out[i][:c, LANES:] + out[i][c:, LANES:])
        trans = [jnp.concatenate([eye_l * w_tot[i] + carry[i][:, :LANES], carry[i][:, LANES:]], axis=1)
                 for i in chunks]
        return q_hat, y_loc, trans

    n_chunks = RW_TILE // c
    chunks = range(n_chunks)
    q_hat, y_loc, trans = chunk_stages(chunks)

    zeros_sq = jnp.zeros((LANES, LANES), F32)

    def compose(later, earlier):
        return _mm(later[:, :LANES], earlier) + jnp.concatenate([zeros_sq, later[:, LANES:]], axis=1)

    scan = list(trans)
    dist = 1
    while dist < len(scan):
        scan = [scan[i] if i < dist else compose(scan[i], scan[i - dist]) for i in range(len(scan))]
        dist *= 2
    prefix = [None] + scan
    q_pre = [None] + [_mm(q_hat[i], prefix[i]) for i in chunks[1:]]
    q_m = jnp.concatenate([q_hat[0]] + [q_pre[i][:, :LANES] for i in chunks[1:]], axis=0)
    y_off = jnp.concatenate([y_loc[0]] + [y_loc[i] + q_pre[i][:, LANES:] for i in chunks[1:]], axis=0)
    state = st_ref[...]
    y = _mm(q_m, state) + y_off
    st_ref[...] = _mm(prefix[-1][:, :LANES], state) + prefix[-1][:, LANES:]

    inv_n = 1.0 / RW_HEAD_DIM
    mean = _seg_sum(y, seg) * inv_n
    d = y - mean
    var = _seg_sum(d * d, seg) * inv_n
    yn = d * lax.rsqrt(var + GN_EPS) * gnw_ref[...] + gnb_ref[...]
    y_ref[0] = ((yn + bonus) * g).astype(y_ref.dtype)


def _rwkv(c_rw, p, batch, seq):
    ts = RW_TILE
    n_pairs = RW_DIM // HEAD_PAIR
    lora_blk = (3 * RW_DIM) // (2 * LANES)
    halo = ts // 8

    def col(off):
        return pl.BlockSpec((1, ts, LANES), lambda b, pp, i, off=off: (b, i, off + pp))

    def col_halo(off):
        return pl.BlockSpec((1, 8, LANES),
                            lambda b, pp, i, off=off: (b, jnp.maximum(i * halo - 1, 0), off + pp))

    vec = pl.BlockSpec((1, LANES), lambda b, pp, i: (0, pp))
    lora_w = pl.BlockSpec((LANES, LANES), lambda b, pp, i: (0, pp))
    in_specs = [
        col(0), col(n_pairs), col(2 * n_pairs),
        pl.BlockSpec((1, ts, 2 * LANES), lambda b, pp, i: (b, i, lora_blk)),
        col_halo(0), col_halo(n_pairs), col_halo(2 * n_pairs),
        pl.BlockSpec((1, 8, 2 * LANES), lambda b, pp, i: (b, jnp.maximum(i * halo - 1, 0), lora_blk)),
        vec, vec, vec, pl.BlockSpec((1, 2 * LANES), lambda b, pp, i: (0, 0)),
        vec, vec, vec, vec, vec, vec, vec, lora_w, lora_w, lora_w,
    ]
    return pl.pallas_call(
        _rwkv_kernel,
        grid=(batch, n_pairs, seq // ts),
        in_specs=in_specs,
        out_specs=pl.BlockSpec((1, ts, LANES), lambda b, pp, i: (b, i, pp)),
        out_shape=jax.ShapeDtypeStruct((batch, seq, RW_DIM), BF16),
        scratch_shapes=[pltpu.VMEM((LANES, LANES), F32)],
        compiler_params=_cparams("parallel", "parallel", "arbitrary"),
        name="rwkv",
    )(c_rw, c_rw, c_rw, c_rw, c_rw, c_rw, c_rw, c_rw,
      p["mu_r"], p["mu_k"], p["mu_v"], p["mu_l"], p["w0"], p["a0"], p["k_k"], p["k_a"], p["r_k"],
      p["gn_w"], p["gn_b"], p["w_up"], p["a_up"], p["g_up"])


MLA_SLOT = 128


def _mla_prep_kernel(cmla_ref, cs_ref, gq_ref, gkv_ref,
                     wqa_ref, wqb_ref, wk_ref, wv_ref, pa_ref, pb_ref, place_ref, one_ref,
                     qt_ref, k_ref, vt_ref):
    cs = cs_ref[...]
    cs_hi = cs.astype(BF16)
    cs_lo = (cs - cs_hi.astype(F32)).astype(BF16)
    tables = (jnp.dot(cs_hi, place_ref[...], preferred_element_type=F32)
              + jnp.dot(cs_lo, place_ref[...], preferred_element_type=F32))
    cos = tables[:, :MLA_SLOT] + one_ref[...]
    sin = tables[:, MLA_SLOT:]
    zq = _rms(cmla_ref[:, :Q_LORA], gq_ref[...]).astype(BF16)
    qa = jnp.dot(zq, wqa_ref[...], preferred_element_type=F32)
    qb = jnp.dot(zq, wqb_ref[...], preferred_element_type=F32)
    ckvr = cmla_ref[:, Q_LORA:]
    zkv = _rms(ckvr[:, :KV_LORA], gkv_ref[...]).astype(BF16)
    kn = jnp.dot(zkv, wk_ref[...], preferred_element_type=F32)
    v = jnp.dot(zkv, wv_ref[...], preferred_element_type=F32)
    for blk in range(v.shape[1] // LANES):
        vt_ref[0, blk * LANES:(blk + 1) * LANES, :] = v[:, blk * LANES:(blk + 1) * LANES].astype(BF16).T
    kr = ckvr[:, KV_LORA:].astype(BF16)
    k_rope = (jnp.dot(kr, pa_ref[...], preferred_element_type=F32) * cos
              + jnp.dot(kr, pb_ref[...], preferred_element_type=F32) * sin)
    scale = math.log2(math.e) / math.sqrt(QK_NOPE + QK_ROPE)
    for h in range(MLA_HEADS):
        sl = slice(h * MLA_SLOT, (h + 1) * MLA_SLOT)
        qt_ref[0, sl, :] = ((qa[:, sl] * cos + qb[:, sl] * sin) * scale).astype(BF16).T
        k_ref[:, sl] = (kn[:, sl] + k_rope).astype(BF16)


def _mla_prep(c_mla, cos_sin, p, tm, batch, seq):
    n = c_mla.shape[0]
    full = lambda w: pl.BlockSpec(w.shape, lambda i: (0, 0))
    row = lambda c: pl.BlockSpec((tm, c), lambda i: (i, 0))
    per_seq = seq // tm
    col = lambda r: pl.BlockSpec((1, r, tm), lambda i: (i // per_seq, 0, i % per_seq))
    ws = [p["g_qa"], p["g_kva"], p["w_qa"], p["w_qb"], p["w_k"], p["w_v"], p["p_a"], p["p_b"],
          p["place"], p["one"]]
    hq = MLA_HEADS * MLA_SLOT
    hv = MLA_HEADS * V_HEAD
    return pl.pallas_call(
        _mla_prep_kernel,
        grid=(n // tm,),
        in_specs=[row(c_mla.shape[1]), row(cos_sin.shape[1])] + [full(w) for w in ws],
        out_specs=[col(hq), row(hq), col(hv)],
        out_shape=[jax.ShapeDtypeStruct((batch, hq, seq), BF16), jax.ShapeDtypeStruct((n, hq), BF16),
                   jax.ShapeDtypeStruct((batch, hv, seq), BF16)],
        compiler_params=_cparams("parallel"),
        name="mla_prep",
    )(c_mla, cos_sin, *ws)


ATT_TILE = 512
ATT_HEADS = 4


def _attn_kernel(qt_ref, k_ref, vt_ref, o_ref):
    qi = pl.program_id(2)
    t = ATT_TILE
    heads = range(ATT_HEADS)
    den_row = (V_HEAD, 0)

    def augment(h, vt):
        r = lax.broadcasted_iota(jnp.int32, vt.shape, 0)
        own = (r < V_HEAD) if h % 2 == 0 else (r >= V_HEAD)
        return jnp.where(own, vt, jnp.where(r == den_row[h % 2], 1.0, 0.0).astype(BF16))

    def block(j, carry, mask):
        keys = pl.ds(pl.multiple_of(j * t, t), t)
        sts = []
        for h in heads:
            qt = qt_ref[0, h * MLA_SLOT:(h + 1) * MLA_SLOT, :]
            kb = k_ref[0, keys, h * MLA_SLOT:(h + 1) * MLA_SLOT]
            st = jnp.dot(kb, qt, preferred_element_type=F32)
            sts.append(st if mask is None else jnp.where(mask, st, -jnp.inf))
        m_new = [jnp.maximum(carry[h][0], jnp.max(sts[h], axis=0, keepdims=True)) for h in heads]
        pts = [jnp.exp2((sts[h] - m_new[h]).astype(BF16)) for h in heads]
        out = []
        for h in heads:
            m, acc = carry[h]
            vt = vt_ref[0, (h // 2) * 2 * V_HEAD:(h // 2 + 1) * 2 * V_HEAD, keys]
            acc = acc * jnp.exp2(m - m_new[h]) + jnp.dot(augment(h, vt), pts[h], preferred_element_type=F32)
            out.append((m_new[h], acc))
        return tuple(out)

    init1 = (jnp.full((1, t), -jnp.inf, F32), jnp.zeros((2 * V_HEAD, t), F32))
    carry = lax.fori_loop(0, qi, lambda j, c: block(j, c, None), tuple(init1 for _ in heads))

    causal = lax.broadcasted_iota(jnp.int32, (t, t), 0) <= lax.broadcasted_iota(jnp.int32, (t, t), 1)
    accs = [mc[1] for mc in block(qi, carry, causal)]
    lo_rows = lax.broadcasted_iota(jnp.int32, (2 * V_HEAD, t), 0) < V_HEAD
    for p in range(ATT_HEADS // 2):
        acc0, acc1 = accs[2 * p], accs[2 * p + 1]
        den0 = acc0[den_row[0]:den_row[0] + 1, :]
        den1 = acc1[den_row[1]:den_row[1] + 1, :]
        out_t = jnp.where(lo_rows, acc0 / den0, acc1 / den1)
        o_ref[0, :, p * 2 * V_HEAD:(p + 1) * 2 * V_HEAD] = out_t.T.astype(o_ref.dtype)


def _mla_attn(qt, k, vt, batch, seq):
    t = ATT_TILE
    g = ATT_HEADS
    return pl.pallas_call(
        _attn_kernel,
        grid=(batch, MLA_HEADS // g, seq // t),
        in_specs=[pl.BlockSpec((1, g * MLA_SLOT, t), lambda b, hp, i: (b, hp, i)),
                  pl.BlockSpec((1, seq, g * MLA_SLOT), lambda b, hp, i: (b, 0, hp)),
                  pl.BlockSpec((1, g * V_HEAD, seq), lambda b, hp, i: (b, hp, 0))],
        out_specs=pl.BlockSpec((1, t, g * V_HEAD), lambda b, hp, i: (b, i, hp)),
        out_shape=jax.ShapeDtypeStruct((batch, seq, MLA_HEADS * V_HEAD), BF16),
        compiler_params=_cparams("parallel", "parallel", "arbitrary"),
        name="mla_attn",
    )(qt, k, vt)


ROUTE_W = 128
ROUTE_FINE0 = 8


def _merge_kernel(x_ref, yrw_ref, ymla_ref, gate_ref, wbr_ref, wbm_ref, wo_ref, fg_ref,
                  wr_ref, br_ref, x1_ref, h2p_ref, route_ref, route_t_ref, hist_ref):
    d = x_ref.shape[1]
    a = jnp.dot(yrw_ref[...], wbr_ref[...], preferred_element_type=F32)
    b = jnp.dot(ymla_ref[...], wbm_ref[...], preferred_element_type=F32)
    merged = gate_ref[:, :d].astype(F32) * a + gate_ref[:, d:].astype(F32) * b
    x1 = x_ref[...] + jnp.dot(merged.astype(BF16), wo_ref[...], preferred_element_type=F32)
    x1_ref[...] = x1
    h2 = _rms(x1, fg_ref[...])
    _slab_store(h2p_ref, _pack_rows(h2))

    h_hi = h2.astype(BF16)
    h_lo = (h2 - h_hi.astype(F32)).astype(BF16)
    both = jnp.dot(h_hi, wr_ref[...], preferred_element_type=F32)
    logits = (both[:, :ROUTE_W] + both[:, ROUTE_W:]
              + jnp.dot(h_lo, wr_ref[:, :ROUTE_W], preferred_element_type=F32)) + br_ref[...]

    tm = logits.shape[0]
    lt = logits.T
    sub = lax.broadcasted_iota(jnp.int32, (EXPERTS_PER_GROUP, tm), 0)
    big = jnp.int32(EXPERTS_PER_GROUP)
    neg = -jnp.inf

    def first_argmax(vals, vmax):
        return jnp.min(jnp.where(vals == vmax, sub, big), axis=0, keepdims=True)

    grp = jnp.where(sub < N_GROUPS, lt[:EXPERTS_PER_GROUP], neg)
    g_max = jnp.max(grp, axis=0, keepdims=True)
    g_den = jnp.sum(jnp.exp(grp - g_max), axis=0, keepdims=True)
    g_sel = first_argmax(grp, g_max)
    gate_g = 1.0 / g_den
    fine = lt[ROUTE_FINE0:ROUTE_FINE0 + EXPERTS_PER_GROUP]
    for g in range(1, N_GROUPS):
        lo = ROUTE_FINE0 + g * EXPERTS_PER_GROUP
        fine = jnp.where(g_sel == g, lt[lo:lo + EXPERTS_PER_GROUP], fine)
    v1 = jnp.max(fine, axis=0, keepdims=True)
    i1 = first_argmax(fine, v1)
    fine2 = jnp.where(sub == i1, neg, fine)
    v2 = jnp.max(fine2, axis=0, keepdims=True)
    i2 = first_argmax(fine2, v2)
    e2 = jnp.exp(v2 - v1)
    den = 1.0 + e2
    w1 = gate_g / den
    w2 = gate_g * e2 / den
    base_e = g_sel * EXPERTS_PER_GROUP
    route_t = jnp.where(sub == 0, (base_e + i1).astype(F32),
                        jnp.where(sub == 1, (base_e + i2).astype(F32),
                                  jnp.where(sub == 2, w1, jnp.where(sub == 3, w2, 0.0))))
    pad = jnp.zeros((ROUTE_W - EXPERTS_PER_GROUP, tm), F32)
    route_t_ref[...] = route_t
    route_ref[...] = jnp.concatenate([route_t, pad], axis=0).T
    picked = (sub == i1) | (sub == i2)
    for g in range(N_GROUPS):
        cnt = jnp.sum(jnp.where(picked & (g_sel == g), 1.0, 0.0), axis=1, keepdims=True)
        hist_ref[0, g * EXPERTS_PER_GROUP:(g + 1) * EXPERTS_PER_GROUP, :] = jnp.broadcast_to(
            cnt, (EXPERTS_PER_GROUP, ROUTE_W))


def _merge(x2, y_rw, y_mla, gates, p, tm):
    n, d = x2.shape
    full = lambda w: pl.BlockSpec(w.shape, lambda i: (0, 0))
    row = lambda c: pl.BlockSpec((tm, c), lambda i: (i, 0))
    ws = [p["w_br"], p["w_bm"], p["w_out"], p["ffn_g"], p["w_route"], p["b_route"]]
    return pl.pallas_call(
        _merge_kernel,
        grid=(n // tm,),
        in_specs=[row(d), row(y_rw.shape[1]), row(y_mla.shape[1]), row(2 * d)] + [full(w) for w in ws],
        out_specs=[row(d), pl.BlockSpec((tm * SLABS, LANES), lambda i: (i, 0)), row(ROUTE_W),
                   pl.BlockSpec((EXPERTS_PER_GROUP, tm), lambda i: (0, i)),
                   pl.BlockSpec((1, N_EXPERTS, ROUTE_W), lambda i: (i, 0, 0))],
        out_shape=[jax.ShapeDtypeStruct((n, d), F32), jax.ShapeDtypeStruct((n * SLABS, LANES), jnp.uint32),
                   jax.ShapeDtypeStruct((n, ROUTE_W), F32),
                   jax.ShapeDtypeStruct((EXPERTS_PER_GROUP, n), F32),
                   jax.ShapeDtypeStruct((n // tm, N_EXPERTS, ROUTE_W), F32)],
        compiler_params=_cparams("parallel"),
        name="merge_route",
    )(x2, y_rw, y_mla, gates, *ws)


def _plan_kernel(route_t_ref, base_ref, dest_ref):
    tm = route_t_ref.shape[1]
    rt = route_t_ref[...]
    expert = lax.broadcasted_iota(jnp.int32, (N_EXPERTS, tm), 0).astype(F32)
    pick = [expert == rt[k:k + 1, :] for k in range(TOP_K)]
    both = jnp.where(pick[0] | pick[1], 1.0, 0.0).astype(BF16)
    r = lax.broadcasted_iota(jnp.int32, (tm, tm), 0)
    c = lax.broadcasted_iota(jnp.int32, (tm, tm), 1)
    earlier = jnp.where(r < c, 1.0, 0.0).astype(BF16)
    offs = jnp.dot(both, earlier, preferred_element_type=F32) + base_ref[0][:, :1]
    rows = [jnp.sum(jnp.where(pk, offs, 0.0), axis=0, keepdims=True) for pk in pick]
    sub = lax.broadcasted_iota(jnp.int32, dest_ref.shape, 0)
    dest_ref[...] = jnp.where(sub == 0, rows[0], jnp.where(sub == 1, rows[1], 0.0)).astype(jnp.int32)


def _plan(route_t, base, tm):
    n = route_t.shape[1]
    return pl.pallas_call(
        _plan_kernel,
        grid=(n // tm,),
        in_specs=[pl.BlockSpec((EXPERTS_PER_GROUP, tm), lambda i: (0, i)),
                  pl.BlockSpec((1, N_EXPERTS, ROUTE_W), lambda i: (i, 0, 0))],
        out_specs=pl.BlockSpec((EXPERTS_PER_GROUP, tm), lambda i: (0, i)),
        out_shape=jax.ShapeDtypeStruct((EXPERTS_PER_GROUP, n), jnp.int32),
        compiler_params=_cparams("parallel"),
        name="route_plan",
    )(route_t, base)


def _dest_tiles(dest_t, tm):
    n = dest_t.shape[1]
    return dest_t[:TOP_K].reshape(TOP_K, n // tm, tm).transpose(1, 0, 2).reshape(n // tm, 1, TOP_K * tm)


DISPATCH_TILE = 1024
ROW_DMA_UNROLL = 8


def _dispatch_kernel(pad_end_ref, dest_ref, h_ref, xs_ref, zbuf, sem, zsem):
    tm = h_ref.shape[0] // SLABS

    @pl.when(pl.program_id(0) == 0)
    def _():
        zbuf[...] = jnp.zeros_like(zbuf)

        def tail(e):
            first = pl.multiple_of((pad_end_ref[e] - EXPERT_BLOCK) * SLABS, EXPERT_BLOCK * SLABS)
            return pltpu.make_async_copy(zbuf, xs_ref.at[pl.ds(first, EXPERT_BLOCK * SLABS), :], zsem)

        def region_rows(e):
            return pad_end_ref[e] - (pad_end_ref[e - 1] if e else 0)

        n_blocks = xs_ref.shape[0] // (EXPERT_BLOCK * SLABS)
        used_rows = pad_end_ref[N_EXPERTS - 1]

        def spare(b):
            return pltpu.make_async_copy(zbuf, xs_ref.at[pl.ds(b * EXPERT_BLOCK * SLABS, EXPERT_BLOCK * SLABS), :],
                                         zsem)

        spare_blocks = range(n_blocks - N_EXPERTS, n_blocks)
        for e in range(N_EXPERTS):
            @pl.when(region_rows(e) > 0)
            def _(e=e):
                tail(e).start()
        for b in spare_blocks:
            @pl.when(b * EXPERT_BLOCK >= used_rows)
            def _(b=b):
                spare(b).start()
        for e in range(N_EXPERTS):
            @pl.when(region_rows(e) > 0)
            def _(e=e):
                tail(e).wait()
        for b in spare_blocks:
            @pl.when(b * EXPERT_BLOCK >= used_rows)
            def _(b=b):
                spare(b).wait()

    def start(t, _):
        for k in range(TOP_K):
            pltpu.make_async_copy(_slab_rows(h_ref, t), _slab_rows(xs_ref, dest_ref[0, 0, k * tm + t]),
                                  sem).start(priority=k % 2)
        return 0

    lax.fori_loop(0, tm, start, 0, unroll=ROW_DMA_UNROLL)
    all_rows = xs_ref.at[pl.ds(0, TOP_K * tm * SLABS), :]
    pltpu.make_async_copy(all_rows, all_rows, sem).wait()


def _dispatch(h2, dest, pad_end, p_rows):
    n = h2.shape[0] // SLABS
    tm = DISPATCH_TILE
    dest3 = _dest_tiles(dest, tm)
    grid_spec = pltpu.PrefetchScalarGridSpec(
        num_scalar_prefetch=1,
        grid=(n // tm,),
        in_specs=[pl.BlockSpec((1, 1, TOP_K * tm), lambda i, pe: (i, 0, 0), memory_space=pltpu.SMEM),
                  pl.BlockSpec((tm * SLABS, LANES), lambda i, pe: (i, 0))],
        out_specs=pl.BlockSpec(memory_space=pl.ANY),
        scratch_shapes=[pltpu.VMEM((EXPERT_BLOCK * SLABS, LANES), jnp.uint32),
                        pltpu.SemaphoreType.DMA(()), pltpu.SemaphoreType.DMA(())],
    )
    return pl.pallas_call(
        _dispatch_kernel,
        grid_spec=grid_spec,
        out_shape=jax.ShapeDtypeStruct((p_rows * SLABS, LANES), jnp.uint32),
        compiler_params=_cparams("arbitrary"),
        name="dispatch",
    )(pad_end, dest3, h2)


def _expert_kernel(blk_e_ref, n_used_ref, x_ref, wgu_ref, wd_ref, y_ref, wgu_b, wd_b):
    i = pl.program_id(0)

    @pl.when((i == 0) | (blk_e_ref[i] != blk_e_ref[jnp.maximum(i - 1, 0)]))
    def _():
        wgu_b[...] = wgu_ref[0].astype(BF16)
        wd_b[...] = wd_ref[0].astype(BF16)

    @pl.when(i < n_used_ref[0])
    def _():
        x = _unpack_rows(_slab_load(x_ref)).astype(BF16)
        h = jnp.dot(x, wgu_b[...], preferred_element_type=F32)
        gt = h[:, :D_EXPERT]
        up = h[:, D_EXPERT:]
        act = (gt * jax.nn.sigmoid(gt) * up).astype(BF16)
        _slab_store(y_ref, _pack_rows(jnp.dot(act, wd_b[...], preferred_element_type=F32)))

    @pl.when(pl.program_id(0) >= n_used_ref[0])
    def _():
        y_ref[...] = jnp.zeros_like(y_ref)


def _experts(xs, blk_expert, n_used, w_gu, w_down):
    p_rows = xs.shape[0] // SLABS
    d = 2 * SLABS * LANES
    n_blocks = p_rows // EXPERT_BLOCK
    rows = pl.BlockSpec((EXPERT_BLOCK * SLABS, LANES), lambda i, be, nu: (i, 0))
    grid_spec = pltpu.PrefetchScalarGridSpec(
        num_scalar_prefetch=2,
        grid=(n_blocks,),
        in_specs=[rows,
                  pl.BlockSpec((1, d, 2 * D_EXPERT), lambda i, be, nu: (be[i], 0, 0)),
                  pl.BlockSpec((1, D_EXPERT, d), lambda i, be, nu: (be[i], 0, 0))],
        out_specs=rows,
        scratch_shapes=[pltpu.VMEM((d, 2 * D_EXPERT), BF16), pltpu.VMEM((D_EXPERT, d), BF16)],
    )
    return pl.pallas_call(
        _expert_kernel,
        grid_spec=grid_spec,
        out_shape=jax.ShapeDtypeStruct(xs.shape, jnp.uint32),
        compiler_params=_cparams("arbitrary"),
        name="experts",
    )(blk_expert, n_used, xs, w_gu, w_down)


COMBINE_TILE = 512


def _combine_kernel(dest_ref, dest_next_ref, x1_ref, route_ref, g_ref, yb_ref, o_ref,
                    buf00, buf01, buf10, buf11, sems, *, final_norm):
    tm = x1_ref.shape[0]
    i = pl.program_id(0)
    bufs = ((buf00, buf01), (buf10, buf11))

    def issue(d_ref, slot):
        def start(t, _):
            for k in range(TOP_K):
                pltpu.make_async_copy(_slab_rows(yb_ref, d_ref[0, 0, k * tm + t]),
                                      _slab_rows(bufs[slot][k], t), sems.at[slot]).start(priority=k % 2)
            return 0

        lax.fori_loop(0, tm, start, 0, unroll=ROW_DMA_UNROLL)

    @pl.when(i == 0)
    def _():
        issue(dest_ref, 0)

    for slot in range(2):
        @pl.when((i % 2 == slot) & (i + 1 < pl.num_programs(0)))
        def _(slot=slot):
            issue(dest_next_ref, 1 - slot)

    for slot in range(2):
        @pl.when(i % 2 == slot)
        def _(slot=slot):
            for b in bufs[slot]:
                pltpu.make_async_copy(b, b, sems.at[slot]).wait()
            route = route_ref[...]
            x2 = (x1_ref[...] + route[:, 2:3] * _unpack_rows(_slab_load(bufs[slot][0]))
                  + route[:, 3:4] * _unpack_rows(_slab_load(bufs[slot][1])))
            o_ref[...] = _rms(x2, g_ref[...]) if final_norm else x2


def _combine(x1, route, dest, yb, final_g, final_norm):
    n, d = x1.shape
    tm = COMBINE_TILE
    n_tiles = n // tm
    dest3 = _dest_tiles(dest, tm)
    return pl.pallas_call(
        functools.partial(_combine_kernel, final_norm=final_norm),
        grid=(n_tiles,),
        in_specs=[pl.BlockSpec((1, 1, TOP_K * tm), lambda i: (i, 0, 0), memory_space=pltpu.SMEM),
                  pl.BlockSpec((1, 1, TOP_K * tm), lambda i: (jnp.minimum(i + 1, n_tiles - 1), 0, 0),
                               memory_space=pltpu.SMEM),
                  pl.BlockSpec((tm, d), lambda i: (i, 0)),
                  pl.BlockSpec((tm, ROUTE_W), lambda i: (i, 0)),
                  pl.BlockSpec((1, d), lambda i: (0, 0)),
                  pl.BlockSpec(memory_space=pl.ANY)],
        out_specs=pl.BlockSpec((tm, d), lambda i: (i, 0)),
        out_shape=jax.ShapeDtypeStruct((n, d), F32),
        scratch_shapes=[pltpu.VMEM((tm * SLABS, LANES), jnp.uint32) for _ in range(2 * TOP_K)]
                       + [pltpu.SemaphoreType.DMA((2,))],
        compiler_params=_cparams("arbitrary"),
        name="combine",
    )(dest3, dest3, x1, route, final_g, yb)


def _rwkv_params(rw_mu, rw_w0, rw_w_up, rw_a0, rw_a_up, rw_g_up, rw_k_k, rw_k_a, rw_r_k, rw_gn_w, rw_gn_b):
    row = lambda v: v.reshape(1, -1).astype(F32)
    zeros = jnp.zeros((A_LORA, RW_DIM), F32)
    return {
        "mu_r": row(rw_mu[:RW_DIM]), "mu_k": row(rw_mu[RW_DIM:2 * RW_DIM]),
        "mu_v": row(rw_mu[2 * RW_DIM:3 * RW_DIM]), "mu_l": row(rw_mu[3 * RW_DIM:]),
        "w0": row(rw_w0), "a0": row(rw_a0), "k_k": row(rw_k_k), "k_a": row(rw_k_a),
        "r_k": row(rw_r_k), "gn_w": row(rw_gn_w), "gn_b": row(rw_gn_b),
        "w_up": jnp.concatenate([rw_w_up, zeros], axis=0).astype(BF16),
        "a_up": jnp.concatenate([zeros, rw_a_up], axis=0).astype(BF16),
        "g_up": rw_g_up.astype(BF16),
    }


def _mla_params(g_qa, w_q_up, g_kva, w_kv_up):
    half = QK_ROPE // 2
    pad = MLA_SLOT - QK_NOPE - QK_ROPE
    wq = w_q_up.reshape(Q_LORA, MLA_HEADS, QK_NOPE + QK_ROPE)
    q_nope, q_r1, q_r2 = wq[..., :QK_NOPE], wq[..., QK_NOPE:QK_NOPE + half], wq[..., QK_NOPE + half:]
    zq = lambda w: jnp.zeros((Q_LORA, MLA_HEADS, w), F32)
    w_qa = jnp.concatenate([q_nope, q_r1, q_r2, zq(pad)], axis=-1).reshape(Q_LORA, -1)
    w_qb = jnp.concatenate([zq(QK_NOPE), -q_r2, q_r1, zq(pad)], axis=-1).reshape(Q_LORA, -1)
    wkv = w_kv_up.reshape(KV_LORA, MLA_HEADS, QK_NOPE + V_HEAD)
    w_k = jnp.concatenate([wkv[..., :QK_NOPE], jnp.zeros((KV_LORA, MLA_HEADS, MLA_SLOT - QK_NOPE), F32)],
                          axis=-1).reshape(KV_LORA, -1)
    w_v = wkv[..., QK_NOPE:].reshape(KV_LORA, -1)
    eye = jnp.eye(half, dtype=F32)
    z = jnp.zeros((half, half), F32)
    zl = jnp.zeros((QK_ROPE, QK_NOPE), F32)
    zr = jnp.zeros((QK_ROPE, pad), F32)
    p_a = jnp.concatenate([zl, jnp.concatenate([eye, z], 0), jnp.concatenate([z, eye], 0), zr], axis=1)
    p_b = jnp.concatenate([zl, jnp.concatenate([z, -eye], 0), jnp.concatenate([eye, z], 0), zr], axis=1)
    place_half = jnp.concatenate([jnp.zeros((half, QK_NOPE), F32), eye, eye, jnp.zeros((half, pad), F32)], axis=1)
    zh = jnp.zeros_like(place_half)
    place = jnp.concatenate([jnp.concatenate([place_half, zh], 1), jnp.concatenate([zh, place_half], 1)], 0)
    one = jnp.concatenate([jnp.ones((1, QK_NOPE), F32), jnp.zeros((1, MLA_SLOT - QK_NOPE), F32)], axis=1)
    return {"g_qa": g_qa.reshape(1, -1), "g_kva": g_kva.reshape(1, -1),
            "w_qa": w_qa.astype(BF16), "w_qb": w_qb.astype(BF16), "w_k": w_k.astype(BF16),
            "w_v": w_v.astype(BF16), "p_a": p_a.astype(BF16), "p_b": p_b.astype(BF16),
            "place": place.astype(BF16), "one": one}


def _rope_cos_sin(positions):
    inv_freq = ROPE_THETA ** (-jnp.arange(0, QK_ROPE, 2, dtype=F32) / QK_ROPE)
    ang = positions.astype(F32).reshape(-1, 1) * inv_freq
    return jnp.concatenate([jnp.cos(ang), jnp.sin(ang)], axis=1)


def _block_layout(hist, n_assign):
    tile_counts = hist[:, :, 0].astype(jnp.int32)
    counts = jnp.sum(tile_counts, axis=0)
    padded = (counts + EXPERT_BLOCK - 1) // EXPERT_BLOCK * EXPERT_BLOCK
    pad_end = jnp.cumsum(padded)
    pad_start = pad_end - padded
    tile_base = jnp.cumsum(tile_counts, axis=0) - tile_counts + pad_start[None, :]
    base = jnp.broadcast_to(tile_base.astype(F32)[:, :, None], tile_base.shape + (ROUTE_W,))
    n_blocks = -(-n_assign // EXPERT_BLOCK) + N_EXPERTS
    blk_row = jnp.arange(n_blocks, dtype=jnp.int32) * EXPERT_BLOCK
    blk_expert = jnp.minimum(jnp.sum((pad_end[None, :] <= blk_row[:, None]).astype(jnp.int32), axis=1),
                             N_EXPERTS - 1)
    n_used = (pad_end[-1] // EXPERT_BLOCK).astype(jnp.int32).reshape(1)
    return base, blk_expert, n_used, n_blocks, pad_end.astype(jnp.int32)


def kernel(x, positions, mix_norm_g, w_in, rw_mu, rw_w0, rw_w_up, rw_a0, rw_a_up, rw_g_up, rw_k_k, rw_k_a, rw_r_k, rw_gn_w, rw_gn_b, mla_g_qa, mla_w_q_up, mla_g_kva, mla_w_kv_up, w_branch_rw, w_branch_mla, w_out, ffn_norm_g, moe_w_group, moe_b_group, moe_w_router, moe_b_router, moe_w_gu, moe_w_down, final_norm_g):
    batch, seq, d = x.shape
    assert d == 2 * SLABS * LANES
    assert seq % RW_TILE == 0 and seq % ATT_TILE == 0 and seq % ROW_TILE == 0
    n = batch * seq
    assert n % DISPATCH_TILE == 0 and n % COMBINE_TILE == 0
    depth = w_in.shape[0]
    rw_cols = 3 * RW_DIM + W_LORA + A_LORA + G_LORA
    mla_cols = Q_LORA + KV_LORA + QK_ROPE
    cos_sin = _rope_cos_sin(positions)
    x2 = x.reshape(n, d)

    for l in range(depth):
        cut = (0, rw_cols, rw_cols + mla_cols, w_in.shape[2])
        w_rw, w_mla, w_gate = (w_in[l][:, cut[j]:cut[j + 1]].astype(BF16) for j in range(3))
        c_rw, c_mla, gates = _in_proj(x2, mix_norm_g[l].reshape(1, d), w_rw, w_mla, w_gate, tm=ROW_TILE)

        rp = _rwkv_params(rw_mu[l], rw_w0[l], rw_w_up[l], rw_a0[l], rw_a_up[l], rw_g_up[l], rw_k_k[l],
                          rw_k_a[l], rw_r_k[l], rw_gn_w[l], rw_gn_b[l])
        y_rw = _rwkv(c_rw.reshape(batch, seq, rw_cols), rp, batch, seq).reshape(n, RW_DIM)

        mp = _mla_params(mla_g_qa[l], mla_w_q_up[l], mla_g_kva[l], mla_w_kv_up[l])
        q_t, k, v_t = _mla_prep(c_mla, cos_sin, mp, ROW_TILE, batch, seq)
        y_mla = _mla_attn(q_t, k.reshape(batch, seq, -1), v_t, batch, seq).reshape(n, MLA_HEADS * V_HEAD)

        gap, tail = ROUTE_FINE0 - N_GROUPS, ROUTE_W - ROUTE_FINE0 - N_EXPERTS
        w_route = jnp.concatenate(
            [moe_w_group[l], jnp.zeros((d, gap), F32), moe_w_router[l], jnp.zeros((d, tail), F32)], axis=1)
        b_route = jnp.concatenate(
            [moe_b_group[l], jnp.zeros((gap,), F32), moe_b_router[l], jnp.zeros((tail,), F32)]).reshape(1, -1)
        wr_hi = w_route.astype(BF16)
        wr_lo = (w_route - wr_hi.astype(F32)).astype(BF16)
        mparams = {"w_br": w_branch_rw[l].astype(BF16), "w_bm": w_branch_mla[l].astype(BF16),
                   "w_out": w_out[l].astype(BF16), "ffn_g": ffn_norm_g[l].reshape(1, d),
                   "w_route": jnp.concatenate([wr_hi, wr_lo], axis=1), "b_route": b_route}
        x1, h2p, route, route_t, hist = _merge(x2, y_rw, y_mla, gates, mparams, tm=ROW_TILE)

        base, blk_expert, n_used, n_blocks, pad_end = _block_layout(hist, n * TOP_K)
        dest = _plan(route_t, base, tm=ROW_TILE)
        xs = _dispatch(h2p, dest, pad_end, n_blocks * EXPERT_BLOCK)
        yb = _experts(xs, blk_expert, n_used, moe_w_gu[l], moe_w_down[l])
        x2 = _combine(x1, route, dest, yb, final_norm_g.reshape(1, d), final_norm=(l == depth - 1))

    return x2.reshape(batch, seq, d)
```

```python
import functools
import math

import jax
import jax.numpy as jnp
from jax import lax
from jax.experimental import pallas as pl
from jax.experimental.pallas import tpu as pltpu

F32 = jnp.float32
BF16 = jnp.bfloat16

RW_HEADS = 8
RW_HEAD_DIM = 64
RW_DIM = RW_HEADS * RW_HEAD_DIM
W_LORA = 64
A_LORA = 64
G_LORA = 128
GN_EPS = 64e-5
MLA_HEADS = 8
QK_NOPE = 64
QK_ROPE = 32
V_HEAD = 64
Q_LORA = 384
KV_LORA = 256
ROPE_THETA = 10000.0
N_GROUPS = 4
EXPERTS_PER_GROUP = 8
N_EXPERTS = N_GROUPS * EXPERTS_PER_GROUP
TOP_K = 2
D_EXPERT = 256
EXPERT_BLOCK = 512
NORM_EPS = 1e-6

LANES = 128
HEAD_PAIR = 2 * RW_HEAD_DIM
VMEM_LIMIT = 48 * 1024 * 1024
ROW_TILE = 512


def _cparams(*sem):
    return pltpu.CompilerParams(dimension_semantics=sem, vmem_limit_bytes=VMEM_LIMIT)


def _mm(a, b, dims=((1,), (0,))):
    return lax.dot_general(a.astype(BF16), b.astype(BF16), (dims, ((), ())), preferred_element_type=F32)


def _mm_sel(sel_bf16, x, dims=((1,), (0,))):
    dn = (dims, ((), ()))
    hi = x.astype(BF16)
    lo = (x - hi.astype(F32)).astype(BF16)
    return (lax.dot_general(sel_bf16, hi, dn, preferred_element_type=F32)
            + lax.dot_general(sel_bf16, lo, dn, preferred_element_type=F32))


def _seg_sum(x, seg_bf16):
    return jnp.dot(x.astype(BF16), seg_bf16, preferred_element_type=F32)


def _rms(x, g):
    return x * lax.rsqrt(jnp.mean(x * x, axis=-1, keepdims=True) + NORM_EPS) * g


def _pack_rows(x):
    half = x.shape[1] // 2
    bits = lambda v: lax.bitcast_convert_type(v.astype(BF16).astype(F32), jnp.uint32)
    return bits(x[:, :half]) | (bits(x[:, half:]) >> 16)


def _unpack_rows(p):
    hi = lax.bitcast_convert_type(p & jnp.uint32(0xFFFF0000), F32)
    lo = lax.bitcast_convert_type(p << 16, F32)
    return jnp.concatenate([hi, lo], axis=1)


SLABS = 4


def _slab_rows(ref, r):
    return ref.at[pl.ds(pl.multiple_of(r * SLABS, SLABS), SLABS), :]


def _slab_load(ref):
    rows = ref.shape[0] // SLABS
    return jnp.concatenate([ref[pl.ds(j, rows, stride=SLABS), :] for j in range(SLABS)], axis=1)


def _slab_store(ref, x):
    rows = ref.shape[0] // SLABS
    for j in range(SLABS):
        ref[pl.ds(j, rows, stride=SLABS), :] = x[:, j * LANES:(j + 1) * LANES]


def _in_proj_kernel(x_ref, g_ref, wrw_ref, wmla_ref, wg_ref, crw_ref, cmla_ref, gate_ref):
    hb = _rms(x_ref[...], g_ref[...]).astype(BF16)
    crw_ref[...] = jnp.dot(hb, wrw_ref[...], preferred_element_type=F32)
    cmla_ref[...] = jnp.dot(hb, wmla_ref[...], preferred_element_type=F32)
    gate_ref[...] = jax.nn.sigmoid(jnp.dot(hb, wg_ref[...], preferred_element_type=F32)).astype(BF16)


def _in_proj(x2, g, w_rw, w_mla, w_gate, tm):
    n, d = x2.shape
    full = lambda w: pl.BlockSpec(w.shape, lambda i: (0, 0))
    row = lambda c: pl.BlockSpec((tm, c), lambda i: (i, 0))
    return pl.pallas_call(
        _in_proj_kernel,
        grid=(n // tm,),
        in_specs=[row(d), full(g), full(w_rw), full(w_mla), full(w_gate)],
        out_specs=[row(w_rw.shape[1]), row(w_mla.shape[1]), row(w_gate.shape[1])],
        out_shape=[jax.ShapeDtypeStruct((n, w_rw.shape[1]), F32),
                   jax.ShapeDtypeStruct((n, w_mla.shape[1]), F32),
                   jax.ShapeDtypeStruct((n, w_gate.shape[1]), BF16)],
        compiler_params=_cparams("parallel"),
        name="in_proj",
    )(x2, g, w_rw, w_mla, w_gate)


RW_CHUNK = 64
RW_TILE = 1024
def _token_shift(cur, halo_ref, first):
    prev_row = jnp.where(first, 0.0, halo_ref[0, 7:8, :])
    rolled = pltpu.roll(cur, 1, 0)
    row = lax.broadcasted_iota(jnp.int32, cur.shape, 0)
    return jnp.where(row == 0, prev_row, rolled)


def _rwkv_kernel(r_ref, k_ref, v_ref, l_ref, hr_ref, hk_ref, hv_ref, hl_ref,
                 mur_ref, muk_ref, muv_ref, mul_ref, w0_ref, a0_ref, kk_ref, ka_ref, rk_ref,
                 gnw_ref, gnb_ref, wup_ref, aup_ref, gup_ref, y_ref, st_ref):
    i = pl.program_id(2)
    first = i == 0

    @pl.when(first)
    def _():
        st_ref[...] = jnp.zeros_like(st_ref)

    def mixed(c_ref, h_ref, mu_ref):
        cur = c_ref[0]
        return cur + (_token_shift(cur, h_ref, first) - cur) * mu_ref[...]

    zr = mixed(r_ref, hr_ref, mur_ref)
    zk = mixed(k_ref, hk_ref, muk_ref)
    zv = mixed(v_ref, hv_ref, muv_ref)
    zl = mixed(l_ref, hl_ref, mul_ref)
    z_wa = zl[:, :LANES]
    z_g = zl[:, LANES:]

    lane = lax.broadcasted_iota(jnp.int32, (LANES, LANES), 1)
    sub = lax.broadcasted_iota(jnp.int32, (LANES, LANES), 0)
    same_head = (lane // RW_HEAD_DIM) == (sub // RW_HEAD_DIM)
    seg = jnp.where(same_head, 1.0, 0.0).astype(BF16)

    w = w0_ref[...] + _mm(jnp.tanh(z_wa), wup_ref[...])
    u = -w
    softplus = jnp.maximum(u, 0.0) + jnp.log(1.0 + jnp.exp(-jnp.abs(u)))
    log_decay = -jnp.exp(-softplus - 0.5)
    a = jax.nn.sigmoid(a0_ref[...] + _mm(z_wa, aup_ref[...]))
    g = _mm(jax.nn.sigmoid(z_g), gup_ref[...])

    kk = zk * kk_ref[...]
    kk = kk / jnp.maximum(jnp.sqrt(_seg_sum(kk * kk, seg)), 1e-12)
    k2 = zk * (1.0 + (a - 1.0) * ka_ref[...])
    bonus = _seg_sum(zr * k2 * rk_ref[...], seg) * zv
    kka = kk * a

    c = RW_CHUNK
    crow = lax.broadcasted_iota(jnp.int32, (c, c), 0)
    ccol = lax.broadcasted_iota(jnp.int32, (c, c), 1)
    cum_sel = jnp.where(crow >= ccol, 1.0, 0.0).astype(BF16)
    tril_incl = sub >= lane
    tril_strict = sub > lane
    eye_l = jnp.where(lane == sub, 1.0, 0.0).astype(F32)
    lo_half = lax.broadcasted_iota(jnp.int32, (c, LANES), 1) < RW_HEAD_DIM
    nt = ((1,), (1,))
    tn = ((0,), (0,))
    zeros_blk = jnp.zeros((2 * c, LANES), BF16)
    zeros_half = jnp.zeros((c, LANES), BF16)

    def stack(t):
        tb = t.astype(BF16)
        return jnp.concatenate([jnp.where(lo_half, tb, zeros_half), jnp.where(lo_half, zeros_half, tb)], axis=0)

    tril_incl2 = jnp.concatenate([tril_incl, tril_incl], axis=1)

    def chunk_stages(ids):
        chunks = range(len(ids))
        x_a, x_b, x_k, x_r, x_v, x_bh, x_kh, w_tot, r_dec = [], [], [], [], [], [], [], [], []
        for ci in ids:
            sl = slice(ci * c, (ci + 1) * c)
            ld = log_decay[sl]
            cum = _mm_sel(cum_sel, ld)
            tot = cum[c - 1:c, :]
            e_neg = jnp.exp(-cum)
            e_rest = jnp.exp(tot - cum)
            x_a.append(stack(-kk[sl] * jnp.exp(cum - ld)))
            x_b.append(stack(kka[sl] * e_neg))
            x_k.append(stack(k2[sl] * e_neg))
            r_dec.append(zr[sl] * jnp.exp(cum))
            x_r.append(stack(r_dec[-1]))
            x_v.append(stack(zv[sl]))
            x_bh.append(stack(kka[sl] * e_rest))
            x_kh.append(stack(k2[sl] * e_rest))
            w_tot.append(jnp.exp(tot))

        inter = [_mm(jnp.concatenate([x_a[i], x_r[i]], axis=0),
                     jnp.concatenate([x_b[i], x_k[i]], axis=0), nt) for i in chunks]
        inter = [m.astype(BF16) for m in inter]
        zeros_sq_b = jnp.zeros((LANES, LANES), BF16)
        a_ab = [jnp.where(tril_strict, m[:2 * c, :2 * c], zeros_sq_b) for m in inter]
        a_ak = [jnp.where(tril_strict, m[:2 * c, 2 * c:], zeros_sq_b) for m in inter]
        a_r = [jnp.where(tril_incl2, m[2 * c:], jnp.concatenate([zeros_sq_b, zeros_sq_b], axis=1))
               for m in inter]
        w_ak = [_mm(a_ak[i], x_v[i]).astype(BF16) for i in chunks]

        t_inv = [eye_l + m.astype(F32) for m in a_ab]
        pw = a_ab
        for _ in range(int(math.log2(c)) - 1):
            pw = [_mm(m, m).astype(BF16) for m in pw]
            t_inv = [t_inv[i] + _mm(t_inv[i], pw[i]) for i in chunks]

        solved = [_mm(t_inv[i], jnp.concatenate([x_a[i], w_ak[i]], axis=1)).astype(BF16)
                  for i in chunks]
        rhs = [jnp.concatenate([solved[i], jnp.concatenate([zeros_blk, x_v[i]], axis=1)], axis=0)
               for i in chunks]
        out = [_mm(a_r[i], rhs[i]) for i in chunks]
        carry = [_mm(jnp.concatenate([x_bh[i], x_kh[i]], axis=0), rhs[i], tn) for i in chunks]
        q_hat, y_loc = [], []
        for i in chunks:
            q_hat.append(r_dec[i] + out[i][:c, :LANES] + out[i][c:, :LANES])
            y_loc.append(out[i][:c, LANES:] + out[i][c:, LANES:])
        trans = [jnp.concatenate([eye_l * w_tot[i] + carry[i][:, :LANES], carry[i][:, LANES:]], axis=1)
                 for i in chunks]
        return q_hat, y_loc, trans

    n_chunks = RW_TILE // c
    chunks = range(n_chunks)
    q_hat, y_loc, trans = chunk_stages(chunks)

    zeros_sq = jnp.zeros((LANES, LANES), F32)

    def compose(later, earlier):
        return _mm(later[:, :LANES], earlier) + jnp.concatenate([zeros_sq, later[:, LANES:]], axis=1)

    scan = list(trans)
    dist = 1
    while dist < len(scan):
        scan = [scan[i] if i < dist else compose(scan[i], scan[i - dist]) for i in range(len(scan))]
        dist *= 2
    prefix = [None] + scan
    q_pre = [None] + [_mm(q_hat[i], prefix[i]) for i in chunks[1:]]
    q_m = jnp.concatenate([q_hat[0]] + [q_pre[i][:, :LANES] for i in chunks[1:]], axis=0)
    y_off = jnp.concatenate([y_loc[0]] + [y_loc[i] + q_pre[i][:, LANES:] for i in chunks[1:]], axis=0)
    state = st_ref[...]
    y = _mm(q_m, state) + y_off
    st_ref[...] = _mm(prefix[-1][:, :LANES], state) + prefix[-1][:, LANES:]

    inv_n = 1.0 / RW_HEAD_DIM
    mean = _seg_sum(y, seg) * inv_n
    d = y - mean
    var = _seg_sum(d * d, seg) * inv_n
    yn = d * lax.rsqrt(var + GN_EPS) * gnw_ref[...] + gnb_ref[...]
    y_ref[0] = ((yn + bonus) * g).astype(y_ref.dtype)


def _rwkv(c_rw, p, batch, seq):
    ts = RW_TILE
    n_pairs = RW_DIM // HEAD_PAIR
    lora_blk = (3 * RW_DIM) // (2 * LANES)
    halo = ts // 8

    def col(off):
        return pl.BlockSpec((1, ts, LANES), lambda b, pp, i, off=off: (b, i, off + pp))

    def col_halo(off):
        return pl.BlockSpec((1, 8, LANES),
                            lambda b, pp, i, off=off: (b, jnp.maximum(i * halo - 1, 0), off + pp))

    vec = pl.BlockSpec((1, LANES), lambda b, pp, i: (0, pp))
    lora_w = pl.BlockSpec((LANES, LANES), lambda b, pp, i: (0, pp))
    in_specs = [
        col(0), col(n_pairs), col(2 * n_pairs),
        pl.BlockSpec((1, ts, 2 * LANES), lambda b, pp, i: (b, i, lora_blk)),
        col_halo(0), col_halo(n_pairs), col_halo(2 * n_pairs),
        pl.BlockSpec((1, 8, 2 * LANES), lambda b, pp, i: (b, jnp.maximum(i * halo - 1, 0), lora_blk)),
        vec, vec, vec, pl.BlockSpec((1, 2 * LANES), lambda b, pp, i: (0, 0)),
        vec, vec, vec, vec, vec, vec, vec, lora_w, lora_w, lora_w,
    ]
    return pl.pallas_call(
        _rwkv_kernel,
        grid=(batch, n_pairs, seq // ts),
        in_specs=in_specs,
        out_specs=pl.BlockSpec((1, ts, LANES), lambda b, pp, i: (b, i, pp)),
        out_shape=jax.ShapeDtypeStruct((batch, seq, RW_DIM), BF16),
        scratch_shapes=[pltpu.VMEM((LANES, LANES), F32)],
        compiler_params=_cparams("parallel", "parallel", "arbitrary"),
        name="rwkv",
    )(c_rw, c_rw, c_rw, c_rw, c_rw, c_rw, c_rw, c_rw,
      p["mu_r"], p["mu_k"], p["mu_v"], p["mu_l"], p["w0"], p["a0"], p["k_k"], p["k_a"], p["r_k"],
      p["gn_w"], p["gn_b"], p["w_up"], p["a_up"], p["g_up"])


MLA_SLOT = 128


def _mla_prep_kernel(cmla_ref, cs_ref, gq_ref, gkv_ref,
                     wqa_ref, wqb_ref, wk_ref, wv_ref, pa_ref, pb_ref, place_ref, one_ref,
                     qt_ref, k_ref, vt_ref):
    cs = cs_ref[...]
    cs_hi = cs.astype(BF16)
    cs_lo = (cs - cs_hi.astype(F32)).astype(BF16)
    tables = (jnp.dot(cs_hi, place_ref[...], preferred_element_type=F32)
              + jnp.dot(cs_lo, place_ref[...], preferred_element_type=F32))
    cos = tables[:, :MLA_SLOT] + one_ref[...]
    sin = tables[:, MLA_SLOT:]
    zq = _rms(cmla_ref[:, :Q_LORA], gq_ref[...]).astype(BF16)
    qa = jnp.dot(zq, wqa_ref[...], preferred_element_type=F32)
    qb = jnp.dot(zq, wqb_ref[...], preferred_element_type=F32)
    ckvr = cmla_ref[:, Q_LORA:]
    zkv = _rms(ckvr[:, :KV_LORA], gkv_ref[...]).astype(BF16)
    kn = jnp.dot(zkv, wk_ref[...], preferred_element_type=F32)
    v = jnp.dot(zkv, wv_ref[...], preferred_element_type=F32)
    for blk in range(v.shape[1] // LANES):
        vt_ref[0, blk * LANES:(blk + 1) * LANES, :] = v[:, blk * LANES:(blk + 1) * LANES].astype(BF16).T
    kr = ckvr[:, KV_LORA:].astype(BF16)
    k_rope = (jnp.dot(kr, pa_ref[...], preferred_element_type=F32) * cos
              + jnp.dot(kr, pb_ref[...], preferred_element_type=F32) * sin)
    scale = math.log2(math.e) / math.sqrt(QK_NOPE + QK_ROPE)
    for h in range(MLA_HEADS):
        sl = slice(h * MLA_SLOT, (h + 1) * MLA_SLOT)
        qt_ref[0, sl, :] = ((qa[:, sl] * cos + qb[:, sl] * sin) * scale).astype(BF16).T
        k_ref[:, sl] = (kn[:, sl] + k_rope).astype(BF16)


def _mla_prep(c_mla, cos_sin, p, tm, batch, seq):
    n = c_mla.shape[0]
    full = lambda w: pl.BlockSpec(w.shape, lambda i: (0, 0))
    row = lambda c: pl.BlockSpec((tm, c), lambda i: (i, 0))
    per_seq = seq // tm
    col = lambda r: pl.BlockSpec((1, r, tm), lambda i: (i // per_seq, 0, i % per_seq))
    ws = [p["g_qa"], p["g_kva"], p["w_qa"], p["w_qb"], p["w_k"], p["w_v"], p["p_a"], p["p_b"],
          p["place"], p["one"]]
    hq = MLA_HEADS * MLA_SLOT
    hv = MLA_HEADS * V_HEAD
    return pl.pallas_call(
        _mla_prep_kernel,
        grid=(n // tm,),
        in_specs=[row(c_mla.shape[1]), row(cos_sin.shape[1])] + [full(w) for w in ws],
        out_specs=[col(hq), row(hq), col(hv)],
        out_shape=[jax.ShapeDtypeStruct((batch, hq, seq), BF16), jax.ShapeDtypeStruct((n, hq), BF16),
                   jax.ShapeDtypeStruct((batch, hv, seq), BF16)],
        compiler_params=_cparams("parallel"),
        name="mla_prep",
    )(c_mla, cos_sin, *ws)


ATT_TILE = 512
ATT_HEADS = 4


def _attn_kernel(qt_ref, k_ref, vt_ref, o_ref):
    qi = pl.program_id(2)
    t = ATT_TILE
    heads = range(ATT_HEADS)
    den_row = (V_HEAD, 0)

    def augment(h, vt):
        r = lax.broadcasted_iota(jnp.int32, vt.shape, 0)
        own = (r < V_HEAD) if h % 2 == 0 else (r >= V_HEAD)
        return jnp.where(own, vt, jnp.where(r == den_row[h % 2], 1.0, 0.0).astype(BF16))

    def block(j, carry, mask):
        keys = pl.ds(pl.multiple_of(j * t, t), t)
        sts = []
        for h in heads:
            qt = qt_ref[0, h * MLA_SLOT:(h + 1) * MLA_SLOT, :]
            kb = k_ref[0, keys, h * MLA_SLOT:(h + 1) * MLA_SLOT]
            st = jnp.dot(kb, qt, preferred_element_type=F32)
            sts.append(st if mask is None else jnp.where(mask, st, -jnp.inf))
        m_new = [jnp.maximum(carry[h][0], jnp.max(sts[h], axis=0, keepdims=True)) for h in heads]
        pts = [jnp.exp2((sts[h] - m_new[h]).astype(BF16)) for h in heads]
        out = []
        for h in heads:
            m, acc = carry[h]
            vt = vt_ref[0, (h // 2) * 2 * V_HEAD:(h // 2 + 1) * 2 * V_HEAD, keys]
            acc = acc * jnp.exp2(m - m_new[h]) + jnp.dot(augment(h, vt), pts[h], preferred_element_type=F32)
            out.append((m_new[h], acc))
        return tuple(out)

    init1 = (jnp.full((1, t), -jnp.inf, F32), jnp.zeros((2 * V_HEAD, t), F32))
    carry = lax.fori_loop(0, qi, lambda j, c: block(j, c, None), tuple(init1 for _ in heads))

    causal = lax.broadcasted_iota(jnp.int32, (t, t), 0) <= lax.broadcasted_iota(jnp.int32, (t, t), 1)
    accs = [mc[1] for mc in block(qi, carry, causal)]
    lo_rows = lax.broadcasted_iota(jnp.int32, (2 * V_HEAD, t), 0) < V_HEAD
    for p in range(ATT_HEADS // 2):
        acc0, acc1 = accs[2 * p], accs[2 * p + 1]
        den0 = acc0[den_row[0]:den_row[0] + 1, :]
        den1 = acc1[den_row[1]:den_row[1] + 1, :]
        out_t = jnp.where(lo_rows, acc0 / den0, acc1 / den1)
        o_ref[0, :, p * 2 * V_HEAD:(p + 1) * 2 * V_HEAD] = out_t.T.astype(o_ref.dtype)


def _mla_attn(qt, k, vt, batch, seq):
    t = ATT_TILE
    g = ATT_HEADS
    return pl.pallas_call(
        _attn_kernel,
        grid=(batch, MLA_HEADS // g, seq // t),
        in_specs=[pl.BlockSpec((1, g * MLA_SLOT, t), lambda b, hp, i: (b, hp, i)),
                  pl.BlockSpec((1, seq, g * MLA_SLOT), lambda b, hp, i: (b, 0, hp)),
                  pl.BlockSpec((1, g * V_HEAD, seq), lambda b, hp, i: (b, hp, 0))],
        out_specs=pl.BlockSpec((1, t, g * V_HEAD), lambda b, hp, i: (b, i, hp)),
        out_shape=jax.ShapeDtypeStruct((batch, seq, MLA_HEADS * V_HEAD), BF16),
        compiler_params=_cparams("parallel", "parallel", "arbitrary"),
        name="mla_attn",
    )(qt, k, vt)


ROUTE_W = 128
ROUTE_FINE0 = 8


def _merge_kernel(x_ref, yrw_ref, ymla_ref, gate_ref, wbr_ref, wbm_ref, wo_ref, fg_ref,
                  wr_ref, br_ref, x1_ref, h2p_ref, route_ref, route_t_ref, hist_ref):
    d = x_ref.shape[1]
    a = jnp.dot(yrw_ref[...], wbr_ref[...], preferred_element_type=F32)
    b = jnp.dot(ymla_ref[...], wbm_ref[...], preferred_element_type=F32)
    merged = gate_ref[:, :d].astype(F32) * a + gate_ref[:, d:].astype(F32) * b
    x1 = x_ref[...] + jnp.dot(merged.astype(BF16), wo_ref[...], preferred_element_type=F32)
    x1_ref[...] = x1
    h2 = _rms(x1, fg_ref[...])
    _slab_store(h2p_ref, _pack_rows(h2))

    h_hi = h2.astype(BF16)
    h_lo = (h2 - h_hi.astype(F32)).astype(BF16)
    both = jnp.dot(h_hi, wr_ref[...], preferred_element_type=F32)
    logits = (both[:, :ROUTE_W] + both[:, ROUTE_W:]
              + jnp.dot(h_lo, wr_ref[:, :ROUTE_W], preferred_element_type=F32)) + br_ref[...]

    tm = logits.shape[0]
    lt = logits.T
    sub = lax.broadcasted_iota(jnp.int32, (EXPERTS_PER_GROUP, tm), 0)
    big = jnp.int32(EXPERTS_PER_GROUP)
    neg = -jnp.inf

    def first_argmax(vals, vmax):
        return jnp.min(jnp.where(vals == vmax, sub, big), axis=0, keepdims=True)

    grp = jnp.where(sub < N_GROUPS, lt[:EXPERTS_PER_GROUP], neg)
    g_max = jnp.max(grp, axis=0, keepdims=True)
    g_den = jnp.sum(jnp.exp(grp - g_max), axis=0, keepdims=True)
    g_sel = first_argmax(grp, g_max)
    gate_g = 1.0 / g_den
    fine = lt[ROUTE_FINE0:ROUTE_FINE0 + EXPERTS_PER_GROUP]
    for g in range(1, N_GROUPS):
        lo = ROUTE_FINE0 + g * EXPERTS_PER_GROUP
        fine = jnp.where(g_sel == g, lt[lo:lo + EXPERTS_PER_GROUP], fine)
    v1 = jnp.max(fine, axis=0, keepdims=True)
    i1 = first_argmax(fine, v1)
    fine2 = jnp.where(sub == i1, neg, fine)
    v2 = jnp.max(fine2, axis=0, keepdims=True)
    i2 = first_argmax(fine2, v2)
    e2 = jnp.exp(v2 - v1)
    den = 1.0 + e2
    w1 = gate_g / den
    w2 = gate_g * e2 / den
    base_e = g_sel * EXPERTS_PER_GROUP
    route_t = jnp.where(sub == 0, (base_e + i1).astype(F32),
                        jnp.where(sub == 1, (base_e + i2).astype(F32),
                                  jnp.where(sub == 2, w1, jnp.where(sub == 3, w2, 0.0))))
    pad = jnp.zeros((ROUTE_W - EXPERTS_PER_GROUP, tm), F32)
    route_t_ref[...] = route_t
    route_ref[...] = jnp.concatenate([route_t, pad], axis=0).T
    picked = (sub == i1) | (sub == i2)
    for g in range(N_GROUPS):
        cnt = jnp.sum(jnp.where(picked & (g_sel == g), 1.0, 0.0), axis=1, keepdims=True)
        hist_ref[0, g * EXPERTS_PER_GROUP:(g + 1) * EXPERTS_PER_GROUP, :] = jnp.broadcast_to(
            cnt, (EXPERTS_PER_GROUP, ROUTE_W))


def _merge(x2, y_rw, y_mla, gates, p, tm):
    n, d = x2.shape
    full = lambda w: pl.BlockSpec(w.shape, lambda i: (0, 0))
    row = lambda c: pl.BlockSpec((tm, c), lambda i: (i, 0))
    ws = [p["w_br"], p["w_bm"], p["w_out"], p["ffn_g"], p["w_route"], p["b_route"]]
    return pl.pallas_call(
        _merge_kernel,
        grid=(n // tm,),
        in_specs=[row(d), row(y_rw.shape[1]), row(y_mla.shape[1]), row(2 * d)] + [full(w) for w in ws],
        out_specs=[row(d), pl.BlockSpec((tm * SLABS, LANES), lambda i: (i, 0)), row(ROUTE_W),
                   pl.BlockSpec((EXPERTS_PER_GROUP, tm), lambda i: (0, i)),
                   pl.BlockSpec((1, N_EXPERTS, ROUTE_W), lambda i: (i, 0, 0))],
        out_shape=[jax.ShapeDtypeStruct((n, d), F32), jax.ShapeDtypeStruct((n * SLABS, LANES), jnp.uint32),
                   jax.ShapeDtypeStruct((n, ROUTE_W), F32),
                   jax.ShapeDtypeStruct((EXPERTS_PER_GROUP, n), F32),
                   jax.ShapeDtypeStruct((n // tm, N_EXPERTS, ROUTE_W), F32)],
        compiler_params=_cparams("parallel"),
        name="merge_route",
    )(x2, y_rw, y_mla, gates, *ws)


def _plan_kernel(route_t_ref, base_ref, dest_ref):
    tm = route_t_ref.shape[1]
    rt = route_t_ref[...]
    expert = lax.broadcasted_iota(jnp.int32, (N_EXPERTS, tm), 0).astype(F32)
    pick = [expert == rt[k:k + 1, :] for k in range(TOP_K)]
    both = jnp.where(pick[0] | pick[1], 1.0, 0.0).astype(BF16)
    r = lax.broadcasted_iota(jnp.int32, (tm, tm), 0)
    c = lax.broadcasted_iota(jnp.int32, (tm, tm), 1)
    earlier = jnp.where(r < c, 1.0, 0.0).astype(BF16)
    offs = jnp.dot(both, earlier, preferred_element_type=F32) + base_ref[0][:, :1]
    rows = [jnp.sum(jnp.where(pk, offs, 0.0), axis=0, keepdims=True) for pk in pick]
    sub = lax.broadcasted_iota(jnp.int32, dest_ref.shape, 0)
    dest_ref[...] = jnp.where(sub == 0, rows[0], jnp.where(sub == 1, rows[1], 0.0)).astype(jnp.int32)


def _plan(route_t, base, tm):
    n = route_t.shape[1]
    return pl.pallas_call(
        _plan_kernel,
        grid=(n // tm,),
        in_specs=[pl.BlockSpec((EXPERTS_PER_GROUP, tm), lambda i: (0, i)),
                  pl.BlockSpec((1, N_EXPERTS, ROUTE_W), lambda i: (i, 0, 0))],
        out_specs=pl.BlockSpec((EXPERTS_PER_GROUP, tm), lambda i: (0, i)),
        out_shape=jax.ShapeDtypeStruct((EXPERTS_PER_GROUP, n), jnp.int32),
        compiler_params=_cparams("parallel"),
        name="route_plan",
    )(route_t, base)


def _dest_tiles(dest_t, tm):
    n = dest_t.shape[1]
    return dest_t[:TOP_K].reshape(TOP_K, n // tm, tm).transpose(1, 0, 2).reshape(n // tm, 1, TOP_K * tm)


DISPATCH_TILE = 2048
ROW_DMA_UNROLL = 8


def _dispatch_kernel(pad_end_ref, dest_ref, h_ref, xs_ref, zbuf, sem, zsem):
    tm = h_ref.shape[0] // SLABS

    @pl.when(pl.program_id(0) == 0)
    def _():
        zbuf[...] = jnp.zeros_like(zbuf)

        def tail(e):
            first = pl.multiple_of((pad_end_ref[e] - EXPERT_BLOCK) * SLABS, EXPERT_BLOCK * SLABS)
            return pltpu.make_async_copy(zbuf, xs_ref.at[pl.ds(first, EXPERT_BLOCK * SLABS), :], zsem)

        def region_rows(e):
            return pad_end_ref[e] - (pad_end_ref[e - 1] if e else 0)

        n_blocks = xs_ref.shape[0] // (EXPERT_BLOCK * SLABS)
        used_rows = pad_end_ref[N_EXPERTS - 1]

        def spare(b):
            return pltpu.make_async_copy(zbuf, xs_ref.at[pl.ds(b * EXPERT_BLOCK * SLABS, EXPERT_BLOCK * SLABS), :],
                                         zsem)

        spare_blocks = range(n_blocks - N_EXPERTS, n_blocks)
        for e in range(N_EXPERTS):
            @pl.when(region_rows(e) > 0)
            def _(e=e):
                tail(e).start()
        for b in spare_blocks:
            @pl.when(b * EXPERT_BLOCK >= used_rows)
            def _(b=b):
                spare(b).start()
        for e in range(N_EXPERTS):
            @pl.when(region_rows(e) > 0)
            def _(e=e):
                tail(e).wait()
        for b in spare_blocks:
            @pl.when(b * EXPERT_BLOCK >= used_rows)
            def _(b=b):
                spare(b).wait()

    def start(t, _):
        for k in range(TOP_K):
            pltpu.make_async_copy(_slab_rows(h_ref, t), _slab_rows(xs_ref, dest_ref[0, 0, k * tm + t]),
                                  sem).start(priority=k % 2)
        return 0

    lax.fori_loop(0, tm, start, 0, unroll=ROW_DMA_UNROLL)
    all_rows = xs_ref.at[pl.ds(0, TOP_K * tm * SLABS), :]
    pltpu.make_async_copy(all_rows, all_rows, sem).wait()


def _dispatch(h2, dest, pad_end, p_rows):
    n = h2.shape[0] // SLABS
    tm = DISPATCH_TILE
    dest3 = _dest_tiles(dest, tm)
    grid_spec = pltpu.PrefetchScalarGridSpec(
        num_scalar_prefetch=1,
        grid=(n // tm,),
        in_specs=[pl.BlockSpec((1, 1, TOP_K * tm), lambda i, pe: (i, 0, 0), memory_space=pltpu.SMEM),
                  pl.BlockSpec((tm * SLABS, LANES), lambda i, pe: (i, 0))],
        out_specs=pl.BlockSpec(memory_space=pl.ANY),
        scratch_shapes=[pltpu.VMEM((EXPERT_BLOCK * SLABS, LANES), jnp.uint32),
                        pltpu.SemaphoreType.DMA(()), pltpu.SemaphoreType.DMA(())],
    )
    return pl.pallas_call(
        _dispatch_kernel,
        grid_spec=grid_spec,
        out_shape=jax.ShapeDtypeStruct((p_rows * SLABS, LANES), jnp.uint32),
        compiler_params=_cparams("arbitrary"),
        name="dispatch",
    )(pad_end, dest3, h2)


def _expert_kernel(blk_e_ref, n_used_ref, x_ref, wgu_ref, wd_ref, y_ref, wgu_b, wd_b):
    i = pl.program_id(0)

    @pl.when((i == 0) | (blk_e_ref[i] != blk_e_ref[jnp.maximum(i - 1, 0)]))
    def _():
        wgu_b[...] = wgu_ref[0].astype(BF16)
        wd_b[...] = wd_ref[0].astype(BF16)

    @pl.when(i < n_used_ref[0])
    def _():
        x = _unpack_rows(_slab_load(x_ref)).astype(BF16)
        h = jnp.dot(x, wgu_b[...], preferred_element_type=F32)
        gt = h[:, :D_EXPERT]
        up = h[:, D_EXPERT:]
        act = (gt * jax.nn.sigmoid(gt) * up).astype(BF16)
        _slab_store(y_ref, _pack_rows(jnp.dot(act, wd_b[...], preferred_element_type=F32)))

    @pl.when(pl.program_id(0) >= n_used_ref[0])
    def _():
        y_ref[...] = jnp.zeros_like(y_ref)


def _experts(xs, blk_expert, n_used, w_gu, w_down):
    p_rows = xs.shape[0] // SLABS
    d = 2 * SLABS * LANES
    n_blocks = p_rows // EXPERT_BLOCK
    rows = pl.BlockSpec((EXPERT_BLOCK * SLABS, LANES), lambda i, be, nu: (i, 0))
    grid_spec = pltpu.PrefetchScalarGridSpec(
        num_scalar_prefetch=2,
        grid=(n_blocks,),
        in_specs=[rows,
                  pl.BlockSpec((1, d, 2 * D_EXPERT), lambda i, be, nu: (be[i], 0, 0)),
                  pl.BlockSpec((1, D_EXPERT, d), lambda i, be, nu: (be[i], 0, 0))],
        out_specs=rows,
        scratch_shapes=[pltpu.VMEM((d, 2 * D_EXPERT), BF16), pltpu.VMEM((D_EXPERT, d), BF16)],
    )
    return pl.pallas_call(
        _expert_kernel,
        grid_spec=grid_spec,
        out_shape=jax.ShapeDtypeStruct(xs.shape, jnp.uint32),
        compiler_params=_cparams("arbitrary"),
        name="experts",
    )(blk_expert, n_used, xs, w_gu, w_down)


COMBINE_TILE = 1024


def _combine_kernel(dest_ref, dest_next_ref, x1_ref, route_ref, g_ref, yb_ref, o_ref,
                    buf00, buf01, buf10, buf11, sems, *, final_norm):
    tm = x1_ref.shape[0]
    i = pl.program_id(0)
    bufs = ((buf00, buf01), (buf10, buf11))

    def issue(d_ref, slot):
        def start(t, _):
            for k in range(TOP_K):
                pltpu.make_async_copy(_slab_rows(yb_ref, d_ref[0, 0, k * tm + t]),
                                      _slab_rows(bufs[slot][k], t), sems.at[slot]).start(priority=k % 2)
            return 0

        lax.fori_loop(0, tm, start, 0, unroll=ROW_DMA_UNROLL)

    @pl.when(i == 0)
    def _():
        issue(dest_ref, 0)

    for slot in range(2):
        @pl.when((i % 2 == slot) & (i + 1 < pl.num_programs(0)))
        def _(slot=slot):
            issue(dest_next_ref, 1 - slot)

    for slot in range(2):
        @pl.when(i % 2 == slot)
        def _(slot=slot):
            for b in bufs[slot]:
                pltpu.make_async_copy(b, b, sems.at[slot]).wait()
            route = route_ref[...]
            x2 = (x1_ref[...] + route[:, 2:3] * _unpack_rows(_slab_load(bufs[slot][0]))
                  + route[:, 3:4] * _unpack_rows(_slab_load(bufs[slot][1])))
            o_ref[...] = _rms(x2, g_ref[...]) if final_norm else x2


def _combine(x1, route, dest, yb, final_g, final_norm):
    n, d = x1.shape
    tm = COMBINE_TILE
    n_tiles = n // tm
    dest3 = _dest_tiles(dest, tm)
    return pl.pallas_call(
        functools.partial(_combine_kernel, final_norm=final_norm),
        grid=(n_tiles,),
        in_specs=[pl.BlockSpec((1, 1, TOP_K * tm), lambda i: (i, 0, 0), memory_space=pltpu.SMEM),
                  pl.BlockSpec((1, 1, TOP_K * tm), lambda i: (jnp.minimum(i + 1, n_tiles - 1), 0, 0),
                               memory_space=pltpu.SMEM),
                  pl.BlockSpec((tm, d), lambda i: (i, 0)),
                  pl.BlockSpec((tm, ROUTE_W), lambda i: (i, 0)),
                  pl.BlockSpec((1, d), lambda i: (0, 0)),
                  pl.BlockSpec(memory_space=pl.ANY)],
        out_specs=pl.BlockSpec((tm, d), lambda i: (i, 0)),
        out_shape=jax.ShapeDtypeStruct((n, d), F32),
        scratch_shapes=[pltpu.VMEM((tm * SLABS, LANES), jnp.uint32) for _ in range(2 * TOP_K)]
                       + [pltpu.SemaphoreType.DMA((2,))],
        compiler_params=_cparams("arbitrary"),
        name="combine",
    )(dest3, dest3, x1, route, final_g, yb)


def _rwkv_params(rw_mu, rw_w0, rw_w_up, rw_a0, rw_a_up, rw_g_up, rw_k_k, rw_k_a, rw_r_k, rw_gn_w, rw_gn_b):
    row = lambda v: v.reshape(1, -1).astype(F32)
    zeros = jnp.zeros((A_LORA, RW_DIM), F32)
    return {
        "mu_r": row(rw_mu[:RW_DIM]), "mu_k": row(rw_mu[RW_DIM:2 * RW_DIM]),
        "mu_v": row(rw_mu[2 * RW_DIM:3 * RW_DIM]), "mu_l": row(rw_mu[3 * RW_DIM:]),
        "w0": row(rw_w0), "a0": row(rw_a0), "k_k": row(rw_k_k), "k_a": row(rw_k_a),
        "r_k": row(rw_r_k), "gn_w": row(rw_gn_w), "gn_b": row(rw_gn_b),
        "w_up": jnp.concatenate([rw_w_up, zeros], axis=0).astype(BF16),
        "a_up": jnp.concatenate([zeros, rw_a_up], axis=0).astype(BF16),
        "g_up": rw_g_up.astype(BF16),
    }


def _mla_params(g_qa, w_q_up, g_kva, w_kv_up):
    half = QK_ROPE // 2
    pad = MLA_SLOT - QK_NOPE - QK_ROPE
    wq = w_q_up.reshape(Q_LORA, MLA_HEADS, QK_NOPE + QK_ROPE)
    q_nope, q_r1, q_r2 = wq[..., :QK_NOPE], wq[..., QK_NOPE:QK_NOPE + half], wq[..., QK_NOPE + half:]
    zq = lambda w: jnp.zeros((Q_LORA, MLA_HEADS, w), F32)
    w_qa = jnp.concatenate([q_nope, q_r1, q_r2, zq(pad)], axis=-1).reshape(Q_LORA, -1)
    w_qb = jnp.concatenate([zq(QK_NOPE), -q_r2, q_r1, zq(pad)], axis=-1).reshape(Q_LORA, -1)
    wkv = w_kv_up.reshape(KV_LORA, MLA_HEADS, QK_NOPE + V_HEAD)
    w_k = jnp.concatenate([wkv[..., :QK_NOPE], jnp.zeros((KV_LORA, MLA_HEADS, MLA_SLOT - QK_NOPE), F32)],
                          axis=-1).reshape(KV_LORA, -1)
    w_v = wkv[..., QK_NOPE:].reshape(KV_LORA, -1)
    eye = jnp.eye(half, dtype=F32)
    z = jnp.zeros((half, half), F32)
    zl = jnp.zeros((QK_ROPE, QK_NOPE), F32)
    zr = jnp.zeros((QK_ROPE, pad), F32)
    p_a = jnp.concatenate([zl, jnp.concatenate([eye, z], 0), jnp.concatenate([z, eye], 0), zr], axis=1)
    p_b = jnp.concatenate([zl, jnp.concatenate([z, -eye], 0), jnp.concatenate([eye, z], 0), zr], axis=1)
    place_half = jnp.concatenate([jnp.zeros((half, QK_NOPE), F32), eye, eye, jnp.zeros((half, pad), F32)], axis=1)
    zh = jnp.zeros_like(place_half)
    place = jnp.concatenate([jnp.concatenate([place_half, zh], 1), jnp.concatenate([zh, place_half], 1)], 0)
    one = jnp.concatenate([jnp.ones((1, QK_NOPE), F32), jnp.zeros((1, MLA_SLOT - QK_NOPE), F32)], axis=1)
    return {"g_qa": g_qa.reshape(1, -1), "g_kva": g_kva.reshape(1, -1),
            "w_qa": w_qa.astype(BF16), "w_qb": w_qb.astype(BF16), "w_k": w_k.astype(BF16),
            "w_v": w_v.astype(BF16), "p_a": p_a.astype(BF16), "p_b": p_b.astype(BF16),
            "place": place.astype(BF16), "one": one}


def _rope_cos_sin(positions):
    inv_freq = ROPE_THETA ** (-jnp.arange(0, QK_ROPE, 2, dtype=F32) / QK_ROPE)
    ang = positions.astype(F32).reshape(-1, 1) * inv_freq
    return jnp.concatenate([jnp.cos(ang), jnp.sin(ang)], axis=1)


def _block_layout(hist, n_assign):
    tile_counts = hist[:, :, 0].astype(jnp.int32)
    counts = jnp.sum(tile_counts, axis=0)
    padded = (counts + EXPERT_BLOCK - 1) // EXPERT_BLOCK * EXPERT_BLOCK
    pad_end = jnp.cumsum(padded)
    pad_start = pad_end - padded
    tile_base = jnp.cumsum(tile_counts, axis=0) - tile_counts + pad_start[None, :]
    base = jnp.broadcast_to(tile_base.astype(F32)[:, :, None], tile_base.shape + (ROUTE_W,))
    n_blocks = -(-n_assign // EXPERT_BLOCK) + N_EXPERTS
    blk_row = jnp.arange(n_blocks, dtype=jnp.int32) * EXPERT_BLOCK
    blk_expert = jnp.minimum(jnp.sum((pad_end[None, :] <= blk_row[:, None]).astype(jnp.int32), axis=1),
                             N_EXPERTS - 1)
    n_used = (pad_end[-1] // EXPERT_BLOCK).astype(jnp.int32).reshape(1)
    return base, blk_expert, n_used, n_blocks, pad_end.astype(jnp.int32)


def kernel(x, positions, mix_norm_g, w_in, rw_mu, rw_w0, rw_w_up, rw_a0, rw_a_up, rw_g_up, rw_k_k, rw_k_a, rw_r_k, rw_gn_w, rw_gn_b, mla_g_qa, mla_w_q_up, mla_g_kva, mla_w_kv_up, w_branch_rw, w_branch_mla, w_out, ffn_norm_g, moe_w_group, moe_b_group, moe_w_router, moe_b_router, moe_w_gu, moe_w_down, final_norm_g):
    batch, seq, d = x.shape
    assert d == 2 * SLABS * LANES
    assert seq % RW_TILE == 0 and seq % ATT_TILE == 0 and seq % ROW_TILE == 0
    n = batch * seq
    assert n % DISPATCH_TILE == 0 and n % COMBINE_TILE == 0
    depth = w_in.shape[0]
    rw_cols = 3 * RW_DIM + W_LORA + A_LORA + G_LORA
    mla_cols = Q_LORA + KV_LORA + QK_ROPE
    cos_sin = _rope_cos_sin(positions)
    x2 = x.reshape(n, d)

    for l in range(depth):
        cut = (0, rw_cols, rw_cols + mla_cols, w_in.shape[2])
        w_rw, w_mla, w_gate = (w_in[l][:, cut[j]:cut[j + 1]].astype(BF16) for j in range(3))
        c_rw, c_mla, gates = _in_proj(x2, mix_norm_g[l].reshape(1, d), w_rw, w_mla, w_gate, tm=ROW_TILE)

        rp = _rwkv_params(rw_mu[l], rw_w0[l], rw_w_up[l], rw_a0[l], rw_a_up[l], rw_g_up[l], rw_k_k[l],
                          rw_k_a[l], rw_r_k[l], rw_gn_w[l], rw_gn_b[l])
        y_rw = _rwkv(c_rw.reshape(batch, seq, rw_cols), rp, batch, seq).reshape(n, RW_DIM)

        mp = _mla_params(mla_g_qa[l], mla_w_q_up[l], mla_g_kva[l], mla_w_kv_up[l])
        q_t, k, v_t = _mla_prep(c_mla, cos_sin, mp, ROW_TILE, batch, seq)
        y_mla = _mla_attn(q_t, k.reshape(batch, seq, -1), v_t, batch, seq).reshape(n, MLA_HEADS * V_HEAD)

        gap, tail = ROUTE_FINE0 - N_GROUPS, ROUTE_W - ROUTE_FINE0 - N_EXPERTS
        w_route = jnp.concatenate(
            [moe_w_group[l], jnp.zeros((d, gap), F32), moe_w_router[l], jnp.zeros((d, tail), F32)], axis=1)
        b_route = jnp.concatenate(
            [moe_b_group[l], jnp.zeros((gap,), F32), moe_b_router[l], jnp.zeros((tail,), F32)]).reshape(1, -1)
        wr_hi = w_route.astype(BF16)
        wr_lo = (w_route - wr_hi.astype(F32)).astype(BF16)
        mparams = {"w_br": w_branch_rw[l].astype(BF16), "w_bm": w_branch_mla[l].astype(BF16),
                   "w_out": w_out[l].astype(BF16), "ffn_g": ffn_norm_g[l].reshape(1, d),
                   "w_route": jnp.concatenate([wr_hi, wr_lo], axis=1), "b_route": b_route}
        x1, h2p, route, route_t, hist = _merge(x2, y_rw, y_mla, gates, mparams, tm=ROW_TILE)

        base, blk_expert, n_used, n_blocks, pad_end = _block_layout(hist, n * TOP_K)
        dest = _plan(route_t, base, tm=ROW_TILE)
        xs = _dispatch(h2p, dest, pad_end, n_blocks * EXPERT_BLOCK)
        yb = _experts(xs, blk_expert, n_used, moe_w_gu[l], moe_w_down[l])
        x2 = _combine(x1, route, dest, yb, final_norm_g.reshape(1, d), final_norm=(l == depth - 1))

    return x2.reshape(batch, seq, d)
```

```python
import functools
import math

import jax
import jax.numpy as jnp
from jax import lax
from jax.experimental import pallas as pl
from jax.experimental.pallas import tpu as pltpu

F32 = jnp.float32
BF16 = jnp.bfloat16

RW_HEADS = 8
RW_HEAD_DIM = 64
RW_DIM = RW_HEADS * RW_HEAD_DIM
W_LORA = 64
A_LORA = 64
G_LORA = 128
GN_EPS = 64e-5
MLA_HEADS = 8
QK_NOPE = 64
QK_ROPE = 32
V_HEAD = 64
Q_LORA = 384
KV_LORA = 256
ROPE_THETA = 10000.0
N_GROUPS = 4
EXPERTS_PER_GROUP = 8
N_EXPERTS = N_GROUPS * EXPERTS_PER_GROUP
TOP_K = 2
D_EXPERT = 256
EXPERT_BLOCK = 512
NORM_EPS = 1e-6

LANES = 128
HEAD_PAIR = 2 * RW_HEAD_DIM
VMEM_LIMIT = 48 * 1024 * 1024
ROW_TILE = 1024


def _cparams(*sem):
    return pltpu.CompilerParams(dimension_semantics=sem, vmem_limit_bytes=VMEM_LIMIT)


def _mm(a, b, dims=((1,), (0,))):
    return lax.dot_general(a.astype(BF16), b.astype(BF16), (dims, ((), ())), preferred_element_type=F32)


def _mm_sel(sel_bf16, x, dims=((1,), (0,))):
    dn = (dims, ((), ()))
    hi = x.astype(BF16)
    lo = (x - hi.astype(F32)).astype(BF16)
    return (lax.dot_general(sel_bf16, hi, dn, preferred_element_type=F32)
            + lax.dot_general(sel_bf16, lo, dn, preferred_element_type=F32))


def _seg_sum(x, seg_bf16):
    return jnp.dot(x.astype(BF16), seg_bf16, preferred_element_type=F32)


def _rms(x, g):
    return x * lax.rsqrt(jnp.mean(x * x, axis=-1, keepdims=True) + NORM_EPS) * g


def _pack_rows(x):
    half = x.shape[1] // 2
    bits = lambda v: lax.bitcast_convert_type(v.astype(BF16).astype(F32), jnp.uint32)
    return bits(x[:, :half]) | (bits(x[:, half:]) >> 16)


def _unpack_rows(p):
    hi = lax.bitcast_convert_type(p & jnp.uint32(0xFFFF0000), F32)
    lo = lax.bitcast_convert_type(p << 16, F32)
    return jnp.concatenate([hi, lo], axis=1)


SLABS = 4


def _slab_rows(ref, r):
    return ref.at[pl.ds(pl.multiple_of(r * SLABS, SLABS), SLABS), :]


def _slab_load(ref):
    rows = ref.shape[0] // SLABS
    return jnp.concatenate([ref[pl.ds(j, rows, stride=SLABS), :] for j in range(SLABS)], axis=1)


def _slab_store(ref, x):
    rows = ref.shape[0] // SLABS
    for j in range(SLABS):
        ref[pl.ds(j, rows, stride=SLABS), :] = x[:, j * LANES:(j + 1) * LANES]


def _in_proj_kernel(x_ref, g_ref, wrw_ref, wmla_ref, wg_ref, crw_ref, cmla_ref, gate_ref):
    hb = _rms(x_ref[...], g_ref[...]).astype(BF16)
    crw_ref[...] = jnp.dot(hb, wrw_ref[...], preferred_element_type=F32)
    cmla_ref[...] = jnp.dot(hb, wmla_ref[...], preferred_element_type=F32)
    gate_ref[...] = jax.nn.sigmoid(jnp.dot(hb, wg_ref[...], preferred_element_type=F32)).astype(BF16)


def _in_proj(x2, g, w_rw, w_mla, w_gate, tm):
    n, d = x2.shape
    full = lambda w: pl.BlockSpec(w.shape, lambda i: (0, 0), pipeline_mode=pl.Buffered(1))
    row = lambda c: pl.BlockSpec((tm, c), lambda i: (i, 0))
    return pl.pallas_call(
        _in_proj_kernel,
        grid=(n // tm,),
        in_specs=[row(d), full(g), full(w_rw), full(w_mla), full(w_gate)],
        out_specs=[row(w_rw.shape[1]), row(w_mla.shape[1]), row(w_gate.shape[1])],
        out_shape=[jax.ShapeDtypeStruct((n, w_rw.shape[1]), F32),
                   jax.ShapeDtypeStruct((n, w_mla.shape[1]), F32),
                   jax.ShapeDtypeStruct((n, w_gate.shape[1]), BF16)],
        compiler_params=_cparams("parallel"),
        name="in_proj",
    )(x2, g, w_rw, w_mla, w_gate)


RW_CHUNK = 64
RW_TILE = 1024
def _token_shift(cur, halo_ref, first):
    prev_row = jnp.where(first, 0.0, halo_ref[0, 7:8, :])
    rolled = pltpu.roll(cur, 1, 0)
    row = lax.broadcasted_iota(jnp.int32, cur.shape, 0)
    return jnp.where(row == 0, prev_row, rolled)


def _rwkv_kernel(r_ref, k_ref, v_ref, l_ref, hr_ref, hk_ref, hv_ref, hl_ref,
                 mur_ref, muk_ref, muv_ref, mul_ref, w0_ref, a0_ref, kk_ref, ka_ref, rk_ref,
                 gnw_ref, gnb_ref, wup_ref, aup_ref, gup_ref, y_ref, st_ref):
    i = pl.program_id(2)
    first = i == 0

    @pl.when(first)
    def _():
        st_ref[...] = jnp.zeros_like(st_ref)

    def mixed(c_ref, h_ref, mu_ref):
        cur = c_ref[0]
        return cur + (_token_shift(cur, h_ref, first) - cur) * mu_ref[...]

    zr = mixed(r_ref, hr_ref, mur_ref)
    zk = mixed(k_ref, hk_ref, muk_ref)
    zv = mixed(v_ref, hv_ref, muv_ref)
    zl = mixed(l_ref, hl_ref, mul_ref)
    z_wa = zl[:, :LANES]
    z_g = zl[:, LANES:]

    lane = lax.broadcasted_iota(jnp.int32, (LANES, LANES), 1)
    sub = lax.broadcasted_iota(jnp.int32, (LANES, LANES), 0)
    same_head = (lane // RW_HEAD_DIM) == (sub // RW_HEAD_DIM)
    seg = jnp.where(same_head, 1.0, 0.0).astype(BF16)

    w = w0_ref[...] + _mm(jnp.tanh(z_wa), wup_ref[...])
    u = -w
    softplus = jnp.maximum(u, 0.0) + jnp.log(1.0 + jnp.exp(-jnp.abs(u)))
    log_decay = -jnp.exp(-softplus - 0.5)
    a = jax.nn.sigmoid(a0_ref[...] + _mm(z_wa, aup_ref[...]))
    g = _mm(jax.nn.sigmoid(z_g), gup_ref[...])

    kk = zk * kk_ref[...]
    kk = kk / jnp.maximum(jnp.sqrt(_seg_sum(kk * kk, seg)), 1e-12)
    k2 = zk * (1.0 + (a - 1.0) * ka_ref[...])
    bonus = _seg_sum(zr * k2 * rk_ref[...], seg) * zv
    kka = kk * a

    c = RW_CHUNK
    crow = lax.broadcasted_iota(jnp.int32, (c, c), 0)
    ccol = lax.broadcasted_iota(jnp.int32, (c, c), 1)
    cum_sel = jnp.where(crow >= ccol, 1.0, 0.0).astype(BF16)
    tril_incl = sub >= lane
    tril_strict = sub > lane
    eye_l = jnp.where(lane == sub, 1.0, 0.0).astype(F32)
    lo_half = lax.broadcasted_iota(jnp.int32, (c, LANES), 1) < RW_HEAD_DIM
    nt = ((1,), (1,))
    tn = ((0,), (0,))
    zeros_blk = jnp.zeros((2 * c, LANES), BF16)
    zeros_half = jnp.zeros((c, LANES), BF16)

    def stack(t):
        tb = t.astype(BF16)
        return jnp.concatenate([jnp.where(lo_half, tb, zeros_half), jnp.where(lo_half, zeros_half, tb)], axis=0)

    tril_incl2 = jnp.concatenate([tril_incl, tril_incl], axis=1)

    def chunk_stages(ids):
        chunks = range(len(ids))
        x_a, x_b, x_k, x_r, x_v, x_bh, x_kh, w_tot, r_dec = [], [], [], [], [], [], [], [], []
        for ci in ids:
            sl = slice(ci * c, (ci + 1) * c)
            ld = log_decay[sl]
            cum = _mm_sel(cum_sel, ld)
            tot = cum[c - 1:c, :]
            e_neg = jnp.exp(-cum)
            e_rest = jnp.exp(tot - cum)
            x_a.append(stack(-kk[sl] * jnp.exp(cum - ld)))
            x_b.append(stack(kka[sl] * e_neg))
            x_k.append(stack(k2[sl] * e_neg))
            r_dec.append(zr[sl] * jnp.exp(cum))
            x_r.append(stack(r_dec[-1]))
            x_v.append(stack(zv[sl]))
            x_bh.append(stack(kka[sl] * e_rest))
            x_kh.append(stack(k2[sl] * e_rest))
            w_tot.append(jnp.exp(tot))

        inter = [_mm(jnp.concatenate([x_a[i], x_r[i]], axis=0),
                     jnp.concatenate([x_b[i], x_k[i]], axis=0), nt) for i in chunks]
        inter = [m.astype(BF16) for m in inter]
        zeros_sq_b = jnp.zeros((LANES, LANES), BF16)
        a_ab = [jnp.where(tril_strict, m[:2 * c, :2 * c], zeros_sq_b) for m in inter]
        a_ak = [jnp.where(tril_strict, m[:2 * c, 2 * c:], zeros_sq_b) for m in inter]
        a_r = [jnp.where(tril_incl2, m[2 * c:], jnp.concatenate([zeros_sq_b, zeros_sq_b], axis=1))
               for m in inter]
        w_ak = [_mm(a_ak[i], x_v[i]).astype(BF16) for i in chunks]

        t_inv = [eye_l + m.astype(F32) for m in a_ab]
        pw = a_ab
        for _ in range(int(math.log2(c)) - 1):
            pw = [_mm(m, m).astype(BF16) for m in pw]
            t_inv = [t_inv[i] + _mm(t_inv[i], pw[i]) for i in chunks]

        solved = [_mm(t_inv[i], jnp.concatenate([x_a[i], w_ak[i]], axis=1)).astype(BF16)
                  for i in chunks]
        rhs = [jnp.concatenate([solved[i], jnp.concatenate([zeros_blk, x_v[i]], axis=1)], axis=0)
               for i in chunks]
        out = [_mm(a_r[i], rhs[i]) for i in chunks]
        carry = [_mm(jnp.concatenate([x_bh[i], x_kh[i]], axis=0), rhs[i], tn) for i in chunks]
        q_hat, y_loc = [], []
        for i in chunks:
            q_hat.append(r_dec[i] + out[i][:c, :LANES] + out[i][c:, :LANES])
            y_loc.append(out[i][:c, LANES:] + out[i][c:, LANES:])
        trans = [jnp.concatenate([eye_l * w_tot[i] + carry[i][:, :LANES], carry[i][:, LANES:]], axis=1)
                 for i in chunks]
        return q_hat, y_loc, trans

    n_chunks = RW_TILE // c
    chunks = range(n_chunks)
    q_hat, y_loc, trans = chunk_stages(chunks)

    zeros_sq = jnp.zeros((LANES, LANES), F32)

    def compose(later, earlier):
        return _mm(later[:, :LANES], earlier) + jnp.concatenate([zeros_sq, later[:, LANES:]], axis=1)

    scan = list(trans)
    dist = 1
    while dist < len(scan):
        scan = [scan[i] if i < dist else compose(scan[i], scan[i - dist]) for i in range(len(scan))]
        dist *= 2
    prefix = [None] + scan
    q_pre = [None] + [_mm(q_hat[i], prefix[i]) for i in chunks[1:]]
    q_m = jnp.concatenate([q_hat[0]] + [q_pre[i][:, :LANES] for i in chunks[1:]], axis=0)
    y_off = jnp.concatenate([y_loc[0]] + [y_loc[i] + q_pre[i][:, LANES:] for i in chunks[1:]], axis=0)
    state = st_ref[...]
    y = _mm(q_m, state) + y_off
    st_ref[...] = _mm(prefix[-1][:, :LANES], state) + prefix[-1][:, LANES:]

    inv_n = 1.0 / RW_HEAD_DIM
    mean = _seg_sum(y, seg) * inv_n
    d = y - mean
    var = _seg_sum(d * d, seg) * inv_n
    yn = d * lax.rsqrt(var + GN_EPS) * gnw_ref[...] + gnb_ref[...]
    y_ref[0] = ((yn + bonus) * g).astype(y_ref.dtype)


def _rwkv(c_rw, p, batch, seq):
    ts = RW_TILE
    n_pairs = RW_DIM // HEAD_PAIR
    lora_blk = (3 * RW_DIM) // (2 * LANES)
    halo = ts // 8

    def col(off):
        return pl.BlockSpec((1, ts, LANES), lambda b, pp, i, off=off: (b, i, off + pp))

    def col_halo(off):
        return pl.BlockSpec((1, 8, LANES),
                            lambda b, pp, i, off=off: (b, jnp.maximum(i * halo - 1, 0), off + pp))

    vec = pl.BlockSpec((1, LANES), lambda b, pp, i: (0, pp))
    lora_w = pl.BlockSpec((LANES, LANES), lambda b, pp, i: (0, pp))
    in_specs = [
        col(0), col(n_pairs), col(2 * n_pairs),
        pl.BlockSpec((1, ts, 2 * LANES), lambda b, pp, i: (b, i, lora_blk)),
        col_halo(0), col_halo(n_pairs), col_halo(2 * n_pairs),
        pl.BlockSpec((1, 8, 2 * LANES), lambda b, pp, i: (b, jnp.maximum(i * halo - 1, 0), lora_blk)),
        vec, vec, vec, pl.BlockSpec((1, 2 * LANES), lambda b, pp, i: (0, 0)),
        vec, vec, vec, vec, vec, vec, vec, lora_w, lora_w, lora_w,
    ]
    return pl.pallas_call(
        _rwkv_kernel,
        grid=(batch, n_pairs, seq // ts),
        in_specs=in_specs,
        out_specs=pl.BlockSpec((1, ts, LANES), lambda b, pp, i: (b, i, pp)),
        out_shape=jax.ShapeDtypeStruct((batch, seq, RW_DIM), BF16),
        scratch_shapes=[pltpu.VMEM((LANES, LANES), F32)],
        compiler_params=_cparams("parallel", "parallel", "arbitrary"),
        name="rwkv",
    )(c_rw, c_rw, c_rw, c_rw, c_rw, c_rw, c_rw, c_rw,
      p["mu_r"], p["mu_k"], p["mu_v"], p["mu_l"], p["w0"], p["a0"], p["k_k"], p["k_a"], p["r_k"],
      p["gn_w"], p["gn_b"], p["w_up"], p["a_up"], p["g_up"])


MLA_SLOT = 128


def _mla_prep_kernel(cmla_ref, cs_ref, gq_ref, gkv_ref,
                     wqa_ref, wqb_ref, wk_ref, wv_ref, pa_ref, pb_ref, place_ref, one_ref,
                     qt_ref, k_ref, vt_ref):
    cs = cs_ref[...]
    cs_hi = cs.astype(BF16)
    cs_lo = (cs - cs_hi.astype(F32)).astype(BF16)
    tables = (jnp.dot(cs_hi, place_ref[...], preferred_element_type=F32)
              + jnp.dot(cs_lo, place_ref[...], preferred_element_type=F32))
    cos = tables[:, :MLA_SLOT] + one_ref[...]
    sin = tables[:, MLA_SLOT:]
    zq = _rms(cmla_ref[:, :Q_LORA], gq_ref[...]).astype(BF16)
    qa = jnp.dot(zq, wqa_ref[...], preferred_element_type=F32)
    qb = jnp.dot(zq, wqb_ref[...], preferred_element_type=F32)
    ckvr = cmla_ref[:, Q_LORA:]
    zkv = _rms(ckvr[:, :KV_LORA], gkv_ref[...]).astype(BF16)
    kn = jnp.dot(zkv, wk_ref[...], preferred_element_type=F32)
    v = jnp.dot(zkv, wv_ref[...], preferred_element_type=F32)
    for blk in range(v.shape[1] // LANES):
        vt_ref[0, blk * LANES:(blk + 1) * LANES, :] = v[:, blk * LANES:(blk + 1) * LANES].astype(BF16).T
    kr = ckvr[:, KV_LORA:].astype(BF16)
    k_rope = (jnp.dot(kr, pa_ref[...], preferred_element_type=F32) * cos
              + jnp.dot(kr, pb_ref[...], preferred_element_type=F32) * sin)
    scale = math.log2(math.e) / math.sqrt(QK_NOPE + QK_ROPE)
    for h in range(MLA_HEADS):
        sl = slice(h * MLA_SLOT, (h + 1) * MLA_SLOT)
        qt_ref[0, sl, :] = ((qa[:, sl] * cos + qb[:, sl] * sin) * scale).astype(BF16).T
        k_ref[:, sl] = (kn[:, sl] + k_rope).astype(BF16)


def _mla_prep(c_mla, cos_sin, p, tm, batch, seq):
    n = c_mla.shape[0]
    full = lambda w: pl.BlockSpec(w.shape, lambda i: (0, 0), pipeline_mode=pl.Buffered(1))
    row = lambda c: pl.BlockSpec((tm, c), lambda i: (i, 0))
    per_seq = seq // tm
    col = lambda r: pl.BlockSpec((1, r, tm), lambda i: (i // per_seq, 0, i % per_seq))
    ws = [p["g_qa"], p["g_kva"], p["w_qa"], p["w_qb"], p["w_k"], p["w_v"], p["p_a"], p["p_b"],
          p["place"], p["one"]]
    hq = MLA_HEADS * MLA_SLOT
    hv = MLA_HEADS * V_HEAD
    return pl.pallas_call(
        _mla_prep_kernel,
        grid=(n // tm,),
        in_specs=[row(c_mla.shape[1]), row(cos_sin.shape[1])] + [full(w) for w in ws],
        out_specs=[col(hq), row(hq), col(hv)],
        out_shape=[jax.ShapeDtypeStruct((batch, hq, seq), BF16), jax.ShapeDtypeStruct((n, hq), BF16),
                   jax.ShapeDtypeStruct((batch, hv, seq), BF16)],
        compiler_params=_cparams("parallel"),
        name="mla_prep",
    )(c_mla, cos_sin, *ws)


ATT_TILE = 512
ATT_HEADS = 4


def _attn_kernel(qt_ref, k_ref, vt_ref, o_ref):
    qi = pl.program_id(2)
    t = ATT_TILE
    heads = range(ATT_HEADS)
    den_row = (V_HEAD, 0)

    def augment(h, vt):
        r = lax.broadcasted_iota(jnp.int32, vt.shape, 0)
        own = (r < V_HEAD) if h % 2 == 0 else (r >= V_HEAD)
        return jnp.where(own, vt, jnp.where(r == den_row[h % 2], 1.0, 0.0).astype(BF16))

    def block(j, carry, mask):
        keys = pl.ds(pl.multiple_of(j * t, t), t)
        sts = []
        for h in heads:
            qt = qt_ref[0, h * MLA_SLOT:(h + 1) * MLA_SLOT, :]
            kb = k_ref[0, keys, h * MLA_SLOT:(h + 1) * MLA_SLOT]
            st = jnp.dot(kb, qt, preferred_element_type=F32)
            sts.append(st if mask is None else jnp.where(mask, st, -jnp.inf))
        m_new = [jnp.maximum(carry[h][0], jnp.max(sts[h], axis=0, keepdims=True)) for h in heads]
        pts = [jnp.exp2((sts[h] - m_new[h]).astype(BF16)) for h in heads]
        out = []
        for h in heads:
            m, acc = carry[h]
            vt = vt_ref[0, (h // 2) * 2 * V_HEAD:(h // 2 + 1) * 2 * V_HEAD, keys]
            acc = acc * jnp.exp2(m - m_new[h]) + jnp.dot(augment(h, vt), pts[h], preferred_element_type=F32)
            out.append((m_new[h], acc))
        return tuple(out)

    init1 = (jnp.full((1, t), -jnp.inf, F32), jnp.zeros((2 * V_HEAD, t), F32))
    carry = lax.fori_loop(0, qi, lambda j, c: block(j, c, None), tuple(init1 for _ in heads))

    causal = lax.broadcasted_iota(jnp.int32, (t, t), 0) <= lax.broadcasted_iota(jnp.int32, (t, t), 1)
    accs = [mc[1] for mc in block(qi, carry, causal)]
    lo_rows = lax.broadcasted_iota(jnp.int32, (2 * V_HEAD, t), 0) < V_HEAD
    for p in range(ATT_HEADS // 2):
        acc0, acc1 = accs[2 * p], accs[2 * p + 1]
        den0 = acc0[den_row[0]:den_row[0] + 1, :]
        den1 = acc1[den_row[1]:den_row[1] + 1, :]
        out_t = jnp.where(lo_rows, acc0 / den0, acc1 / den1)
        o_ref[0, :, p * 2 * V_HEAD:(p + 1) * 2 * V_HEAD] = out_t.T.astype(o_ref.dtype)


def _mla_attn(qt, k, vt, batch, seq):
    t = ATT_TILE
    g = ATT_HEADS
    return pl.pallas_call(
        _attn_kernel,
        grid=(batch, MLA_HEADS // g, seq // t),
        in_specs=[pl.BlockSpec((1, g * MLA_SLOT, t), lambda b, hp, i: (b, hp, i)),
                  pl.BlockSpec((1, seq, g * MLA_SLOT), lambda b, hp, i: (b, 0, hp)),
                  pl.BlockSpec((1, g * V_HEAD, seq), lambda b, hp, i: (b, hp, 0))],
        out_specs=pl.BlockSpec((1, t, g * V_HEAD), lambda b, hp, i: (b, i, hp)),
        out_shape=jax.ShapeDtypeStruct((batch, seq, MLA_HEADS * V_HEAD), BF16),
        compiler_params=_cparams("parallel", "parallel", "arbitrary"),
        name="mla_attn",
    )(qt, k, vt)


ROUTE_W = 128
ROUTE_FINE0 = 8


def _merge_kernel(x_ref, yrw_ref, ymla_ref, gate_ref, wbr_ref, wbm_ref, wo_ref, fg_ref,
                  wr_ref, br_ref, x1_ref, h2p_ref, route_ref, route_t_ref, hist_ref):
    d = x_ref.shape[1]
    a = jnp.dot(yrw_ref[...], wbr_ref[...], preferred_element_type=F32)
    b = jnp.dot(ymla_ref[...], wbm_ref[...], preferred_element_type=F32)
    merged = gate_ref[:, :d].astype(F32) * a + gate_ref[:, d:].astype(F32) * b
    x1 = x_ref[...] + jnp.dot(merged.astype(BF16), wo_ref[...], preferred_element_type=F32)
    x1_ref[...] = x1
    h2 = _rms(x1, fg_ref[...])
    _slab_store(h2p_ref, _pack_rows(h2))

    h_hi = h2.astype(BF16)
    h_lo = (h2 - h_hi.astype(F32)).astype(BF16)
    both = jnp.dot(h_hi, wr_ref[...], preferred_element_type=F32)
    logits = (both[:, :ROUTE_W] + both[:, ROUTE_W:]
              + jnp.dot(h_lo, wr_ref[:, :ROUTE_W], preferred_element_type=F32)) + br_ref[...]

    tm = logits.shape[0]
    lt = logits.T
    sub = lax.broadcasted_iota(jnp.int32, (EXPERTS_PER_GROUP, tm), 0)
    big = jnp.int32(EXPERTS_PER_GROUP)
    neg = -jnp.inf

    def first_argmax(vals, vmax):
        return jnp.min(jnp.where(vals == vmax, sub, big), axis=0, keepdims=True)

    grp = jnp.where(sub < N_GROUPS, lt[:EXPERTS_PER_GROUP], neg)
    g_max = jnp.max(grp, axis=0, keepdims=True)
    g_den = jnp.sum(jnp.exp(grp - g_max), axis=0, keepdims=True)
    g_sel = first_argmax(grp, g_max)
    gate_g = 1.0 / g_den
    fine = lt[ROUTE_FINE0:ROUTE_FINE0 + EXPERTS_PER_GROUP]
    for g in range(1, N_GROUPS):
        lo = ROUTE_FINE0 + g * EXPERTS_PER_GROUP
        fine = jnp.where(g_sel == g, lt[lo:lo + EXPERTS_PER_GROUP], fine)
    v1 = jnp.max(fine, axis=0, keepdims=True)
    i1 = first_argmax(fine, v1)
    fine2 = jnp.where(sub == i1, neg, fine)
    v2 = jnp.max(fine2, axis=0, keepdims=True)
    i2 = first_argmax(fine2, v2)
    e2 = jnp.exp(v2 - v1)
    den = 1.0 + e2
    w1 = gate_g / den
    w2 = gate_g * e2 / den
    base_e = g_sel * EXPERTS_PER_GROUP
    route_t = jnp.where(sub == 0, (base_e + i1).astype(F32),
                        jnp.where(sub == 1, (base_e + i2).astype(F32),
                                  jnp.where(sub == 2, w1, jnp.where(sub == 3, w2, 0.0))))
    pad = jnp.zeros((ROUTE_W - EXPERTS_PER_GROUP, tm), F32)
    route_t_ref[...] = route_t
    route_ref[...] = jnp.concatenate([route_t, pad], axis=0).T
    picked = (sub == i1) | (sub == i2)
    for g in range(N_GROUPS):
        cnt = jnp.sum(jnp.where(picked & (g_sel == g), 1.0, 0.0), axis=1, keepdims=True)
        hist_ref[0, g * EXPERTS_PER_GROUP:(g + 1) * EXPERTS_PER_GROUP, :] = jnp.broadcast_to(
            cnt, (EXPERTS_PER_GROUP, ROUTE_W))


def _merge(x2, y_rw, y_mla, gates, p, tm):
    n, d = x2.shape
    full = lambda w: pl.BlockSpec(w.shape, lambda i: (0, 0), pipeline_mode=pl.Buffered(1))
    row = lambda c: pl.BlockSpec((tm, c), lambda i: (i, 0))
    ws = [p["w_br"], p["w_bm"], p["w_out"], p["ffn_g"], p["w_route"], p["b_route"]]
    return pl.pallas_call(
        _merge_kernel,
        grid=(n // tm,),
        in_specs=[row(d), row(y_rw.shape[1]), row(y_mla.shape[1]), row(2 * d)] + [full(w) for w in ws],
        out_specs=[row(d), pl.BlockSpec((tm * SLABS, LANES), lambda i: (i, 0)), row(ROUTE_W),
                   pl.BlockSpec((EXPERTS_PER_GROUP, tm), lambda i: (0, i)),
                   pl.BlockSpec((1, N_EXPERTS, ROUTE_W), lambda i: (i, 0, 0))],
        out_shape=[jax.ShapeDtypeStruct((n, d), F32), jax.ShapeDtypeStruct((n * SLABS, LANES), jnp.uint32),
                   jax.ShapeDtypeStruct((n, ROUTE_W), F32),
                   jax.ShapeDtypeStruct((EXPERTS_PER_GROUP, n), F32),
                   jax.ShapeDtypeStruct((n // tm, N_EXPERTS, ROUTE_W), F32)],
        compiler_params=_cparams("parallel"),
        name="merge_route",
    )(x2, y_rw, y_mla, gates, *ws)


def _plan_kernel(route_t_ref, base_ref, dest_ref):
    tm = route_t_ref.shape[1]
    rt = route_t_ref[...]
    expert = lax.broadcasted_iota(jnp.int32, (N_EXPERTS, tm), 0).astype(F32)
    pick = [expert == rt[k:k + 1, :] for k in range(TOP_K)]
    both = jnp.where(pick[0] | pick[1], 1.0, 0.0).astype(BF16)
    r = lax.broadcasted_iota(jnp.int32, (tm, tm), 0)
    c = lax.broadcasted_iota(jnp.int32, (tm, tm), 1)
    earlier = jnp.where(r < c, 1.0, 0.0).astype(BF16)
    offs = jnp.dot(both, earlier, preferred_element_type=F32) + base_ref[0][:, :1]
    rows = [jnp.sum(jnp.where(pk, offs, 0.0), axis=0, keepdims=True) for pk in pick]
    sub = lax.broadcasted_iota(jnp.int32, dest_ref.shape, 0)
    dest_ref[...] = jnp.where(sub == 0, rows[0], jnp.where(sub == 1, rows[1], 0.0)).astype(jnp.int32)


def _plan(route_t, base, tm):
    n = route_t.shape[1]
    return pl.pallas_call(
        _plan_kernel,
        grid=(n // tm,),
        in_specs=[pl.BlockSpec((EXPERTS_PER_GROUP, tm), lambda i: (0, i)),
                  pl.BlockSpec((1, N_EXPERTS, ROUTE_W), lambda i: (i, 0, 0))],
        out_specs=pl.BlockSpec((EXPERTS_PER_GROUP, tm), lambda i: (0, i)),
        out_shape=jax.ShapeDtypeStruct((EXPERTS_PER_GROUP, n), jnp.int32),
        compiler_params=_cparams("parallel"),
        name="route_plan",
    )(route_t, base)


def _dest_tiles(dest_t, tm):
    n = dest_t.shape[1]
    return dest_t[:TOP_K].reshape(TOP_K, n // tm, tm).transpose(1, 0, 2).reshape(n // tm, 1, TOP_K * tm)


DISPATCH_TILE = 1024
ROW_DMA_UNROLL = 8


def _dispatch_kernel(pad_end_ref, dest_ref, h_ref, xs_ref, zbuf, sem, zsem):
    tm = h_ref.shape[0] // SLABS

    @pl.when(pl.program_id(0) == 0)
    def _():
        zbuf[...] = jnp.zeros_like(zbuf)

        def tail(e):
            first = pl.multiple_of((pad_end_ref[e] - EXPERT_BLOCK) * SLABS, EXPERT_BLOCK * SLABS)
            return pltpu.make_async_copy(zbuf, xs_ref.at[pl.ds(first, EXPERT_BLOCK * SLABS), :], zsem)

        def region_rows(e):
            return pad_end_ref[e] - (pad_end_ref[e - 1] if e else 0)

        n_blocks = xs_ref.shape[0] // (EXPERT_BLOCK * SLABS)
        used_rows = pad_end_ref[N_EXPERTS - 1]

        def spare(b):
            return pltpu.make_async_copy(zbuf, xs_ref.at[pl.ds(b * EXPERT_BLOCK * SLABS, EXPERT_BLOCK * SLABS), :],
                                         zsem)

        spare_blocks = range(n_blocks - N_EXPERTS, n_blocks)
        for e in range(N_EXPERTS):
            @pl.when(region_rows(e) > 0)
            def _(e=e):
                tail(e).start()
        for b in spare_blocks:
            @pl.when(b * EXPERT_BLOCK >= used_rows)
            def _(b=b):
                spare(b).start()
        for e in range(N_EXPERTS):
            @pl.when(region_rows(e) > 0)
            def _(e=e):
                tail(e).wait()
        for b in spare_blocks:
            @pl.when(b * EXPERT_BLOCK >= used_rows)
            def _(b=b):
                spare(b).wait()

    def start(t, _):
        for k in range(TOP_K):
            pltpu.make_async_copy(_slab_rows(h_ref, t), _slab_rows(xs_ref, dest_ref[0, 0, k * tm + t]),
                                  sem).start(priority=k % 2)
        return 0

    lax.fori_loop(0, tm, start, 0, unroll=ROW_DMA_UNROLL)
    all_rows = xs_ref.at[pl.ds(0, TOP_K * tm * SLABS), :]
    pltpu.make_async_copy(all_rows, all_rows, sem).wait()


def _dispatch(h2, dest, pad_end, p_rows):
    n = h2.shape[0] // SLABS
    tm = DISPATCH_TILE
    dest3 = _dest_tiles(dest, tm)
    grid_spec = pltpu.PrefetchScalarGridSpec(
        num_scalar_prefetch=1,
        grid=(n // tm,),
        in_specs=[pl.BlockSpec((1, 1, TOP_K * tm), lambda i, pe: (i, 0, 0), memory_space=pltpu.SMEM),
                  pl.BlockSpec((tm * SLABS, LANES), lambda i, pe: (i, 0))],
        out_specs=pl.BlockSpec(memory_space=pl.ANY),
        scratch_shapes=[pltpu.VMEM((EXPERT_BLOCK * SLABS, LANES), jnp.uint32),
                        pltpu.SemaphoreType.DMA(()), pltpu.SemaphoreType.DMA(())],
    )
    return pl.pallas_call(
        _dispatch_kernel,
        grid_spec=grid_spec,
        out_shape=jax.ShapeDtypeStruct((p_rows * SLABS, LANES), jnp.uint32),
        compiler_params=_cparams("arbitrary"),
        name="dispatch",
    )(pad_end, dest3, h2)


def _expert_kernel(blk_e_ref, n_used_ref, x_ref, wgu_ref, wd_ref, y_ref, wgu_b, wd_b):
    i = pl.program_id(0)

    @pl.when((i == 0) | (blk_e_ref[i] != blk_e_ref[jnp.maximum(i - 1, 0)]))
    def _():
        wgu_b[...] = wgu_ref[0].astype(BF16)
        wd_b[...] = wd_ref[0].astype(BF16)

    @pl.when(i < n_used_ref[0])
    def _():
        x = _unpack_rows(_slab_load(x_ref)).astype(BF16)
        h = jnp.dot(x, wgu_b[...], preferred_element_type=F32)
        gt = h[:, :D_EXPERT]
        up = h[:, D_EXPERT:]
        act = (gt * jax.nn.sigmoid(gt) * up).astype(BF16)
        _slab_store(y_ref, _pack_rows(jnp.dot(act, wd_b[...], preferred_element_type=F32)))

    @pl.when(pl.program_id(0) >= n_used_ref[0])
    def _():
        y_ref[...] = jnp.zeros_like(y_ref)


def _experts(xs, blk_expert, n_used, w_gu, w_down):
    p_rows = xs.shape[0] // SLABS
    d = 2 * SLABS * LANES
    n_blocks = p_rows // EXPERT_BLOCK
    rows = pl.BlockSpec((EXPERT_BLOCK * SLABS, LANES), lambda i, be, nu: (i, 0))
    grid_spec = pltpu.PrefetchScalarGridSpec(
        num_scalar_prefetch=2,
        grid=(n_blocks,),
        in_specs=[rows,
                  pl.BlockSpec((1, d, 2 * D_EXPERT), lambda i, be, nu: (be[i], 0, 0)),
                  pl.BlockSpec((1, D_EXPERT, d), lambda i, be, nu: (be[i], 0, 0))],
        out_specs=rows,
        scratch_shapes=[pltpu.VMEM((d, 2 * D_EXPERT), BF16), pltpu.VMEM((D_EXPERT, d), BF16)],
    )
    return pl.pallas_call(
        _expert_kernel,
        grid_spec=grid_spec,
        out_shape=jax.ShapeDtypeStruct(xs.shape, jnp.uint32),
        compiler_params=_cparams("arbitrary"),
        name="experts",
    )(blk_expert, n_used, xs, w_gu, w_down)


COMBINE_TILE = 512


def _combine_kernel(dest_ref, dest_next_ref, x1_ref, route_ref, g_ref, yb_ref, o_ref,
                    buf00, buf01, buf10, buf11, sems, *, final_norm):
    tm = x1_ref.shape[0]
    i = pl.program_id(0)
    bufs = ((buf00, buf01), (buf10, buf11))

    def issue(d_ref, slot):
        def start(t, _):
            for k in range(TOP_K):
                pltpu.make_async_copy(_slab_rows(yb_ref, d_ref[0, 0, k * tm + t]),
                                      _slab_rows(bufs[slot][k], t), sems.at[slot]).start(priority=k % 2)
            return 0

        lax.fori_loop(0, tm, start, 0, unroll=ROW_DMA_UNROLL)

    @pl.when(i == 0)
    def _():
        issue(dest_ref, 0)

    for slot in range(2):
        @pl.when((i % 2 == slot) & (i + 1 < pl.num_programs(0)))
        def _(slot=slot):
            issue(dest_next_ref, 1 - slot)

    for slot in range(2):
        @pl.when(i % 2 == slot)
        def _(slot=slot):
            for b in bufs[slot]:
                pltpu.make_async_copy(b, b, sems.at[slot]).wait()
            route = route_ref[...]
            x2 = (x1_ref[...] + route[:, 2:3] * _unpack_rows(_slab_load(bufs[slot][0]))
                  + route[:, 3:4] * _unpack_rows(_slab_load(bufs[slot][1])))
            o_ref[...] = _rms(x2, g_ref[...]) if final_norm else x2


def _combine(x1, route, dest, yb, final_g, final_norm):
    n, d = x1.shape
    tm = COMBINE_TILE
    n_tiles = n // tm
    dest3 = _dest_tiles(dest, tm)
    return pl.pallas_call(
        functools.partial(_combine_kernel, final_norm=final_norm),
        grid=(n_tiles,),
        in_specs=[pl.BlockSpec((1, 1, TOP_K * tm), lambda i: (i, 0, 0), memory_space=pltpu.SMEM),
                  pl.BlockSpec((1, 1, TOP_K * tm), lambda i: (jnp.minimum(i + 1, n_tiles - 1), 0, 0),
                               memory_space=pltpu.SMEM),
                  pl.BlockSpec((tm, d), lambda i: (i, 0)),
                  pl.BlockSpec((tm, ROUTE_W), lambda i: (i, 0)),
                  pl.BlockSpec((1, d), lambda i: (0, 0)),
                  pl.BlockSpec(memory_space=pl.ANY)],
        out_specs=pl.BlockSpec((tm, d), lambda i: (i, 0)),
        out_shape=jax.ShapeDtypeStruct((n, d), F32),
        scratch_shapes=[pltpu.VMEM((tm * SLABS, LANES), jnp.uint32) for _ in range(2 * TOP_K)]
                       + [pltpu.SemaphoreType.DMA((2,))],
        compiler_params=_cparams("arbitrary"),
        name="combine",
    )(dest3, dest3, x1, route, final_g, yb)


def _rwkv_params(rw_mu, rw_w0, rw_w_up, rw_a0, rw_a_up, rw_g_up, rw_k_k, rw_k_a, rw_r_k, rw_gn_w, rw_gn_b):
    row = lambda v: v.reshape(1, -1).astype(F32)
    zeros = jnp.zeros((A_LORA, RW_DIM), F32)
    return {
        "mu_r": row(rw_mu[:RW_DIM]), "mu_k": row(rw_mu[RW_DIM:2 * RW_DIM]),
        "mu_v": row(rw_mu[2 * RW_DIM:3 * RW_DIM]), "mu_l": row(rw_mu[3 * RW_DIM:]),
        "w0": row(rw_w0), "a0": row(rw_a0), "k_k": row(rw_k_k), "k_a": row(rw_k_a),
        "r_k": row(rw_r_k), "gn_w": row(rw_gn_w), "gn_b": row(rw_gn_b),
        "w_up": jnp.concatenate([rw_w_up, zeros], axis=0).astype(BF16),
        "a_up": jnp.concatenate([zeros, rw_a_up], axis=0).astype(BF16),
        "g_up": rw_g_up.astype(BF16),
    }


def _mla_params(g_qa, w_q_up, g_kva, w_kv_up):
    half = QK_ROPE // 2
    pad = MLA_SLOT - QK_NOPE - QK_ROPE
    wq = w_q_up.reshape(Q_LORA, MLA_HEADS, QK_NOPE + QK_ROPE)
    q_nope, q_r1, q_r2 = wq[..., :QK_NOPE], wq[..., QK_NOPE:QK_NOPE + half], wq[..., QK_NOPE + half:]
    zq = lambda w: jnp.zeros((Q_LORA, MLA_HEADS, w), F32)
    w_qa = jnp.concatenate([q_nope, q_r1, q_r2, zq(pad)], axis=-1).reshape(Q_LORA, -1)
    w_qb = jnp.concatenate([zq(QK_NOPE), -q_r2, q_r1, zq(pad)], axis=-1).reshape(Q_LORA, -1)
    wkv = w_kv_up.reshape(KV_LORA, MLA_HEADS, QK_NOPE + V_HEAD)
    w_k = jnp.concatenate([wkv[..., :QK_NOPE], jnp.zeros((KV_LORA, MLA_HEADS, MLA_SLOT - QK_NOPE), F32)],
                          axis=-1).reshape(KV_LORA, -1)
    w_v = wkv[..., QK_NOPE:].reshape(KV_LORA, -1)
    eye = jnp.eye(half, dtype=F32)
    z = jnp.zeros((half, half), F32)
    zl = jnp.zeros((QK_ROPE, QK_NOPE), F32)
    zr = jnp.zeros((QK_ROPE, pad), F32)
    p_a = jnp.concatenate([zl, jnp.concatenate([eye, z], 0), jnp.concatenate([z, eye], 0), zr], axis=1)
    p_b = jnp.concatenate([zl, jnp.concatenate([z, -eye], 0), jnp.concatenate([eye, z], 0), zr], axis=1)
    place_half = jnp.concatenate([jnp.zeros((half, QK_NOPE), F32), eye, eye, jnp.zeros((half, pad), F32)], axis=1)
    zh = jnp.zeros_like(place_half)
    place = jnp.concatenate([jnp.concatenate([place_half, zh], 1), jnp.concatenate([zh, place_half], 1)], 0)
    one = jnp.concatenate([jnp.ones((1, QK_NOPE), F32), jnp.zeros((1, MLA_SLOT - QK_NOPE), F32)], axis=1)
    return {"g_qa": g_qa.reshape(1, -1), "g_kva": g_kva.reshape(1, -1),
            "w_qa": w_qa.astype(BF16), "w_qb": w_qb.astype(BF16), "w_k": w_k.astype(BF16),
            "w_v": w_v.astype(BF16), "p_a": p_a.astype(BF16), "p_b": p_b.astype(BF16),
            "place": place.astype(BF16), "one": one}


def _rope_cos_sin(positions):
    inv_freq = ROPE_THETA ** (-jnp.arange(0, QK_ROPE, 2, dtype=F32) / QK_ROPE)
    ang = positions.astype(F32).reshape(-1, 1) * inv_freq
    return jnp.concatenate([jnp.cos(ang), jnp.sin(ang)], axis=1)


def _block_layout(hist, n_assign):
    tile_counts = hist[:, :, 0].astype(jnp.int32)
    counts = jnp.sum(tile_counts, axis=0)
    padded = (counts + EXPERT_BLOCK - 1) // EXPERT_BLOCK * EXPERT_BLOCK
    pad_end = jnp.cumsum(padded)
    pad_start = pad_end - padded
    tile_base = jnp.cumsum(tile_counts, axis=0) - tile_counts + pad_start[None, :]
    base = jnp.broadcast_to(tile_base.astype(F32)[:, :, None], tile_base.shape + (ROUTE_W,))
    n_blocks = -(-n_assign // EXPERT_BLOCK) + N_EXPERTS
    blk_row = jnp.arange(n_blocks, dtype=jnp.int32) * EXPERT_BLOCK
    blk_expert = jnp.minimum(jnp.sum((pad_end[None, :] <= blk_row[:, None]).astype(jnp.int32), axis=1),
                             N_EXPERTS - 1)
    n_used = (pad_end[-1] // EXPERT_BLOCK).astype(jnp.int32).reshape(1)
    return base, blk_expert, n_used, n_blocks, pad_end.astype(jnp.int32)


def kernel(x, positions, mix_norm_g, w_in, rw_mu, rw_w0, rw_w_up, rw_a0, rw_a_up, rw_g_up, rw_k_k, rw_k_a, rw_r_k, rw_gn_w, rw_gn_b, mla_g_qa, mla_w_q_up, mla_g_kva, mla_w_kv_up, w_branch_rw, w_branch_mla, w_out, ffn_norm_g, moe_w_group, moe_b_group, moe_w_router, moe_b_router, moe_w_gu, moe_w_down, final_norm_g):
    batch, seq, d = x.shape
    assert d == 2 * SLABS * LANES
    assert seq % RW_TILE == 0 and seq % ATT_TILE == 0 and seq % ROW_TILE == 0
    n = batch * seq
    assert n % DISPATCH_TILE == 0 and n % COMBINE_TILE == 0
    depth = w_in.shape[0]
    rw_cols = 3 * RW_DIM + W_LORA + A_LORA + G_LORA
    mla_cols = Q_LORA + KV_LORA + QK_ROPE
    cos_sin = _rope_cos_sin(positions)
    x2 = x.reshape(n, d)

    for l in range(depth):
        cut = (0, rw_cols, rw_cols + mla_cols, w_in.shape[2])
        w_rw, w_mla, w_gate = (w_in[l][:, cut[j]:cut[j + 1]].astype(BF16) for j in range(3))
        c_rw, c_mla, gates = _in_proj(x2, mix_norm_g[l].reshape(1, d), w_rw, w_mla, w_gate, tm=ROW_TILE)

        rp = _rwkv_params(rw_mu[l], rw_w0[l], rw_w_up[l], rw_a0[l], rw_a_up[l], rw_g_up[l], rw_k_k[l],
                          rw_k_a[l], rw_r_k[l], rw_gn_w[l], rw_gn_b[l])
        y_rw = _rwkv(c_rw.reshape(batch, seq, rw_cols), rp, batch, seq).reshape(n, RW_DIM)

        mp = _mla_params(mla_g_qa[l], mla_w_q_up[l], mla_g_kva[l], mla_w_kv_up[l])
        q_t, k, v_t = _mla_prep(c_mla, cos_sin, mp, ROW_TILE, batch, seq)
        y_mla = _mla_attn(q_t, k.reshape(batch, seq, -1), v_t, batch, seq).reshape(n, MLA_HEADS * V_HEAD)

        gap, tail = ROUTE_FINE0 - N_GROUPS, ROUTE_W - ROUTE_FINE0 - N_EXPERTS
        w_route = jnp.concatenate(
            [moe_w_group[l], jnp.zeros((d, gap), F32), moe_w_router[l], jnp.zeros((d, tail), F32)], axis=1)
        b_route = jnp.concatenate(
            [moe_b_group[l], jnp.zeros((gap,), F32), moe_b_router[l], jnp.zeros((tail,), F32)]).reshape(1, -1)
        wr_hi = w_route.astype(BF16)
        wr_lo = (w_route - wr_hi.astype(F32)).astype(BF16)
        mparams = {"w_br": w_branch_rw[l].astype(BF16), "w_bm": w_branch_mla[l].astype(BF16),
                   "w_out": w_out[l].astype(BF16), "ffn_g": ffn_norm_g[l].reshape(1, d),
                   "w_route": jnp.concatenate([wr_hi, wr_lo], axis=1), "b_route": b_route}
        x1, h2p, route, route_t, hist = _merge(x2, y_rw, y_mla, gates, mparams, tm=ROW_TILE)

        base, blk_expert, n_used, n_blocks, pad_end = _block_layout(hist, n * TOP_K)
        dest = _plan(route_t, base, tm=ROW_TILE)
        xs = _dispatch(h2p, dest, pad_end, n_blocks * EXPERT_BLOCK)
        yb = _experts(xs, blk_expert, n_used, moe_w_gu[l], moe_w_down[l])
        x2 = _combine(x1, route, dest, yb, final_norm_g.reshape(1, d), final_norm=(l == depth - 1))

    return x2.reshape(batch, seq, d)
```
